```python
import math
import jax, jax.numpy as jnp
from jax import lax
import numpy as np

D_MODEL = 1024
BATCH = 8
SEQ = 4096
DEPTH = 2

D_MIX = D_MODEL
SSD_HEADS = 8
SSD_HEAD_DIM = 64
SSD_INNER = SSD_HEADS * SSD_HEAD_DIM
SSD_GROUPS = 2
SSD_STATE = 128
SSD_CONV = 4
SSD_CHUNK = 128
SSD_XBC = SSD_INNER + 2 * SSD_GROUPS * SSD_STATE
ATT_Q_HEADS = 4
ATT_KV_HEADS = 2
ATT_HEAD_DIM = 64
ATT_WIDTH = ATT_Q_HEADS * ATT_HEAD_DIM
ATT_KV_WIDTH = ATT_KV_HEADS * ATT_HEAD_DIM
WINDOW = 128
REL_BUCKETS = 32
REL_MAX_DIST = 128
GM_GROUPS = 4
GM_GROUP_DIM = 64
GM_WIDTH = GM_GROUPS * GM_GROUP_DIM
GM_CHUNK = 128
D_FF = 2816
EPS = 1e-6

OFF_Z = 0
OFF_XBC = OFF_Z + SSD_INNER
OFF_DT = OFF_XBC + SSD_XBC
OFF_Q = OFF_DT + SSD_HEADS
OFF_K = OFF_Q + ATT_WIDTH
OFF_V = OFF_K + ATT_KV_WIDTH
OFF_U = OFF_V + ATT_KV_WIDTH
OFF_GV = OFF_U + GM_WIDTH
D_IN_PROJ = OFF_GV + GM_WIDTH

kernel_name = "hybrid_ssd_swa_sgu_macaron"


def rmsnorm(x, w):
    xf = x.astype(jnp.float32)
    y = xf * lax.rsqrt(jnp.mean(xf * xf, axis=-1, keepdims=True) + EPS)
    return (y * w.astype(jnp.float32)).astype(x.dtype)


def layernorm(x, w, b):
    xf = x.astype(jnp.float32)
    mu = jnp.mean(xf, axis=-1, keepdims=True)
    xc = xf - mu
    y = xc * lax.rsqrt(jnp.mean(xc * xc, axis=-1, keepdims=True) + EPS)
    return (y * w.astype(jnp.float32) + b.astype(jnp.float32)).astype(x.dtype)


def swiglu(x, w_gate, w_up, w_down):
    return (jax.nn.silu(x @ w_gate) * (x @ w_up)) @ w_down


def causal_depthwise_conv(x, w, b):
    c = x.shape[-1]
    y = lax.conv_general_dilated(
        x, w.astype(x.dtype)[:, None, :], window_strides=(1,),
        padding=[(SSD_CONV - 1, 0)], dimension_numbers=("NWC", "WIO", "NWC"),
        feature_group_count=c)
    return y + b.astype(x.dtype)


def ssd_mixer(z, xbc_raw, dt_raw, conv_w, conv_b, dt_bias, a_log, d_skip, norm_w):
    bsz, s, _ = z.shape
    L, G, N, P = SSD_CHUNK, SSD_GROUPS, SSD_STATE, SSD_HEAD_DIM
    hpg = SSD_HEADS // G
    nc = s // L
    xbc = jax.nn.silu(causal_depthwise_conv(xbc_raw, conv_w, conv_b))
    x = xbc[..., :SSD_INNER].reshape(bsz, nc, L, G, hpg, P)
    bm = xbc[..., SSD_INNER:SSD_INNER + G * N].reshape(bsz, nc, L, G, N)
    cm = xbc[..., SSD_INNER + G * N:].reshape(bsz, nc, L, G, N)
    dt = jax.nn.softplus(dt_raw.astype(jnp.float32) + dt_bias.astype(jnp.float32))
    dt = dt.reshape(bsz, nc, L, G, hpg)
    a = dt * (-jnp.exp(a_log.astype(jnp.float32))).reshape(G, hpg)
    a_cs = jnp.cumsum(a, axis=2)
    xdt = x * dt[..., None].astype(x.dtype)
    causal = jnp.tril(jnp.ones((L, L), dtype=bool))[None, None, :, :, None, None]
    seg = a_cs[:, :, :, None] - a_cs[:, :, None]
    decay = jnp.exp(jnp.where(causal, seg, -jnp.inf)).astype(x.dtype)
    cb = jnp.einsum('bclgn,bcsgn->bclsg', cm, bm)
    y_diag = jnp.einsum('bclsgh,bcsghp->bclghp', cb[..., None] * decay, xdt)
    decay_end = jnp.exp(a_cs[:, :, -1:] - a_cs).astype(x.dtype)
    states = jnp.einsum('bclgn,bclghp->bcghpn', bm, xdt * decay_end[..., None])
    chunk_decay = jnp.exp(a_cs[:, :, -1])

    def step(carry, inp):
        st, dec = inp
        return carry * dec[..., None, None] + st, carry

    init = jnp.zeros((bsz, G, hpg, P, N), jnp.float32)
    _, prev = lax.scan(step, init, (jnp.swapaxes(states.astype(jnp.float32), 0, 1),
                                    jnp.swapaxes(chunk_decay, 0, 1)))
    prev = jnp.swapaxes(prev, 0, 1).astype(x.dtype)
    y_off = jnp.einsum('bclgn,bcghpn->bclghp', cm, prev) * jnp.exp(a_cs)[..., None].astype(x.dtype)
    y = y_diag + y_off + x * d_skip.astype(x.dtype).reshape(G, hpg)[:, :, None]
    y = y.reshape(bsz, s, SSD_INNER)
    return rmsnorm(y * jax.nn.silu(z), norm_w)


def rel_bucket_band():
    W = WINDOW
    dist = np.arange(W)[:, None] - np.arange(2 * W)[None, :] + W
    n = np.maximum(dist, 0)
    max_exact = REL_BUCKETS // 2
    large = max_exact + (np.log(np.maximum(n, 1) / max_exact) / np.log(REL_MAX_DIST / max_exact)
                         * (REL_BUCKETS - max_exact)).astype(np.int32)
    large = np.minimum(large, REL_BUCKETS - 1)
    bucket = np.where(n < max_exact, n, large).astype(np.int32)
    valid = (dist >= 0) & (dist < W)
    return bucket, valid


def swa_attention(q, k, v, sinks, rel_table):
    bsz, s, _ = q.shape
    W, KV, Dh = WINDOW, ATT_KV_HEADS, ATT_HEAD_DIM
    G = ATT_Q_HEADS // KV
    nb = s // W
    q = q.reshape(bsz, nb, W, KV, G, Dh)
    k = k.reshape(bsz, nb, W, KV, Dh)
    v = v.reshape(bsz, nb, W, KV, Dh)
    pad = ((0, 0), (1, 0), (0, 0), (0, 0), (0, 0))
    kk = jnp.concatenate([jnp.pad(k, pad)[:, :-1], k], axis=2)
    vv = jnp.concatenate([jnp.pad(v, pad)[:, :-1], v], axis=2)
    scores = jnp.einsum('bnqhgd,bnkhd->bnhgqk', q, kk).astype(jnp.float32) / math.sqrt(Dh)
    bucket, valid = rel_bucket_band()
    bias = rel_table.astype(jnp.float32)[bucket]
    bias = jnp.transpose(bias, (2, 0, 1)).reshape(KV, G, W, 2 * W)
    first_ok = np.arange(2 * W) >= W
    mask = jnp.asarray(valid)[None] & ((jnp.arange(nb) > 0)[:, None, None] | jnp.asarray(first_ok)[None, None])
    scores = jnp.where(mask[None, :, None, None], scores + bias, -jnp.inf)
    sink = sinks.astype(jnp.float32).reshape(KV, G)[None, None, :, :, None, None]
    m = jnp.maximum(jnp.max(scores, axis=-1, keepdims=True), sink)
    p = jnp.exp(scores - m)
    p = p / (jnp.sum(p, axis=-1, keepdims=True) + jnp.exp(sink - m))
    out = jnp.einsum('bnhgqk,bnkhd->bnqhgd', p.astype(vv.dtype), vv)
    return out.reshape(bsz, s, ATT_WIDTH)


def chunk_sgu(u, gv, ln_w, ln_b, w_s, b_s):
    bsz, s, _ = u.shape
    C = GM_CHUNK
    nc = s // C
    u = jax.nn.gelu(u)
    gv = layernorm(jax.nn.gelu(gv), ln_w, ln_b).reshape(bsz, nc, C, GM_GROUPS, GM_GROUP_DIM)
    w = w_s * jnp.tril(jnp.ones((C, C), dtype=w_s.dtype))[None]
    mixed = jnp.einsum('gts,bcsgd->bctgd', w, gv) + jnp.transpose(b_s)[None, None, :, :, None]
    return u * mixed.reshape(bsz, s, GM_WIDTH)


def setup_inputs(seed: int = 0) -> dict:
    key = jax.random.key(seed)
    ks = jax.random.split(key, 32)

    def nrm(k, shape, scale):
        return jax.random.normal(k, shape, jnp.float32) * scale

    def gain(k, shape):
        return 1.0 + 0.02 * jax.random.normal(k, shape, jnp.float32)

    dt = jnp.exp(jax.random.uniform(ks[10], (DEPTH, SSD_HEADS), jnp.float32)
                 * (math.log(0.1) - math.log(0.001)) + math.log(0.001))
    return {
        "x": jax.random.normal(ks[0], (BATCH, SEQ, D_MODEL), jnp.float32),
        "ffn1_norm": gain(ks[1], (DEPTH, D_MODEL)),
        "ffn1_w_gate": nrm(ks[2], (DEPTH, D_MODEL, D_FF), D_MODEL ** -0.5),
        "ffn1_w_up": nrm(ks[3], (DEPTH, D_MODEL, D_FF), D_MODEL ** -0.5),
        "ffn1_w_down": nrm(ks[4], (DEPTH, D_FF, D_MODEL), D_FF ** -0.5),
        "mix_norm": gain(ks[5], (DEPTH, D_MODEL)),
        "w_in": nrm(ks[6], (DEPTH, D_MODEL, D_IN_PROJ), D_MODEL ** -0.5),
        "conv_w": nrm(ks[7], (DEPTH, SSD_CONV, SSD_XBC), SSD_CONV ** -0.5),
        "conv_b": nrm(ks[8], (DEPTH, SSD_XBC), 0.02),
        "dt_bias": dt + jnp.log(-jnp.expm1(-dt)),
        "a_log": jnp.log(jax.random.uniform(ks[11], (DEPTH, SSD_HEADS), jnp.float32, 1.0, 16.0)),
        "d_skip": gain(ks[12], (DEPTH, SSD_HEADS)),
        "ssd_norm": gain(ks[13], (DEPTH, SSD_INNER)),
        "attn_sinks": nrm(ks[14], (DEPTH, ATT_Q_HEADS), 0.5),
        "rel_bias": nrm(ks[15], (REL_BUCKETS, ATT_Q_HEADS), 0.5),
        "attn_out_norm": gain(ks[16], (DEPTH, ATT_WIDTH)),
        "sgu_ln_w": gain(ks[17], (DEPTH, GM_WIDTH)),
        "sgu_ln_b": nrm(ks[18], (DEPTH, GM_WIDTH), 0.02),
        "sgu_w": nrm(ks[19], (DEPTH, GM_GROUPS, GM_CHUNK, GM_CHUNK), GM_CHUNK ** -0.5),
        "sgu_b": 1.0 + nrm(ks[20], (DEPTH, GM_GROUPS, GM_CHUNK), 0.1),
        "sgu_out_norm": gain(ks[21], (DEPTH, GM_WIDTH)),
        "w_out": nrm(ks[22], (DEPTH, D_MIX, D_MODEL), D_MIX ** -0.5),
        "ffn2_norm": gain(ks[23], (DEPTH, D_MODEL)),
        "ffn2_w_gate": nrm(ks[24], (DEPTH, D_MODEL, D_FF), D_MODEL ** -0.5),
        "ffn2_w_up": nrm(ks[25], (DEPTH, D_MODEL, D_FF), D_MODEL ** -0.5),
        "ffn2_w_down": nrm(ks[26], (DEPTH, D_FF, D_MODEL), D_FF ** -0.5),
        "final_norm": gain(ks[27], (D_MODEL,)),
    }


def reference(x, ffn1_norm, ffn1_w_gate, ffn1_w_up, ffn1_w_down, mix_norm, w_in,
              conv_w, conv_b, dt_bias, a_log, d_skip, ssd_norm, attn_sinks, rel_bias,
              attn_out_norm, sgu_ln_w, sgu_ln_b, sgu_w, sgu_b, sgu_out_norm, w_out,
              ffn2_norm, ffn2_w_gate, ffn2_w_up, ffn2_w_down, final_norm):
    for l in range(DEPTH):
        x = x + 0.5 * swiglu(rmsnorm(x, ffn1_norm[l]), ffn1_w_gate[l], ffn1_w_up[l], ffn1_w_down[l])
        h = rmsnorm(x, mix_norm[l])
        proj = h @ w_in[l]
        y_ssd = ssd_mixer(proj[..., OFF_Z:OFF_XBC], proj[..., OFF_XBC:OFF_DT],
                          proj[..., OFF_DT:OFF_Q], conv_w[l], conv_b[l], dt_bias[l],
                          a_log[l], d_skip[l], ssd_norm[l])
        y_att = rmsnorm(swa_attention(proj[..., OFF_Q:OFF_K], proj[..., OFF_K:OFF_V],
                                      proj[..., OFF_V:OFF_U], attn_sinks[l], rel_bias),
                        attn_out_norm[l])
        y_sgu = rmsnorm(chunk_sgu(proj[..., OFF_U:OFF_GV], proj[..., OFF_GV:],
                                  sgu_ln_w[l], sgu_ln_b[l], sgu_w[l], sgu_b[l]),
                        sgu_out_norm[l])
        x = x + jnp.concatenate([y_ssd, y_att, y_sgu], axis=-1) @ w_out[l]
        x = x + 0.5 * swiglu(rmsnorm(x, ffn2_norm[l]), ffn2_w_gate[l], ffn2_w_up[l], ffn2_w_down[l])
    return rmsnorm(x, final_norm)
```

```python
import functools
import math

import jax
import jax.numpy as jnp
import numpy as np
from jax import lax
from jax.experimental import pallas as pl
from jax.experimental.pallas import tpu as pltpu

F32 = jnp.float32
BF16 = jnp.bfloat16

D_MODEL = 1024
D_FF = 2816
EPS = 1e-6

SSD_HEADS = 8
SSD_HEAD_DIM = 64
SSD_INNER = SSD_HEADS * SSD_HEAD_DIM
SSD_GROUPS = 2
SSD_STATE = 128
SSD_CONV = 4
SSD_XBC = SSD_INNER + 2 * SSD_GROUPS * SSD_STATE
HEADS_PER_GROUP = SSD_HEADS // SSD_GROUPS
GROUP_WIDTH = HEADS_PER_GROUP * SSD_HEAD_DIM

ATT_Q_HEADS = 4
ATT_KV_HEADS = 2
ATT_HEAD_DIM = 64
ATT_WIDTH = ATT_Q_HEADS * ATT_HEAD_DIM
ATT_KV_WIDTH = ATT_KV_HEADS * ATT_HEAD_DIM
REL_BUCKETS = 32
REL_MAX_DIST = 128

GM_GROUPS = 4
GM_GROUP_DIM = 64
GM_WIDTH = GM_GROUPS * GM_GROUP_DIM

CHUNK = 128

SUBLANES = 8
LANES = 128
DT_PAD = LANES

P_Z = 0
P_XBC = P_Z + SSD_INNER
P_Q = P_XBC + SSD_XBC
P_K = P_Q + ATT_WIDTH
P_V = P_K + ATT_KV_WIDTH
P_U = P_V + ATT_KV_WIDTH
P_GV = P_U + GM_WIDTH
P_DT = P_GV + GM_WIDTH
P_TOTAL = P_DT + DT_PAD

R_XBC = SSD_INNER
R_DT = R_XBC + SSD_XBC
R_Q = R_DT + SSD_HEADS

MASKED = -1e30

FFN_TOKENS = 512
FFN_CHUNK = 256
VMEM_LIMIT_FFN = 48 * 1024 * 1024
VMEM_LIMIT_MIXER = 40 * 1024 * 1024


def _rms(x, w):
    return x * lax.rsqrt(jnp.mean(x * x, axis=-1, keepdims=True) + EPS) * w


def _sigmoid(x):
    return 1.0 / (1.0 + jnp.exp(-x))


def _gelu_tanh(x):
    c = math.sqrt(2.0 / math.pi)
    return 0.5 * x * (1.0 + jnp.tanh(c * (x + 0.044715 * (x * x * x))))


def _split3(x):
    hi = x.astype(BF16)
    r = x - hi.astype(F32)
    mid = r.astype(BF16)
    lo = (r - mid.astype(F32)).astype(BF16)
    return hi, mid, lo


def _dot(a, b):
    return jnp.dot(a, b, preferred_element_type=F32)


def _dot_nt(a, b):
    return lax.dot_general(a, b, (((1,), (1,)), ((), ())), preferred_element_type=F32)


def _dot_f32_lhs(x, sel):
    hi, mid, lo = _split3(x)
    return _dot(hi, sel) + _dot(mid, sel) + _dot(lo, sel)


def _dot_f32_rhs(sel, x):
    hi, mid, lo = _split3(x)
    return _dot(sel, hi) + _dot(sel, mid) + _dot(sel, lo)


def _ffn_kernel(x_ref, nw_ref, wg_ref, wu_ref, wd_ref, fw_ref, o_ref, h_ref, *, final_norm):
    x = x_ref[...]
    xn = _rms(x, nw_ref[...]).astype(BF16)
    for c in range(D_FF // FFN_CHUNK):
        cols = slice(c * FFN_CHUNK, (c + 1) * FFN_CHUNK)
        g = _dot(xn, wg_ref[:, cols])
        u = _dot(xn, wu_ref[:, cols])
        h_ref[:, cols] = (g * _sigmoid(g) * u).astype(BF16)
    out = x + 0.5 * _dot(h_ref[...], wd_ref[...])
    if final_norm:
        out = _rms(out, fw_ref[...])
    o_ref[...] = out


def _ffn(x2d, norm_w, w_gate, w_up, w_down, final_w, *, final_norm):
    tokens = x2d.shape[0]
    const = lambda i: (0, 0)
    resident = functools.partial(pl.BlockSpec, index_map=const, pipeline_mode=pl.Buffered(1))
    return pl.pallas_call(
        functools.partial(_ffn_kernel, final_norm=final_norm),
        out_shape=jax.ShapeDtypeStruct(x2d.shape, F32),
        grid=(tokens // FFN_TOKENS,),
        in_specs=[
            pl.BlockSpec((FFN_TOKENS, D_MODEL), lambda i: (i, 0)),
            resident((1, D_MODEL)),
            resident((D_MODEL, D_FF)),
            resident((D_MODEL, D_FF)),
            resident((D_FF, D_MODEL)),
            resident((1, D_MODEL)),
        ],
        out_specs=pl.BlockSpec((FFN_TOKENS, D_MODEL), lambda i: (i, 0)),
        scratch_shapes=[pltpu.VMEM((FFN_TOKENS, D_FF), BF16)],
        compiler_params=pltpu.CompilerParams(
            dimension_semantics=("arbitrary",), vmem_limit_bytes=VMEM_LIMIT_FFN),
        name="ffn_halfstep",
    )(x2d, norm_w, w_gate, w_up, w_down, final_w)


def _mixer_kernel(sink_ref, rel_ref,
                  x_ref, nw_ref, win_ref, convw_ref, convb_ref, dtb_ref, alog_ref,
                  dskip_ref, ssdnw_ref, tril_ref, expand_ref, bucket_ref, attnw_ref,
                  lnw_ref, lnb_ref, sguw_ref, sgub_ref, sgunw_ref, wout_ref,
                  o_ref,
                  proj_ref, ext_ref, st_ref, kk_ref, vv_ref, bias_ref, wt_ref):
    b = pl.program_id(0)
    c = pl.program_id(1)
    row = lax.broadcasted_iota(jnp.int32, (CHUNK, CHUNK), 0)
    col = lax.broadcasted_iota(jnp.int32, (CHUNK, CHUNK), 1)
    causal = row >= col

    @pl.when(jnp.logical_and(b == 0, c == 0))
    def _build_tables():
        qi = lax.broadcasted_iota(jnp.int32, (CHUNK, 2 * CHUNK), 0)
        kj = lax.broadcasted_iota(jnp.int32, (CHUNK, 2 * CHUNK), 1)
        dist = qi - kj + CHUNK
        valid = jnp.logical_and(dist >= 0, dist < CHUNK)
        bucket = bucket_ref[...]
        for h in range(ATT_Q_HEADS):
            acc = jnp.zeros((CHUNK, 2 * CHUNK), F32)
            for k in range(REL_BUCKETS):
                acc = jnp.where(bucket == k, rel_ref[k * ATT_Q_HEADS + h], acc)
            bias_ref[h] = jnp.where(valid, acc, MASKED)
        for g in range(GM_GROUPS):
            wt_ref[g] = jnp.where(causal, sguw_ref[g], 0.0).astype(BF16)

    @pl.when(c == 0)
    def _reset_sequence_state():
        st_ref[...] = jnp.zeros_like(st_ref)
        ext_ref[0:SUBLANES, :] = jnp.zeros((SUBLANES, SSD_XBC), F32)
        kk_ref[0:CHUNK, :] = jnp.zeros((CHUNK, ATT_KV_WIDTH), BF16)
        vv_ref[0:CHUNK, :] = jnp.zeros((CHUNK, ATT_KV_WIDTH), BF16)

    x = x_ref[0]
    hn = _rms(x, nw_ref[...]).astype(BF16)
    proj_ref[...] = _dot(hn, win_ref[...])

    xbc_raw = proj_ref[:, P_XBC:P_XBC + SSD_XBC]
    ext_ref[SUBLANES:SUBLANES + CHUNK, :] = xbc_raw
    conv = convb_ref[...]
    for k in range(SSD_CONV):
        start = SUBLANES - (SSD_CONV - 1) + k
        conv = conv + convw_ref[k:k + 1, :] * ext_ref[start:start + CHUNK, :]
    ext_ref[0:SUBLANES, :] = xbc_raw[CHUNK - SUBLANES:, :]
    xbc = conv * _sigmoid(conv)
    xs = xbc[:, :SSD_INNER]
    bm = xbc[:, SSD_INNER:SSD_INNER + SSD_GROUPS * SSD_STATE]
    cm = xbc[:, SSD_INNER + SSD_GROUPS * SSD_STATE:]

    dtr = proj_ref[:, P_DT:P_DT + DT_PAD] + dtb_ref[...]
    dt = jnp.maximum(dtr, 0.0) + jnp.log1p(jnp.exp(-jnp.abs(dtr)))
    a = dt * (-jnp.exp(alog_ref[...]))
    a_cs = _dot_f32_rhs(tril_ref[...], a)
    a_cs_t = a_cs.T
    dt_e = _dot_f32_lhs(dt, expand_ref[...])
    acs_e = _dot_f32_lhs(a_cs, expand_ref[...])
    acs_last = acs_e[CHUNK - 1:CHUNK, :]
    xdt = xs * dt_e
    xdt_b = xdt.astype(BF16)
    xd_b = (xdt * jnp.exp(acs_last - acs_e)).astype(BF16)
    chunk_decay = jnp.exp(acs_last)

    bm_b = bm.astype(BF16)
    cm_b = cm.astype(BF16)
    y_parts = []
    for g in range(SSD_GROUPS):
        n0 = g * SSD_STATE
        cg = cm_b[:, n0:n0 + SSD_STATE]
        bg = bm_b[:, n0:n0 + SSD_STATE]
        cb = _dot_nt(cg, bg)
        for hh in range(HEADS_PER_GROUP):
            h = g * HEADS_PER_GROUP + hh
            seg = a_cs[:, h:h + 1] - a_cs_t[h:h + 1, :]
            decay = jnp.exp(jnp.where(causal, seg, -jnp.inf))
            y_parts.append(_dot((cb * decay).astype(BF16),
                                xdt_b[:, h * SSD_HEAD_DIM:(h + 1) * SSD_HEAD_DIM]))
    y_diag = jnp.concatenate(y_parts, axis=-1)
    y_off_parts = []
    for g in range(SSD_GROUPS):
        n0 = g * SSD_STATE
        w0 = g * GROUP_WIDTH
        prev = st_ref[g]
        y_off_parts.append(_dot(cm_b[:, n0:n0 + SSD_STATE], prev.astype(BF16)))
        bt = bm[:, n0:n0 + SSD_STATE].T.astype(BF16)
        st_ref[g] = prev * chunk_decay[:, w0:w0 + GROUP_WIDTH] + _dot(bt, xd_b[:, w0:w0 + GROUP_WIDTH])
    y_off = jnp.concatenate(y_off_parts, axis=-1) * jnp.exp(acs_e)
    y = y_diag + y_off + xs * dskip_ref[...]
    z = proj_ref[:, P_Z:P_Z + SSD_INNER]
    y_ssd = _rms(y * (z * _sigmoid(z)), ssdnw_ref[...])

    q_b = (proj_ref[:, P_Q:P_Q + ATT_WIDTH] * (1.0 / math.sqrt(ATT_HEAD_DIM))).astype(BF16)
    k_b = proj_ref[:, P_K:P_K + ATT_KV_WIDTH].astype(BF16)
    v_b = proj_ref[:, P_V:P_V + ATT_KV_WIDTH].astype(BF16)
    kk_ref[CHUNK:, :] = k_b
    vv_ref[CHUNK:, :] = v_b
    kcol = lax.broadcasted_iota(jnp.int32, (CHUNK, 2 * CHUNK), 1)
    has_prev = jnp.logical_or(c > 0, kcol >= CHUNK)
    att_parts = []
    for hq in range(ATT_Q_HEADS):
        kvh = hq // (ATT_Q_HEADS // ATT_KV_HEADS)
        d0 = kvh * ATT_HEAD_DIM
        s = _dot_nt(q_b[:, hq * ATT_HEAD_DIM:(hq + 1) * ATT_HEAD_DIM], kk_ref[:, d0:d0 + ATT_HEAD_DIM])
        s = jnp.where(has_prev, s + bias_ref[hq], MASKED)
        sink = sink_ref[hq]
        m = jnp.maximum(jnp.max(s, axis=-1, keepdims=True), sink)
        p = jnp.exp(s - m)
        denom = jnp.sum(p, axis=-1, keepdims=True) + jnp.exp(sink - m)
        pv = _dot(p.astype(BF16), vv_ref[:, d0:d0 + ATT_HEAD_DIM])
        att_parts.append(pv / denom)
    kk_ref[0:CHUNK, :] = k_b
    vv_ref[0:CHUNK, :] = v_b
    y_att = _rms(jnp.concatenate(att_parts, axis=-1), attnw_ref[...])

    u = _gelu_tanh(proj_ref[:, P_U:P_U + GM_WIDTH])
    gv = _gelu_tanh(proj_ref[:, P_GV:P_GV + GM_WIDTH])
    mu = jnp.mean(gv, axis=-1, keepdims=True)
    gc = gv - mu
    gv = gc * lax.rsqrt(jnp.mean(gc * gc, axis=-1, keepdims=True) + EPS) * lnw_ref[...] + lnb_ref[...]
    gv_b = gv.astype(BF16)
    mixed = jnp.concatenate(
        [_dot(wt_ref[g], gv_b[:, g * GM_GROUP_DIM:(g + 1) * GM_GROUP_DIM]) for g in range(GM_GROUPS)],
        axis=-1) + sgub_ref[...]
    y_sgu = _rms(u * mixed, sgunw_ref[...])

    ycat = jnp.concatenate([y_ssd, y_att, y_sgu], axis=-1).astype(BF16)
    o_ref[0] = x + _dot(ycat, wout_ref[...])


def _rel_bucket_band():
    dist = np.arange(CHUNK)[:, None] - np.arange(2 * CHUNK)[None, :] + CHUNK
    n = np.maximum(dist, 0)
    max_exact = REL_BUCKETS // 2
    large = max_exact + (np.log(np.maximum(n, 1) / max_exact) / np.log(REL_MAX_DIST / max_exact)
                         * (REL_BUCKETS - max_exact)).astype(np.int32)
    large = np.minimum(large, REL_BUCKETS - 1)
    return np.where(n < max_exact, n, large).astype(np.int32)


def _mixer(x, mix_norm, w_in, conv_w, conv_b, dt_bias, a_log, d_skip, ssd_norm, sinks, rel_bias,
           attn_out_norm, ln_w, ln_b, sgu_w, sgu_b, sgu_out_norm, w_out):
    batch, seq, _ = x.shape
    w_in_p = jnp.concatenate([
        w_in[:, :R_DT], w_in[:, R_Q:],
        w_in[:, R_DT:R_Q], jnp.zeros((D_MODEL, DT_PAD - SSD_HEADS), w_in.dtype)], axis=1).astype(BF16)
    pad_heads = lambda v: jnp.pad(v, (0, DT_PAD - SSD_HEADS)).reshape(1, DT_PAD)
    row = lambda v: v.reshape(1, -1)
    tril = jnp.asarray(np.tril(np.ones((CHUNK, CHUNK), np.float32)), BF16)
    expand = np.zeros((DT_PAD, SSD_INNER), np.float32)
    for h in range(SSD_HEADS):
        expand[h, h * SSD_HEAD_DIM:(h + 1) * SSD_HEAD_DIM] = 1.0
    expand = jnp.asarray(expand, BF16)
    bucket = jnp.asarray(_rel_bucket_band())
    sgu_b_e = jnp.repeat(jnp.transpose(sgu_b), GM_GROUP_DIM, axis=1)

    const2 = lambda b, c, *_: (0, 0)
    const3 = lambda b, c, *_: (0, 0, 0)
    full2 = lambda shape: pl.BlockSpec(shape, const2)
    grid_spec = pltpu.PrefetchScalarGridSpec(
        num_scalar_prefetch=2,
        grid=(batch, seq // CHUNK),
        in_specs=[
            pl.BlockSpec((1, CHUNK, D_MODEL), lambda b, c, *_: (b, c, 0)),
            full2((1, D_MODEL)),
            full2((D_MODEL, P_TOTAL)),
            full2((SSD_CONV, SSD_XBC)),
            full2((1, SSD_XBC)),
            full2((1, DT_PAD)),
            full2((1, DT_PAD)),
            full2((1, SSD_INNER)),
            full2((1, SSD_INNER)),
            full2((CHUNK, CHUNK)),
            full2((DT_PAD, SSD_INNER)),
            full2((CHUNK, 2 * CHUNK)),
            full2((1, ATT_WIDTH)),
            full2((1, GM_WIDTH)),
            full2((1, GM_WIDTH)),
            pl.BlockSpec((GM_GROUPS, CHUNK, CHUNK), const3),
            full2((CHUNK, GM_WIDTH)),
            full2((1, GM_WIDTH)),
            full2((D_MODEL, D_MODEL)),
        ],
        out_specs=pl.BlockSpec((1, CHUNK, D_MODEL), lambda b, c, *_: (b, c, 0)),
        scratch_shapes=[
            pltpu.VMEM((CHUNK, P_TOTAL), F32),
            pltpu.VMEM((SUBLANES + CHUNK, SSD_XBC), F32),
            pltpu.VMEM((SSD_GROUPS, SSD_STATE, GROUP_WIDTH), F32),
            pltpu.VMEM((2 * CHUNK, ATT_KV_WIDTH), BF16),
            pltpu.VMEM((2 * CHUNK, ATT_KV_WIDTH), BF16),
            pltpu.VMEM((ATT_Q_HEADS, CHUNK, 2 * CHUNK), F32),
            pltpu.VMEM((GM_GROUPS, CHUNK, CHUNK), BF16),
        ],
    )
    return pl.pallas_call(
        _mixer_kernel,
        out_shape=jax.ShapeDtypeStruct(x.shape, F32),
        grid_spec=grid_spec,
        compiler_params=pltpu.CompilerParams(
            dimension_semantics=("arbitrary", "arbitrary"), vmem_limit_bytes=VMEM_LIMIT_MIXER),
        name="token_mixer",
    )(sinks.astype(F32), rel_bias.reshape(-1).astype(F32),
      x, row(mix_norm), w_in_p, conv_w, row(conv_b), pad_heads(dt_bias), pad_heads(a_log),
      row(jnp.repeat(d_skip, SSD_HEAD_DIM)), row(ssd_norm), tril, expand, bucket,
      row(attn_out_norm), row(ln_w), row(ln_b), sgu_w, sgu_b_e, row(sgu_out_norm),
      w_out.astype(BF16))


def kernel(x, ffn1_norm, ffn1_w_gate, ffn1_w_up, ffn1_w_down, mix_norm, w_in, conv_w, conv_b, dt_bias, a_log, d_skip, ssd_norm, attn_sinks, rel_bias, attn_out_norm, sgu_ln_w, sgu_ln_b, sgu_w, sgu_b, sgu_out_norm, w_out, ffn2_norm, ffn2_w_gate, ffn2_w_up, ffn2_w_down, final_norm):
    batch, seq, d = x.shape
    depth = w_in.shape[0]
    fw = final_norm.reshape(1, d)
    for l in range(depth):
        x2 = _ffn(x.reshape(batch * seq, d), ffn1_norm[l].reshape(1, d), ffn1_w_gate[l].astype(BF16),
                  ffn1_w_up[l].astype(BF16), ffn1_w_down[l].astype(BF16), fw, final_norm=False)
        x = _mixer(x2.reshape(batch, seq, d), mix_norm[l], w_in[l], conv_w[l], conv_b[l], dt_bias[l],
                   a_log[l], d_skip[l], ssd_norm[l], attn_sinks[l], rel_bias, attn_out_norm[l],
                   sgu_ln_w[l], sgu_ln_b[l], sgu_w[l], sgu_b[l], sgu_out_norm[l], w_out[l])
        x2 = _ffn(x.reshape(batch * seq, d), ffn2_norm[l].reshape(1, d), ffn2_w_gate[l].astype(BF16),
                  ffn2_w_up[l].astype(BF16), ffn2_w_down[l].astype(BF16), fw,
                  final_norm=(l == depth - 1))
        x = x2.reshape(batch, seq, d)
    return x
```

```python
import functools
import math

import jax
import jax.numpy as jnp
import numpy as np
from jax import lax
from jax.experimental import pallas as pl
from jax.experimental.pallas import tpu as pltpu

F32 = jnp.float32
BF16 = jnp.bfloat16

D_MODEL = 1024
D_FF = 2816
EPS = 1e-6

SSD_HEADS = 8
SSD_HEAD_DIM = 64
SSD_INNER = SSD_HEADS * SSD_HEAD_DIM
SSD_GROUPS = 2
SSD_STATE = 128
SSD_CONV = 4
SSD_XBC = SSD_INNER + 2 * SSD_GROUPS * SSD_STATE
HEADS_PER_GROUP = SSD_HEADS // SSD_GROUPS
GROUP_WIDTH = HEADS_PER_GROUP * SSD_HEAD_DIM

ATT_Q_HEADS = 4
ATT_KV_HEADS = 2
ATT_HEAD_DIM = 64
ATT_WIDTH = ATT_Q_HEADS * ATT_HEAD_DIM
ATT_KV_WIDTH = ATT_KV_HEADS * ATT_HEAD_DIM
REL_BUCKETS = 32
REL_MAX_DIST = 128
ATT_POS_HEADS = (0, 2, 1, 3)

GM_GROUPS = 4
GM_GROUP_DIM = 64
GM_WIDTH = GM_GROUPS * GM_GROUP_DIM

CHUNK = 128
PAIR = 2 * CHUNK

SUBLANES = 8
LANES = 128
HALF_LANES = LANES // 2
DT_PAD = LANES
DT_COPIES = 3

P_Z = 0
P_XBC = P_Z + SSD_INNER
P_Q = P_XBC + SSD_XBC
P_K = P_Q + ATT_WIDTH
P_V = P_K + ATT_KV_WIDTH
P_U = P_V + ATT_KV_WIDTH
P_GV = P_U + GM_WIDTH
P_DT = P_GV + GM_WIDTH
P_TOTAL = P_DT + DT_PAD

R_XBC = SSD_INNER
R_DT = R_XBC + SSD_XBC
R_Q = R_DT + SSD_HEADS
R_K = R_Q + ATT_WIDTH

Y_ATT = SSD_INNER
Y_SGU = Y_ATT + ATT_WIDTH

MASKED = -1e30
LOG2E = math.log2(math.e)

FFN_TOKENS = 512
FFN_CHUNK = 256
VMEM_LIMIT_FFN = 48 * 1024 * 1024
VMEM_LIMIT_MIXER = 48 * 1024 * 1024


def _unit_rms(x):
    return x * lax.rsqrt(jnp.mean(x * x, axis=-1, keepdims=True) + EPS)


def _sigmoid(x):
    return 1.0 / (1.0 + jnp.exp(-x))


def _gelu_tanh(x):
    c = math.sqrt(2.0 / math.pi)
    return 0.5 * x * (1.0 + jnp.tanh(c * (x + 0.044715 * (x * x * x))))


def _split3(x):
    hi = x.astype(BF16)
    r = x - hi.astype(F32)
    mid = r.astype(BF16)
    lo = (r - mid.astype(F32)).astype(BF16)
    return hi, mid, lo


def _top_bits(x):
    bits = lax.bitcast_convert_type(x, jnp.uint32) & jnp.uint32(0xFFFF0000)
    return lax.bitcast_convert_type(bits, F32)


def _stacked_split(x, lane):
    hi = _top_bits(x)
    r = x - hi
    mid = _top_bits(r)
    lo = r - mid
    return jnp.where(lane < SSD_HEADS, hi, jnp.where(lane < 2 * SSD_HEADS, mid, lo)).astype(BF16)


def _dot(a, b):
    return jnp.dot(a, b, preferred_element_type=F32)


def _dot_nt(a, b):
    return lax.dot_general(a, b, (((1,), (1,)), ((), ())), preferred_element_type=F32)


def _dot_f32_rhs(sel, x):
    hi, mid, lo = _split3(x)
    return _dot(sel, hi) + _dot(sel, mid) + _dot(sel, lo)


def _ffn_kernel(x_ref, nw_ref, wg_ref, wu_ref, wd_ref, fw_ref, o_ref, h_ref, *, final_norm):
    x = x_ref[...]
    xn = (_unit_rms(x) * nw_ref[...]).astype(BF16)
    for c in range(D_FF // FFN_CHUNK):
        cols = slice(c * FFN_CHUNK, (c + 1) * FFN_CHUNK)
        g = _dot(xn, wg_ref[:, cols])
        u = _dot(xn, wu_ref[:, cols])
        h_ref[:, cols] = (g * _sigmoid(g) * u).astype(BF16)
    out = x + 0.5 * _dot(h_ref[...], wd_ref[...])
    if final_norm:
        out = _unit_rms(out) * fw_ref[...]
    o_ref[...] = out


def _ffn(x2d, norm_w, w_gate, w_up, w_down, final_w, *, final_norm):
    tokens = x2d.shape[0]
    const = lambda i: (0, 0)
    resident = functools.partial(pl.BlockSpec, index_map=const, pipeline_mode=pl.Buffered(1))
    return pl.pallas_call(
        functools.partial(_ffn_kernel, final_norm=final_norm),
        out_shape=jax.ShapeDtypeStruct(x2d.shape, F32),
        grid=(tokens // FFN_TOKENS,),
        in_specs=[
            pl.BlockSpec((FFN_TOKENS, D_MODEL), lambda i: (i, 0)),
            resident((1, D_MODEL)),
            resident((D_MODEL, D_FF)),
            resident((D_MODEL, D_FF)),
            resident((D_FF, D_MODEL)),
            resident((1, D_MODEL)),
        ],
        out_specs=pl.BlockSpec((FFN_TOKENS, D_MODEL), lambda i: (i, 0)),
        scratch_shapes=[pltpu.VMEM((FFN_TOKENS, D_FF), BF16)],
        compiler_params=pltpu.CompilerParams(
            dimension_semantics=("arbitrary",), vmem_limit_bytes=VMEM_LIMIT_FFN),
        name="ffn_halfstep",
    )(x2d, norm_w, w_gate, w_up, w_down, final_w)


def _mix_chunk(proj_ref, ycat_ref, keep, bias_idx, p):
    row = lax.broadcasted_iota(jnp.int32, (CHUNK, CHUNK), 0)
    lane = lax.broadcasted_iota(jnp.int32, (CHUNK, CHUNK), 1)
    causal = row >= lane
    row_b = row.astype(F32).astype(BF16)
    lane_b = lane.astype(F32).astype(BF16)
    causal_b = row_b >= lane_b
    upper_half_b = lane_b >= HALF_LANES
    zero_b = jnp.zeros((CHUNK, CHUNK), BF16)
    group_b = (lax.broadcasted_iota(jnp.int32, (CHUNK, GROUP_WIDTH), 1) // SSD_HEAD_DIM).astype(F32).astype(BF16)

    def block_diag(x_b):
        zero = jnp.zeros_like(x_b)
        return jnp.concatenate([jnp.where(group_b == i, x_b, zero) for i in range(HEADS_PER_GROUP)], axis=0)

    xbc_raw = proj_ref[:, P_XBC:P_XBC + SSD_XBC]
    if keep is not None:
        p.ext[0:SUBLANES, :] = p.ext[0:SUBLANES, :] * keep
    p.ext[SUBLANES:, :] = xbc_raw
    ext = p.ext[...]
    conv = p.convb[...] + p.convw[SSD_CONV - 1:SSD_CONV, :] * xbc_raw
    for shift in range(1, SSD_CONV):
        tap = SSD_CONV - 1 - shift
        conv = conv + p.convw[tap:tap + 1, :] * pltpu.roll(ext, shift, 0)[SUBLANES:, :]
    p.ext[0:SUBLANES, :] = xbc_raw[CHUNK - SUBLANES:, :]
    xbc = conv * _sigmoid(conv)
    xs = xbc[:, :SSD_INNER]
    bm = xbc[:, SSD_INNER:SSD_INNER + SSD_GROUPS * SSD_STATE]
    cm = xbc[:, SSD_INNER + SSD_GROUPS * SSD_STATE:]
    x_b = xs.astype(BF16)
    bm_b = bm.astype(BF16)
    cm_b = cm.astype(BF16)

    dtr = proj_ref[:, P_DT:P_DT + DT_PAD] + p.dtb[...]
    dt = jnp.maximum(dtr, 0.0) + jnp.log1p(jnp.exp(-jnp.abs(dtr)))
    a = dt * (-jnp.exp(p.alog[...]))
    a_cs = _dot_f32_rhs(p.tril[...], a)
    a_cs_t = a_cs.T
    dt_t = dt.T
    grow = jnp.exp(a_cs)
    to_end = dt * jnp.exp(a_cs[CHUNK - 1:CHUNK, :] - a_cs)
    grow_e = _dot(_stacked_split(grow, lane), p.expand[...])
    to_end_e = _dot(_stacked_split(to_end, lane), p.expand[...])
    xd_b = (xs * to_end_e).astype(BF16)

    y_diag = []
    for g in range(SSD_GROUPS):
        n0 = g * SSD_STATE
        cb = _dot_nt(cm_b[:, n0:n0 + SSD_STATE], bm_b[:, n0:n0 + SSD_STATE])
        parts = []
        for hh in range(HEADS_PER_GROUP):
            h = g * HEADS_PER_GROUP + hh
            seg = a_cs[:, h:h + 1] - a_cs_t[h:h + 1, :]
            decay = jnp.exp(jnp.where(causal, seg, -jnp.inf))
            parts.append((cb * decay * dt_t[h:h + 1, :]).astype(BF16))
        w0 = g * GROUP_WIDTH
        y_diag.append(_dot(jnp.concatenate(parts, axis=-1), block_diag(x_b[:, w0:w0 + GROUP_WIDTH])))
    y_off = []
    for g in range(SSD_GROUPS):
        n0 = g * SSD_STATE
        w0 = g * GROUP_WIDTH
        prev = p.st[g]
        if keep is not None:
            prev = prev * keep
        y_off.append(_dot(cm_b[:, n0:n0 + SSD_STATE], prev.astype(BF16)))
        bt = bm[:, n0:n0 + SSD_STATE].T.astype(BF16)
        p.st[g] = (prev * grow_e[CHUNK - 1:CHUNK, w0:w0 + GROUP_WIDTH]
                   + _dot(bt, xd_b[:, w0:w0 + GROUP_WIDTH]))
    y = (jnp.concatenate(y_diag, axis=-1) + jnp.concatenate(y_off, axis=-1) * grow_e
         + xs * p.dskip[...])
    z = proj_ref[:, P_Z:P_Z + SSD_INNER]
    ycat_ref[:, 0:Y_ATT] = _unit_rms(y * (z * _sigmoid(z))).astype(BF16)

    q_b = (proj_ref[:, P_Q:P_Q + ATT_WIDTH] * (LOG2E / math.sqrt(ATT_HEAD_DIM))).astype(BF16)
    k_b = proj_ref[:, P_K:P_K + ATT_KV_WIDTH].astype(BF16)
    v_b = proj_ref[:, P_V:P_V + ATT_KV_WIDTH].astype(BF16)
    keys = jnp.concatenate([p.kprev[...], k_b], axis=0)
    v_prev = p.vprev[...]
    halves = lambda t: (jnp.where(upper_half_b, zero_b, t), jnp.where(upper_half_b, t, zero_b))
    v_prev_half, v_half = halves(v_prev), halves(v_b)
    probs, values = [], []
    for pos, hq in enumerate(ATT_POS_HEADS):
        tile, half = divmod(pos, 2)
        q_tile = q_b[:, tile * LANES:(tile + 1) * LANES]
        q_head = halves(q_tile)[half]
        s = _dot_nt(q_head, keys)
        s = jnp.where(causal, s[:, CHUNK:], s[:, :CHUNK]) + p.bias[bias_idx, pos]
        sink = p.sink[hq] * LOG2E
        m = jnp.maximum(jnp.max(s, axis=-1, keepdims=True), sink)
        e = jnp.exp2(s - m)
        denom = jnp.sum(e, axis=-1, keepdims=True) + jnp.exp2(sink - m)
        pn = (e * (1.0 / denom)).astype(BF16)
        probs.append(jnp.where(causal_b, zero_b, pn))
        probs.append(jnp.where(causal_b, pn, zero_b))
        for vh in (v_prev_half[half], v_half[half]):
            values.append(jnp.concatenate([zero_b, vh] if tile else [vh, zero_b], axis=-1))
    att = _dot(jnp.concatenate(probs, axis=-1), jnp.concatenate(values, axis=0))
    p.kprev[...] = k_b
    p.vprev[...] = v_b
    ycat_ref[:, Y_ATT:Y_SGU] = _unit_rms(att).astype(BF16)

    u = _gelu_tanh(proj_ref[:, P_U:P_U + GM_WIDTH])
    gv = _gelu_tanh(proj_ref[:, P_GV:P_GV + GM_WIDTH])
    mu = jnp.mean(gv, axis=-1, keepdims=True)
    gc = gv - mu
    gv = gc * lax.rsqrt(jnp.mean(gc * gc, axis=-1, keepdims=True) + EPS) * p.lnw[...] + p.lnb[...]
    gv_b = gv.astype(BF16)
    mixed = _dot(p.wcat[...], block_diag(gv_b)) + p.sgub[...]
    ycat_ref[:, Y_SGU:] = _unit_rms(u * mixed).astype(BF16)


class _MixerRefs:
    def __init__(self, **refs):
        self.__dict__.update(refs)


def _mixer_kernel(sink_ref, rel_ref,
                  xa_ref, xc_ref, win_ref, convw_ref, convb_ref, dtb_ref, alog_ref,
                  dskip_ref, tril_ref, expand_ref, bucket_ref,
                  lnw_ref, lnb_ref, sguw_ref, sgub_ref, wout_ref,
                  o_ref,
                  proj_ref, projm_ref, ycat_ref, ycato_ref,
                  ext_ref, st_ref, kprev_ref, vprev_ref, bias_ref, wcat_ref,
                  *, chunks_per_seq):
    t = pl.program_id(0)
    p = _MixerRefs(sink=sink_ref, convw=convw_ref, convb=convb_ref, dtb=dtb_ref, alog=alog_ref,
                   dskip=dskip_ref, tril=tril_ref, expand=expand_ref, lnw=lnw_ref, lnb=lnb_ref,
                   sgub=sgub_ref, ext=ext_ref, st=st_ref, kprev=kprev_ref, vprev=vprev_ref,
                   bias=bias_ref, wcat=wcat_ref)

    @pl.when(t == 0)
    def _init():
        row = lax.broadcasted_iota(jnp.int32, (CHUNK, CHUNK), 0)
        col = lax.broadcasted_iota(jnp.int32, (CHUNK, CHUNK), 1)
        bucket = bucket_ref[...]
        for pos, hq in enumerate(ATT_POS_HEADS):
            acc = jnp.zeros((CHUNK, CHUNK), F32)
            for k in range(REL_BUCKETS):
                acc = jnp.where(bucket == k, rel_ref[k * ATT_Q_HEADS + hq] * LOG2E, acc)
            bias_ref[0, pos] = acc
            bias_ref[1, pos] = jnp.where(row >= col, acc, MASKED)
        for g in range(GM_GROUPS):
            wcat_ref[:, g * CHUNK:(g + 1) * CHUNK] = jnp.where(row >= col, sguw_ref[g], 0.0).astype(BF16)
        proj_ref[...] = jnp.zeros_like(proj_ref)
        ycat_ref[...] = jnp.zeros_like(ycat_ref)
        st_ref[...] = jnp.zeros_like(st_ref)
        ext_ref[0:SUBLANES, :] = jnp.zeros((SUBLANES, SSD_XBC), F32)
        kprev_ref[...] = jnp.zeros_like(kprev_ref)
        vprev_ref[...] = jnp.zeros_like(vprev_ref)

    starts_seq = lax.rem(2 * (t - 1), chunks_per_seq) == 0
    keep = jnp.where(starts_seq, 0.0, 1.0).astype(F32)
    first_idx = jnp.where(starts_seq, 1, 0).astype(jnp.int32)

    projm_ref[...] = proj_ref[...]
    ycato_ref[...] = ycat_ref[...]
    proj_ref[...] = _dot(_unit_rms(xa_ref[...]).astype(BF16), win_ref[...])
    even, odd = pl.ds(0, CHUNK), pl.ds(CHUNK, CHUNK)
    _mix_chunk(projm_ref.at[even], ycat_ref.at[even], keep, first_idx, p)
    _mix_chunk(projm_ref.at[odd], ycat_ref.at[odd], None, 0, p)
    o_ref[...] = xc_ref[...] + _dot(ycato_ref[...], wout_ref[...])


def _folded_bucket_tile():
    i = np.arange(CHUNK)[:, None]
    j = np.arange(CHUNK)[None, :]
    n = np.where(j <= i, i - j, i - j + CHUNK)
    max_exact = REL_BUCKETS // 2
    large = max_exact + (np.log(np.maximum(n, 1) / max_exact) / np.log(REL_MAX_DIST / max_exact)
                         * (REL_BUCKETS - max_exact)).astype(np.int32)
    large = np.minimum(large, REL_BUCKETS - 1)
    return np.where(n < max_exact, n, large).astype(np.int32)


def _mixer(x2d, seq, mix_norm, w_in, conv_w, conv_b, dt_bias, a_log, d_skip, ssd_norm, sinks, rel_bias,
           attn_out_norm, ln_w, ln_b, sgu_w, sgu_b, sgu_out_norm, w_out):
    tokens = x2d.shape[0]
    n_pairs = tokens // PAIR
    chunks_per_seq = seq // CHUNK
    assert tokens % PAIR == 0 and seq % PAIR == 0
    att_cols = np.concatenate([np.arange(h * ATT_HEAD_DIM, (h + 1) * ATT_HEAD_DIM) for h in ATT_POS_HEADS])
    w_dt = w_in[:, R_DT:R_Q]
    w_in_p = jnp.concatenate(
        [w_in[:, :R_DT], w_in[:, R_Q + att_cols], w_in[:, R_K:]] + [w_dt] * DT_COPIES
        + [jnp.zeros((D_MODEL, DT_PAD - DT_COPIES * SSD_HEADS), w_in.dtype)], axis=1)
    w_in_p = (mix_norm[:, None] * w_in_p).astype(BF16)
    gains = jnp.concatenate([ssd_norm, attn_out_norm[att_cols], sgu_out_norm])
    w_out_p = jnp.concatenate([w_out[:Y_ATT], w_out[Y_ATT + att_cols], w_out[Y_SGU:]], axis=0)
    w_out_p = (gains[:, None] * w_out_p).astype(BF16)
    rep_heads = lambda v: jnp.pad(jnp.tile(v, DT_COPIES), (0, DT_PAD - DT_COPIES * SSD_HEADS)).reshape(1, DT_PAD)
    row = lambda v: v.reshape(1, -1)
    tril = jnp.asarray(np.tril(np.ones((CHUNK, CHUNK), np.float32)), BF16)
    expand = np.zeros((DT_PAD, SSD_INNER), np.float32)
    for r in range(DT_COPIES * SSD_HEADS):
        h = r % SSD_HEADS
        expand[r, h * SSD_HEAD_DIM:(h + 1) * SSD_HEAD_DIM] = 1.0
    expand = jnp.asarray(expand, BF16)
    bucket = jnp.asarray(_folded_bucket_tile())
    sgu_b_e = jnp.repeat(jnp.transpose(sgu_b), GM_GROUP_DIM, axis=1)

    const2 = lambda t, *_: (0, 0)
    const3 = lambda t, *_: (0, 0, 0)
    full2 = lambda shape: pl.BlockSpec(shape, const2)
    projected = lambda t, *_: (jnp.minimum(t, n_pairs - 1), 0)
    finished = lambda t, *_: (jnp.clip(t - 2, 0, n_pairs - 1), 0)
    grid_spec = pltpu.PrefetchScalarGridSpec(
        num_scalar_prefetch=2,
        grid=(n_pairs + 2,),
        in_specs=[
            pl.BlockSpec((PAIR, D_MODEL), projected),
            pl.BlockSpec((PAIR, D_MODEL), finished),
            full2((D_MODEL, P_TOTAL)),
            full2((SSD_CONV, SSD_XBC)),
            full2((1, SSD_XBC)),
            full2((1, DT_PAD)),
            full2((1, DT_PAD)),
            full2((1, SSD_INNER)),
            full2((CHUNK, CHUNK)),
            full2((DT_PAD, SSD_INNER)),
            full2((CHUNK, CHUNK)),
            full2((1, GM_WIDTH)),
            full2((1, GM_WIDTH)),
            pl.BlockSpec((GM_GROUPS, CHUNK, CHUNK), const3),
            full2((CHUNK, GM_WIDTH)),
            full2((D_MODEL, D_MODEL)),
        ],
        out_specs=pl.BlockSpec((PAIR, D_MODEL), finished),
        scratch_shapes=[
            pltpu.VMEM((PAIR, P_TOTAL), F32),
            pltpu.VMEM((PAIR, P_TOTAL), F32),
            pltpu.VMEM((PAIR, D_MODEL), BF16),
            pltpu.VMEM((PAIR, D_MODEL), BF16),
            pltpu.VMEM((SUBLANES + CHUNK, SSD_XBC), F32),
            pltpu.VMEM((SSD_GROUPS, SSD_STATE, GROUP_WIDTH), F32),
            pltpu.VMEM((CHUNK, ATT_KV_WIDTH), BF16),
            pltpu.VMEM((CHUNK, ATT_KV_WIDTH), BF16),
            pltpu.VMEM((2, ATT_Q_HEADS, CHUNK, CHUNK), F32),
            pltpu.VMEM((CHUNK, GM_GROUPS * CHUNK), BF16),
        ],
    )
    return pl.pallas_call(
        functools.partial(_mixer_kernel, chunks_per_seq=chunks_per_seq),
        out_shape=jax.ShapeDtypeStruct(x2d.shape, F32),
        grid_spec=grid_spec,
        compiler_params=pltpu.CompilerParams(
            dimension_semantics=("arbitrary",), vmem_limit_bytes=VMEM_LIMIT_MIXER),
        name="token_mixer",
    )(sinks.astype(F32), rel_bias.reshape(-1).astype(F32),
      x2d, x2d, w_in_p, conv_w, row(conv_b), rep_heads(dt_bias), rep_heads(a_log),
      row(jnp.repeat(d_skip, SSD_HEAD_DIM)), tril, expand, bucket,
      row(ln_w), row(ln_b), sgu_w, sgu_b_e, w_out_p)


def kernel(x, ffn1_norm, ffn1_w_gate, ffn1_w_up, ffn1_w_down, mix_norm, w_in, conv_w, conv_b, dt_bias, a_log, d_skip, ssd_norm, attn_sinks, rel_bias, attn_out_norm, sgu_ln_w, sgu_ln_b, sgu_w, sgu_b, sgu_out_norm, w_out, ffn2_norm, ffn2_w_gate, ffn2_w_up, ffn2_w_down, final_norm):
    batch, seq, d = x.shape
    depth = w_in.shape[0]
    fw = final_norm.reshape(1, d)
    x2 = x.reshape(batch * seq, d)
    for l in range(depth):
        x2 = _ffn(x2, ffn1_norm[l].reshape(1, d), ffn1_w_gate[l].astype(BF16),
                  ffn1_w_up[l].astype(BF16), ffn1_w_down[l].astype(BF16), fw, final_norm=False)
        x2 = _mixer(x2, seq, mix_norm[l], w_in[l], conv_w[l], conv_b[l], dt_bias[l],
                    a_log[l], d_skip[l], ssd_norm[l], attn_sinks[l], rel_bias, attn_out_norm[l],
                    sgu_ln_w[l], sgu_ln_b[l], sgu_w[l], sgu_b[l], sgu_out_norm[l], w_out[l])
        x2 = _ffn(x2, ffn2_norm[l].reshape(1, d), ffn2_w_gate[l].astype(BF16),
                  ffn2_w_up[l].astype(BF16), ffn2_w_down[l].astype(BF16), fw,
                  final_norm=(l == depth - 1))
    return x2.reshape(batch, seq, d)
```

```python
import functools
import math

import jax
import jax.numpy as jnp
import numpy as np
from jax import lax
from jax.experimental import pallas as pl
from jax.experimental.pallas import tpu as pltpu

F32 = jnp.float32
BF16 = jnp.bfloat16

D_MODEL = 1024
D_FF = 2816
EPS = 1e-6

SSD_HEADS = 8
SSD_HEAD_DIM = 64
SSD_INNER = SSD_HEADS * SSD_HEAD_DIM
SSD_GROUPS = 2
SSD_STATE = 128
SSD_CONV = 4
SSD_XBC = SSD_INNER + 2 * SSD_GROUPS * SSD_STATE
HEADS_PER_GROUP = SSD_HEADS // SSD_GROUPS
GROUP_WIDTH = HEADS_PER_GROUP * SSD_HEAD_DIM

ATT_Q_HEADS = 4
ATT_KV_HEADS = 2
ATT_HEAD_DIM = 64
ATT_WIDTH = ATT_Q_HEADS * ATT_HEAD_DIM
ATT_KV_WIDTH = ATT_KV_HEADS * ATT_HEAD_DIM
REL_BUCKETS = 32
REL_MAX_DIST = 128
ATT_POS_HEADS = (0, 2, 1, 3)

GM_GROUPS = 4
GM_GROUP_DIM = 64
GM_WIDTH = GM_GROUPS * GM_GROUP_DIM

CHUNK = 128
PAIR = 2 * CHUNK
QUAD = 2 * PAIR
PROJ_PIECE = 256

SUBLANES = 8
LANES = 128
HALF_LANES = LANES // 2
DT_PAD = LANES
DT_COPIES = 3

P_Z = 0
P_XBC = P_Z + SSD_INNER
P_Q = P_XBC + SSD_XBC
P_K = P_Q + ATT_WIDTH
P_V = P_K + ATT_KV_WIDTH
P_U = P_V + ATT_KV_WIDTH
P_GV = P_U + GM_WIDTH
P_DT = P_GV + GM_WIDTH
P_TOTAL = P_DT + DT_PAD

R_XBC = SSD_INNER
R_DT = R_XBC + SSD_XBC
R_Q = R_DT + SSD_HEADS
R_K = R_Q + ATT_WIDTH

Y_ATT = SSD_INNER
Y_SGU = Y_ATT + ATT_WIDTH

MASKED = -1e30
LOG2E = math.log2(math.e)

FFN_TOKENS = 512
FFN_CHUNK = 256
VMEM_LIMIT_FFN = 48 * 1024 * 1024
VMEM_LIMIT_MIXER = 48 * 1024 * 1024


def _unit_rms(x):
    return x * lax.rsqrt(jnp.mean(x * x, axis=-1, keepdims=True) + EPS)


def _sigmoid(x):
    return 1.0 / (1.0 + jnp.exp(-x))


def _gelu_tanh(x):
    c = math.sqrt(2.0 / math.pi)
    return 0.5 * x * (1.0 + jnp.tanh(c * (x + 0.044715 * (x * x * x))))


def _split3(x):
    hi = x.astype(BF16)
    r = x - hi.astype(F32)
    mid = r.astype(BF16)
    lo = (r - mid.astype(F32)).astype(BF16)
    return hi, mid, lo


def _top_bits(x):
    bits = lax.bitcast_convert_type(x, jnp.uint32) & jnp.uint32(0xFFFF0000)
    return lax.bitcast_convert_type(bits, F32)


def _stacked_split(x, lane):
    hi = _top_bits(x)
    r = x - hi
    mid = _top_bits(r)
    lo = r - mid
    return jnp.where(lane < SSD_HEADS, hi, jnp.where(lane < 2 * SSD_HEADS, mid, lo)).astype(BF16)


def _dot(a, b):
    return jnp.dot(a, b, preferred_element_type=F32)


def _dot_nt(a, b):
    return lax.dot_general(a, b, (((1,), (1,)), ((), ())), preferred_element_type=F32)


def _dot_f32_rhs(sel, x):
    hi, mid, lo = _split3(x)
    return _dot(sel, hi) + _dot(sel, mid) + _dot(sel, lo)


def _ffn_kernel(x_ref, nw_ref, wg_ref, wu_ref, wd_ref, fw_ref, o_ref, h_ref, *, final_norm):
    x = x_ref[...]
    xn = (_unit_rms(x) * nw_ref[...]).astype(BF16)
    for c in range(D_FF // FFN_CHUNK):
        cols = slice(c * FFN_CHUNK, (c + 1) * FFN_CHUNK)
        g = _dot(xn, wg_ref[:, cols])
        u = _dot(xn, wu_ref[:, cols])
        h_ref[:, cols] = (g * _sigmoid(g) * u).astype(BF16)
    out = x + 0.5 * _dot(h_ref[...], wd_ref[...])
    if final_norm:
        out = _unit_rms(out) * fw_ref[...]
    o_ref[...] = out


def _ffn(x2d, norm_w, w_gate, w_up, w_down, final_w, *, final_norm):
    tokens = x2d.shape[0]
    const = lambda i: (0, 0)
    resident = functools.partial(pl.BlockSpec, index_map=const, pipeline_mode=pl.Buffered(1))
    return pl.pallas_call(
        functools.partial(_ffn_kernel, final_norm=final_norm),
        out_shape=jax.ShapeDtypeStruct(x2d.shape, F32),
        grid=(tokens // FFN_TOKENS,),
        in_specs=[
            pl.BlockSpec((FFN_TOKENS, D_MODEL), lambda i: (i, 0)),
            resident((1, D_MODEL)),
            resident((D_MODEL, D_FF)),
            resident((D_MODEL, D_FF)),
            resident((D_FF, D_MODEL)),
            resident((1, D_MODEL)),
        ],
        out_specs=pl.BlockSpec((FFN_TOKENS, D_MODEL), lambda i: (i, 0)),
        scratch_shapes=[pltpu.VMEM((FFN_TOKENS, D_FF), BF16)],
        compiler_params=pltpu.CompilerParams(
            dimension_semantics=("arbitrary",), vmem_limit_bytes=VMEM_LIMIT_FFN),
        name="ffn_halfstep",
    )(x2d, norm_w, w_gate, w_up, w_down, final_w)


def _mix_chunk(proj_ref, ycat_ref, keep, bias_idx, p, fill):
    row = lax.broadcasted_iota(jnp.int32, (CHUNK, CHUNK), 0)
    lane = lax.broadcasted_iota(jnp.int32, (CHUNK, CHUNK), 1)
    causal = row >= lane
    row_b = row.astype(F32).astype(BF16)
    lane_b = lane.astype(F32).astype(BF16)
    causal_b = row_b >= lane_b
    upper_half_b = lane_b >= HALF_LANES
    zero_b = jnp.zeros((CHUNK, CHUNK), BF16)
    group_b = (lax.broadcasted_iota(jnp.int32, (CHUNK, GROUP_WIDTH), 1) // SSD_HEAD_DIM).astype(F32).astype(BF16)

    def block_diag(x_b):
        zero = jnp.zeros_like(x_b)
        return jnp.concatenate([jnp.where(group_b == i, x_b, zero) for i in range(HEADS_PER_GROUP)], axis=0)

    xbc_raw = proj_ref[:, P_XBC:P_XBC + SSD_XBC]
    if keep is not None:
        p.ext[0:SUBLANES, :] = p.ext[0:SUBLANES, :] * keep
    p.ext[SUBLANES:, :] = xbc_raw
    ext = p.ext[...]
    conv = p.convb[...] + p.convw[SSD_CONV - 1:SSD_CONV, :] * xbc_raw
    for shift in range(1, SSD_CONV):
        tap = SSD_CONV - 1 - shift
        conv = conv + p.convw[tap:tap + 1, :] * pltpu.roll(ext, shift, 0)[SUBLANES:, :]
    p.ext[0:SUBLANES, :] = xbc_raw[CHUNK - SUBLANES:, :]
    fill()
    xbc = conv * _sigmoid(conv)
    xs = xbc[:, :SSD_INNER]
    bm = xbc[:, SSD_INNER:SSD_INNER + SSD_GROUPS * SSD_STATE]
    cm = xbc[:, SSD_INNER + SSD_GROUPS * SSD_STATE:]
    x_b = xs.astype(BF16)
    bm_b = bm.astype(BF16)
    cm_b = cm.astype(BF16)
    fill()

    dtr = proj_ref[:, P_DT:P_DT + DT_PAD] + p.dtb[...]
    dt = jnp.maximum(dtr, 0.0) + jnp.log1p(jnp.exp(-jnp.abs(dtr)))
    a = dt * (-jnp.exp(p.alog[...]))
    a_cs = _dot_f32_rhs(p.tril[...], a)
    a_cs_t = a_cs.T
    dt_t = dt.T
    grow = jnp.exp(a_cs)
    to_end = dt * jnp.exp(a_cs[CHUNK - 1:CHUNK, :] - a_cs)
    grow_e = _dot(_stacked_split(grow, lane), p.expand[...])
    to_end_e = _dot(_stacked_split(to_end, lane), p.expand[...])
    xd_b = (xs * to_end_e).astype(BF16)
    fill()

    y_diag = []
    for g in range(SSD_GROUPS):
        n0 = g * SSD_STATE
        cb = _dot_nt(cm_b[:, n0:n0 + SSD_STATE], bm_b[:, n0:n0 + SSD_STATE])
        parts = []
        for hh in range(HEADS_PER_GROUP):
            h = g * HEADS_PER_GROUP + hh
            seg = a_cs[:, h:h + 1] - a_cs_t[h:h + 1, :]
            decay = jnp.exp(jnp.where(causal, seg, -jnp.inf))
            parts.append((cb * decay * dt_t[h:h + 1, :]).astype(BF16))
        w0 = g * GROUP_WIDTH
        y_diag.append(_dot(jnp.concatenate(parts, axis=-1), block_diag(x_b[:, w0:w0 + GROUP_WIDTH])))
        fill()
    y_off = []
    for g in range(SSD_GROUPS):
        n0 = g * SSD_STATE
        w0 = g * GROUP_WIDTH
        prev = p.st[g]
        if keep is not None:
            prev = prev * keep
        y_off.append(_dot(cm_b[:, n0:n0 + SSD_STATE], prev.astype(BF16)))
        bt = bm[:, n0:n0 + SSD_STATE].T.astype(BF16)
        p.st[g] = (prev * grow_e[CHUNK - 1:CHUNK, w0:w0 + GROUP_WIDTH]
                   + _dot(bt, xd_b[:, w0:w0 + GROUP_WIDTH]))
    fill()
    y = (jnp.concatenate(y_diag, axis=-1) + jnp.concatenate(y_off, axis=-1) * grow_e
         + xs * p.dskip[...])
    z = proj_ref[:, P_Z:P_Z + SSD_INNER]
    ycat_ref[:, 0:Y_ATT] = _unit_rms(y * (z * _sigmoid(z))).astype(BF16)
    fill()

    q_b = (proj_ref[:, P_Q:P_Q + ATT_WIDTH] * (LOG2E / math.sqrt(ATT_HEAD_DIM))).astype(BF16)
    k_b = proj_ref[:, P_K:P_K + ATT_KV_WIDTH].astype(BF16)
    v_b = proj_ref[:, P_V:P_V + ATT_KV_WIDTH].astype(BF16)
    keys = jnp.concatenate([p.kprev[...], k_b], axis=0)
    v_prev = p.vprev[...]
    halves = lambda t: (jnp.where(upper_half_b, zero_b, t), jnp.where(upper_half_b, t, zero_b))
    v_prev_half, v_half = halves(v_prev), halves(v_b)
    probs, values = [], []
    for pos, hq in enumerate(ATT_POS_HEADS):
        tile, half = divmod(pos, 2)
        q_tile = q_b[:, tile * LANES:(tile + 1) * LANES]
        q_head = halves(q_tile)[half]
        s = _dot_nt(q_head, keys)
        s = jnp.where(causal, s[:, CHUNK:], s[:, :CHUNK]) + p.bias[bias_idx, pos]
        sink = p.sink[hq] * LOG2E
        m = jnp.maximum(jnp.max(s, axis=-1, keepdims=True), sink)
        e = jnp.exp2(s - m)
        denom = jnp.sum(e, axis=-1, keepdims=True) + jnp.exp2(sink - m)
        pn = (e * (1.0 / denom)).astype(BF16)
        probs.append(jnp.where(causal_b, zero_b, pn))
        probs.append(jnp.where(causal_b, pn, zero_b))
        for vh in (v_prev_half[half], v_half[half]):
            values.append(jnp.concatenate([zero_b, vh] if tile else [vh, zero_b], axis=-1))
        fill()
    att = _dot(jnp.concatenate(probs, axis=-1), jnp.concatenate(values, axis=0))
    p.kprev[...] = k_b
    p.vprev[...] = v_b
    ycat_ref[:, Y_ATT:Y_SGU] = _unit_rms(att).astype(BF16)
    fill()

    u = _gelu_tanh(proj_ref[:, P_U:P_U + GM_WIDTH])
    gv = _gelu_tanh(proj_ref[:, P_GV:P_GV + GM_WIDTH])
    fill()
    mu = jnp.mean(gv, axis=-1, keepdims=True)
    gc = gv - mu
    gv = gc * lax.rsqrt(jnp.mean(gc * gc, axis=-1, keepdims=True) + EPS) * p.lnw[...] + p.lnb[...]
    gv_b = gv.astype(BF16)
    mixed = _dot(p.wcat[...], block_diag(gv_b)) + p.sgub[...]
    ycat_ref[:, Y_SGU:] = _unit_rms(u * mixed).astype(BF16)
    fill()


class _MixerRefs:
    def __init__(self, **refs):
        self.__dict__.update(refs)


def _mixer_kernel(sink_ref, rel_ref,
                  xa_ref, xc_ref, win_ref, convw_ref, convb_ref, dtb_ref, alog_ref,
                  dskip_ref, tril_ref, expand_ref, bucket_ref,
                  lnw_ref, lnb_ref, sguw_ref, sgub_ref, wout_ref,
                  o_ref,
                  projx_ref, projy_ref, ycatx_ref, ycaty_ref, xn_ref,
                  ext_ref, st_ref, kprev_ref, vprev_ref, bias_ref, wcat_ref,
                  *, chunks_per_seq):
    t = pl.program_id(0)
    p = _MixerRefs(sink=sink_ref, convw=convw_ref, convb=convb_ref, dtb=dtb_ref, alog=alog_ref,
                   dskip=dskip_ref, tril=tril_ref, expand=expand_ref, lnw=lnw_ref, lnb=lnb_ref,
                   sgub=sgub_ref, ext=ext_ref, st=st_ref, kprev=kprev_ref, vprev=vprev_ref,
                   bias=bias_ref, wcat=wcat_ref)

    @pl.when(t == 0)
    def _init():
        row = lax.broadcasted_iota(jnp.int32, (CHUNK, CHUNK), 0)
        col = lax.broadcasted_iota(jnp.int32, (CHUNK, CHUNK), 1)
        bucket = bucket_ref[...]
        for pos, hq in enumerate(ATT_POS_HEADS):
            acc = jnp.zeros((CHUNK, CHUNK), F32)
            for k in range(REL_BUCKETS):
                acc = jnp.where(bucket == k, rel_ref[k * ATT_Q_HEADS + hq] * LOG2E, acc)
            bias_ref[0, pos] = acc
            bias_ref[1, pos] = jnp.where(row >= col, acc, MASKED)
        for g in range(GM_GROUPS):
            wcat_ref[:, g * CHUNK:(g + 1) * CHUNK] = jnp.where(row >= col, sguw_ref[g], 0.0).astype(BF16)
        projy_ref[...] = jnp.zeros_like(projy_ref)
        ycatx_ref[...] = jnp.zeros_like(ycatx_ref)
        st_ref[...] = jnp.zeros_like(st_ref)
        ext_ref[0:SUBLANES, :] = jnp.zeros((SUBLANES, SSD_XBC), F32)
        kprev_ref[...] = jnp.zeros_like(kprev_ref)
        vprev_ref[...] = jnp.zeros_like(vprev_ref)

    starts_seq = lax.rem(4 * t, chunks_per_seq) == 0
    keep = jnp.where(starts_seq, 0.0, 1.0).astype(F32)
    first_idx = jnp.where(starts_seq, 1, 0).astype(jnp.int32)
    even, odd = pl.ds(0, CHUNK), pl.ds(CHUNK, CHUNK)

    def phase(rows, proj_new, proj_mix, ycat_mix, ycat_out, keep, bias_idx):
        def normalize():
            xn_ref[...] = _unit_rms(xa_ref[rows, :]).astype(BF16)

        def project(c0, c1):
            proj_new[:, c0:c1] = _dot(xn_ref[...], win_ref[:, c0:c1])

        def output(c0, c1):
            o_ref[rows, c0:c1] = xc_ref[rows, c0:c1] + _dot(ycat_out[...], wout_ref[:, c0:c1])

        in_cols = list(range(0, P_TOTAL - PROJ_PIECE, PROJ_PIECE)) + [P_TOTAL]
        pieces = [normalize]
        pieces += [functools.partial(project, a, b) for a, b in zip(in_cols[:-1], in_cols[1:])]
        out_pieces = [functools.partial(output, c, c + PROJ_PIECE) for c in range(0, D_MODEL, PROJ_PIECE)]
        for i, piece in enumerate(out_pieces):
            pieces.insert(3 + 3 * i, piece)
        pending = iter(pieces)
        slot = [0]

        def fill():
            slot[0] += 1
            if slot[0] % 2 == 1:
                next(pending, lambda: None)()

        _mix_chunk(proj_mix.at[even], ycat_mix.at[even], keep, bias_idx, p, fill)
        _mix_chunk(proj_mix.at[odd], ycat_mix.at[odd], None, 0, p, fill)
        for piece in pending:
            piece()

    phase(pl.ds(0, PAIR), projx_ref, projy_ref, ycaty_ref, ycatx_ref, None, 0)
    phase(pl.ds(PAIR, PAIR), projy_ref, projx_ref, ycatx_ref, ycaty_ref, keep, first_idx)


def _folded_bucket_tile():
    i = np.arange(CHUNK)[:, None]
    j = np.arange(CHUNK)[None, :]
    n = np.where(j <= i, i - j, i - j + CHUNK)
    max_exact = REL_BUCKETS // 2
    large = max_exact + (np.log(np.maximum(n, 1) / max_exact) / np.log(REL_MAX_DIST / max_exact)
                         * (REL_BUCKETS - max_exact)).astype(np.int32)
    large = np.minimum(large, REL_BUCKETS - 1)
    return np.where(n < max_exact, n, large).astype(np.int32)


def _mixer(x2d, seq, mix_norm, w_in, conv_w, conv_b, dt_bias, a_log, d_skip, ssd_norm, sinks, rel_bias,
           attn_out_norm, ln_w, ln_b, sgu_w, sgu_b, sgu_out_norm, w_out):
    tokens = x2d.shape[0]
    n_quads = tokens // QUAD
    chunks_per_seq = seq // CHUNK
    assert tokens % QUAD == 0 and seq % QUAD == 0 and GM_GROUPS == HEADS_PER_GROUP
    att_cols = np.concatenate([np.arange(h * ATT_HEAD_DIM, (h + 1) * ATT_HEAD_DIM) for h in ATT_POS_HEADS])
    w_dt = w_in[:, R_DT:R_Q]
    w_in_p = jnp.concatenate(
        [w_in[:, :R_DT], w_in[:, R_Q + att_cols], w_in[:, R_K:]] + [w_dt] * DT_COPIES
        + [jnp.zeros((D_MODEL, DT_PAD - DT_COPIES * SSD_HEADS), w_in.dtype)], axis=1)
    w_in_p = (mix_norm[:, None] * w_in_p).astype(BF16)
    gains = jnp.concatenate([ssd_norm, attn_out_norm[att_cols], sgu_out_norm])
    w_out_p = jnp.concatenate([w_out[:Y_ATT], w_out[Y_ATT + att_cols], w_out[Y_SGU:]], axis=0)
    w_out_p = (gains[:, None] * w_out_p).astype(BF16)
    rep_heads = lambda v: jnp.pad(jnp.tile(v, DT_COPIES), (0, DT_PAD - DT_COPIES * SSD_HEADS)).reshape(1, DT_PAD)
    row = lambda v: v.reshape(1, -1)
    tril = jnp.asarray(np.tril(np.ones((CHUNK, CHUNK), np.float32)), BF16)
    expand = np.zeros((DT_PAD, SSD_INNER), np.float32)
    for r in range(DT_COPIES * SSD_HEADS):
        h = r % SSD_HEADS
        expand[r, h * SSD_HEAD_DIM:(h + 1) * SSD_HEAD_DIM] = 1.0
    expand = jnp.asarray(expand, BF16)
    bucket = jnp.asarray(_folded_bucket_tile())
    sgu_b_e = jnp.repeat(jnp.transpose(sgu_b), GM_GROUP_DIM, axis=1)

    const2 = lambda t, *_: (0, 0)
    const3 = lambda t, *_: (0, 0, 0)
    full2 = lambda shape: pl.BlockSpec(shape, const2)
    projected = lambda t, *_: (jnp.minimum(t, n_quads - 1), 0)
    finished = lambda t, *_: (jnp.maximum(t - 1, 0), 0)
    grid_spec = pltpu.PrefetchScalarGridSpec(
        num_scalar_prefetch=2,
        grid=(n_quads + 1,),
        in_specs=[
            pl.BlockSpec((QUAD, D_MODEL), projected),
            pl.BlockSpec((QUAD, D_MODEL), finished),
            full2((D_MODEL, P_TOTAL)),
            full2((SSD_CONV, SSD_XBC)),
            full2((1, SSD_XBC)),
            full2((1, DT_PAD)),
            full2((1, DT_PAD)),
            full2((1, SSD_INNER)),
            full2((CHUNK, CHUNK)),
            full2((DT_PAD, SSD_INNER)),
            full2((CHUNK, CHUNK)),
            full2((1, GM_WIDTH)),
            full2((1, GM_WIDTH)),
            pl.BlockSpec((GM_GROUPS, CHUNK, CHUNK), const3),
            full2((CHUNK, GM_WIDTH)),
            full2((D_MODEL, D_MODEL)),
        ],
        out_specs=pl.BlockSpec((QUAD, D_MODEL), finished),
        scratch_shapes=[
            pltpu.VMEM((PAIR, P_TOTAL), F32),
            pltpu.VMEM((PAIR, P_TOTAL), F32),
            pltpu.VMEM((PAIR, D_MODEL), BF16),
            pltpu.VMEM((PAIR, D_MODEL), BF16),
            pltpu.VMEM((PAIR, D_MODEL), BF16),
            pltpu.VMEM((SUBLANES + CHUNK, SSD_XBC), F32),
            pltpu.VMEM((SSD_GROUPS, SSD_STATE, GROUP_WIDTH), F32),
            pltpu.VMEM((CHUNK, ATT_KV_WIDTH), BF16),
            pltpu.VMEM((CHUNK, ATT_KV_WIDTH), BF16),
            pltpu.VMEM((2, ATT_Q_HEADS, CHUNK, CHUNK), F32),
            pltpu.VMEM((CHUNK, GM_GROUPS * CHUNK), BF16),
        ],
    )
    return pl.pallas_call(
        functools.partial(_mixer_kernel, chunks_per_seq=chunks_per_seq),
        out_shape=jax.ShapeDtypeStruct(x2d.shape, F32),
        grid_spec=grid_spec,
        compiler_params=pltpu.CompilerParams(
            dimension_semantics=("arbitrary",), vmem_limit_bytes=VMEM_LIMIT_MIXER),
        name="token_mixer",
    )(sinks.astype(F32), rel_bias.reshape(-1).astype(F32),
      x2d, x2d, w_in_p, conv_w, row(conv_b), rep_heads(dt_bias), rep_heads(a_log),
      row(jnp.repeat(d_skip, SSD_HEAD_DIM)), tril, expand, bucket,
      row(ln_w), row(ln_b), sgu_w, sgu_b_e, w_out_p)


def kernel(x, ffn1_norm, ffn1_w_gate, ffn1_w_up, ffn1_w_down, mix_norm, w_in, conv_w, conv_b, dt_bias, a_log, d_skip, ssd_norm, attn_sinks, rel_bias, attn_out_norm, sgu_ln_w, sgu_ln_b, sgu_w, sgu_b, sgu_out_norm, w_out, ffn2_norm, ffn2_w_gate, ffn2_w_up, ffn2_w_down, final_norm):
    batch, seq, d = x.shape
    depth = w_in.shape[0]
    fw = final_norm.reshape(1, d)
    x2 = x.reshape(batch * seq, d)
    for l in range(depth):
        x2 = _ffn(x2, ffn1_norm[l].reshape(1, d), ffn1_w_gate[l].astype(BF16),
                  ffn1_w_up[l].astype(BF16), ffn1_w_down[l].astype(BF16), fw, final_norm=False)
        x2 = _mixer(x2, seq, mix_norm[l], w_in[l], conv_w[l], conv_b[l], dt_bias[l],
                    a_log[l], d_skip[l], ssd_norm[l], attn_sinks[l], rel_bias, attn_out_norm[l],
                    sgu_ln_w[l], sgu_ln_b[l], sgu_w[l], sgu_b[l], sgu_out_norm[l], w_out[l])
        x2 = _ffn(x2, ffn2_norm[l].reshape(1, d), ffn2_w_gate[l].astype(BF16),
                  ffn2_w_up[l].astype(BF16), ffn2_w_down[l].astype(BF16), fw,
                  final_norm=(l == depth - 1))
    return x2.reshape(batch, seq, d)
```

```python
import functools
import math

import jax
import jax.numpy as jnp
import numpy as np
from jax import lax
from jax.experimental import pallas as pl
from jax.experimental.pallas import tpu as pltpu

F32 = jnp.float32
BF16 = jnp.bfloat16

D_MODEL = 1024
D_FF = 2816
EPS = 1e-6

SSD_HEADS = 8
SSD_HEAD_DIM = 64
SSD_INNER = SSD_HEADS * SSD_HEAD_DIM
SSD_GROUPS = 2
SSD_STATE = 128
SSD_CONV = 4
SSD_XBC = SSD_INNER + 2 * SSD_GROUPS * SSD_STATE
HEADS_PER_GROUP = SSD_HEADS // SSD_GROUPS
GROUP_WIDTH = HEADS_PER_GROUP * SSD_HEAD_DIM

ATT_Q_HEADS = 4
ATT_KV_HEADS = 2
ATT_HEAD_DIM = 64
ATT_WIDTH = ATT_Q_HEADS * ATT_HEAD_DIM
ATT_KV_WIDTH = ATT_KV_HEADS * ATT_HEAD_DIM
REL_BUCKETS = 32
REL_MAX_DIST = 128
ATT_POS_HEADS = (0, 2, 1, 3)

GM_GROUPS = 4
GM_GROUP_DIM = 64
GM_WIDTH = GM_GROUPS * GM_GROUP_DIM

CHUNK = 128
PAIR = 2 * CHUNK
QUAD = 2 * PAIR
PROJ_PIECE = 256

SUBLANES = 8
LANES = 128
HALF_LANES = LANES // 2
DT_PAD = LANES
DT_COPIES = 3

P_Z = 0
P_XBC = P_Z + SSD_INNER
P_Q = P_XBC + SSD_XBC
P_K = P_Q + ATT_WIDTH
P_V = P_K + ATT_KV_WIDTH
P_U = P_V + ATT_KV_WIDTH
P_GV = P_U + GM_WIDTH
P_DT = P_GV + GM_WIDTH
P_TOTAL = P_DT + DT_PAD

R_XBC = SSD_INNER
R_DT = R_XBC + SSD_XBC
R_Q = R_DT + SSD_HEADS
R_K = R_Q + ATT_WIDTH

Y_ATT = SSD_INNER
Y_SGU = Y_ATT + ATT_WIDTH

MASKED = -1e30
LOG2E = math.log2(math.e)

FFN_TOKENS = 512
FFN_CHUNK = 256
FFN_PIECES = D_FF // FFN_CHUNK
VMEM_LIMIT_FFN = 48 * 1024 * 1024
VMEM_LIMIT_MIXER = 48 * 1024 * 1024


def _unit_rms(x):
    return x * lax.rsqrt(jnp.mean(x * x, axis=-1, keepdims=True) + EPS)


def _sigmoid(x):
    return 1.0 / (1.0 + jnp.exp(-x))


def _gelu_tanh(x):
    c = math.sqrt(2.0 / math.pi)
    return 0.5 * x * (1.0 + jnp.tanh(c * (x + 0.044715 * (x * x * x))))


def _split3(x):
    hi = x.astype(BF16)
    r = x - hi.astype(F32)
    mid = r.astype(BF16)
    lo = (r - mid.astype(F32)).astype(BF16)
    return hi, mid, lo


def _top_bits(x):
    bits = lax.bitcast_convert_type(x, jnp.uint32) & jnp.uint32(0xFFFF0000)
    return lax.bitcast_convert_type(bits, F32)


def _stacked_split(x, lane):
    hi = _top_bits(x)
    r = x - hi
    mid = _top_bits(r)
    lo = r - mid
    return jnp.where(lane < SSD_HEADS, hi, jnp.where(lane < 2 * SSD_HEADS, mid, lo)).astype(BF16)


def _dot(a, b):
    return jnp.dot(a, b, preferred_element_type=F32)


def _dot_nt(a, b):
    return lax.dot_general(a, b, (((1,), (1,)), ((), ())), preferred_element_type=F32)


def _dot_f32_rhs(sel, x):
    hi, mid, lo = _split3(x)
    return _dot(sel, hi) + _dot(sel, mid) + _dot(sel, lo)


def _ffn_kernel(x_ref, nw_ref, wg_ref, wu_ref, wd_ref, fw_ref, o_ref, wg_s, wu_s, wd_s, h_ref, *, final_norm):
    i = pl.program_id(0)

    @pl.when(i < FFN_PIECES)
    def _load_weight_slice():
        wg_s[i] = wg_ref[...].astype(BF16)
        wu_s[i] = wu_ref[...].astype(BF16)
        wd_s[i] = wd_ref[...].astype(BF16)

    @pl.when(i >= FFN_PIECES)
    def _token_rows():
        x = x_ref[...]
        xn = (_unit_rms(x) * nw_ref[...]).astype(BF16)
        for c in range(FFN_PIECES):
            g = _dot(xn, wg_s[c])
            u = _dot(xn, wu_s[c])
            h_ref[:, c * FFN_CHUNK:(c + 1) * FFN_CHUNK] = (g * _sigmoid(g) * u).astype(BF16)
        out = x + 0.5 * _dot(h_ref[...], wd_s[...].reshape(D_FF, D_MODEL))
        if final_norm:
            out = _unit_rms(out) * fw_ref[...]
        o_ref[...] = out


def _ffn(x2d, layer, norm_w, w_gate, w_up, w_down, final_w, *, final_norm):
    tokens = x2d.shape[0]
    last = FFN_PIECES - 1
    rows = lambda i: (jnp.maximum(i - FFN_PIECES, 0), 0)
    return pl.pallas_call(
        functools.partial(_ffn_kernel, final_norm=final_norm),
        out_shape=jax.ShapeDtypeStruct(x2d.shape, F32),
        grid=(FFN_PIECES + tokens // FFN_TOKENS,),
        in_specs=[
            pl.BlockSpec((FFN_TOKENS, D_MODEL), rows),
            pl.BlockSpec((None, 1, D_MODEL), lambda i: (layer, 0, 0)),
            pl.BlockSpec((None, D_MODEL, FFN_CHUNK), lambda i: (layer, 0, jnp.minimum(i, last))),
            pl.BlockSpec((None, D_MODEL, FFN_CHUNK), lambda i: (layer, 0, jnp.minimum(i, last))),
            pl.BlockSpec((None, FFN_CHUNK, D_MODEL), lambda i: (layer, jnp.minimum(i, last), 0)),
            pl.BlockSpec((1, D_MODEL), lambda i: (0, 0)),
        ],
        out_specs=pl.BlockSpec((FFN_TOKENS, D_MODEL), rows),
        scratch_shapes=[
            pltpu.VMEM((FFN_PIECES, D_MODEL, FFN_CHUNK), BF16),
            pltpu.VMEM((FFN_PIECES, D_MODEL, FFN_CHUNK), BF16),
            pltpu.VMEM((FFN_PIECES, FFN_CHUNK, D_MODEL), BF16),
            pltpu.VMEM((FFN_TOKENS, D_FF), BF16),
        ],
        compiler_params=pltpu.CompilerParams(
            dimension_semantics=("arbitrary",), vmem_limit_bytes=VMEM_LIMIT_FFN),
        name="ffn_halfstep",
    )(x2d, norm_w, w_gate, w_up, w_down, final_w)


def _mix_chunk(proj_ref, ycat_ref, keep, bias_idx, p, fill):
    row = lax.broadcasted_iota(jnp.int32, (CHUNK, CHUNK), 0)
    lane = lax.broadcasted_iota(jnp.int32, (CHUNK, CHUNK), 1)
    causal = row >= lane
    row_b = row.astype(F32).astype(BF16)
    lane_b = lane.astype(F32).astype(BF16)
    causal_b = row_b >= lane_b
    upper_half_b = lane_b >= HALF_LANES
    zero_b = jnp.zeros((CHUNK, CHUNK), BF16)
    group_b = (lax.broadcasted_iota(jnp.int32, (CHUNK, GROUP_WIDTH), 1) // SSD_HEAD_DIM).astype(F32).astype(BF16)

    def block_diag(x_b):
        zero = jnp.zeros_like(x_b)
        return jnp.concatenate([jnp.where(group_b == i, x_b, zero) for i in range(HEADS_PER_GROUP)], axis=0)

    xbc_raw = proj_ref[:, P_XBC:P_XBC + SSD_XBC]
    if keep is not None:
        p.ext[0:SUBLANES, :] = p.ext[0:SUBLANES, :] * keep
    p.ext[SUBLANES:, :] = xbc_raw
    ext = p.ext[...]
    conv = p.convb[...] + p.convw[SSD_CONV - 1:SSD_CONV, :] * xbc_raw
    for shift in range(1, SSD_CONV):
        tap = SSD_CONV - 1 - shift
        conv = conv + p.convw[tap:tap + 1, :] * pltpu.roll(ext, shift, 0)[SUBLANES:, :]
    p.ext[0:SUBLANES, :] = xbc_raw[CHUNK - SUBLANES:, :]
    fill()
    xbc = conv * _sigmoid(conv)
    xs = xbc[:, :SSD_INNER]
    bm = xbc[:, SSD_INNER:SSD_INNER + SSD_GROUPS * SSD_STATE]
    cm = xbc[:, SSD_INNER + SSD_GROUPS * SSD_STATE:]
    x_b = xs.astype(BF16)
    bm_b = bm.astype(BF16)
    cm_b = cm.astype(BF16)
    fill()

    dtr = proj_ref[:, P_DT:P_DT + DT_PAD] + p.dtb[...]
    dt = jnp.maximum(dtr, 0.0) + jnp.log1p(jnp.exp(-jnp.abs(dtr)))
    a = dt * (-jnp.exp(p.alog[...]))
    a_cs = _dot_f32_rhs(p.tril[...], a)
    a_cs_t = a_cs.T
    dt_t = dt.T
    grow = jnp.exp(a_cs)
    to_end = dt * jnp.exp(a_cs[CHUNK - 1:CHUNK, :] - a_cs)
    grow_e = _dot(_stacked_split(grow, lane), p.expand[...])
    to_end_e = _dot(_stacked_split(to_end, lane), p.expand[...])
    xd_b = (xs * to_end_e).astype(BF16)
    fill()

    y_diag = []
    for g in range(SSD_GROUPS):
        n0 = g * SSD_STATE
        cb = _dot_nt(cm_b[:, n0:n0 + SSD_STATE], bm_b[:, n0:n0 + SSD_STATE])
        parts = []
        for hh in range(HEADS_PER_GROUP):
            h = g * HEADS_PER_GROUP + hh
            seg = a_cs[:, h:h + 1] - a_cs_t[h:h + 1, :]
            decay = jnp.exp(jnp.where(causal, seg, -jnp.inf))
            parts.append((cb * decay * dt_t[h:h + 1, :]).astype(BF16))
        w0 = g * GROUP_WIDTH
        y_diag.append(_dot(jnp.concatenate(parts, axis=-1), block_diag(x_b[:, w0:w0 + GROUP_WIDTH])))
        fill()
    y_off = []
    for g in range(SSD_GROUPS):
        n0 = g * SSD_STATE
        w0 = g * GROUP_WIDTH
        prev = p.st[g]
        if keep is not None:
            prev = prev * keep
        y_off.append(_dot(cm_b[:, n0:n0 + SSD_STATE], prev.astype(BF16)))
        bt = bm[:, n0:n0 + SSD_STATE].T.astype(BF16)
        p.st[g] = (prev * grow_e[CHUNK - 1:CHUNK, w0:w0 + GROUP_WIDTH]
                   + _dot(bt, xd_b[:, w0:w0 + GROUP_WIDTH]))
    fill()
    y = (jnp.concatenate(y_diag, axis=-1) + jnp.concatenate(y_off, axis=-1) * grow_e
         + xs * p.dskip[...])
    z = proj_ref[:, P_Z:P_Z + SSD_INNER]
    ycat_ref[:, 0:Y_ATT] = _unit_rms(y * (z * _sigmoid(z))).astype(BF16)
    fill()

    q_b = (proj_ref[:, P_Q:P_Q + ATT_WIDTH] * (LOG2E / math.sqrt(ATT_HEAD_DIM))).astype(BF16)
    k_b = proj_ref[:, P_K:P_K + ATT_KV_WIDTH].astype(BF16)
    v_b = proj_ref[:, P_V:P_V + ATT_KV_WIDTH].astype(BF16)
    keys = jnp.concatenate([p.kprev[...], k_b], axis=0)
    v_prev = p.vprev[...]
    halves = lambda t: (jnp.where(upper_half_b, zero_b, t), jnp.where(upper_half_b, t, zero_b))
    v_prev_half, v_half = halves(v_prev), halves(v_b)
    probs, values = [], []
    for pos, hq in enumerate(ATT_POS_HEADS):
        tile, half = divmod(pos, 2)
        q_tile = q_b[:, tile * LANES:(tile + 1) * LANES]
        q_head = halves(q_tile)[half]
        s = _dot_nt(q_head, keys)
        s = jnp.where(causal, s[:, CHUNK:], s[:, :CHUNK]) + p.bias[bias_idx, pos]
        sink = p.sink[hq] * LOG2E
        m = jnp.maximum(jnp.max(s, axis=-1, keepdims=True), sink)
        e = jnp.exp2(s - m)
        denom = jnp.sum(e, axis=-1, keepdims=True) + jnp.exp2(sink - m)
        pn = (e * (1.0 / denom)).astype(BF16)
        probs.append(jnp.where(causal_b, zero_b, pn))
        probs.append(jnp.where(causal_b, pn, zero_b))
        for vh in (v_prev_half[half], v_half[half]):
            values.append(jnp.concatenate([zero_b, vh] if tile else [vh, zero_b], axis=-1))
        fill()
    att = _dot(jnp.concatenate(probs, axis=-1), jnp.concatenate(values, axis=0))
    p.kprev[...] = k_b
    p.vprev[...] = v_b
    ycat_ref[:, Y_ATT:Y_SGU] = _unit_rms(att).astype(BF16)
    fill()

    u = _gelu_tanh(proj_ref[:, P_U:P_U + GM_WIDTH])
    gv = _gelu_tanh(proj_ref[:, P_GV:P_GV + GM_WIDTH])
    fill()
    mu = jnp.mean(gv, axis=-1, keepdims=True)
    gc = gv - mu
    gv = gc * lax.rsqrt(jnp.mean(gc * gc, axis=-1, keepdims=True) + EPS) * p.lnw[...] + p.lnb[...]
    gv_b = gv.astype(BF16)
    mixed = _dot(p.wcat[...], block_diag(gv_b)) + p.sgub[...]
    ycat_ref[:, Y_SGU:] = _unit_rms(u * mixed).astype(BF16)
    fill()


class _MixerRefs:
    def __init__(self, **refs):
        self.__dict__.update(refs)


def _mixer_kernel(sink_ref, rel_ref,
                  xa_ref, xc_ref, win_ref, convw_ref, convb_ref, dtb_ref, alog_ref,
                  dskip_ref, tril_ref, expand_ref, bucket_ref,
                  lnw_ref, lnb_ref, sguw_ref, sgub_ref, wout_ref,
                  o_ref,
                  projx_ref, projy_ref, ycatx_ref, ycaty_ref, xn_ref,
                  ext_ref, st_ref, kprev_ref, vprev_ref, bias_ref, wcat_ref,
                  *, chunks_per_seq):
    t = pl.program_id(0)
    p = _MixerRefs(sink=sink_ref, convw=convw_ref, convb=convb_ref, dtb=dtb_ref, alog=alog_ref,
                   dskip=dskip_ref, tril=tril_ref, expand=expand_ref, lnw=lnw_ref, lnb=lnb_ref,
                   sgub=sgub_ref, ext=ext_ref, st=st_ref, kprev=kprev_ref, vprev=vprev_ref,
                   bias=bias_ref, wcat=wcat_ref)

    @pl.when(t == 0)
    def _init():
        row = lax.broadcasted_iota(jnp.int32, (CHUNK, CHUNK), 0)
        col = lax.broadcasted_iota(jnp.int32, (CHUNK, CHUNK), 1)
        bucket = bucket_ref[...]
        for pos, hq in enumerate(ATT_POS_HEADS):
            acc = jnp.zeros((CHUNK, CHUNK), F32)
            for k in range(REL_BUCKETS):
                acc = jnp.where(bucket == k, rel_ref[k * ATT_Q_HEADS + hq] * LOG2E, acc)
            bias_ref[0, pos] = acc
            bias_ref[1, pos] = jnp.where(row >= col, acc, MASKED)
        for g in range(GM_GROUPS):
            wcat_ref[:, g * CHUNK:(g + 1) * CHUNK] = jnp.where(row >= col, sguw_ref[g], 0.0).astype(BF16)
        projy_ref[...] = jnp.zeros_like(projy_ref)
        ycatx_ref[...] = jnp.zeros_like(ycatx_ref)
        st_ref[...] = jnp.zeros_like(st_ref)
        ext_ref[0:SUBLANES, :] = jnp.zeros((SUBLANES, SSD_XBC), F32)
        kprev_ref[...] = jnp.zeros_like(kprev_ref)
        vprev_ref[...] = jnp.zeros_like(vprev_ref)

    starts_seq = lax.rem(4 * t, chunks_per_seq) == 0
    keep = jnp.where(starts_seq, 0.0, 1.0).astype(F32)
    first_idx = jnp.where(starts_seq, 1, 0).astype(jnp.int32)
    even, odd = pl.ds(0, CHUNK), pl.ds(CHUNK, CHUNK)

    def phase(rows, proj_new, proj_mix, ycat_mix, ycat_out, keep, bias_idx):
        def normalize():
            xn_ref[...] = _unit_rms(xa_ref[rows, :]).astype(BF16)

        def project(c0, c1):
            proj_new[:, c0:c1] = _dot(xn_ref[...], win_ref[:, c0:c1])

        def output(c0, c1):
            o_ref[rows, c0:c1] = xc_ref[rows, c0:c1] + _dot(ycat_out[...], wout_ref[:, c0:c1])

        in_cols = list(range(0, P_TOTAL - PROJ_PIECE, PROJ_PIECE)) + [P_TOTAL]
        pieces = [normalize]
        pieces += [functools.partial(project, a, b) for a, b in zip(in_cols[:-1], in_cols[1:])]
        out_pieces = [functools.partial(output, c, c + PROJ_PIECE) for c in range(0, D_MODEL, PROJ_PIECE)]
        for i, piece in enumerate(out_pieces):
            pieces.insert(3 + 3 * i, piece)
        pending = iter(pieces)
        slot = [0]

        def fill():
            slot[0] += 1
            if slot[0] % 2 == 1:
                next(pending, lambda: None)()

        _mix_chunk(proj_mix.at[even], ycat_mix.at[even], keep, bias_idx, p, fill)
        _mix_chunk(proj_mix.at[odd], ycat_mix.at[odd], None, 0, p, fill)
        for piece in pending:
            piece()

    phase(pl.ds(0, PAIR), projx_ref, projy_ref, ycaty_ref, ycatx_ref, None, 0)
    phase(pl.ds(PAIR, PAIR), projy_ref, projx_ref, ycatx_ref, ycaty_ref, keep, first_idx)


def _folded_bucket_tile():
    i = np.arange(CHUNK)[:, None]
    j = np.arange(CHUNK)[None, :]
    n = np.where(j <= i, i - j, i - j + CHUNK)
    max_exact = REL_BUCKETS // 2
    large = max_exact + (np.log(np.maximum(n, 1) / max_exact) / np.log(REL_MAX_DIST / max_exact)
                         * (REL_BUCKETS - max_exact)).astype(np.int32)
    large = np.minimum(large, REL_BUCKETS - 1)
    return np.where(n < max_exact, n, large).astype(np.int32)


def _mixer(x2d, seq, mix_norm, w_in, conv_w, conv_b, dt_bias, a_log, d_skip, ssd_norm, sinks, rel_bias,
           attn_out_norm, ln_w, ln_b, sgu_w, sgu_b, sgu_out_norm, w_out):
    tokens = x2d.shape[0]
    n_quads = tokens // QUAD
    chunks_per_seq = seq // CHUNK
    assert tokens % QUAD == 0 and seq % QUAD == 0 and GM_GROUPS == HEADS_PER_GROUP
    att_cols = np.concatenate([np.arange(h * ATT_HEAD_DIM, (h + 1) * ATT_HEAD_DIM) for h in ATT_POS_HEADS])
    w_dt = w_in[:, R_DT:R_Q]
    w_in_p = jnp.concatenate(
        [w_in[:, :R_DT], w_in[:, R_Q + att_cols], w_in[:, R_K:]] + [w_dt] * DT_COPIES
        + [jnp.zeros((D_MODEL, DT_PAD - DT_COPIES * SSD_HEADS), w_in.dtype)], axis=1)
    w_in_p = (mix_norm[:, None] * w_in_p).astype(BF16)
    gains = jnp.concatenate([ssd_norm, attn_out_norm[att_cols], sgu_out_norm])
    w_out_p = jnp.concatenate([w_out[:Y_ATT], w_out[Y_ATT + att_cols], w_out[Y_SGU:]], axis=0)
    w_out_p = (gains[:, None] * w_out_p).astype(BF16)
    rep_heads = lambda v: jnp.pad(jnp.tile(v, DT_COPIES), (0, DT_PAD - DT_COPIES * SSD_HEADS)).reshape(1, DT_PAD)
    row = lambda v: v.reshape(1, -1)
    tril = jnp.asarray(np.tril(np.ones((CHUNK, CHUNK), np.float32)), BF16)
    expand = np.zeros((DT_PAD, SSD_INNER), np.float32)
    for r in range(DT_COPIES * SSD_HEADS):
        h = r % SSD_HEADS
        expand[r, h * SSD_HEAD_DIM:(h + 1) * SSD_HEAD_DIM] = 1.0
    expand = jnp.asarray(expand, BF16)
    bucket = jnp.asarray(_folded_bucket_tile())
    sgu_b_e = jnp.repeat(jnp.transpose(sgu_b), GM_GROUP_DIM, axis=1)

    const2 = lambda t, *_: (0, 0)
    const3 = lambda t, *_: (0, 0, 0)
    full2 = lambda shape: pl.BlockSpec(shape, const2)
    projected = lambda t, *_: (jnp.minimum(t, n_quads - 1), 0)
    finished = lambda t, *_: (jnp.maximum(t - 1, 0), 0)
    grid_spec = pltpu.PrefetchScalarGridSpec(
        num_scalar_prefetch=2,
        grid=(n_quads + 1,),
        in_specs=[
            pl.BlockSpec((QUAD, D_MODEL), projected),
            pl.BlockSpec((QUAD, D_MODEL), finished),
            full2((D_MODEL, P_TOTAL)),
            full2((SSD_CONV, SSD_XBC)),
            full2((1, SSD_XBC)),
            full2((1, DT_PAD)),
            full2((1, DT_PAD)),
            full2((1, SSD_INNER)),
            full2((CHUNK, CHUNK)),
            full2((DT_PAD, SSD_INNER)),
            full2((CHUNK, CHUNK)),
            full2((1, GM_WIDTH)),
            full2((1, GM_WIDTH)),
            pl.BlockSpec((GM_GROUPS, CHUNK, CHUNK), const3),
            full2((CHUNK, GM_WIDTH)),
            full2((D_MODEL, D_MODEL)),
        ],
        out_specs=pl.BlockSpec((QUAD, D_MODEL), finished),
        scratch_shapes=[
            pltpu.VMEM((PAIR, P_TOTAL), F32),
            pltpu.VMEM((PAIR, P_TOTAL), F32),
            pltpu.VMEM((PAIR, D_MODEL), BF16),
            pltpu.VMEM((PAIR, D_MODEL), BF16),
            pltpu.VMEM((PAIR, D_MODEL), BF16),
            pltpu.VMEM((SUBLANES + CHUNK, SSD_XBC), F32),
            pltpu.VMEM((SSD_GROUPS, SSD_STATE, GROUP_WIDTH), F32),
            pltpu.VMEM((CHUNK, ATT_KV_WIDTH), BF16),
            pltpu.VMEM((CHUNK, ATT_KV_WIDTH), BF16),
            pltpu.VMEM((2, ATT_Q_HEADS, CHUNK, CHUNK), F32),
            pltpu.VMEM((CHUNK, GM_GROUPS * CHUNK), BF16),
        ],
    )
    return pl.pallas_call(
        functools.partial(_mixer_kernel, chunks_per_seq=chunks_per_seq),
        out_shape=jax.ShapeDtypeStruct(x2d.shape, F32),
        grid_spec=grid_spec,
        compiler_params=pltpu.CompilerParams(
            dimension_semantics=("arbitrary",), vmem_limit_bytes=VMEM_LIMIT_MIXER),
        name="token_mixer",
    )(sinks.astype(F32), rel_bias.reshape(-1).astype(F32),
      x2d, x2d, w_in_p, conv_w, row(conv_b), rep_heads(dt_bias), rep_heads(a_log),
      row(jnp.repeat(d_skip, SSD_HEAD_DIM)), tril, expand, bucket,
      row(ln_w), row(ln_b), sgu_w, sgu_b_e, w_out_p)


def kernel(x, ffn1_norm, ffn1_w_gate, ffn1_w_up, ffn1_w_down, mix_norm, w_in, conv_w, conv_b, dt_bias, a_log, d_skip, ssd_norm, attn_sinks, rel_bias, attn_out_norm, sgu_ln_w, sgu_ln_b, sgu_w, sgu_b, sgu_out_norm, w_out, ffn2_norm, ffn2_w_gate, ffn2_w_up, ffn2_w_down, final_norm):
    batch, seq, d = x.shape
    depth = w_in.shape[0]
    fw = final_norm.reshape(1, d)
    x2 = x.reshape(batch * seq, d)
    ffn1_gain = ffn1_norm.reshape(depth, 1, d)
    ffn2_gain = ffn2_norm.reshape(depth, 1, d)
    for l in range(depth):
        x2 = _ffn(x2, l, ffn1_gain, ffn1_w_gate, ffn1_w_up, ffn1_w_down, fw, final_norm=False)
        x2 = _mixer(x2, seq, mix_norm[l], w_in[l], conv_w[l], conv_b[l], dt_bias[l],
                    a_log[l], d_skip[l], ssd_norm[l], attn_sinks[l], rel_bias, attn_out_norm[l],
                    sgu_ln_w[l], sgu_ln_b[l], sgu_w[l], sgu_b[l], sgu_out_norm[l], w_out[l])
        x2 = _ffn(x2, l, ffn2_gain, ffn2_w_gate, ffn2_w_up, ffn2_w_down, fw,
                  final_norm=(l == depth - 1))
    return x2.reshape(batch, seq, d)
```

```python
import functools
import math

import jax
import jax.numpy as jnp
import numpy as np
from jax import lax
from jax.experimental import pallas as pl
from jax.experimental.pallas import tpu as pltpu

F32 = jnp.float32
BF16 = jnp.bfloat16

D_MODEL = 1024
D_FF = 2816
EPS = 1e-6

SSD_HEADS = 8
SSD_HEAD_DIM = 64
SSD_INNER = SSD_HEADS * SSD_HEAD_DIM
SSD_GROUPS = 2
SSD_STATE = 128
SSD_CONV = 4
SSD_XBC = SSD_INNER + 2 * SSD_GROUPS * SSD_STATE
HEADS_PER_GROUP = SSD_HEADS // SSD_GROUPS
GROUP_WIDTH = HEADS_PER_GROUP * SSD_HEAD_DIM

ATT_Q_HEADS = 4
ATT_KV_HEADS = 2
ATT_HEAD_DIM = 64
ATT_WIDTH = ATT_Q_HEADS * ATT_HEAD_DIM
ATT_KV_WIDTH = ATT_KV_HEADS * ATT_HEAD_DIM
REL_BUCKETS = 32
REL_MAX_DIST = 128
ATT_POS_HEADS = (0, 2, 1, 3)

GM_GROUPS = 4
GM_GROUP_DIM = 64
GM_WIDTH = GM_GROUPS * GM_GROUP_DIM

CHUNK = 128
PAIR = 2 * CHUNK
QUAD = 2 * PAIR
PROJ_PIECE = 256
MIX_SECTIONS = 14
MIX_LAG = 5

SUBLANES = 8
LANES = 128
HALF_LANES = LANES // 2
DT_PAD = LANES

P_Z = 0
P_XBC = P_Z + SSD_INNER
P_Q = P_XBC + SSD_XBC
P_K = P_Q + ATT_WIDTH
P_V = P_K + ATT_KV_WIDTH
P_U = P_V + ATT_KV_WIDTH
P_GV = P_U + GM_WIDTH
P_DT = P_GV + GM_WIDTH
P_TOTAL = P_DT + DT_PAD

R_XBC = SSD_INNER
R_DT = R_XBC + SSD_XBC
R_Q = R_DT + SSD_HEADS
R_K = R_Q + ATT_WIDTH

Y_ATT = SSD_INNER
Y_SGU = Y_ATT + ATT_WIDTH

MASKED = -1e30
LOG2E = math.log2(math.e)

FFN_TOKENS = 512
FFN_CHUNK = 256
FFN_PIECES = D_FF // FFN_CHUNK
VMEM_LIMIT_FFN = 48 * 1024 * 1024
VMEM_LIMIT_MIXER = 48 * 1024 * 1024


def _unit_rms(x):
    return x * lax.rsqrt(jnp.mean(x * x, axis=-1, keepdims=True) + EPS)


def _sigmoid(x):
    return 1.0 / (1.0 + jnp.exp(-x))


def _gelu_tanh(x):
    c = math.sqrt(2.0 / math.pi)
    return 0.5 * x * (1.0 + jnp.tanh(c * (x + 0.044715 * (x * x * x))))


def _top_bits(x):
    bits = lax.bitcast_convert_type(x, jnp.uint32) & jnp.uint32(0xFFFF0000)
    return lax.bitcast_convert_type(bits, F32)


def _split3(x):
    hi = _top_bits(x)
    r = x - hi
    mid = _top_bits(r)
    return hi, mid, r - mid


def _dot(a, b):
    return jnp.dot(a, b, preferred_element_type=F32)


def _dot_nt(a, b):
    return lax.dot_general(a, b, (((1,), (1,)), ((), ())), preferred_element_type=F32)


def _lane_cumsum(x, upper_ones):
    rows = x.shape[0]
    parts = _dot(jnp.concatenate(_split3(x), axis=0).astype(BF16), upper_ones)
    return parts[0:rows] + parts[rows:2 * rows] + parts[2 * rows:3 * rows]


def _ffn_kernel(x_ref, nw_ref, wg_ref, wu_ref, wd_ref, fw_ref, o_ref, wg_s, wu_s, wd_s, h_ref, *, final_norm):
    i = pl.program_id(0)

    @pl.when(i < FFN_PIECES)
    def _load_weight_slice():
        wg_s[i] = wg_ref[...].astype(BF16)
        wu_s[i] = wu_ref[...].astype(BF16)
        wd_s[i] = wd_ref[...].astype(BF16)

    @pl.when(i >= FFN_PIECES)
    def _token_rows():
        x = x_ref[...]
        xn = (_unit_rms(x) * nw_ref[...]).astype(BF16)
        for c in range(FFN_PIECES):
            g = _dot(xn, wg_s[c])
            u = _dot(xn, wu_s[c])
            h_ref[:, c * FFN_CHUNK:(c + 1) * FFN_CHUNK] = (g * _sigmoid(g) * u).astype(BF16)
        out = x + 0.5 * _dot(h_ref[...], wd_s[...].reshape(D_FF, D_MODEL))
        if final_norm:
            out = _unit_rms(out) * fw_ref[...]
        o_ref[...] = out


def _ffn(x2d, layer, norm_w, w_gate, w_up, w_down, final_w, *, final_norm):
    tokens = x2d.shape[0]
    last = FFN_PIECES - 1
    rows = lambda i: (jnp.maximum(i - FFN_PIECES, 0), 0)
    return pl.pallas_call(
        functools.partial(_ffn_kernel, final_norm=final_norm),
        out_shape=jax.ShapeDtypeStruct(x2d.shape, F32),
        grid=(FFN_PIECES + tokens // FFN_TOKENS,),
        in_specs=[
            pl.BlockSpec((FFN_TOKENS, D_MODEL), rows),
            pl.BlockSpec((None, 1, D_MODEL), lambda i: (layer, 0, 0)),
            pl.BlockSpec((None, D_MODEL, FFN_CHUNK), lambda i: (layer, 0, jnp.minimum(i, last))),
            pl.BlockSpec((None, D_MODEL, FFN_CHUNK), lambda i: (layer, 0, jnp.minimum(i, last))),
            pl.BlockSpec((None, FFN_CHUNK, D_MODEL), lambda i: (layer, jnp.minimum(i, last), 0)),
            pl.BlockSpec((1, D_MODEL), lambda i: (0, 0)),
        ],
        out_specs=pl.BlockSpec((FFN_TOKENS, D_MODEL), rows),
        scratch_shapes=[
            pltpu.VMEM((FFN_PIECES, D_MODEL, FFN_CHUNK), BF16),
            pltpu.VMEM((FFN_PIECES, D_MODEL, FFN_CHUNK), BF16),
            pltpu.VMEM((FFN_PIECES, FFN_CHUNK, D_MODEL), BF16),
            pltpu.VMEM((FFN_TOKENS, D_FF), BF16),
        ],
        compiler_params=pltpu.CompilerParams(
            dimension_semantics=("arbitrary",), vmem_limit_bytes=VMEM_LIMIT_FFN),
        name="ffn_halfstep",
    )(x2d, norm_w, w_gate, w_up, w_down, final_w)


def _mix_chunk(proj_ref, ycat_ref, keep, bias_idx, p):
    row = lax.broadcasted_iota(jnp.int32, (CHUNK, CHUNK), 0)
    lane = lax.broadcasted_iota(jnp.int32, (CHUNK, CHUNK), 1)
    causal = row >= lane
    row_b = row.astype(F32).astype(BF16)
    lane_b = lane.astype(F32).astype(BF16)
    causal_b = row_b >= lane_b
    upper_half_b = lane_b >= HALF_LANES
    zero_b = jnp.zeros((CHUNK, CHUNK), BF16)
    group_b = (lax.broadcasted_iota(jnp.int32, (CHUNK, GROUP_WIDTH), 1) // SSD_HEAD_DIM).astype(F32).astype(BF16)

    def block_diag(x_b):
        zero = jnp.zeros_like(x_b)
        return jnp.concatenate([jnp.where(group_b == i, x_b, zero) for i in range(HEADS_PER_GROUP)], axis=0)

    xbc_raw = proj_ref[:, P_XBC:P_XBC + SSD_XBC]
    if keep is not None:
        p.ext[0:SUBLANES, :] = p.ext[0:SUBLANES, :] * keep
    p.ext[SUBLANES:, :] = xbc_raw
    ext = p.ext[...]
    ext1 = pltpu.roll(ext, 1, 0)
    older = p.convw[1:2, :] * ext + p.convw[0:1, :] * ext1
    conv = (p.convb[...] + p.convw[3:4, :] * xbc_raw + p.convw[2:3, :] * ext1[SUBLANES:, :]
            + pltpu.roll(older, 2, 0)[SUBLANES:, :])
    p.ext[0:SUBLANES, :] = xbc_raw[CHUNK - SUBLANES:, :]
    yield
    xbc = conv * _sigmoid(conv)
    xs = xbc[:, :SSD_INNER]
    bm = xbc[:, SSD_INNER:SSD_INNER + SSD_GROUPS * SSD_STATE]
    cm = xbc[:, SSD_INNER + SSD_GROUPS * SSD_STATE:]
    x_b = xs.astype(BF16)
    bm_b = bm.astype(BF16)
    cm_b = cm.astype(BF16)
    yield

    dtr = proj_ref[:, P_DT:P_DT + DT_PAD].T[0:SSD_HEADS, :] + p.dtb[...]
    dt = jnp.maximum(dtr, 0.0) + jnp.log1p(jnp.exp(-jnp.abs(dtr)))
    acs = _lane_cumsum(dt * (-LOG2E * jnp.exp(p.alog[...])), p.triu[...])
    grow = jnp.exp2(acs)
    to_end = dt * jnp.exp2(acs[:, CHUNK - 1:CHUNK] - acs)
    src = acs - jnp.log2(dt)
    stack = jnp.concatenate(
        [acs, *_split3(grow), *_split3(to_end), jnp.zeros((CHUNK - 7 * SSD_HEADS, LANES), F32)], axis=0)
    cols = stack.T
    expanded = _dot(cols.astype(BF16), p.expand[...])
    grow_e = expanded[:, :SSD_INNER]
    to_end_e = expanded[:, SSD_INNER:]
    xd_b = (xs * to_end_e).astype(BF16)
    yield

    y_diag = []
    for g in range(SSD_GROUPS):
        n0 = g * SSD_STATE
        cb = _dot_nt(cm_b[:, n0:n0 + SSD_STATE], bm_b[:, n0:n0 + SSD_STATE])
        parts = []
        for hh in range(HEADS_PER_GROUP):
            h = g * HEADS_PER_GROUP + hh
            seg = cols[:, h:h + 1] - src[h:h + 1, :]
            parts.append((cb * jnp.exp2(jnp.where(causal, seg, -jnp.inf))).astype(BF16))
        w0 = g * GROUP_WIDTH
        y_diag.append(_dot(jnp.concatenate(parts, axis=-1), block_diag(x_b[:, w0:w0 + GROUP_WIDTH])))
        yield
    y_off = []
    for g in range(SSD_GROUPS):
        n0 = g * SSD_STATE
        w0 = g * GROUP_WIDTH
        prev = p.st[g]
        if keep is not None:
            prev = prev * keep
        y_off.append(_dot(cm_b[:, n0:n0 + SSD_STATE], prev.astype(BF16)))
        bt = bm[:, n0:n0 + SSD_STATE].T.astype(BF16)
        p.st[g] = (prev * grow_e[CHUNK - 1:CHUNK, w0:w0 + GROUP_WIDTH]
                   + _dot(bt, xd_b[:, w0:w0 + GROUP_WIDTH]))
    yield
    y = (jnp.concatenate(y_diag, axis=-1) + jnp.concatenate(y_off, axis=-1) * grow_e
         + xs * p.dskip[...])
    z = proj_ref[:, P_Z:P_Z + SSD_INNER]
    ycat_ref[:, 0:Y_ATT] = _unit_rms(y * (z * _sigmoid(z))).astype(BF16)
    yield

    q_b = (proj_ref[:, P_Q:P_Q + ATT_WIDTH] * (LOG2E / math.sqrt(ATT_HEAD_DIM))).astype(BF16)
    k_b = proj_ref[:, P_K:P_K + ATT_KV_WIDTH].astype(BF16)
    v_b = proj_ref[:, P_V:P_V + ATT_KV_WIDTH].astype(BF16)
    keys = jnp.concatenate([p.kprev[...], k_b], axis=0)
    v_prev = p.vprev[...]
    halves = lambda t: (jnp.where(upper_half_b, zero_b, t), jnp.where(upper_half_b, t, zero_b))
    v_prev_half, v_half = halves(v_prev), halves(v_b)
    probs, values = [], []
    for pos, hq in enumerate(ATT_POS_HEADS):
        tile, half = divmod(pos, 2)
        q_tile = q_b[:, tile * LANES:(tile + 1) * LANES]
        q_head = halves(q_tile)[half]
        s = _dot_nt(q_head, keys)
        s = jnp.where(causal, s[:, CHUNK:], s[:, :CHUNK]) + p.bias[bias_idx, pos]
        sink = p.sink[hq] * LOG2E
        m = jnp.maximum(jnp.max(s, axis=-1, keepdims=True), sink)
        e = jnp.exp2(s - m)
        denom = jnp.sum(e, axis=-1, keepdims=True) + jnp.exp2(sink - m)
        pn = (e * (1.0 / denom)).astype(BF16)
        probs.append(jnp.where(causal_b, zero_b, pn))
        probs.append(jnp.where(causal_b, pn, zero_b))
        for vh in (v_prev_half[half], v_half[half]):
            values.append(jnp.concatenate([zero_b, vh] if tile else [vh, zero_b], axis=-1))
        yield
    att = _dot(jnp.concatenate(probs, axis=-1), jnp.concatenate(values, axis=0))
    p.kprev[...] = k_b
    p.vprev[...] = v_b
    ycat_ref[:, Y_ATT:Y_SGU] = _unit_rms(att).astype(BF16)
    yield

    u = _gelu_tanh(proj_ref[:, P_U:P_U + GM_WIDTH])
    gv = _gelu_tanh(proj_ref[:, P_GV:P_GV + GM_WIDTH])
    yield
    mu = jnp.mean(gv, axis=-1, keepdims=True)
    gc = gv - mu
    gv = gc * lax.rsqrt(jnp.mean(gc * gc, axis=-1, keepdims=True) + EPS) * p.lnw[...] + p.lnb[...]
    gv_b = gv.astype(BF16)
    mixed = _dot(p.wcat[...], block_diag(gv_b)) + p.sgub[...]
    ycat_ref[:, Y_SGU:] = _unit_rms(u * mixed).astype(BF16)
    yield


class _MixerRefs:
    def __init__(self, **refs):
        self.__dict__.update(refs)


def _mixer_kernel(sink_ref, rel_ref,
                  xa_ref, xc_ref, win_ref, convw_ref, convb_ref, dtb_ref, alog_ref,
                  dskip_ref, triu_ref, expand_ref, bucket_ref,
                  lnw_ref, lnb_ref, sguw_ref, sgub_ref, wout_ref,
                  o_ref,
                  projx_ref, projy_ref, ycatx_ref, ycaty_ref, xn_ref,
                  ext_ref, st_ref, kprev_ref, vprev_ref, bias_ref, wcat_ref,
                  *, chunks_per_seq):
    t = pl.program_id(0)
    p = _MixerRefs(sink=sink_ref, convw=convw_ref, convb=convb_ref, dtb=dtb_ref, alog=alog_ref,
                   dskip=dskip_ref, triu=triu_ref, expand=expand_ref, lnw=lnw_ref, lnb=lnb_ref,
                   sgub=sgub_ref, ext=ext_ref, st=st_ref, kprev=kprev_ref, vprev=vprev_ref,
                   bias=bias_ref, wcat=wcat_ref)

    @pl.when(t == 0)
    def _init():
        row = lax.broadcasted_iota(jnp.int32, (CHUNK, CHUNK), 0)
        col = lax.broadcasted_iota(jnp.int32, (CHUNK, CHUNK), 1)
        bucket = bucket_ref[...]
        for pos, hq in enumerate(ATT_POS_HEADS):
            acc = jnp.zeros((CHUNK, CHUNK), F32)
            for k in range(REL_BUCKETS):
                acc = jnp.where(bucket == k, rel_ref[k * ATT_Q_HEADS + hq] * LOG2E, acc)
            bias_ref[0, pos] = acc
            bias_ref[1, pos] = jnp.where(row >= col, acc, MASKED)
        for g in range(GM_GROUPS):
            wcat_ref[:, g * CHUNK:(g + 1) * CHUNK] = jnp.where(row >= col, sguw_ref[g], 0.0).astype(BF16)
        projy_ref[...] = jnp.zeros_like(projy_ref)
        ycatx_ref[...] = jnp.zeros_like(ycatx_ref)
        st_ref[...] = jnp.zeros_like(st_ref)
        ext_ref[0:SUBLANES, :] = jnp.zeros((SUBLANES, SSD_XBC), F32)
        kprev_ref[...] = jnp.zeros_like(kprev_ref)
        vprev_ref[...] = jnp.zeros_like(vprev_ref)

    starts_seq = lax.rem(4 * t, chunks_per_seq) == 0
    keep = jnp.where(starts_seq, 0.0, 1.0).astype(F32)
    first_idx = jnp.where(starts_seq, 1, 0).astype(jnp.int32)
    even, odd = pl.ds(0, CHUNK), pl.ds(CHUNK, CHUNK)

    def projection_pieces(rows, proj_new, ycat_out):
        def normalize():
            xn_ref[...] = _unit_rms(xa_ref[rows, :]).astype(BF16)

        def project(c0, c1):
            proj_new[:, c0:c1] = _dot(xn_ref[...], win_ref[:, c0:c1])

        def output(c0, c1):
            o_ref[rows, c0:c1] = xc_ref[rows, c0:c1] + _dot(ycat_out[...], wout_ref[:, c0:c1])

        in_cols = list(range(0, P_TOTAL - PROJ_PIECE, PROJ_PIECE)) + [P_TOTAL]
        pieces = [normalize]
        pieces += [functools.partial(project, a, b) for a, b in zip(in_cols[:-1], in_cols[1:])]
        out_pieces = [functools.partial(output, c, c + PROJ_PIECE) for c in range(0, D_MODEL, PROJ_PIECE)]
        for i, piece in enumerate(out_pieces):
            pieces.insert(3 + 3 * i, piece)
        return pieces

    pieces1 = projection_pieces(pl.ds(0, PAIR), projx_ref, ycatx_ref)
    pieces2 = projection_pieces(pl.ds(PAIR, PAIR), projy_ref, ycaty_ref)
    mix2_start = max(len(pieces1), 2 * MIX_LAG)
    pieces2_start = MIX_LAG + MIX_SECTIONS
    mixes = [
        (0, _mix_chunk(projy_ref.at[even], ycaty_ref.at[even], None, 0, p)),
        (MIX_LAG, _mix_chunk(projy_ref.at[odd], ycaty_ref.at[odd], None, 0, p)),
        (mix2_start, _mix_chunk(projx_ref.at[even], ycatx_ref.at[even], keep, first_idx, p)),
        (mix2_start + MIX_LAG, _mix_chunk(projx_ref.at[odd], ycatx_ref.at[odd], None, 0, p)),
    ]
    piece_lists = [(0, pieces1), (pieces2_start, pieces2)]
    last_tick = max(mix2_start + MIX_LAG + MIX_SECTIONS, pieces2_start + len(pieces2))
    for tick in range(last_tick):
        for start, mix in mixes:
            if start <= tick < start + MIX_SECTIONS:
                next(mix)
        for start, pieces in piece_lists:
            if start <= tick < start + len(pieces):
                pieces[tick - start]()


def _folded_bucket_tile():
    i = np.arange(CHUNK)[:, None]
    j = np.arange(CHUNK)[None, :]
    n = np.where(j <= i, i - j, i - j + CHUNK)
    max_exact = REL_BUCKETS // 2
    large = max_exact + (np.log(np.maximum(n, 1) / max_exact) / np.log(REL_MAX_DIST / max_exact)
                         * (REL_BUCKETS - max_exact)).astype(np.int32)
    large = np.minimum(large, REL_BUCKETS - 1)
    return np.where(n < max_exact, n, large).astype(np.int32)


def _mixer(x2d, seq, mix_norm, w_in, conv_w, conv_b, dt_bias, a_log, d_skip, ssd_norm, sinks, rel_bias,
           attn_out_norm, ln_w, ln_b, sgu_w, sgu_b, sgu_out_norm, w_out):
    tokens = x2d.shape[0]
    n_quads = tokens // QUAD
    chunks_per_seq = seq // CHUNK
    assert tokens % QUAD == 0 and seq % QUAD == 0 and GM_GROUPS == HEADS_PER_GROUP
    att_cols = np.concatenate([np.arange(h * ATT_HEAD_DIM, (h + 1) * ATT_HEAD_DIM) for h in ATT_POS_HEADS])
    w_in_p = jnp.concatenate(
        [w_in[:, :R_DT], w_in[:, R_Q + att_cols], w_in[:, R_K:], w_in[:, R_DT:R_Q],
         jnp.zeros((D_MODEL, DT_PAD - SSD_HEADS), w_in.dtype)], axis=1)
    w_in_p = (mix_norm[:, None] * w_in_p).astype(BF16)
    gains = jnp.concatenate([ssd_norm, attn_out_norm[att_cols], sgu_out_norm])
    w_out_p = jnp.concatenate([w_out[:Y_ATT], w_out[Y_ATT + att_cols], w_out[Y_SGU:]], axis=0)
    w_out_p = (gains[:, None] * w_out_p).astype(BF16)
    per_head_rows = lambda v: jnp.broadcast_to(v[:, None], (SSD_HEADS, LANES))
    row = lambda v: v.reshape(1, -1)
    expand = np.zeros((CHUNK, 2 * SSD_INNER), np.float32)
    for r in range(SSD_HEADS, 7 * SSD_HEADS):
        h = r % SSD_HEADS
        c0 = ((r // SSD_HEADS - 1) // 3) * SSD_INNER + h * SSD_HEAD_DIM
        expand[r, c0:c0 + SSD_HEAD_DIM] = 1.0
    expand = jnp.asarray(expand, BF16)
    triu = jnp.asarray(np.triu(np.ones((CHUNK, CHUNK), np.float32)), BF16)
    bucket = jnp.asarray(_folded_bucket_tile())
    sgu_b_e = jnp.repeat(jnp.transpose(sgu_b), GM_GROUP_DIM, axis=1)

    const2 = lambda t, *_: (0, 0)
    const3 = lambda t, *_: (0, 0, 0)
    full2 = lambda shape: pl.BlockSpec(shape, const2)
    projected = lambda t, *_: (jnp.minimum(t, n_quads - 1), 0)
    finished = lambda t, *_: (jnp.maximum(t - 1, 0), 0)
    grid_spec = pltpu.PrefetchScalarGridSpec(
        num_scalar_prefetch=2,
        grid=(n_quads + 1,),
        in_specs=[
            pl.BlockSpec((QUAD, D_MODEL), projected),
            pl.BlockSpec((QUAD, D_MODEL), finished),
            full2((D_MODEL, P_TOTAL)),
            full2((SSD_CONV, SSD_XBC)),
            full2((1, SSD_XBC)),
            full2((SSD_HEADS, LANES)),
            full2((SSD_HEADS, LANES)),
            full2((1, SSD_INNER)),
            full2((CHUNK, CHUNK)),
            full2((CHUNK, 2 * SSD_INNER)),
            full2((CHUNK, CHUNK)),
            full2((1, GM_WIDTH)),
            full2((1, GM_WIDTH)),
            pl.BlockSpec((GM_GROUPS, CHUNK, CHUNK), const3),
            full2((CHUNK, GM_WIDTH)),
            full2((D_MODEL, D_MODEL)),
        ],
        out_specs=pl.BlockSpec((QUAD, D_MODEL), finished),
        scratch_shapes=[
            pltpu.VMEM((PAIR, P_TOTAL), F32),
            pltpu.VMEM((PAIR, P_TOTAL), F32),
            pltpu.VMEM((PAIR, D_MODEL), BF16),
            pltpu.VMEM((PAIR, D_MODEL), BF16),
            pltpu.VMEM((PAIR, D_MODEL), BF16),
            pltpu.VMEM((SUBLANES + CHUNK, SSD_XBC), F32),
            pltpu.VMEM((SSD_GROUPS, SSD_STATE, GROUP_WIDTH), F32),
            pltpu.VMEM((CHUNK, ATT_KV_WIDTH), BF16),
            pltpu.VMEM((CHUNK, ATT_KV_WIDTH), BF16),
            pltpu.VMEM((2, ATT_Q_HEADS, CHUNK, CHUNK), F32),
            pltpu.VMEM((CHUNK, GM_GROUPS * CHUNK), BF16),
        ],
    )
    return pl.pallas_call(
        functools.partial(_mixer_kernel, chunks_per_seq=chunks_per_seq),
        out_shape=jax.ShapeDtypeStruct(x2d.shape, F32),
        grid_spec=grid_spec,
        compiler_params=pltpu.CompilerParams(
            dimension_semantics=("arbitrary",), vmem_limit_bytes=VMEM_LIMIT_MIXER),
        name="token_mixer",
    )(sinks.astype(F32), rel_bias.reshape(-1).astype(F32),
      x2d, x2d, w_in_p, conv_w, row(conv_b), per_head_rows(dt_bias), per_head_rows(a_log),
      row(jnp.repeat(d_skip, SSD_HEAD_DIM)), triu, expand, bucket,
      row(ln_w), row(ln_b), sgu_w, sgu_b_e, w_out_p)


def kernel(x, ffn1_norm, ffn1_w_gate, ffn1_w_up, ffn1_w_down, mix_norm, w_in, conv_w, conv_b, dt_bias, a_log, d_skip, ssd_norm, attn_sinks, rel_bias, attn_out_norm, sgu_ln_w, sgu_ln_b, sgu_w, sgu_b, sgu_out_norm, w_out, ffn2_norm, ffn2_w_gate, ffn2_w_up, ffn2_w_down, final_norm):
    batch, seq, d = x.shape
    depth = w_in.shape[0]
    fw = final_norm.reshape(1, d)
    x2 = x.reshape(batch * seq, d)
    ffn1_gain = ffn1_norm.reshape(depth, 1, d)
    ffn2_gain = ffn2_norm.reshape(depth, 1, d)
    for l in range(depth):
        x2 = _ffn(x2, l, ffn1_gain, ffn1_w_gate, ffn1_w_up, ffn1_w_down, fw, final_norm=False)
        x2 = _mixer(x2, seq, mix_norm[l], w_in[l], conv_w[l], conv_b[l], dt_bias[l],
                    a_log[l], d_skip[l], ssd_norm[l], attn_sinks[l], rel_bias, attn_out_norm[l],
                    sgu_ln_w[l], sgu_ln_b[l], sgu_w[l], sgu_b[l], sgu_out_norm[l], w_out[l])
        x2 = _ffn(x2, l, ffn2_gain, ffn2_w_gate, ffn2_w_up, ffn2_w_down, fw,
                  final_norm=(l == depth - 1))
    return x2.reshape(batch, seq, d)
```

```python
import functools
import math

import jax
import jax.numpy as jnp
import numpy as np
from jax import lax
from jax.experimental import pallas as pl
from jax.experimental.pallas import tpu as pltpu

F32 = jnp.float32
BF16 = jnp.bfloat16

D_MODEL = 1024
D_FF = 2816
EPS = 1e-6

SSD_HEADS = 8
SSD_HEAD_DIM = 64
SSD_INNER = SSD_HEADS * SSD_HEAD_DIM
SSD_GROUPS = 2
SSD_STATE = 128
SSD_CONV = 4
SSD_XBC = SSD_INNER + 2 * SSD_GROUPS * SSD_STATE
HEADS_PER_GROUP = SSD_HEADS // SSD_GROUPS
GROUP_WIDTH = HEADS_PER_GROUP * SSD_HEAD_DIM

ATT_Q_HEADS = 4
ATT_KV_HEADS = 2
ATT_HEAD_DIM = 64
ATT_WIDTH = ATT_Q_HEADS * ATT_HEAD_DIM
ATT_KV_WIDTH = ATT_KV_HEADS * ATT_HEAD_DIM
REL_BUCKETS = 32
REL_MAX_DIST = 128
ATT_POS_HEADS = (0, 2, 1, 3)

GM_GROUPS = 4
GM_GROUP_DIM = 64
GM_WIDTH = GM_GROUPS * GM_GROUP_DIM

CHUNK = 128
PAIR = 2 * CHUNK
QUAD = 2 * PAIR
PROJ_PIECE = 256
MIX_SECTIONS = 14
MIX_LAG = 5

SUBLANES = 8
LANES = 128
HALF_LANES = LANES // 2
DT_PAD = LANES

P_Z = 0
P_XBC = P_Z + SSD_INNER
P_Q = P_XBC + SSD_XBC
P_K = P_Q + ATT_WIDTH
P_V = P_K + ATT_KV_WIDTH
P_U = P_V + ATT_KV_WIDTH
P_GV = P_U + GM_WIDTH
P_DT = P_GV + GM_WIDTH
P_TOTAL = P_DT + DT_PAD

R_XBC = SSD_INNER
R_DT = R_XBC + SSD_XBC
R_Q = R_DT + SSD_HEADS
R_K = R_Q + ATT_WIDTH

Y_ATT = SSD_INNER
Y_SGU = Y_ATT + ATT_WIDTH

MASKED = -1e30
LOG2E = math.log2(math.e)

FFN_TOKENS = 1024
FFN_CHUNK = 256
FFN_PIECES = D_FF // FFN_CHUNK
VMEM_LIMIT_FFN = 56 * 1024 * 1024
VMEM_LIMIT_MIXER = 48 * 1024 * 1024


def _unit_rms(x):
    return x * lax.rsqrt(jnp.mean(x * x, axis=-1, keepdims=True) + EPS)


def _sigmoid(x):
    return 1.0 / (1.0 + jnp.exp(-x))


def _gelu_tanh(x):
    c = math.sqrt(2.0 / math.pi)
    return 0.5 * x * (1.0 + jnp.tanh(c * (x + 0.044715 * (x * x * x))))


def _top_bits(x):
    bits = lax.bitcast_convert_type(x, jnp.uint32) & jnp.uint32(0xFFFF0000)
    return lax.bitcast_convert_type(bits, F32)


def _split3(x):
    hi = _top_bits(x)
    r = x - hi
    mid = _top_bits(r)
    return hi, mid, r - mid


def _dot(a, b):
    return jnp.dot(a, b, preferred_element_type=F32)


def _dot_nt(a, b):
    return lax.dot_general(a, b, (((1,), (1,)), ((), ())), preferred_element_type=F32)


def _lane_cumsum(x, upper_ones):
    rows = x.shape[0]
    parts = _dot(jnp.concatenate(_split3(x), axis=0).astype(BF16), upper_ones)
    return parts[0:rows] + parts[rows:2 * rows] + parts[2 * rows:3 * rows]


def _ffn_kernel(x_ref, nw_ref, wg_ref, wu_ref, wd_ref, fw_ref, o_ref, wg_s, wu_s, wd_s, h_ref, *, final_norm):
    i = pl.program_id(0)

    @pl.when(i < FFN_PIECES)
    def _load_weight_slice():
        wg_s[i] = wg_ref[...].astype(BF16)
        wu_s[i] = wu_ref[...].astype(BF16)
        wd_s[i] = wd_ref[...].astype(BF16)

    @pl.when(i >= FFN_PIECES)
    def _token_rows():
        x = x_ref[...]
        xn = (_unit_rms(x) * nw_ref[...]).astype(BF16)
        for c in range(FFN_PIECES):
            g = _dot(xn, wg_s[c])
            u = _dot(xn, wu_s[c])
            h_ref[:, c * FFN_CHUNK:(c + 1) * FFN_CHUNK] = (g * _sigmoid(g) * u).astype(BF16)
        out = x + 0.5 * _dot(h_ref[...], wd_s[...].reshape(D_FF, D_MODEL))
        if final_norm:
            out = _unit_rms(out) * fw_ref[...]
        o_ref[...] = out


def _ffn(x2d, layer, norm_w, w_gate, w_up, w_down, final_w, *, final_norm):
    tokens = x2d.shape[0]
    last = FFN_PIECES - 1
    rows = lambda i: (jnp.maximum(i - FFN_PIECES, 0), 0)
    return pl.pallas_call(
        functools.partial(_ffn_kernel, final_norm=final_norm),
        out_shape=jax.ShapeDtypeStruct(x2d.shape, F32),
        grid=(FFN_PIECES + tokens // FFN_TOKENS,),
        in_specs=[
            pl.BlockSpec((FFN_TOKENS, D_MODEL), rows),
            pl.BlockSpec((None, 1, D_MODEL), lambda i: (layer, 0, 0)),
            pl.BlockSpec((None, D_MODEL, FFN_CHUNK), lambda i: (layer, 0, jnp.minimum(i, last))),
            pl.BlockSpec((None, D_MODEL, FFN_CHUNK), lambda i: (layer, 0, jnp.minimum(i, last))),
            pl.BlockSpec((None, FFN_CHUNK, D_MODEL), lambda i: (layer, jnp.minimum(i, last), 0)),
            pl.BlockSpec((1, D_MODEL), lambda i: (0, 0)),
        ],
        out_specs=pl.BlockSpec((FFN_TOKENS, D_MODEL), rows),
        scratch_shapes=[
            pltpu.VMEM((FFN_PIECES, D_MODEL, FFN_CHUNK), BF16),
            pltpu.VMEM((FFN_PIECES, D_MODEL, FFN_CHUNK), BF16),
            pltpu.VMEM((FFN_PIECES, FFN_CHUNK, D_MODEL), BF16),
            pltpu.VMEM((FFN_TOKENS, D_FF), BF16),
        ],
        compiler_params=pltpu.CompilerParams(
            dimension_semantics=("arbitrary",), vmem_limit_bytes=VMEM_LIMIT_FFN),
        name="ffn_halfstep",
    )(x2d, norm_w, w_gate, w_up, w_down, final_w)


def _mix_chunk(proj_ref, ycat_ref, keep, bias_idx, p):
    row = lax.broadcasted_iota(jnp.int32, (CHUNK, CHUNK), 0)
    lane = lax.broadcasted_iota(jnp.int32, (CHUNK, CHUNK), 1)
    causal = row >= lane
    row_b = row.astype(F32).astype(BF16)
    lane_b = lane.astype(F32).astype(BF16)
    causal_b = row_b >= lane_b
    upper_half_b = lane_b >= HALF_LANES
    zero_b = jnp.zeros((CHUNK, CHUNK), BF16)
    group_b = (lax.broadcasted_iota(jnp.int32, (CHUNK, GROUP_WIDTH), 1) // SSD_HEAD_DIM).astype(F32).astype(BF16)

    def block_diag(x_b):
        zero = jnp.zeros_like(x_b)
        return jnp.concatenate([jnp.where(group_b == i, x_b, zero) for i in range(HEADS_PER_GROUP)], axis=0)

    xbc_raw = proj_ref[:, P_XBC:P_XBC + SSD_XBC]
    if keep is not None:
        p.ext[0:SUBLANES, :] = p.ext[0:SUBLANES, :] * keep
    p.ext[SUBLANES:, :] = xbc_raw
    ext = p.ext[...]
    ext1 = pltpu.roll(ext, 1, 0)
    older = p.convw[1:2, :] * ext + p.convw[0:1, :] * ext1
    conv = (p.convb[...] + p.convw[3:4, :] * xbc_raw + p.convw[2:3, :] * ext1[SUBLANES:, :]
            + pltpu.roll(older, 2, 0)[SUBLANES:, :])
    p.ext[0:SUBLANES, :] = xbc_raw[CHUNK - SUBLANES:, :]
    yield
    xbc = conv * _sigmoid(conv)
    xs = xbc[:, :SSD_INNER]
    bm = xbc[:, SSD_INNER:SSD_INNER + SSD_GROUPS * SSD_STATE]
    cm = xbc[:, SSD_INNER + SSD_GROUPS * SSD_STATE:]
    x_b = xs.astype(BF16)
    bm_b = bm.astype(BF16)
    cm_b = cm.astype(BF16)
    yield

    dtr = proj_ref[:, P_DT:P_DT + DT_PAD].T[0:SSD_HEADS, :] + p.dtb[...]
    dt = jnp.maximum(dtr, 0.0) + jnp.log1p(jnp.exp(-jnp.abs(dtr)))
    acs = _lane_cumsum(dt * (-LOG2E * jnp.exp(p.alog[...])), p.triu[...])
    grow = jnp.exp2(acs)
    to_end = dt * jnp.exp2(acs[:, CHUNK - 1:CHUNK] - acs)
    src = acs - jnp.log2(dt)
    stack = jnp.concatenate(
        [acs, *_split3(grow), *_split3(to_end), jnp.zeros((CHUNK - 7 * SSD_HEADS, LANES), F32)], axis=0)
    cols = stack.T
    expanded = _dot(cols.astype(BF16), p.expand[...])
    grow_e = expanded[:, :SSD_INNER]
    to_end_e = expanded[:, SSD_INNER:]
    xd_b = (xs * to_end_e).astype(BF16)
    yield

    y_diag = []
    for g in range(SSD_GROUPS):
        n0 = g * SSD_STATE
        cb = _dot_nt(cm_b[:, n0:n0 + SSD_STATE], bm_b[:, n0:n0 + SSD_STATE])
        parts = []
        for hh in range(HEADS_PER_GROUP):
            h = g * HEADS_PER_GROUP + hh
            seg = cols[:, h:h + 1] - src[h:h + 1, :]
            parts.append((cb * jnp.exp2(jnp.where(causal, seg, -jnp.inf))).astype(BF16))
        w0 = g * GROUP_WIDTH
        y_diag.append(_dot(jnp.concatenate(parts, axis=-1), block_diag(x_b[:, w0:w0 + GROUP_WIDTH])))
        yield
    y_off = []
    for g in range(SSD_GROUPS):
        n0 = g * SSD_STATE
        w0 = g * GROUP_WIDTH
        prev = p.st[g]
        if keep is not None:
            prev = prev * keep
        y_off.append(_dot(cm_b[:, n0:n0 + SSD_STATE], prev.astype(BF16)))
        bt = bm[:, n0:n0 + SSD_STATE].T.astype(BF16)
        p.st[g] = (prev * grow_e[CHUNK - 1:CHUNK, w0:w0 + GROUP_WIDTH]
                   + _dot(bt, xd_b[:, w0:w0 + GROUP_WIDTH]))
    yield
    y = (jnp.concatenate(y_diag, axis=-1) + jnp.concatenate(y_off, axis=-1) * grow_e
         + xs * p.dskip[...])
    z = proj_ref[:, P_Z:P_Z + SSD_INNER]
    ycat_ref[:, 0:Y_ATT] = _unit_rms(y * (z * _sigmoid(z))).astype(BF16)
    yield

    q_b = (proj_ref[:, P_Q:P_Q + ATT_WIDTH] * (LOG2E / math.sqrt(ATT_HEAD_DIM))).astype(BF16)
    k_b = proj_ref[:, P_K:P_K + ATT_KV_WIDTH].astype(BF16)
    v_b = proj_ref[:, P_V:P_V + ATT_KV_WIDTH].astype(BF16)
    keys = jnp.concatenate([p.kprev[...], k_b], axis=0)
    v_prev = p.vprev[...]
    halves = lambda t: (jnp.where(upper_half_b, zero_b, t), jnp.where(upper_half_b, t, zero_b))
    v_prev_half, v_half = halves(v_prev), halves(v_b)
    probs, values = [], []
    for pos, hq in enumerate(ATT_POS_HEADS):
        tile, half = divmod(pos, 2)
        q_tile = q_b[:, tile * LANES:(tile + 1) * LANES]
        q_head = halves(q_tile)[half]
        s = _dot_nt(q_head, keys)
        s = jnp.where(causal, s[:, CHUNK:], s[:, :CHUNK]) + p.bias[bias_idx, pos]
        sink = p.sink[hq] * LOG2E
        m = jnp.maximum(jnp.max(s, axis=-1, keepdims=True), sink)
        e = jnp.exp2(s - m)
        denom = jnp.sum(e, axis=-1, keepdims=True) + jnp.exp2(sink - m)
        pn = (e * (1.0 / denom)).astype(BF16)
        probs.append(jnp.where(causal_b, zero_b, pn))
        probs.append(jnp.where(causal_b, pn, zero_b))
        for vh in (v_prev_half[half], v_half[half]):
            values.append(jnp.concatenate([zero_b, vh] if tile else [vh, zero_b], axis=-1))
        yield
    att = _dot(jnp.concatenate(probs, axis=-1), jnp.concatenate(values, axis=0))
    p.kprev[...] = k_b
    p.vprev[...] = v_b
    ycat_ref[:, Y_ATT:Y_SGU] = _unit_rms(att).astype(BF16)
    yield

    u = _gelu_tanh(proj_ref[:, P_U:P_U + GM_WIDTH])
    gv = _gelu_tanh(proj_ref[:, P_GV:P_GV + GM_WIDTH])
    yield
    mu = jnp.mean(gv, axis=-1, keepdims=True)
    gc = gv - mu
    gv = gc * lax.rsqrt(jnp.mean(gc * gc, axis=-1, keepdims=True) + EPS) * p.lnw[...] + p.lnb[...]
    gv_b = gv.astype(BF16)
    mixed = _dot(p.wcat[...], block_diag(gv_b)) + p.sgub[...]
    ycat_ref[:, Y_SGU:] = _unit_rms(u * mixed).astype(BF16)
    yield


class _MixerRefs:
    def __init__(self, **refs):
        self.__dict__.update(refs)


def _mixer_kernel(sink_ref, rel_ref,
                  xa_ref, xc_ref, ing_ref, win_ref, convw_ref, convb_ref, dtb_ref, alog_ref,
                  dskip_ref, triu_ref, expand_ref, bucket_ref,
                  lnw_ref, lnb_ref, sguw_ref, sgub_ref, outg_ref, wout_ref,
                  o_ref,
                  win_s, wout_s, projx_ref, projy_ref, ycatx_ref, ycaty_ref, xn_ref,
                  ext_ref, st_ref, kprev_ref, vprev_ref, bias_ref, wcat_ref,
                  *, chunks_per_seq):
    t = pl.program_id(0)
    p = _MixerRefs(sink=sink_ref, convw=convw_ref, convb=convb_ref, dtb=dtb_ref, alog=alog_ref,
                   dskip=dskip_ref, triu=triu_ref, expand=expand_ref, lnw=lnw_ref, lnb=lnb_ref,
                   sgub=sgub_ref, ext=ext_ref, st=st_ref, kprev=kprev_ref, vprev=vprev_ref,
                   bias=bias_ref, wcat=wcat_ref)

    @pl.when(t == 0)
    def _init():
        row = lax.broadcasted_iota(jnp.int32, (CHUNK, CHUNK), 0)
        col = lax.broadcasted_iota(jnp.int32, (CHUNK, CHUNK), 1)
        bucket = bucket_ref[...]
        for pos, hq in enumerate(ATT_POS_HEADS):
            acc = jnp.zeros((CHUNK, CHUNK), F32)
            for k in range(REL_BUCKETS):
                acc = jnp.where(bucket == k, rel_ref[k * ATT_Q_HEADS + hq] * LOG2E, acc)
            bias_ref[0, pos] = acc
            bias_ref[1, pos] = jnp.where(row >= col, acc, MASKED)
        for g in range(GM_GROUPS):
            wcat_ref[:, g * CHUNK:(g + 1) * CHUNK] = jnp.where(row >= col, sguw_ref[g], 0.0).astype(BF16)

        def pack_rows(i, carry):
            r = pl.ds(pl.multiple_of(i * CHUNK, CHUNK), CHUNK)
            gain = ing_ref[r, :]
            win_s[r, 0:R_DT] = (win_ref[r, 0:R_DT] * gain).astype(BF16)
            tail = win_ref[r, R_DT:] * gain
            q0 = SSD_HEADS
            q = [tail[:, q0 + h * ATT_HEAD_DIM:q0 + (h + 1) * ATT_HEAD_DIM] for h in ATT_POS_HEADS]
            dt_tile = jnp.concatenate([tail[:, :q0], jnp.zeros((CHUNK, DT_PAD - SSD_HEADS), F32)], axis=1)
            win_s[r, R_DT:] = jnp.concatenate(q + [tail[:, q0 + ATT_WIDTH:], dt_tile], axis=1).astype(BF16)
            return carry

        lax.fori_loop(0, D_MODEL // CHUNK, pack_rows, 0)
        blocks = D_MODEL // ATT_HEAD_DIM
        first_att = Y_ATT // ATT_HEAD_DIM
        for j in range(blocks):
            src = first_att + ATT_POS_HEADS[j - first_att] if first_att <= j < first_att + ATT_Q_HEADS else j
            dst_rows = slice(j * ATT_HEAD_DIM, (j + 1) * ATT_HEAD_DIM)
            src_rows = slice(src * ATT_HEAD_DIM, (src + 1) * ATT_HEAD_DIM)
            wout_s[dst_rows, :] = (wout_ref[src_rows, :] * outg_ref[dst_rows, :]).astype(BF16)
        projy_ref[...] = jnp.zeros_like(projy_ref)
        ycatx_ref[...] = jnp.zeros_like(ycatx_ref)
        st_ref[...] = jnp.zeros_like(st_ref)
        ext_ref[0:SUBLANES, :] = jnp.zeros((SUBLANES, SSD_XBC), F32)
        kprev_ref[...] = jnp.zeros_like(kprev_ref)
        vprev_ref[...] = jnp.zeros_like(vprev_ref)

    starts_seq = lax.rem(4 * t, chunks_per_seq) == 0
    keep = jnp.where(starts_seq, 0.0, 1.0).astype(F32)
    first_idx = jnp.where(starts_seq, 1, 0).astype(jnp.int32)
    even, odd = pl.ds(0, CHUNK), pl.ds(CHUNK, CHUNK)

    def projection_pieces(rows, proj_new, ycat_out):
        def normalize():
            xn_ref[...] = _unit_rms(xa_ref[rows, :]).astype(BF16)

        def project(c0, c1):
            proj_new[:, c0:c1] = _dot(xn_ref[...], win_s[:, c0:c1])

        def output(c0, c1):
            o_ref[rows, c0:c1] = xc_ref[rows, c0:c1] + _dot(ycat_out[...], wout_s[:, c0:c1])

        in_cols = list(range(0, P_TOTAL - PROJ_PIECE, PROJ_PIECE)) + [P_TOTAL]
        pieces = [normalize]
        pieces += [functools.partial(project, a, b) for a, b in zip(in_cols[:-1], in_cols[1:])]
        out_pieces = [functools.partial(output, c, c + PROJ_PIECE) for c in range(0, D_MODEL, PROJ_PIECE)]
        for i, piece in enumerate(out_pieces):
            pieces.insert(3 + 3 * i, piece)
        return pieces

    pieces1 = projection_pieces(pl.ds(0, PAIR), projx_ref, ycatx_ref)
    pieces2 = projection_pieces(pl.ds(PAIR, PAIR), projy_ref, ycaty_ref)
    mix2_start = max(len(pieces1), 2 * MIX_LAG)
    pieces2_start = MIX_LAG + MIX_SECTIONS
    mixes = [
        (0, _mix_chunk(projy_ref.at[even], ycaty_ref.at[even], None, 0, p)),
        (MIX_LAG, _mix_chunk(projy_ref.at[odd], ycaty_ref.at[odd], None, 0, p)),
        (mix2_start, _mix_chunk(projx_ref.at[even], ycatx_ref.at[even], keep, first_idx, p)),
        (mix2_start + MIX_LAG, _mix_chunk(projx_ref.at[odd], ycatx_ref.at[odd], None, 0, p)),
    ]
    piece_lists = [(0, pieces1), (pieces2_start, pieces2)]
    last_tick = max(mix2_start + MIX_LAG + MIX_SECTIONS, pieces2_start + len(pieces2))
    for tick in range(last_tick):
        for start, mix in mixes:
            if start <= tick < start + MIX_SECTIONS:
                next(mix)
        for start, pieces in piece_lists:
            if start <= tick < start + len(pieces):
                pieces[tick - start]()


def _folded_bucket_tile():
    i = np.arange(CHUNK)[:, None]
    j = np.arange(CHUNK)[None, :]
    n = np.where(j <= i, i - j, i - j + CHUNK)
    max_exact = REL_BUCKETS // 2
    large = max_exact + (np.log(np.maximum(n, 1) / max_exact) / np.log(REL_MAX_DIST / max_exact)
                         * (REL_BUCKETS - max_exact)).astype(np.int32)
    large = np.minimum(large, REL_BUCKETS - 1)
    return np.where(n < max_exact, n, large).astype(np.int32)


def _mixer(x2d, seq, layer, in_gain, w_in, conv_w, conv_b, dt_bias, a_log, d_skip, sinks, rel_bias,
           ln_w, ln_b, sgu_w, sgu_b, out_gain, w_out):
    tokens = x2d.shape[0]
    n_quads = tokens // QUAD
    chunks_per_seq = seq // CHUNK
    assert tokens % QUAD == 0 and seq % QUAD == 0 and GM_GROUPS == HEADS_PER_GROUP
    assert P_Q == R_DT and w_in.shape[-1] == R_K + P_DT - P_K
    per_head_rows = lambda v: jnp.broadcast_to(v[:, None], (SSD_HEADS, LANES))
    row = lambda v: v.reshape(1, -1)
    expand = np.zeros((CHUNK, 2 * SSD_INNER), np.float32)
    for r in range(SSD_HEADS, 7 * SSD_HEADS):
        h = r % SSD_HEADS
        c0 = ((r // SSD_HEADS - 1) // 3) * SSD_INNER + h * SSD_HEAD_DIM
        expand[r, c0:c0 + SSD_HEAD_DIM] = 1.0
    expand = jnp.asarray(expand, BF16)
    triu = jnp.asarray(np.triu(np.ones((CHUNK, CHUNK), np.float32)), BF16)
    bucket = jnp.asarray(_folded_bucket_tile())
    sgu_b_e = jnp.repeat(jnp.transpose(sgu_b), GM_GROUP_DIM, axis=1)

    const2 = lambda t, *_: (0, 0)
    const3 = lambda t, *_: (0, 0, 0)
    full2 = lambda shape: pl.BlockSpec(shape, const2)
    layer_block = lambda shape: pl.BlockSpec((None,) + shape, lambda t, *_: (layer, 0, 0),
                                             pipeline_mode=pl.Buffered(1))
    projected = lambda t, *_: (jnp.minimum(t, n_quads - 1), 0)
    finished = lambda t, *_: (jnp.maximum(t - 1, 0), 0)
    grid_spec = pltpu.PrefetchScalarGridSpec(
        num_scalar_prefetch=2,
        grid=(n_quads + 1,),
        in_specs=[
            pl.BlockSpec((QUAD, D_MODEL), projected),
            pl.BlockSpec((QUAD, D_MODEL), finished),
            layer_block((D_MODEL, 1)),
            layer_block((D_MODEL, w_in.shape[-1])),
            full2((SSD_CONV, SSD_XBC)),
            full2((1, SSD_XBC)),
            full2((SSD_HEADS, LANES)),
            full2((SSD_HEADS, LANES)),
            full2((1, SSD_INNER)),
            full2((CHUNK, CHUNK)),
            full2((CHUNK, 2 * SSD_INNER)),
            full2((CHUNK, CHUNK)),
            full2((1, GM_WIDTH)),
            full2((1, GM_WIDTH)),
            pl.BlockSpec((GM_GROUPS, CHUNK, CHUNK), const3),
            full2((CHUNK, GM_WIDTH)),
            layer_block((D_MODEL, 1)),
            layer_block((D_MODEL, D_MODEL)),
        ],
        out_specs=pl.BlockSpec((QUAD, D_MODEL), finished),
        scratch_shapes=[
            pltpu.VMEM((D_MODEL, P_TOTAL), BF16),
            pltpu.VMEM((D_MODEL, D_MODEL), BF16),
            pltpu.VMEM((PAIR, P_TOTAL), F32),
            pltpu.VMEM((PAIR, P_TOTAL), F32),
            pltpu.VMEM((PAIR, D_MODEL), BF16),
            pltpu.VMEM((PAIR, D_MODEL), BF16),
            pltpu.VMEM((PAIR, D_MODEL), BF16),
            pltpu.VMEM((SUBLANES + CHUNK, SSD_XBC), F32),
            pltpu.VMEM((SSD_GROUPS, SSD_STATE, GROUP_WIDTH), F32),
            pltpu.VMEM((CHUNK, ATT_KV_WIDTH), BF16),
            pltpu.VMEM((CHUNK, ATT_KV_WIDTH), BF16),
            pltpu.VMEM((2, ATT_Q_HEADS, CHUNK, CHUNK), F32),
            pltpu.VMEM((CHUNK, GM_GROUPS * CHUNK), BF16),
        ],
    )
    return pl.pallas_call(
        functools.partial(_mixer_kernel, chunks_per_seq=chunks_per_seq),
        out_shape=jax.ShapeDtypeStruct(x2d.shape, F32),
        grid_spec=grid_spec,
        compiler_params=pltpu.CompilerParams(
            dimension_semantics=("arbitrary",), vmem_limit_bytes=VMEM_LIMIT_MIXER),
        name="token_mixer",
    )(sinks.astype(F32), rel_bias.reshape(-1).astype(F32),
      x2d, x2d, in_gain, w_in, conv_w, row(conv_b), per_head_rows(dt_bias), per_head_rows(a_log),
      row(jnp.repeat(d_skip, SSD_HEAD_DIM)), triu, expand, bucket,
      row(ln_w), row(ln_b), sgu_w, sgu_b_e, out_gain, w_out)


def kernel(x, ffn1_norm, ffn1_w_gate, ffn1_w_up, ffn1_w_down, mix_norm, w_in, conv_w, conv_b, dt_bias, a_log, d_skip, ssd_norm, attn_sinks, rel_bias, attn_out_norm, sgu_ln_w, sgu_ln_b, sgu_w, sgu_b, sgu_out_norm, w_out, ffn2_norm, ffn2_w_gate, ffn2_w_up, ffn2_w_down, final_norm):
    batch, seq, d = x.shape
    depth = w_in.shape[0]
    fw = final_norm.reshape(1, d)
    x2 = x.reshape(batch * seq, d)
    ffn1_gain = ffn1_norm.reshape(depth, 1, d)
    ffn2_gain = ffn2_norm.reshape(depth, 1, d)
    mix_in_gain = mix_norm.reshape(depth, d, 1)
    att_cols = np.concatenate([np.arange(h * ATT_HEAD_DIM, (h + 1) * ATT_HEAD_DIM) for h in ATT_POS_HEADS])
    mix_out_gain = jnp.concatenate([ssd_norm, attn_out_norm[:, att_cols], sgu_out_norm], axis=1).reshape(depth, d, 1)
    for l in range(depth):
        x2 = _ffn(x2, l, ffn1_gain, ffn1_w_gate, ffn1_w_up, ffn1_w_down, fw, final_norm=False)
        x2 = _mixer(x2, seq, l, mix_in_gain, w_in, conv_w[l], conv_b[l], dt_bias[l],
                    a_log[l], d_skip[l], attn_sinks[l], rel_bias,
                    sgu_ln_w[l], sgu_ln_b[l], sgu_w[l], sgu_b[l], mix_out_gain, w_out)
        x2 = _ffn(x2, l, ffn2_gain, ffn2_w_gate, ffn2_w_up, ffn2_w_down, fw,
                  final_norm=(l == depth - 1))
    return x2.reshape(batch, seq, d)
```

```python
import functools
import math

import jax
import jax.numpy as jnp
import numpy as np
from jax import lax
from jax.experimental import pallas as pl
from jax.experimental.pallas import tpu as pltpu

F32 = jnp.float32
BF16 = jnp.bfloat16

D_MODEL = 1024
D_FF = 2816
EPS = 1e-6

SSD_HEADS = 8
SSD_HEAD_DIM = 64
SSD_INNER = SSD_HEADS * SSD_HEAD_DIM
SSD_GROUPS = 2
SSD_STATE = 128
SSD_CONV = 4
SSD_XBC = SSD_INNER + 2 * SSD_GROUPS * SSD_STATE
HEADS_PER_GROUP = SSD_HEADS // SSD_GROUPS
GROUP_WIDTH = HEADS_PER_GROUP * SSD_HEAD_DIM

ATT_Q_HEADS = 4
ATT_KV_HEADS = 2
ATT_HEAD_DIM = 64
ATT_WIDTH = ATT_Q_HEADS * ATT_HEAD_DIM
ATT_KV_WIDTH = ATT_KV_HEADS * ATT_HEAD_DIM
REL_BUCKETS = 32
REL_MAX_DIST = 128
ATT_POS_HEADS = (0, 2, 1, 3)

GM_GROUPS = 4
GM_GROUP_DIM = 64
GM_WIDTH = GM_GROUPS * GM_GROUP_DIM

CHUNK = 128
PAIR = 2 * CHUNK
QUAD = 2 * PAIR
PROJ_PIECE = 256
CONV_COLS = 256
MIX_SECTIONS = 18
MIX_LAG = 3
PIECES_PER_TICK = 1

SUBLANES = 8
LANES = 128
HALF_LANES = LANES // 2
DT_PAD = LANES

P_Z = 0
P_XBC = P_Z + SSD_INNER
P_Q = P_XBC + SSD_XBC
P_K = P_Q + ATT_WIDTH
P_V = P_K + ATT_KV_WIDTH
P_U = P_V + ATT_KV_WIDTH
P_GV = P_U + GM_WIDTH
P_DT = P_GV + GM_WIDTH
P_TOTAL = P_DT + DT_PAD

R_XBC = SSD_INNER
R_DT = R_XBC + SSD_XBC
R_Q = R_DT + SSD_HEADS
R_K = R_Q + ATT_WIDTH

Y_ATT = SSD_INNER
Y_SGU = Y_ATT + ATT_WIDTH

MASKED = -1e30
LOG2E = math.log2(math.e)

FFN_TOKENS = 1024
FFN_CHUNK = 256
FFN_PIECES = D_FF // FFN_CHUNK
VMEM_LIMIT_FFN = 56 * 1024 * 1024
VMEM_LIMIT_MIXER = 48 * 1024 * 1024


def _unit_rms(x):
    return x * lax.rsqrt(jnp.mean(x * x, axis=-1, keepdims=True) + EPS)


def _sigmoid(x):
    return 1.0 / (1.0 + jnp.exp(-x))


def _gelu_tanh(x):
    c = math.sqrt(2.0 / math.pi)
    return 0.5 * x * (1.0 + jnp.tanh(c * (x + 0.044715 * (x * x * x))))


def _top_bits(x):
    bits = lax.bitcast_convert_type(x, jnp.uint32) & jnp.uint32(0xFFFF0000)
    return lax.bitcast_convert_type(bits, F32)


def _split3(x):
    hi = _top_bits(x)
    r = x - hi
    mid = _top_bits(r)
    return hi, mid, r - mid


def _dot(a, b):
    return jnp.dot(a, b, preferred_element_type=F32)


def _dot_nt(a, b):
    return lax.dot_general(a, b, (((1,), (1,)), ((), ())), preferred_element_type=F32)


def _lane_cumsum(x, upper_ones):
    rows = x.shape[0]
    parts = _dot(jnp.concatenate(_split3(x), axis=0).astype(BF16), upper_ones)
    return parts[0:rows] + parts[rows:2 * rows] + parts[2 * rows:3 * rows]


def _ffn_kernel(x_ref, nw_ref, wg_ref, wu_ref, wd_ref, fw_ref, o_ref, wg_s, wu_s, wd_s, h_ref, *, final_norm):
    i = pl.program_id(0)

    @pl.when(i < FFN_PIECES)
    def _load_weight_slice():
        wg_s[i] = wg_ref[...].astype(BF16)
        wu_s[i] = wu_ref[...].astype(BF16)
        wd_s[i] = wd_ref[...].astype(BF16)

    @pl.when(i >= FFN_PIECES)
    def _token_rows():
        x = x_ref[...]
        xn = (_unit_rms(x) * nw_ref[...]).astype(BF16)
        for c in range(FFN_PIECES):
            g = _dot(xn, wg_s[c])
            u = _dot(xn, wu_s[c])
            h_ref[:, c * FFN_CHUNK:(c + 1) * FFN_CHUNK] = (g * _sigmoid(g) * u).astype(BF16)
        out = x + 0.5 * _dot(h_ref[...], wd_s[...].reshape(D_FF, D_MODEL))
        if final_norm:
            out = _unit_rms(out) * fw_ref[...]
        o_ref[...] = out


def _ffn(x2d, layer, norm_w, w_gate, w_up, w_down, final_w, *, final_norm):
    tokens = x2d.shape[0]
    last = FFN_PIECES - 1
    rows = lambda i: (jnp.maximum(i - FFN_PIECES, 0), 0)
    return pl.pallas_call(
        functools.partial(_ffn_kernel, final_norm=final_norm),
        out_shape=jax.ShapeDtypeStruct(x2d.shape, F32),
        grid=(FFN_PIECES + tokens // FFN_TOKENS,),
        in_specs=[
            pl.BlockSpec((FFN_TOKENS, D_MODEL), rows),
            pl.BlockSpec((None, 1, D_MODEL), lambda i: (layer, 0, 0)),
            pl.BlockSpec((None, D_MODEL, FFN_CHUNK), lambda i: (layer, 0, jnp.minimum(i, last))),
            pl.BlockSpec((None, D_MODEL, FFN_CHUNK), lambda i: (layer, 0, jnp.minimum(i, last))),
            pl.BlockSpec((None, FFN_CHUNK, D_MODEL), lambda i: (layer, jnp.minimum(i, last), 0)),
            pl.BlockSpec((1, D_MODEL), lambda i: (0, 0)),
        ],
        out_specs=pl.BlockSpec((FFN_TOKENS, D_MODEL), rows),
        scratch_shapes=[
            pltpu.VMEM((FFN_PIECES, D_MODEL, FFN_CHUNK), BF16),
            pltpu.VMEM((FFN_PIECES, D_MODEL, FFN_CHUNK), BF16),
            pltpu.VMEM((FFN_PIECES, FFN_CHUNK, D_MODEL), BF16),
            pltpu.VMEM((FFN_TOKENS, D_FF), BF16),
        ],
        compiler_params=pltpu.CompilerParams(
            dimension_semantics=("arbitrary",), vmem_limit_bytes=VMEM_LIMIT_FFN),
        name="ffn_halfstep",
    )(x2d, norm_w, w_gate, w_up, w_down, final_w)


def _mix_chunk(proj_ref, ycat_ref, keep, bias_idx, p):
    row = lax.broadcasted_iota(jnp.int32, (CHUNK, CHUNK), 0)
    lane = lax.broadcasted_iota(jnp.int32, (CHUNK, CHUNK), 1)
    causal = row >= lane
    row_b = row.astype(F32).astype(BF16)
    lane_b = lane.astype(F32).astype(BF16)
    causal_b = row_b >= lane_b
    upper_half_b = lane_b >= HALF_LANES
    zero_b = jnp.zeros((CHUNK, CHUNK), BF16)
    group_b = (lax.broadcasted_iota(jnp.int32, (CHUNK, GROUP_WIDTH), 1) // SSD_HEAD_DIM).astype(F32).astype(BF16)

    def block_diag(x_b):
        zero = jnp.zeros_like(x_b)
        return jnp.concatenate([jnp.where(group_b == i, x_b, zero) for i in range(HEADS_PER_GROUP)], axis=0)

    if keep is not None:
        p.ext[0:SUBLANES, :] = p.ext[0:SUBLANES, :] * keep
    xbc_parts = []
    for c0 in range(0, SSD_XBC, CONV_COLS):
        cs = slice(c0, c0 + CONV_COLS)
        raw = proj_ref[:, P_XBC + c0:P_XBC + c0 + CONV_COLS]
        p.ext[SUBLANES:, cs] = raw
        ext = p.ext[:, cs]
        w = p.convw[:, cs]
        ext1 = pltpu.roll(ext, 1, 0)
        older = w[1:2, :] * ext + w[0:1, :] * ext1
        conv = (p.convb[:, cs] + w[3:4, :] * raw + w[2:3, :] * ext1[SUBLANES:, :]
                + pltpu.roll(older, 2, 0)[SUBLANES:, :])
        p.ext[0:SUBLANES, cs] = raw[CHUNK - SUBLANES:, :]
        xbc_parts.append(conv * _sigmoid(conv))
        yield
    xbc = jnp.concatenate(xbc_parts, axis=-1)
    xs = xbc[:, :SSD_INNER]
    bm = xbc[:, SSD_INNER:SSD_INNER + SSD_GROUPS * SSD_STATE]
    cm = xbc[:, SSD_INNER + SSD_GROUPS * SSD_STATE:]
    x_b = xs.astype(BF16)
    bm_b = bm.astype(BF16)
    cm_b = cm.astype(BF16)

    dtr = proj_ref[:, P_DT:P_DT + DT_PAD].T[0:SSD_HEADS, :] + p.dtb[...]
    dt = jnp.maximum(dtr, 0.0) + jnp.log1p(jnp.exp(-jnp.abs(dtr)))
    acs = _lane_cumsum(dt * (-LOG2E * jnp.exp(p.alog[...])), p.triu[...])
    grow = jnp.exp2(acs)
    to_end = dt * jnp.exp2(acs[:, CHUNK - 1:CHUNK] - acs)
    src = acs - jnp.log2(dt)
    stack = jnp.concatenate(
        [acs, *_split3(grow), *_split3(to_end), jnp.zeros((CHUNK - 7 * SSD_HEADS, LANES), F32)], axis=0)
    cols = stack.T
    expanded = _dot(cols.astype(BF16), p.expand[...])
    grow_e = expanded[:, :SSD_INNER]
    to_end_e = expanded[:, SSD_INNER:]
    xd_b = (xs * to_end_e).astype(BF16)
    yield

    y_diag = []
    for g in range(SSD_GROUPS):
        n0 = g * SSD_STATE
        cb = _dot_nt(cm_b[:, n0:n0 + SSD_STATE], bm_b[:, n0:n0 + SSD_STATE])
        parts = []
        for hh in range(HEADS_PER_GROUP):
            h = g * HEADS_PER_GROUP + hh
            seg = cols[:, h:h + 1] - src[h:h + 1, :]
            parts.append((cb * jnp.exp2(jnp.where(causal, seg, -jnp.inf))).astype(BF16))
        w0 = g * GROUP_WIDTH
        y_diag.append(_dot(jnp.concatenate(parts, axis=-1), block_diag(x_b[:, w0:w0 + GROUP_WIDTH])))
        yield
    y_off = []
    for g in range(SSD_GROUPS):
        n0 = g * SSD_STATE
        w0 = g * GROUP_WIDTH
        prev = p.st[g]
        if keep is not None:
            prev = prev * keep
        y_off.append(_dot(cm_b[:, n0:n0 + SSD_STATE], prev.astype(BF16)))
        bt = bm[:, n0:n0 + SSD_STATE].T.astype(BF16)
        p.st[g] = (prev * grow_e[CHUNK - 1:CHUNK, w0:w0 + GROUP_WIDTH]
                   + _dot(bt, xd_b[:, w0:w0 + GROUP_WIDTH]))
    yield
    gated, sumsq = [], 0.0
    for g in range(SSD_GROUPS):
        ws = slice(g * GROUP_WIDTH, (g + 1) * GROUP_WIDTH)
        y = y_diag[g] + y_off[g] * grow_e[:, ws] + xs[:, ws] * p.dskip[:, ws]
        z = proj_ref[:, P_Z + g * GROUP_WIDTH:P_Z + (g + 1) * GROUP_WIDTH]
        gated.append(y * (z * _sigmoid(z)))
        sumsq = sumsq + jnp.sum(gated[g] * gated[g], axis=-1, keepdims=True)
        yield
    inv_rms = lax.rsqrt(sumsq * (1.0 / SSD_INNER) + EPS)
    for g in range(SSD_GROUPS):
        ycat_ref[:, g * GROUP_WIDTH:(g + 1) * GROUP_WIDTH] = (gated[g] * inv_rms).astype(BF16)

    q_b = (proj_ref[:, P_Q:P_Q + ATT_WIDTH] * (LOG2E / math.sqrt(ATT_HEAD_DIM))).astype(BF16)
    k_b = proj_ref[:, P_K:P_K + ATT_KV_WIDTH].astype(BF16)
    v_b = proj_ref[:, P_V:P_V + ATT_KV_WIDTH].astype(BF16)
    keys = jnp.concatenate([p.kprev[...], k_b], axis=0)
    v_prev = p.vprev[...]
    halves = lambda t: (jnp.where(upper_half_b, zero_b, t), jnp.where(upper_half_b, t, zero_b))
    v_prev_half, v_half = halves(v_prev), halves(v_b)
    p.kprev[...] = k_b
    p.vprev[...] = v_b
    probs, values = [], []
    for pos, hq in enumerate(ATT_POS_HEADS):
        tile, half = divmod(pos, 2)
        q_tile = q_b[:, tile * LANES:(tile + 1) * LANES]
        q_head = halves(q_tile)[half]
        s = _dot_nt(q_head, keys)
        s = jnp.where(causal, s[:, CHUNK:], s[:, :CHUNK]) + p.bias[bias_idx, pos]
        sink = p.sink[hq] * LOG2E
        m = jnp.maximum(jnp.max(s, axis=-1, keepdims=True), sink)
        e = jnp.exp2(s - m)
        denom = jnp.sum(e, axis=-1, keepdims=True) + jnp.exp2(sink - m)
        pn = (e * (1.0 / denom)).astype(BF16)
        probs.append(jnp.where(causal_b, zero_b, pn))
        probs.append(jnp.where(causal_b, pn, zero_b))
        for vh in (v_prev_half[half], v_half[half]):
            values.append(jnp.concatenate([zero_b, vh] if tile else [vh, zero_b], axis=-1))
        yield
    att = _dot(jnp.concatenate(probs, axis=-1), jnp.concatenate(values, axis=0))
    ycat_ref[:, Y_ATT:Y_SGU] = _unit_rms(att).astype(BF16)
    yield

    u = _gelu_tanh(proj_ref[:, P_U:P_U + GM_WIDTH])
    yield
    gv = _gelu_tanh(proj_ref[:, P_GV:P_GV + GM_WIDTH])
    yield
    mu = jnp.mean(gv, axis=-1, keepdims=True)
    gc = gv - mu
    gv = gc * lax.rsqrt(jnp.mean(gc * gc, axis=-1, keepdims=True) + EPS) * p.lnw[...] + p.lnb[...]
    gv_b = gv.astype(BF16)
    mixed = _dot(p.wcat[...], block_diag(gv_b)) + p.sgub[...]
    ycat_ref[:, Y_SGU:] = _unit_rms(u * mixed).astype(BF16)
    yield


class _MixerRefs:
    def __init__(self, **refs):
        self.__dict__.update(refs)


def _mixer_kernel(sink_ref, rel_ref,
                  xa_ref, xc_ref, ing_ref, win_ref, convw_ref, convb_ref, dtb_ref, alog_ref,
                  dskip_ref, triu_ref, expand_ref, bucket_ref,
                  lnw_ref, lnb_ref, sguw_ref, sgub_ref, outg_ref, wout_ref,
                  o_ref,
                  win_s, wout_s, projx_ref, projy_ref, ycatx_ref, ycaty_ref, xn_ref,
                  ext_ref, st_ref, kprev_ref, vprev_ref, bias_ref, wcat_ref,
                  *, chunks_per_seq):
    t = pl.program_id(0)
    p = _MixerRefs(sink=sink_ref, convw=convw_ref, convb=convb_ref, dtb=dtb_ref, alog=alog_ref,
                   dskip=dskip_ref, triu=triu_ref, expand=expand_ref, lnw=lnw_ref, lnb=lnb_ref,
                   sgub=sgub_ref, ext=ext_ref, st=st_ref, kprev=kprev_ref, vprev=vprev_ref,
                   bias=bias_ref, wcat=wcat_ref)

    @pl.when(t == 0)
    def _init():
        row = lax.broadcasted_iota(jnp.int32, (CHUNK, CHUNK), 0)
        col = lax.broadcasted_iota(jnp.int32, (CHUNK, CHUNK), 1)
        bucket = bucket_ref[...]
        for pos, hq in enumerate(ATT_POS_HEADS):
            acc = jnp.zeros((CHUNK, CHUNK), F32)
            for k in range(REL_BUCKETS):
                acc = jnp.where(bucket == k, rel_ref[k * ATT_Q_HEADS + hq] * LOG2E, acc)
            bias_ref[0, pos] = acc
            bias_ref[1, pos] = jnp.where(row >= col, acc, MASKED)
        for g in range(GM_GROUPS):
            wcat_ref[:, g * CHUNK:(g + 1) * CHUNK] = jnp.where(row >= col, sguw_ref[g], 0.0).astype(BF16)

        def pack_rows(i, carry):
            r = pl.ds(pl.multiple_of(i * CHUNK, CHUNK), CHUNK)
            gain = ing_ref[r, :]
            win_s[r, 0:R_DT] = (win_ref[r, 0:R_DT] * gain).astype(BF16)
            tail = win_ref[r, R_DT:] * gain
            q0 = SSD_HEADS
            q = [tail[:, q0 + h * ATT_HEAD_DIM:q0 + (h + 1) * ATT_HEAD_DIM] for h in ATT_POS_HEADS]
            dt_tile = jnp.concatenate([tail[:, :q0], jnp.zeros((CHUNK, DT_PAD - SSD_HEADS), F32)], axis=1)
            win_s[r, R_DT:] = jnp.concatenate(q + [tail[:, q0 + ATT_WIDTH:], dt_tile], axis=1).astype(BF16)
            return carry

        lax.fori_loop(0, D_MODEL // CHUNK, pack_rows, 0)
        blocks = D_MODEL // ATT_HEAD_DIM
        first_att = Y_ATT // ATT_HEAD_DIM
        for j in range(blocks):
            src = first_att + ATT_POS_HEADS[j - first_att] if first_att <= j < first_att + ATT_Q_HEADS else j
            dst_rows = slice(j * ATT_HEAD_DIM, (j + 1) * ATT_HEAD_DIM)
            src_rows = slice(src * ATT_HEAD_DIM, (src + 1) * ATT_HEAD_DIM)
            wout_s[dst_rows, :] = (wout_ref[src_rows, :] * outg_ref[dst_rows, :]).astype(BF16)
        projy_ref[...] = jnp.zeros_like(projy_ref)
        ycatx_ref[...] = jnp.zeros_like(ycatx_ref)
        st_ref[...] = jnp.zeros_like(st_ref)
        ext_ref[0:SUBLANES, :] = jnp.zeros((SUBLANES, SSD_XBC), F32)
        kprev_ref[...] = jnp.zeros_like(kprev_ref)
        vprev_ref[...] = jnp.zeros_like(vprev_ref)

    starts_seq = lax.rem(4 * t, chunks_per_seq) == 0
    keep = jnp.where(starts_seq, 0.0, 1.0).astype(F32)
    first_idx = jnp.where(starts_seq, 1, 0).astype(jnp.int32)
    even, odd = pl.ds(0, CHUNK), pl.ds(CHUNK, CHUNK)

    def projection_pieces(rows, proj_new, ycat_out):
        def normalize():
            xn_ref[...] = _unit_rms(xa_ref[rows, :]).astype(BF16)

        def project(c0, c1):
            proj_new[:, c0:c1] = _dot(xn_ref[...], win_s[:, c0:c1])

        def output(c0, c1):
            o_ref[rows, c0:c1] = xc_ref[rows, c0:c1] + _dot(ycat_out[...], wout_s[:, c0:c1])

        in_cols = list(range(0, P_TOTAL - PROJ_PIECE, PROJ_PIECE)) + [P_TOTAL]
        pieces = [normalize]
        pieces += [functools.partial(project, a, b) for a, b in zip(in_cols[:-1], in_cols[1:])]
        out_pieces = [functools.partial(output, c, c + PROJ_PIECE) for c in range(0, D_MODEL, PROJ_PIECE)]
        for i, piece in enumerate(out_pieces):
            pieces.insert(3 + 3 * i, piece)
        return pieces

    pieces1 = projection_pieces(pl.ds(0, PAIR), projx_ref, ycatx_ref)
    pieces2 = projection_pieces(pl.ds(PAIR, PAIR), projy_ref, ycaty_ref)
    piece_ticks = pl.cdiv(len(pieces1), PIECES_PER_TICK)
    mix2_start = max(piece_ticks, 2 * MIX_LAG)
    pieces2_start = MIX_LAG + MIX_SECTIONS
    mixes = [
        (0, _mix_chunk(projy_ref.at[even], ycaty_ref.at[even], None, 0, p)),
        (MIX_LAG, _mix_chunk(projy_ref.at[odd], ycaty_ref.at[odd], None, 0, p)),
        (mix2_start, _mix_chunk(projx_ref.at[even], ycatx_ref.at[even], keep, first_idx, p)),
        (mix2_start + MIX_LAG, _mix_chunk(projx_ref.at[odd], ycatx_ref.at[odd], None, 0, p)),
    ]
    piece_lists = [(0, pieces1), (pieces2_start, pieces2)]
    last_tick = max(mix2_start + MIX_LAG + MIX_SECTIONS, pieces2_start + piece_ticks)
    for tick in range(last_tick):
        for start, mix in mixes:
            if start <= tick < start + MIX_SECTIONS:
                next(mix)
        for start, pieces in piece_lists:
            first = (tick - start) * PIECES_PER_TICK
            if first >= 0:
                for piece in pieces[first:first + PIECES_PER_TICK]:
                    piece()
    for _, mix in mixes:
        assert next(mix, None) is None and not mix.gi_running and mix.gi_frame is None, "MIX_SECTIONS is stale"


def _folded_bucket_tile():
    i = np.arange(CHUNK)[:, None]
    j = np.arange(CHUNK)[None, :]
    n = np.where(j <= i, i - j, i - j + CHUNK)
    max_exact = REL_BUCKETS // 2
    large = max_exact + (np.log(np.maximum(n, 1) / max_exact) / np.log(REL_MAX_DIST / max_exact)
                         * (REL_BUCKETS - max_exact)).astype(np.int32)
    large = np.minimum(large, REL_BUCKETS - 1)
    return np.where(n < max_exact, n, large).astype(np.int32)


def _mixer(x2d, seq, layer, in_gain, w_in, conv_w, conv_b, dt_bias, a_log, d_skip, sinks, rel_bias,
           ln_w, ln_b, sgu_w, sgu_b, out_gain, w_out):
    tokens = x2d.shape[0]
    n_quads = tokens // QUAD
    chunks_per_seq = seq // CHUNK
    assert tokens % QUAD == 0 and seq % QUAD == 0 and GM_GROUPS == HEADS_PER_GROUP
    assert P_Q == R_DT and w_in.shape[-1] == R_K + P_DT - P_K
    per_head_rows = lambda v: jnp.broadcast_to(v[:, None], (SSD_HEADS, LANES))
    row = lambda v: v.reshape(1, -1)
    expand = np.zeros((CHUNK, 2 * SSD_INNER), np.float32)
    for r in range(SSD_HEADS, 7 * SSD_HEADS):
        h = r % SSD_HEADS
        c0 = ((r // SSD_HEADS - 1) // 3) * SSD_INNER + h * SSD_HEAD_DIM
        expand[r, c0:c0 + SSD_HEAD_DIM] = 1.0
    expand = jnp.asarray(expand, BF16)
    triu = jnp.asarray(np.triu(np.ones((CHUNK, CHUNK), np.float32)), BF16)
    bucket = jnp.asarray(_folded_bucket_tile())
    sgu_b_e = jnp.repeat(jnp.transpose(sgu_b), GM_GROUP_DIM, axis=1)

    const2 = lambda t, *_: (0, 0)
    const3 = lambda t, *_: (0, 0, 0)
    full2 = lambda shape: pl.BlockSpec(shape, const2)
    layer_block = lambda shape: pl.BlockSpec((None,) + shape, lambda t, *_: (layer, 0, 0),
                                             pipeline_mode=pl.Buffered(1))
    projected = lambda t, *_: (jnp.minimum(t, n_quads - 1), 0)
    finished = lambda t, *_: (jnp.maximum(t - 1, 0), 0)
    grid_spec = pltpu.PrefetchScalarGridSpec(
        num_scalar_prefetch=2,
        grid=(n_quads + 1,),
        in_specs=[
            pl.BlockSpec((QUAD, D_MODEL), projected),
            pl.BlockSpec((QUAD, D_MODEL), finished),
            layer_block((D_MODEL, 1)),
            layer_block((D_MODEL, w_in.shape[-1])),
            full2((SSD_CONV, SSD_XBC)),
            full2((1, SSD_XBC)),
            full2((SSD_HEADS, LANES)),
            full2((SSD_HEADS, LANES)),
            full2((1, SSD_INNER)),
            full2((CHUNK, CHUNK)),
            full2((CHUNK, 2 * SSD_INNER)),
            full2((CHUNK, CHUNK)),
            full2((1, GM_WIDTH)),
            full2((1, GM_WIDTH)),
            pl.BlockSpec((GM_GROUPS, CHUNK, CHUNK), const3),
            full2((CHUNK, GM_WIDTH)),
            layer_block((D_MODEL, 1)),
            layer_block((D_MODEL, D_MODEL)),
        ],
        out_specs=pl.BlockSpec((QUAD, D_MODEL), finished),
        scratch_shapes=[
            pltpu.VMEM((D_MODEL, P_TOTAL), BF16),
            pltpu.VMEM((D_MODEL, D_MODEL), BF16),
            pltpu.VMEM((PAIR, P_TOTAL), F32),
            pltpu.VMEM((PAIR, P_TOTAL), F32),
            pltpu.VMEM((PAIR, D_MODEL), BF16),
            pltpu.VMEM((PAIR, D_MODEL), BF16),
            pltpu.VMEM((PAIR, D_MODEL), BF16),
            pltpu.VMEM((SUBLANES + CHUNK, SSD_XBC), F32),
            pltpu.VMEM((SSD_GROUPS, SSD_STATE, GROUP_WIDTH), F32),
            pltpu.VMEM((CHUNK, ATT_KV_WIDTH), BF16),
            pltpu.VMEM((CHUNK, ATT_KV_WIDTH), BF16),
            pltpu.VMEM((2, ATT_Q_HEADS, CHUNK, CHUNK), F32),
            pltpu.VMEM((CHUNK, GM_GROUPS * CHUNK), BF16),
        ],
    )
    return pl.pallas_call(
        functools.partial(_mixer_kernel, chunks_per_seq=chunks_per_seq),
        out_shape=jax.ShapeDtypeStruct(x2d.shape, F32),
        grid_spec=grid_spec,
        compiler_params=pltpu.CompilerParams(
            dimension_semantics=("arbitrary",), vmem_limit_bytes=VMEM_LIMIT_MIXER),
        name="token_mixer",
    )(sinks.astype(F32), rel_bias.reshape(-1).astype(F32),
      x2d, x2d, in_gain, w_in, conv_w, row(conv_b), per_head_rows(dt_bias), per_head_rows(a_log),
      row(jnp.repeat(d_skip, SSD_HEAD_DIM)), triu, expand, bucket,
      row(ln_w), row(ln_b), sgu_w, sgu_b_e, out_gain, w_out)


def kernel(x, ffn1_norm, ffn1_w_gate, ffn1_w_up, ffn1_w_down, mix_norm, w_in, conv_w, conv_b, dt_bias, a_log, d_skip, ssd_norm, attn_sinks, rel_bias, attn_out_norm, sgu_ln_w, sgu_ln_b, sgu_w, sgu_b, sgu_out_norm, w_out, ffn2_norm, ffn2_w_gate, ffn2_w_up, ffn2_w_down, final_norm):
    batch, seq, d = x.shape
    depth = w_in.shape[0]
    fw = final_norm.reshape(1, d)
    x2 = x.reshape(batch * seq, d)
    ffn1_gain = ffn1_norm.reshape(depth, 1, d)
    ffn2_gain = ffn2_norm.reshape(depth, 1, d)
    mix_in_gain = mix_norm.reshape(depth, d, 1)
    att_cols = np.concatenate([np.arange(h * ATT_HEAD_DIM, (h + 1) * ATT_HEAD_DIM) for h in ATT_POS_HEADS])
    mix_out_gain = jnp.concatenate([ssd_norm, attn_out_norm[:, att_cols], sgu_out_norm], axis=1).reshape(depth, d, 1)
    for l in range(depth):
        x2 = _ffn(x2, l, ffn1_gain, ffn1_w_gate, ffn1_w_up, ffn1_w_down, fw, final_norm=False)
        x2 = _mixer(x2, seq, l, mix_in_gain, w_in, conv_w[l], conv_b[l], dt_bias[l],
                    a_log[l], d_skip[l], attn_sinks[l], rel_bias,
                    sgu_ln_w[l], sgu_ln_b[l], sgu_w[l], sgu_b[l], mix_out_gain, w_out)
        x2 = _ffn(x2, l, ffn2_gain, ffn2_w_gate, ffn2_w_up, ffn2_w_down, fw,
                  final_norm=(l == depth - 1))
    return x2.reshape(batch, seq, d)
```

```python
import functools
import math

import jax
import jax.numpy as jnp
import numpy as np
from jax import lax
from jax.experimental import pallas as pl
from jax.experimental.pallas import tpu as pltpu

F32 = jnp.float32
BF16 = jnp.bfloat16

D_MODEL = 1024
D_FF = 2816
EPS = 1e-6

SSD_HEADS = 8
SSD_HEAD_DIM = 64
SSD_INNER = SSD_HEADS * SSD_HEAD_DIM
SSD_GROUPS = 2
SSD_STATE = 128
SSD_CONV = 4
SSD_XBC = SSD_INNER + 2 * SSD_GROUPS * SSD_STATE
HEADS_PER_GROUP = SSD_HEADS // SSD_GROUPS
GROUP_WIDTH = HEADS_PER_GROUP * SSD_HEAD_DIM

ATT_Q_HEADS = 4
ATT_KV_HEADS = 2
ATT_HEAD_DIM = 64
ATT_WIDTH = ATT_Q_HEADS * ATT_HEAD_DIM
ATT_KV_WIDTH = ATT_KV_HEADS * ATT_HEAD_DIM
REL_BUCKETS = 32
REL_MAX_DIST = 128
ATT_POS_HEADS = (0, 2, 1, 3)

GM_GROUPS = 4
GM_GROUP_DIM = 64
GM_WIDTH = GM_GROUPS * GM_GROUP_DIM

CHUNK = 128
PAIR = 2 * CHUNK
QUAD = 2 * PAIR
PROJ_PIECE = 256
CONV_COLS = 256
MIX_SECTIONS = 18
MIX_LAG = 4

SUBLANES = 8
LANES = 128
HALF_LANES = LANES // 2
DT_PAD = LANES

P_Z = 0
P_XBC = P_Z + SSD_INNER
P_Q = P_XBC + SSD_XBC
P_K = P_Q + ATT_WIDTH
P_V = P_K + ATT_KV_WIDTH
P_U = P_V + ATT_KV_WIDTH
P_GV = P_U + GM_WIDTH
P_DT = P_GV + GM_WIDTH
P_TOTAL = P_DT + DT_PAD

R_XBC = SSD_INNER
R_DT = R_XBC + SSD_XBC
R_Q = R_DT + SSD_HEADS
R_K = R_Q + ATT_WIDTH

Y_ATT = SSD_INNER
Y_SGU = Y_ATT + ATT_WIDTH

MIX_READS = {
    **{i: (P_XBC + i * CONV_COLS, P_XBC + (i + 1) * CONV_COLS) for i in range(SSD_XBC // CONV_COLS)},
    4: (P_DT, P_TOTAL),
    8: (P_Z, P_Z + GROUP_WIDTH),
    9: (P_Z + GROUP_WIDTH, P_Z + 2 * GROUP_WIDTH),
    10: (P_Q, P_U),
    15: (P_U, P_GV),
    16: (P_GV, P_DT),
}

MASKED = -1e30
LOG2E = math.log2(math.e)

FFN_TOKENS = 1024
FFN_CHUNK = 256
FFN_PIECES = D_FF // FFN_CHUNK
VMEM_LIMIT_FFN = 56 * 1024 * 1024
VMEM_LIMIT_MIXER = 48 * 1024 * 1024


def _unit_rms(x):
    return x * lax.rsqrt(jnp.mean(x * x, axis=-1, keepdims=True) + EPS)


def _sigmoid(x):
    return 1.0 / (1.0 + jnp.exp(-x))


def _gelu_tanh(x):
    c = math.sqrt(2.0 / math.pi)
    return 0.5 * x * (1.0 + jnp.tanh(c * (x + 0.044715 * (x * x * x))))


def _top_bits(x):
    bits = lax.bitcast_convert_type(x, jnp.uint32) & jnp.uint32(0xFFFF0000)
    return lax.bitcast_convert_type(bits, F32)


def _split3(x):
    hi = _top_bits(x)
    r = x - hi
    mid = _top_bits(r)
    return hi, mid, r - mid


def _dot(a, b):
    return jnp.dot(a, b, preferred_element_type=F32)


def _dot_nt(a, b):
    return lax.dot_general(a, b, (((1,), (1,)), ((), ())), preferred_element_type=F32)


def _lane_cumsum(x, upper_ones):
    rows = x.shape[0]
    parts = _dot(jnp.concatenate(_split3(x), axis=0).astype(BF16), upper_ones)
    return parts[0:rows] + parts[rows:2 * rows] + parts[2 * rows:3 * rows]


def _ffn_kernel(x_ref, nw_ref, wg_ref, wu_ref, wd_ref, fw_ref, o_ref, wg_s, wu_s, wd_s, h_ref, *, final_norm):
    i = pl.program_id(0)

    @pl.when(i < FFN_PIECES)
    def _load_weight_slice():
        wg_s[i] = wg_ref[...].astype(BF16)
        wu_s[i] = wu_ref[...].astype(BF16)
        wd_s[i] = wd_ref[...].astype(BF16)

    @pl.when(i >= FFN_PIECES)
    def _token_rows():
        x = x_ref[...]
        xn = (_unit_rms(x) * nw_ref[...]).astype(BF16)
        for c in range(FFN_PIECES):
            g = _dot(xn, wg_s[c])
            u = _dot(xn, wu_s[c])
            h_ref[:, c * FFN_CHUNK:(c + 1) * FFN_CHUNK] = (g * _sigmoid(g) * u).astype(BF16)
        out = x + 0.5 * _dot(h_ref[...], wd_s[...].reshape(D_FF, D_MODEL))
        if final_norm:
            out = _unit_rms(out) * fw_ref[...]
        o_ref[...] = out


def _ffn(x2d, layer, norm_w, w_gate, w_up, w_down, final_w, *, final_norm):
    tokens = x2d.shape[0]
    last = FFN_PIECES - 1
    rows = lambda i: (jnp.maximum(i - FFN_PIECES, 0), 0)
    return pl.pallas_call(
        functools.partial(_ffn_kernel, final_norm=final_norm),
        out_shape=jax.ShapeDtypeStruct(x2d.shape, F32),
        grid=(FFN_PIECES + tokens // FFN_TOKENS,),
        in_specs=[
            pl.BlockSpec((FFN_TOKENS, D_MODEL), rows),
            pl.BlockSpec((None, 1, D_MODEL), lambda i: (layer, 0, 0)),
            pl.BlockSpec((None, D_MODEL, FFN_CHUNK), lambda i: (layer, 0, jnp.minimum(i, last))),
            pl.BlockSpec((None, D_MODEL, FFN_CHUNK), lambda i: (layer, 0, jnp.minimum(i, last))),
            pl.BlockSpec((None, FFN_CHUNK, D_MODEL), lambda i: (layer, jnp.minimum(i, last), 0)),
            pl.BlockSpec((1, D_MODEL), lambda i: (0, 0)),
        ],
        out_specs=pl.BlockSpec((FFN_TOKENS, D_MODEL), rows),
        scratch_shapes=[
            pltpu.VMEM((FFN_PIECES, D_MODEL, FFN_CHUNK), BF16),
            pltpu.VMEM((FFN_PIECES, D_MODEL, FFN_CHUNK), BF16),
            pltpu.VMEM((FFN_PIECES, FFN_CHUNK, D_MODEL), BF16),
            pltpu.VMEM((FFN_TOKENS, D_FF), BF16),
        ],
        compiler_params=pltpu.CompilerParams(
            dimension_semantics=("arbitrary",), vmem_limit_bytes=VMEM_LIMIT_FFN),
        name="ffn_halfstep",
    )(x2d, norm_w, w_gate, w_up, w_down, final_w)


def _mix_chunk(proj_ref, ycat_ref, keep, bias_idx, p):
    row = lax.broadcasted_iota(jnp.int32, (CHUNK, CHUNK), 0)
    lane = lax.broadcasted_iota(jnp.int32, (CHUNK, CHUNK), 1)
    causal = row >= lane
    row_b = row.astype(F32).astype(BF16)
    lane_b = lane.astype(F32).astype(BF16)
    causal_b = row_b >= lane_b
    upper_half_b = lane_b >= HALF_LANES
    zero_b = jnp.zeros((CHUNK, CHUNK), BF16)
    group_b = (lax.broadcasted_iota(jnp.int32, (CHUNK, GROUP_WIDTH), 1) // SSD_HEAD_DIM).astype(F32).astype(BF16)

    def block_diag(x_b):
        zero = jnp.zeros_like(x_b)
        return jnp.concatenate([jnp.where(group_b == i, x_b, zero) for i in range(HEADS_PER_GROUP)], axis=0)

    xbc_parts = []
    for c0 in range(0, SSD_XBC, CONV_COLS):
        cs = slice(c0, c0 + CONV_COLS)
        if keep is not None:
            p.ext[0:SUBLANES, cs] = p.ext[0:SUBLANES, cs] * keep
        raw = proj_ref[:, P_XBC + c0:P_XBC + c0 + CONV_COLS]
        p.ext[SUBLANES:, cs] = raw
        ext = p.ext[:, cs]
        w = p.convw[:, cs]
        ext1 = pltpu.roll(ext, 1, 0)
        older = w[1:2, :] * ext + w[0:1, :] * ext1
        conv = (p.convb[:, cs] + w[3:4, :] * raw + w[2:3, :] * ext1[SUBLANES:, :]
                + pltpu.roll(older, 2, 0)[SUBLANES:, :])
        p.ext[0:SUBLANES, cs] = raw[CHUNK - SUBLANES:, :]
        xbc_parts.append(conv * _sigmoid(conv))
        yield P_XBC + c0, P_XBC + c0 + CONV_COLS
    xbc = jnp.concatenate(xbc_parts, axis=-1)
    xs = xbc[:, :SSD_INNER]
    bm = xbc[:, SSD_INNER:SSD_INNER + SSD_GROUPS * SSD_STATE]
    cm = xbc[:, SSD_INNER + SSD_GROUPS * SSD_STATE:]
    x_b = xs.astype(BF16)
    bm_b = bm.astype(BF16)
    cm_b = cm.astype(BF16)

    dtr = proj_ref[:, P_DT:P_DT + DT_PAD].T[0:SSD_HEADS, :] + p.dtb[...]
    dt = jnp.maximum(dtr, 0.0) + jnp.log1p(jnp.exp(-jnp.abs(dtr)))
    acs = _lane_cumsum(dt * (-LOG2E * jnp.exp(p.alog[...])), p.triu[...])
    grow = jnp.exp2(acs)
    to_end = dt * jnp.exp2(acs[:, CHUNK - 1:CHUNK] - acs)
    src = acs - jnp.log2(dt)
    stack = jnp.concatenate(
        [acs, *_split3(grow), *_split3(to_end), jnp.zeros((CHUNK - 7 * SSD_HEADS, LANES), F32)], axis=0)
    cols = stack.T
    expanded = _dot(cols.astype(BF16), p.expand[...])
    grow_e = expanded[:, :SSD_INNER]
    to_end_e = expanded[:, SSD_INNER:]
    xd_b = (xs * to_end_e).astype(BF16)
    yield P_DT, P_TOTAL

    y_diag = []
    for g in range(SSD_GROUPS):
        n0 = g * SSD_STATE
        cb = _dot_nt(cm_b[:, n0:n0 + SSD_STATE], bm_b[:, n0:n0 + SSD_STATE])
        parts = []
        for hh in range(HEADS_PER_GROUP):
            h = g * HEADS_PER_GROUP + hh
            seg = cols[:, h:h + 1] - src[h:h + 1, :]
            parts.append((cb * jnp.exp2(jnp.where(causal, seg, -jnp.inf))).astype(BF16))
        w0 = g * GROUP_WIDTH
        y_diag.append(_dot(jnp.concatenate(parts, axis=-1), block_diag(x_b[:, w0:w0 + GROUP_WIDTH])))
        yield
    y_off = []
    for g in range(SSD_GROUPS):
        n0 = g * SSD_STATE
        w0 = g * GROUP_WIDTH
        prev = p.st[g]
        if keep is not None:
            prev = prev * keep
        y_off.append(_dot(cm_b[:, n0:n0 + SSD_STATE], prev.astype(BF16)))
        bt = bm[:, n0:n0 + SSD_STATE].T.astype(BF16)
        p.st[g] = (prev * grow_e[CHUNK - 1:CHUNK, w0:w0 + GROUP_WIDTH]
                   + _dot(bt, xd_b[:, w0:w0 + GROUP_WIDTH]))
    yield
    gated, sumsq = [], 0.0
    for g in range(SSD_GROUPS):
        ws = slice(g * GROUP_WIDTH, (g + 1) * GROUP_WIDTH)
        y = y_diag[g] + y_off[g] * grow_e[:, ws] + xs[:, ws] * p.dskip[:, ws]
        z = proj_ref[:, P_Z + g * GROUP_WIDTH:P_Z + (g + 1) * GROUP_WIDTH]
        gated.append(y * (z * _sigmoid(z)))
        sumsq = sumsq + jnp.sum(gated[g] * gated[g], axis=-1, keepdims=True)
        yield P_Z + g * GROUP_WIDTH, P_Z + (g + 1) * GROUP_WIDTH
    inv_rms = lax.rsqrt(sumsq * (1.0 / SSD_INNER) + EPS)
    for g in range(SSD_GROUPS):
        ycat_ref[:, g * GROUP_WIDTH:(g + 1) * GROUP_WIDTH] = (gated[g] * inv_rms).astype(BF16)

    q_b = (proj_ref[:, P_Q:P_Q + ATT_WIDTH] * (LOG2E / math.sqrt(ATT_HEAD_DIM))).astype(BF16)
    k_b = proj_ref[:, P_K:P_K + ATT_KV_WIDTH].astype(BF16)
    v_b = proj_ref[:, P_V:P_V + ATT_KV_WIDTH].astype(BF16)
    keys = jnp.concatenate([p.kprev[...], k_b], axis=0)
    v_prev = p.vprev[...]
    halves = lambda t: (jnp.where(upper_half_b, zero_b, t), jnp.where(upper_half_b, t, zero_b))
    v_prev_half, v_half = halves(v_prev), halves(v_b)
    p.kprev[...] = k_b
    p.vprev[...] = v_b
    probs, values = [], []
    for pos, hq in enumerate(ATT_POS_HEADS):
        tile, half = divmod(pos, 2)
        q_tile = q_b[:, tile * LANES:(tile + 1) * LANES]
        q_head = halves(q_tile)[half]
        s = _dot_nt(q_head, keys)
        s = jnp.where(causal, s[:, CHUNK:], s[:, :CHUNK]) + p.bias[bias_idx, pos]
        sink = p.sink[hq] * LOG2E
        m = jnp.maximum(jnp.max(s, axis=-1, keepdims=True), sink)
        e = jnp.exp2(s - m)
        denom = jnp.sum(e, axis=-1, keepdims=True) + jnp.exp2(sink - m)
        pn = (e * (1.0 / denom)).astype(BF16)
        probs.append(jnp.where(causal_b, zero_b, pn))
        probs.append(jnp.where(causal_b, pn, zero_b))
        for vh in (v_prev_half[half], v_half[half]):
            values.append(jnp.concatenate([zero_b, vh] if tile else [vh, zero_b], axis=-1))
        yield (P_Q, P_U) if pos == 0 else None
    att = _dot(jnp.concatenate(probs, axis=-1), jnp.concatenate(values, axis=0))
    ycat_ref[:, Y_ATT:Y_SGU] = _unit_rms(att).astype(BF16)
    yield

    u = _gelu_tanh(proj_ref[:, P_U:P_U + GM_WIDTH])
    yield P_U, P_GV
    gv = _gelu_tanh(proj_ref[:, P_GV:P_GV + GM_WIDTH])
    yield P_GV, P_DT
    mu = jnp.mean(gv, axis=-1, keepdims=True)
    gc = gv - mu
    gv = gc * lax.rsqrt(jnp.mean(gc * gc, axis=-1, keepdims=True) + EPS) * p.lnw[...] + p.lnb[...]
    gv_b = gv.astype(BF16)
    mixed = _dot(p.wcat[...], block_diag(gv_b)) + p.sgub[...]
    ycat_ref[:, Y_SGU:] = _unit_rms(u * mixed).astype(BF16)
    yield


class _MixerRefs:
    def __init__(self, **refs):
        self.__dict__.update(refs)


def _mixer_kernel(sink_ref, rel_ref,
                  xa_ref, xc_ref, ing_ref, win_ref, convw_ref, convb_ref, dtb_ref, alog_ref,
                  dskip_ref, triu_ref, expand_ref, bucket_ref,
                  lnw_ref, lnb_ref, sguw_ref, sgub_ref, outg_ref, wout_ref,
                  o_ref,
                  win_s, wout_s, projx_ref, projy_ref, ycatx_ref, ycaty_ref, xnx_ref, xny_ref,
                  ext_ref, st_ref, kprev_ref, vprev_ref, bias_ref, wcat_ref,
                  *, chunks_per_seq):
    t = pl.program_id(0)
    p = _MixerRefs(sink=sink_ref, convw=convw_ref, convb=convb_ref, dtb=dtb_ref, alog=alog_ref,
                   dskip=dskip_ref, triu=triu_ref, expand=expand_ref, lnw=lnw_ref, lnb=lnb_ref,
                   sgub=sgub_ref, ext=ext_ref, st=st_ref, kprev=kprev_ref, vprev=vprev_ref,
                   bias=bias_ref, wcat=wcat_ref)

    @pl.when(t == 0)
    def _init():
        row = lax.broadcasted_iota(jnp.int32, (CHUNK, CHUNK), 0)
        col = lax.broadcasted_iota(jnp.int32, (CHUNK, CHUNK), 1)
        bucket = bucket_ref[...]
        for pos, hq in enumerate(ATT_POS_HEADS):
            acc = jnp.zeros((CHUNK, CHUNK), F32)
            for k in range(REL_BUCKETS):
                acc = jnp.where(bucket == k, rel_ref[k * ATT_Q_HEADS + hq] * LOG2E, acc)
            bias_ref[0, pos] = acc
            bias_ref[1, pos] = jnp.where(row >= col, acc, MASKED)
        for g in range(GM_GROUPS):
            wcat_ref[:, g * CHUNK:(g + 1) * CHUNK] = jnp.where(row >= col, sguw_ref[g], 0.0).astype(BF16)

        def pack_rows(i, carry):
            r = pl.ds(pl.multiple_of(i * CHUNK, CHUNK), CHUNK)
            gain = ing_ref[r, :]
            win_s[r, 0:R_DT] = (win_ref[r, 0:R_DT] * gain).astype(BF16)
            tail = win_ref[r, R_DT:] * gain
            q0 = SSD_HEADS
            q = [tail[:, q0 + h * ATT_HEAD_DIM:q0 + (h + 1) * ATT_HEAD_DIM] for h in ATT_POS_HEADS]
            dt_tile = jnp.concatenate([tail[:, :q0], jnp.zeros((CHUNK, DT_PAD - SSD_HEADS), F32)], axis=1)
            win_s[r, R_DT:] = jnp.concatenate(q + [tail[:, q0 + ATT_WIDTH:], dt_tile], axis=1).astype(BF16)
            return carry

        lax.fori_loop(0, D_MODEL // CHUNK, pack_rows, 0)
        blocks = D_MODEL // ATT_HEAD_DIM
        first_att = Y_ATT // ATT_HEAD_DIM
        for j in range(blocks):
            src = first_att + ATT_POS_HEADS[j - first_att] if first_att <= j < first_att + ATT_Q_HEADS else j
            dst_rows = slice(j * ATT_HEAD_DIM, (j + 1) * ATT_HEAD_DIM)
            src_rows = slice(src * ATT_HEAD_DIM, (src + 1) * ATT_HEAD_DIM)
            wout_s[dst_rows, :] = (wout_ref[src_rows, :] * outg_ref[dst_rows, :]).astype(BF16)
        projy_ref[...] = jnp.zeros_like(projy_ref)
        ycatx_ref[...] = jnp.zeros_like(ycatx_ref)
        st_ref[...] = jnp.zeros_like(st_ref)
        ext_ref[0:SUBLANES, :] = jnp.zeros((SUBLANES, SSD_XBC), F32)
        kprev_ref[...] = jnp.zeros_like(kprev_ref)
        vprev_ref[...] = jnp.zeros_like(vprev_ref)

    starts_seq = lax.rem(4 * t, chunks_per_seq) == 0
    keep = jnp.where(starts_seq, 0.0, 1.0).astype(F32)
    first_idx = jnp.where(starts_seq, 1, 0).astype(jnp.int32)
    even, odd = pl.ds(0, CHUNK), pl.ds(CHUNK, CHUNK)

    def projection_pieces(rows, proj_new, ycat_out, xn):
        def normalize():
            xn[...] = _unit_rms(xa_ref[rows, :]).astype(BF16)

        def project(c0, c1):
            proj_new[:, c0:c1] = _dot(xn[...], win_s[:, c0:c1])

        def output(c0, c1):
            o_ref[rows, c0:c1] = xc_ref[rows, c0:c1] + _dot(ycat_out[...], wout_s[:, c0:c1])

        in_piece = lambda lo, hi: ("in", lo, hi, functools.partial(project, lo, hi))
        out_piece = lambda i: ("out", i * PROJ_PIECE, (i + 1) * PROJ_PIECE,
                               functools.partial(output, i * PROJ_PIECE, (i + 1) * PROJ_PIECE))
        xbc = [in_piece(P_XBC + c, P_XBC + c + PROJ_PIECE) for c in range(0, SSD_XBC, PROJ_PIECE)]
        z = [in_piece(P_Z + c, P_Z + c + PROJ_PIECE) for c in range(0, SSD_INNER, PROJ_PIECE)]
        return ([("norm", 0, 0, normalize)] + z + [out_piece(0)] + xbc[:2] + [out_piece(1)] + xbc[2:]
                + [out_piece(2), in_piece(P_Q, P_K), in_piece(P_K, P_U), out_piece(3),
                   in_piece(P_U, P_GV), in_piece(P_GV, P_TOTAL)])

    mixes = [
        _mix_chunk(projy_ref.at[even], ycaty_ref.at[even], None, 0, p),
        _mix_chunk(projy_ref.at[odd], ycaty_ref.at[odd], None, 0, p),
        _mix_chunk(projx_ref.at[even], ycatx_ref.at[even], keep, first_idx, p),
        _mix_chunk(projx_ref.at[odd], ycatx_ref.at[odd], None, 0, p),
    ]
    pieces = [projection_pieces(pl.ds(0, PAIR), projx_ref, ycatx_ref, xnx_ref),
              projection_pieces(pl.ds(PAIR, PAIR), projy_ref, ycaty_ref, xny_ref)]
    overlaps = lambda reads, piece: reads is not None and reads[0] < piece[2] and piece[1] < reads[1]
    done = [0] * len(mixes)
    issued = [0, 0]

    def mix_may_run(k):
        section = done[k]
        if section == MIX_SECTIONS:
            return False
        if k > 0 and done[k - 1] < min(section + MIX_LAG, MIX_SECTIONS):
            return False
        if k >= 2 and issued[0] < len(pieces[0]):
            return False
        return True

    def piece_may_issue(phase):
        if issued[phase] == len(pieces[phase]):
            return False
        piece = pieces[phase][issued[phase]]
        if phase == 1:
            if piece[0] == "in":
                return all(done[k] > s for k in (0, 1) for s, r in MIX_READS.items() if overlaps(r, piece))
            if piece[0] == "out":
                return done[0] == done[1] == MIX_SECTIONS
        return True

    while min(done) < MIX_SECTIONS or issued != [len(pieces[0]), len(pieces[1])]:
        progressed = False
        for k, mix in enumerate(mixes):
            if mix_may_run(k):
                assert next(mix) == MIX_READS.get(done[k]), "MIX_READS is stale"
                done[k] += 1
                progressed = True
        for phase in (0, 1):
            if piece_may_issue(phase):
                pieces[phase][issued[phase]][3]()
                issued[phase] += 1
                progressed = True
                break
        assert progressed, "mixer step schedule is stuck"
    for mix in mixes:
        assert next(mix, None) is None and mix.gi_frame is None, "MIX_SECTIONS is stale"


def _folded_bucket_tile():
    i = np.arange(CHUNK)[:, None]
    j = np.arange(CHUNK)[None, :]
    n = np.where(j <= i, i - j, i - j + CHUNK)
    max_exact = REL_BUCKETS // 2
    large = max_exact + (np.log(np.maximum(n, 1) / max_exact) / np.log(REL_MAX_DIST / max_exact)
                         * (REL_BUCKETS - max_exact)).astype(np.int32)
    large = np.minimum(large, REL_BUCKETS - 1)
    return np.where(n < max_exact, n, large).astype(np.int32)


def _mixer(x2d, seq, layer, in_gain, w_in, conv_w, conv_b, dt_bias, a_log, d_skip, sinks, rel_bias,
           ln_w, ln_b, sgu_w, sgu_b, out_gain, w_out):
    tokens = x2d.shape[0]
    n_quads = tokens // QUAD
    chunks_per_seq = seq // CHUNK
    assert tokens % QUAD == 0 and seq % QUAD == 0 and GM_GROUPS == HEADS_PER_GROUP
    assert P_Q == R_DT and w_in.shape[-1] == R_K + P_DT - P_K
    per_head_rows = lambda v: jnp.broadcast_to(v[:, None], (SSD_HEADS, LANES))
    row = lambda v: v.reshape(1, -1)
    expand = np.zeros((CHUNK, 2 * SSD_INNER), np.float32)
    for r in range(SSD_HEADS, 7 * SSD_HEADS):
        h = r % SSD_HEADS
        c0 = ((r // SSD_HEADS - 1) // 3) * SSD_INNER + h * SSD_HEAD_DIM
        expand[r, c0:c0 + SSD_HEAD_DIM] = 1.0
    expand = jnp.asarray(expand, BF16)
    triu = jnp.asarray(np.triu(np.ones((CHUNK, CHUNK), np.float32)), BF16)
    bucket = jnp.asarray(_folded_bucket_tile())
    sgu_b_e = jnp.repeat(jnp.transpose(sgu_b), GM_GROUP_DIM, axis=1)

    const2 = lambda t, *_: (0, 0)
    const3 = lambda t, *_: (0, 0, 0)
    full2 = lambda shape: pl.BlockSpec(shape, const2)
    layer_block = lambda shape: pl.BlockSpec((None,) + shape, lambda t, *_: (layer, 0, 0),
                                             pipeline_mode=pl.Buffered(1))
    projected = lambda t, *_: (jnp.minimum(t, n_quads - 1), 0)
    finished = lambda t, *_: (jnp.maximum(t - 1, 0), 0)
    grid_spec = pltpu.PrefetchScalarGridSpec(
        num_scalar_prefetch=2,
        grid=(n_quads + 1,),
        in_specs=[
            pl.BlockSpec((QUAD, D_MODEL), projected),
            pl.BlockSpec((QUAD, D_MODEL), finished),
            layer_block((D_MODEL, 1)),
            layer_block((D_MODEL, w_in.shape[-1])),
            full2((SSD_CONV, SSD_XBC)),
            full2((1, SSD_XBC)),
            full2((SSD_HEADS, LANES)),
            full2((SSD_HEADS, LANES)),
            full2((1, SSD_INNER)),
            full2((CHUNK, CHUNK)),
            full2((CHUNK, 2 * SSD_INNER)),
            full2((CHUNK, CHUNK)),
            full2((1, GM_WIDTH)),
            full2((1, GM_WIDTH)),
            pl.BlockSpec((GM_GROUPS, CHUNK, CHUNK), const3),
            full2((CHUNK, GM_WIDTH)),
            layer_block((D_MODEL, 1)),
            layer_block((D_MODEL, D_MODEL)),
        ],
        out_specs=pl.BlockSpec((QUAD, D_MODEL), finished),
        scratch_shapes=[
            pltpu.VMEM((D_MODEL, P_TOTAL), BF16),
            pltpu.VMEM((D_MODEL, D_MODEL), BF16),
            pltpu.VMEM((PAIR, P_TOTAL), F32),
            pltpu.VMEM((PAIR, P_TOTAL), F32),
            pltpu.VMEM((PAIR, D_MODEL), BF16),
            pltpu.VMEM((PAIR, D_MODEL), BF16),
            pltpu.VMEM((PAIR, D_MODEL), BF16),
            pltpu.VMEM((PAIR, D_MODEL), BF16),
            pltpu.VMEM((SUBLANES + CHUNK, SSD_XBC), F32),
            pltpu.VMEM((SSD_GROUPS, SSD_STATE, GROUP_WIDTH), F32),
            pltpu.VMEM((CHUNK, ATT_KV_WIDTH), BF16),
            pltpu.VMEM((CHUNK, ATT_KV_WIDTH), BF16),
            pltpu.VMEM((2, ATT_Q_HEADS, CHUNK, CHUNK), F32),
            pltpu.VMEM((CHUNK, GM_GROUPS * CHUNK), BF16),
        ],
    )
    return pl.pallas_call(
        functools.partial(_mixer_kernel, chunks_per_seq=chunks_per_seq),
        out_shape=jax.ShapeDtypeStruct(x2d.shape, F32),
        grid_spec=grid_spec,
        compiler_params=pltpu.CompilerParams(
            dimension_semantics=("arbitrary",), vmem_limit_bytes=VMEM_LIMIT_MIXER),
        name="token_mixer",
    )(sinks.astype(F32), rel_bias.reshape(-1).astype(F32),
      x2d, x2d, in_gain, w_in, conv_w, row(conv_b), per_head_rows(dt_bias), per_head_rows(a_log),
      row(jnp.repeat(d_skip, SSD_HEAD_DIM)), triu, expand, bucket,
      row(ln_w), row(ln_b), sgu_w, sgu_b_e, out_gain, w_out)


def kernel(x, ffn1_norm, ffn1_w_gate, ffn1_w_up, ffn1_w_down, mix_norm, w_in, conv_w, conv_b, dt_bias, a_log, d_skip, ssd_norm, attn_sinks, rel_bias, attn_out_norm, sgu_ln_w, sgu_ln_b, sgu_w, sgu_b, sgu_out_norm, w_out, ffn2_norm, ffn2_w_gate, ffn2_w_up, ffn2_w_down, final_norm):
    batch, seq, d = x.shape
    depth = w_in.shape[0]
    fw = final_norm.reshape(1, d)
    x2 = x.reshape(batch * seq, d)
    ffn1_gain = ffn1_norm.reshape(depth, 1, d)
    ffn2_gain = ffn2_norm.reshape(depth, 1, d)
    mix_in_gain = mix_norm.reshape(depth, d, 1)
    att_cols = np.concatenate([np.arange(h * ATT_HEAD_DIM, (h + 1) * ATT_HEAD_DIM) for h in ATT_POS_HEADS])
    mix_out_gain = jnp.concatenate([ssd_norm, attn_out_norm[:, att_cols], sgu_out_norm], axis=1).reshape(depth, d, 1)
    for l in range(depth):
        x2 = _ffn(x2, l, ffn1_gain, ffn1_w_gate, ffn1_w_up, ffn1_w_down, fw, final_norm=False)
        x2 = _mixer(x2, seq, l, mix_in_gain, w_in, conv_w[l], conv_b[l], dt_bias[l],
                    a_log[l], d_skip[l], attn_sinks[l], rel_bias,
                    sgu_ln_w[l], sgu_ln_b[l], sgu_w[l], sgu_b[l], mix_out_gain, w_out)
        x2 = _ffn(x2, l, ffn2_gain, ffn2_w_gate, ffn2_w_up, ffn2_w_down, fw,
                  final_norm=(l == depth - 1))
    return x2.reshape(batch, seq, d)
```

```python
import functools
import math

import jax
import jax.numpy as jnp
import numpy as np
from jax import lax
from jax.experimental import pallas as pl
from jax.experimental.pallas import tpu as pltpu

F32 = jnp.float32
BF16 = jnp.bfloat16

D_MODEL = 1024
D_FF = 2816
EPS = 1e-6

SSD_HEADS = 8
SSD_HEAD_DIM = 64
SSD_INNER = SSD_HEADS * SSD_HEAD_DIM
SSD_GROUPS = 2
SSD_STATE = 128
SSD_CONV = 4
SSD_XBC = SSD_INNER + 2 * SSD_GROUPS * SSD_STATE
HEADS_PER_GROUP = SSD_HEADS // SSD_GROUPS
GROUP_WIDTH = HEADS_PER_GROUP * SSD_HEAD_DIM

ATT_Q_HEADS = 4
ATT_KV_HEADS = 2
ATT_HEAD_DIM = 64
ATT_WIDTH = ATT_Q_HEADS * ATT_HEAD_DIM
ATT_KV_WIDTH = ATT_KV_HEADS * ATT_HEAD_DIM
REL_BUCKETS = 32
REL_MAX_DIST = 128
ATT_POS_HEADS = (0, 2, 1, 3)

GM_GROUPS = 4
GM_GROUP_DIM = 64
GM_WIDTH = GM_GROUPS * GM_GROUP_DIM

CHUNK = 128
PAIR = 2 * CHUNK
QUAD = 2 * PAIR
PROJ_PIECE = 256
CONV_COLS = 256
MIX_SECTIONS = 18
MIX_LAG = 4

SUBLANES = 8
LANES = 128
HALF_LANES = LANES // 2
DT_PAD = LANES

P_Z = 0
P_XBC = P_Z + SSD_INNER
P_Q = P_XBC + SSD_XBC
P_K = P_Q + ATT_WIDTH
P_V = P_K + ATT_KV_WIDTH
P_U = P_V + ATT_KV_WIDTH
P_GV = P_U + GM_WIDTH
P_DT = P_GV + GM_WIDTH
P_TOTAL = P_DT + DT_PAD

R_XBC = SSD_INNER
R_DT = R_XBC + SSD_XBC
R_Q = R_DT + SSD_HEADS
R_K = R_Q + ATT_WIDTH

Y_ATT = SSD_INNER
Y_SGU = Y_ATT + ATT_WIDTH

MIX_READS = {
    **{i: (P_XBC + i * CONV_COLS, P_XBC + (i + 1) * CONV_COLS) for i in range(SSD_XBC // CONV_COLS)},
    4: (P_DT, P_TOTAL),
    8: (P_Z, P_Z + GROUP_WIDTH),
    9: (P_Z + GROUP_WIDTH, P_Z + 2 * GROUP_WIDTH),
    10: (P_Q, P_U),
    15: (P_U, P_GV),
    16: (P_GV, P_DT),
}

MASKED = -1e30
LOG2E = math.log2(math.e)

FFN_TOKENS = 1024
FFN_CHUNK = 256
FFN_PIECES = D_FF // FFN_CHUNK
VMEM_LIMIT_FFN = 56 * 1024 * 1024
VMEM_LIMIT_MIXER = 48 * 1024 * 1024


def _unit_rms(x):
    return x * lax.rsqrt(jnp.mean(x * x, axis=-1, keepdims=True) + EPS)


def _sigmoid(x):
    return 1.0 / (1.0 + jnp.exp(-x))


def _gelu_tanh(x):
    c = math.sqrt(2.0 / math.pi)
    return 0.5 * x * (1.0 + jnp.tanh(c * (x + 0.044715 * (x * x * x))))


def _top_bits(x):
    bits = lax.bitcast_convert_type(x, jnp.uint32) & jnp.uint32(0xFFFF0000)
    return lax.bitcast_convert_type(bits, F32)


def _split3(x):
    hi = _top_bits(x)
    r = x - hi
    mid = _top_bits(r)
    return hi, mid, r - mid


def _dot(a, b):
    return jnp.dot(a, b, preferred_element_type=F32)


def _dot_nt(a, b):
    return lax.dot_general(a, b, (((1,), (1,)), ((), ())), preferred_element_type=F32)


def _lane_cumsum(x, upper_ones):
    rows = x.shape[0]
    parts = _dot(jnp.concatenate(_split3(x), axis=0).astype(BF16), upper_ones)
    return parts[0:rows] + parts[rows:2 * rows] + parts[2 * rows:3 * rows]


def _ffn_kernel(x_ref, nw_ref, wg_ref, wu_ref, wd_ref, fw_ref, o_ref,
                wg_s, wu_s, wd_s, h_ref, xn_s, acc_s, *, final_norm):
    i = pl.program_id(0)

    def normalized(x):
        return (_unit_rms(x) * nw_ref[...]).astype(BF16)

    def gated(xn, wg, wu):
        g = _dot(xn, wg)
        u = _dot(xn, wu)
        return (g * _sigmoid(g) * u).astype(BF16)

    def finish(x, y):
        out = x + 0.5 * y
        if final_norm:
            out = _unit_rms(out) * fw_ref[...]
        o_ref[...] = out

    @pl.when(i < FFN_PIECES)
    def _first_rows_while_loading():
        wg_s[i] = wg_ref[...].astype(BF16)
        wu_s[i] = wu_ref[...].astype(BF16)
        wd_s[i] = wd_ref[...].astype(BF16)

        @pl.when(i == 0)
        def _start():
            xn_s[...] = normalized(x_ref[...])
            acc_s[...] = jnp.zeros_like(acc_s)

        acc_s[...] += _dot(gated(xn_s[...], wg_s[i], wu_s[i]), wd_s[i])

        @pl.when(i == FFN_PIECES - 1)
        def _end():
            finish(x_ref[...], acc_s[...])

    @pl.when(i >= FFN_PIECES)
    def _token_rows():
        x = x_ref[...]
        xn = normalized(x)
        for c in range(FFN_PIECES):
            h_ref[:, c * FFN_CHUNK:(c + 1) * FFN_CHUNK] = gated(xn, wg_s[c], wu_s[c])
        finish(x, _dot(h_ref[...], wd_s[...].reshape(D_FF, D_MODEL)))


def _ffn(x2d, layer, norm_w, w_gate, w_up, w_down, final_w, *, final_norm):
    tokens = x2d.shape[0]
    last = FFN_PIECES - 1
    rows = lambda i: (jnp.maximum(i - last, 0), 0)
    return pl.pallas_call(
        functools.partial(_ffn_kernel, final_norm=final_norm),
        out_shape=jax.ShapeDtypeStruct(x2d.shape, F32),
        grid=(last + tokens // FFN_TOKENS,),
        in_specs=[
            pl.BlockSpec((FFN_TOKENS, D_MODEL), rows),
            pl.BlockSpec((None, 1, D_MODEL), lambda i: (layer, 0, 0)),
            pl.BlockSpec((None, D_MODEL, FFN_CHUNK), lambda i: (layer, 0, jnp.minimum(i, last))),
            pl.BlockSpec((None, D_MODEL, FFN_CHUNK), lambda i: (layer, 0, jnp.minimum(i, last))),
            pl.BlockSpec((None, FFN_CHUNK, D_MODEL), lambda i: (layer, jnp.minimum(i, last), 0)),
            pl.BlockSpec((1, D_MODEL), lambda i: (0, 0)),
        ],
        out_specs=pl.BlockSpec((FFN_TOKENS, D_MODEL), rows),
        scratch_shapes=[
            pltpu.VMEM((FFN_PIECES, D_MODEL, FFN_CHUNK), BF16),
            pltpu.VMEM((FFN_PIECES, D_MODEL, FFN_CHUNK), BF16),
            pltpu.VMEM((FFN_PIECES, FFN_CHUNK, D_MODEL), BF16),
            pltpu.VMEM((FFN_TOKENS, D_FF), BF16),
            pltpu.VMEM((FFN_TOKENS, D_MODEL), BF16),
            pltpu.VMEM((FFN_TOKENS, D_MODEL), F32),
        ],
        compiler_params=pltpu.CompilerParams(
            dimension_semantics=("arbitrary",), vmem_limit_bytes=VMEM_LIMIT_FFN),
        name="ffn_halfstep",
    )(x2d, norm_w, w_gate, w_up, w_down, final_w)


def _mix_chunk(proj_ref, ycat_ref, keep, bias_idx, p):
    row = lax.broadcasted_iota(jnp.int32, (CHUNK, CHUNK), 0)
    lane = lax.broadcasted_iota(jnp.int32, (CHUNK, CHUNK), 1)
    causal = row >= lane
    row_b = row.astype(F32).astype(BF16)
    lane_b = lane.astype(F32).astype(BF16)
    causal_b = row_b >= lane_b
    upper_half_b = lane_b >= HALF_LANES
    zero_b = jnp.zeros((CHUNK, CHUNK), BF16)
    group_b = (lax.broadcasted_iota(jnp.int32, (CHUNK, GROUP_WIDTH), 1) // SSD_HEAD_DIM).astype(F32).astype(BF16)

    def block_diag(x_b):
        zero = jnp.zeros_like(x_b)
        return jnp.concatenate([jnp.where(group_b == i, x_b, zero) for i in range(HEADS_PER_GROUP)], axis=0)

    xbc_parts = []
    for c0 in range(0, SSD_XBC, CONV_COLS):
        cs = slice(c0, c0 + CONV_COLS)
        if keep is not None:
            p.ext[0:SUBLANES, cs] = p.ext[0:SUBLANES, cs] * keep
        raw = proj_ref[:, P_XBC + c0:P_XBC + c0 + CONV_COLS]
        p.ext[SUBLANES:, cs] = raw
        ext = p.ext[:, cs]
        w = p.convw[:, cs]
        ext1 = pltpu.roll(ext, 1, 0)
        older = w[1:2, :] * ext + w[0:1, :] * ext1
        conv = (p.convb[:, cs] + w[3:4, :] * raw + w[2:3, :] * ext1[SUBLANES:, :]
                + pltpu.roll(older, 2, 0)[SUBLANES:, :])
        p.ext[0:SUBLANES, cs] = raw[CHUNK - SUBLANES:, :]
        xbc_parts.append(conv * _sigmoid(conv))
        yield P_XBC + c0, P_XBC + c0 + CONV_COLS
    xbc = jnp.concatenate(xbc_parts, axis=-1)
    xs = xbc[:, :SSD_INNER]
    bm = xbc[:, SSD_INNER:SSD_INNER + SSD_GROUPS * SSD_STATE]
    cm = xbc[:, SSD_INNER + SSD_GROUPS * SSD_STATE:]
    x_b = xs.astype(BF16)
    bm_b = bm.astype(BF16)
    cm_b = cm.astype(BF16)

    dtr = proj_ref[:, P_DT:P_DT + DT_PAD].T[0:SSD_HEADS, :] + p.dtb[...]
    dt = jnp.maximum(dtr, 0.0) + jnp.log1p(jnp.exp(-jnp.abs(dtr)))
    acs = _lane_cumsum(dt * (-LOG2E * jnp.exp(p.alog[...])), p.triu[...])
    grow = jnp.exp2(acs)
    to_end = dt * jnp.exp2(acs[:, CHUNK - 1:CHUNK] - acs)
    src = acs - jnp.log2(dt)
    stack = jnp.concatenate(
        [acs, *_split3(grow), *_split3(to_end), jnp.zeros((CHUNK - 7 * SSD_HEADS, LANES), F32)], axis=0)
    cols = stack.T
    expanded = _dot(cols.astype(BF16), p.expand[...])
    grow_e = expanded[:, :SSD_INNER]
    to_end_e = expanded[:, SSD_INNER:]
    xd_b = (xs * to_end_e).astype(BF16)
    yield P_DT, P_TOTAL

    y_diag = []
    for g in range(SSD_GROUPS):
        n0 = g * SSD_STATE
        cb = _dot_nt(cm_b[:, n0:n0 + SSD_STATE], bm_b[:, n0:n0 + SSD_STATE])
        parts = []
        for hh in range(HEADS_PER_GROUP):
            h = g * HEADS_PER_GROUP + hh
            seg = cols[:, h:h + 1] - src[h:h + 1, :]
            parts.append((cb * jnp.exp2(jnp.where(causal, seg, -jnp.inf))).astype(BF16))
        w0 = g * GROUP_WIDTH
        y_diag.append(_dot(jnp.concatenate(parts, axis=-1), block_diag(x_b[:, w0:w0 + GROUP_WIDTH])))
        yield
    y_off = []
    for g in range(SSD_GROUPS):
        n0 = g * SSD_STATE
        w0 = g * GROUP_WIDTH
        prev = p.st[g]
        if keep is not None:
            prev = prev * keep
        y_off.append(_dot(cm_b[:, n0:n0 + SSD_STATE], prev.astype(BF16)))
        bt = bm[:, n0:n0 + SSD_STATE].T.astype(BF16)
        p.st[g] = (prev * grow_e[CHUNK - 1:CHUNK, w0:w0 + GROUP_WIDTH]
                   + _dot(bt, xd_b[:, w0:w0 + GROUP_WIDTH]))
    yield
    gated, sumsq = [], 0.0
    for g in range(SSD_GROUPS):
        ws = slice(g * GROUP_WIDTH, (g + 1) * GROUP_WIDTH)
        y = y_diag[g] + y_off[g] * grow_e[:, ws] + xs[:, ws] * p.dskip[:, ws]
        z = proj_ref[:, P_Z + g * GROUP_WIDTH:P_Z + (g + 1) * GROUP_WIDTH]
        gated.append(y * (z * _sigmoid(z)))
        sumsq = sumsq + jnp.sum(gated[g] * gated[g], axis=-1, keepdims=True)
        yield P_Z + g * GROUP_WIDTH, P_Z + (g + 1) * GROUP_WIDTH
    inv_rms = lax.rsqrt(sumsq * (1.0 / SSD_INNER) + EPS)
    for g in range(SSD_GROUPS):
        ycat_ref[:, g * GROUP_WIDTH:(g + 1) * GROUP_WIDTH] = (gated[g] * inv_rms).astype(BF16)

    q_b = (proj_ref[:, P_Q:P_Q + ATT_WIDTH] * (LOG2E / math.sqrt(ATT_HEAD_DIM))).astype(BF16)
    k_b = proj_ref[:, P_K:P_K + ATT_KV_WIDTH].astype(BF16)
    v_b = proj_ref[:, P_V:P_V + ATT_KV_WIDTH].astype(BF16)
    keys = jnp.concatenate([p.kprev[...], k_b], axis=0)
    v_prev = p.vprev[...]
    halves = lambda t: (jnp.where(upper_half_b, zero_b, t), jnp.where(upper_half_b, t, zero_b))
    v_prev_half, v_half = halves(v_prev), halves(v_b)
    p.kprev[...] = k_b
    p.vprev[...] = v_b
    probs, values = [], []
    for pos, hq in enumerate(ATT_POS_HEADS):
        tile, half = divmod(pos, 2)
        q_tile = q_b[:, tile * LANES:(tile + 1) * LANES]
        q_head = halves(q_tile)[half]
        s = _dot_nt(q_head, keys)
        s = jnp.where(causal, s[:, CHUNK:], s[:, :CHUNK]) + p.bias[bias_idx, pos]
        sink = p.sink[hq] * LOG2E
        m = jnp.maximum(jnp.max(s, axis=-1, keepdims=True), sink)
        e = jnp.exp2(s - m)
        denom = jnp.sum(e, axis=-1, keepdims=True) + jnp.exp2(sink - m)
        pn = (e * (1.0 / denom)).astype(BF16)
        probs.append(jnp.where(causal_b, zero_b, pn))
        probs.append(jnp.where(causal_b, pn, zero_b))
        for vh in (v_prev_half[half], v_half[half]):
            values.append(jnp.concatenate([zero_b, vh] if tile else [vh, zero_b], axis=-1))
        yield (P_Q, P_U) if pos == 0 else None
    att = _dot(jnp.concatenate(probs, axis=-1), jnp.concatenate(values, axis=0))
    ycat_ref[:, Y_ATT:Y_SGU] = _unit_rms(att).astype(BF16)
    yield

    u = _gelu_tanh(proj_ref[:, P_U:P_U + GM_WIDTH])
    yield P_U, P_GV
    gv = _gelu_tanh(proj_ref[:, P_GV:P_GV + GM_WIDTH])
    yield P_GV, P_DT
    mu = jnp.mean(gv, axis=-1, keepdims=True)
    gc = gv - mu
    gv = gc * lax.rsqrt(jnp.mean(gc * gc, axis=-1, keepdims=True) + EPS) * p.lnw[...] + p.lnb[...]
    gv_b = gv.astype(BF16)
    mixed = _dot(p.wcat[...], block_diag(gv_b)) + p.sgub[...]
    ycat_ref[:, Y_SGU:] = _unit_rms(u * mixed).astype(BF16)
    yield


class _MixerRefs:
    def __init__(self, **refs):
        self.__dict__.update(refs)


def _mixer_kernel(sink_ref, rel_ref,
                  xa_ref, xc_ref, ing_ref, win_ref, convw_ref, convb_ref, dtb_ref, alog_ref,
                  dskip_ref, triu_ref, expand_ref, bucket_ref,
                  lnw_ref, lnb_ref, sguw_ref, sgub_ref, outg_ref, wout_ref,
                  o_ref,
                  win_s, wout_s, projx_ref, projy_ref, ycatx_ref, ycaty_ref, xnx_ref, xny_ref,
                  ext_ref, st_ref, kprev_ref, vprev_ref, bias_ref, wcat_ref,
                  *, chunks_per_seq):
    t = pl.program_id(0)
    p = _MixerRefs(sink=sink_ref, convw=convw_ref, convb=convb_ref, dtb=dtb_ref, alog=alog_ref,
                   dskip=dskip_ref, triu=triu_ref, expand=expand_ref, lnw=lnw_ref, lnb=lnb_ref,
                   sgub=sgub_ref, ext=ext_ref, st=st_ref, kprev=kprev_ref, vprev=vprev_ref,
                   bias=bias_ref, wcat=wcat_ref)

    @pl.when(t == 0)
    def _init():
        row = lax.broadcasted_iota(jnp.int32, (CHUNK, CHUNK), 0)
        col = lax.broadcasted_iota(jnp.int32, (CHUNK, CHUNK), 1)
        bucket = bucket_ref[...]
        for pos, hq in enumerate(ATT_POS_HEADS):
            acc = jnp.zeros((CHUNK, CHUNK), F32)
            for k in range(REL_BUCKETS):
                acc = jnp.where(bucket == k, rel_ref[k * ATT_Q_HEADS + hq] * LOG2E, acc)
            bias_ref[0, pos] = acc
            bias_ref[1, pos] = jnp.where(row >= col, acc, MASKED)
        for g in range(GM_GROUPS):
            wcat_ref[:, g * CHUNK:(g + 1) * CHUNK] = jnp.where(row >= col, sguw_ref[g], 0.0).astype(BF16)

        def pack_rows(i, carry):
            r = pl.ds(pl.multiple_of(i * CHUNK, CHUNK), CHUNK)
            gain = ing_ref[r, :]
            win_s[r, 0:R_DT] = (win_ref[r, 0:R_DT] * gain).astype(BF16)
            tail = win_ref[r, R_DT:] * gain
            q0 = SSD_HEADS
            q = [tail[:, q0 + h * ATT_HEAD_DIM:q0 + (h + 1) * ATT_HEAD_DIM] for h in ATT_POS_HEADS]
            dt_tile = jnp.concatenate([tail[:, :q0], jnp.zeros((CHUNK, DT_PAD - SSD_HEADS), F32)], axis=1)
            win_s[r, R_DT:] = jnp.concatenate(q + [tail[:, q0 + ATT_WIDTH:], dt_tile], axis=1).astype(BF16)
            return carry

        lax.fori_loop(0, D_MODEL // CHUNK, pack_rows, 0)
        blocks = D_MODEL // ATT_HEAD_DIM
        first_att = Y_ATT // ATT_HEAD_DIM
        for j in range(blocks):
            src = first_att + ATT_POS_HEADS[j - first_att] if first_att <= j < first_att + ATT_Q_HEADS else j
            dst_rows = slice(j * ATT_HEAD_DIM, (j + 1) * ATT_HEAD_DIM)
            src_rows = slice(src * ATT_HEAD_DIM, (src + 1) * ATT_HEAD_DIM)
            wout_s[dst_rows, :] = (wout_ref[src_rows, :] * outg_ref[dst_rows, :]).astype(BF16)
        projy_ref[...] = jnp.zeros_like(projy_ref)
        ycatx_ref[...] = jnp.zeros_like(ycatx_ref)
        st_ref[...] = jnp.zeros_like(st_ref)
        ext_ref[0:SUBLANES, :] = jnp.zeros((SUBLANES, SSD_XBC), F32)
        kprev_ref[...] = jnp.zeros_like(kprev_ref)
        vprev_ref[...] = jnp.zeros_like(vprev_ref)

    starts_seq = lax.rem(4 * t, chunks_per_seq) == 0
    keep = jnp.where(starts_seq, 0.0, 1.0).astype(F32)
    first_idx = jnp.where(starts_seq, 1, 0).astype(jnp.int32)
    even, odd = pl.ds(0, CHUNK), pl.ds(CHUNK, CHUNK)

    def projection_pieces(rows, proj_new, ycat_out, xn):
        def normalize():
            xn[...] = _unit_rms(xa_ref[rows, :]).astype(BF16)

        def project(c0, c1):
            proj_new[:, c0:c1] = _dot(xn[...], win_s[:, c0:c1])

        def output(c0, c1):
            o_ref[rows, c0:c1] = xc_ref[rows, c0:c1] + _dot(ycat_out[...], wout_s[:, c0:c1])

        in_piece = lambda lo, hi: ("in", lo, hi, functools.partial(project, lo, hi))
        out_piece = lambda i: ("out", i * PROJ_PIECE, (i + 1) * PROJ_PIECE,
                               functools.partial(output, i * PROJ_PIECE, (i + 1) * PROJ_PIECE))
        xbc = [in_piece(P_XBC + c, P_XBC + c + PROJ_PIECE) for c in range(0, SSD_XBC, PROJ_PIECE)]
        z = [in_piece(P_Z + c, P_Z + c + PROJ_PIECE) for c in range(0, SSD_INNER, PROJ_PIECE)]
        return ([("norm", 0, 0, normalize)] + z + [out_piece(0)] + xbc[:2] + [out_piece(1)] + xbc[2:]
                + [out_piece(2), in_piece(P_Q, P_K), in_piece(P_K, P_U), out_piece(3),
                   in_piece(P_U, P_GV), in_piece(P_GV, P_TOTAL)])

    mixes = [
        _mix_chunk(projy_ref.at[even], ycaty_ref.at[even], None, 0, p),
        _mix_chunk(projy_ref.at[odd], ycaty_ref.at[odd], None, 0, p),
        _mix_chunk(projx_ref.at[even], ycatx_ref.at[even], keep, first_idx, p),
        _mix_chunk(projx_ref.at[odd], ycatx_ref.at[odd], None, 0, p),
    ]
    pieces = [projection_pieces(pl.ds(0, PAIR), projx_ref, ycatx_ref, xnx_ref),
              projection_pieces(pl.ds(PAIR, PAIR), projy_ref, ycaty_ref, xny_ref)]
    overlaps = lambda reads, piece: reads is not None and reads[0] < piece[2] and piece[1] < reads[1]
    done = [0] * len(mixes)
    issued = [0, 0]

    def mix_may_run(k):
        section = done[k]
        if section == MIX_SECTIONS:
            return False
        if k > 0 and done[k - 1] < min(section + MIX_LAG, MIX_SECTIONS):
            return False
        if k >= 2 and issued[0] < len(pieces[0]):
            return False
        return True

    def piece_may_issue(phase):
        if issued[phase] == len(pieces[phase]):
            return False
        piece = pieces[phase][issued[phase]]
        if phase == 1:
            if piece[0] == "in":
                return all(done[k] > s for k in (0, 1) for s, r in MIX_READS.items() if overlaps(r, piece))
            if piece[0] == "out":
                return done[0] == done[1] == MIX_SECTIONS
        return True

    while min(done) < MIX_SECTIONS or issued != [len(pieces[0]), len(pieces[1])]:
        progressed = False
        for k, mix in enumerate(mixes):
            if mix_may_run(k):
                assert next(mix) == MIX_READS.get(done[k]), "MIX_READS is stale"
                done[k] += 1
                progressed = True
        for phase in (0, 1):
            if piece_may_issue(phase):
                pieces[phase][issued[phase]][3]()
                issued[phase] += 1
                progressed = True
                break
        assert progressed, "mixer step schedule is stuck"
    for mix in mixes:
        assert next(mix, None) is None and mix.gi_frame is None, "MIX_SECTIONS is stale"


def _folded_bucket_tile():
    i = np.arange(CHUNK)[:, None]
    j = np.arange(CHUNK)[None, :]
    n = np.where(j <= i, i - j, i - j + CHUNK)
    max_exact = REL_BUCKETS // 2
    large = max_exact + (np.log(np.maximum(n, 1) / max_exact) / np.log(REL_MAX_DIST / max_exact)
                         * (REL_BUCKETS - max_exact)).astype(np.int32)
    large = np.minimum(large, REL_BUCKETS - 1)
    return np.where(n < max_exact, n, large).astype(np.int32)


def _mixer(x2d, seq, layer, in_gain, w_in, conv_w, conv_b, dt_bias, a_log, d_skip, sinks, rel_bias,
           ln_w, ln_b, sgu_w, sgu_b, out_gain, w_out):
    tokens = x2d.shape[0]
    n_quads = tokens // QUAD
    chunks_per_seq = seq // CHUNK
    assert tokens % QUAD == 0 and seq % QUAD == 0 and GM_GROUPS == HEADS_PER_GROUP
    assert P_Q == R_DT and w_in.shape[-1] == R_K + P_DT - P_K
    per_head_rows = lambda v: jnp.broadcast_to(v[:, None], (SSD_HEADS, LANES))
    row = lambda v: v.reshape(1, -1)
    expand = np.zeros((CHUNK, 2 * SSD_INNER), np.float32)
    for r in range(SSD_HEADS, 7 * SSD_HEADS):
        h = r % SSD_HEADS
        c0 = ((r // SSD_HEADS - 1) // 3) * SSD_INNER + h * SSD_HEAD_DIM
        expand[r, c0:c0 + SSD_HEAD_DIM] = 1.0
    expand = jnp.asarray(expand, BF16)
    triu = jnp.asarray(np.triu(np.ones((CHUNK, CHUNK), np.float32)), BF16)
    bucket = jnp.asarray(_folded_bucket_tile())
    sgu_b_e = jnp.repeat(jnp.transpose(sgu_b), GM_GROUP_DIM, axis=1)

    const2 = lambda t, *_: (0, 0)
    const3 = lambda t, *_: (0, 0, 0)
    full2 = lambda shape: pl.BlockSpec(shape, const2)
    layer_block = lambda shape: pl.BlockSpec((None,) + shape, lambda t, *_: (layer, 0, 0),
                                             pipeline_mode=pl.Buffered(1))
    projected = lambda t, *_: (jnp.minimum(t, n_quads - 1), 0)
    finished = lambda t, *_: (jnp.maximum(t - 1, 0), 0)
    grid_spec = pltpu.PrefetchScalarGridSpec(
        num_scalar_prefetch=2,
        grid=(n_quads + 1,),
        in_specs=[
            pl.BlockSpec((QUAD, D_MODEL), projected),
            pl.BlockSpec((QUAD, D_MODEL), finished),
            layer_block((D_MODEL, 1)),
            layer_block((D_MODEL, w_in.shape[-1])),
            full2((SSD_CONV, SSD_XBC)),
            full2((1, SSD_XBC)),
            full2((SSD_HEADS, LANES)),
            full2((SSD_HEADS, LANES)),
            full2((1, SSD_INNER)),
            full2((CHUNK, CHUNK)),
            full2((CHUNK, 2 * SSD_INNER)),
            full2((CHUNK, CHUNK)),
            full2((1, GM_WIDTH)),
            full2((1, GM_WIDTH)),
            pl.BlockSpec((GM_GROUPS, CHUNK, CHUNK), const3),
            full2((CHUNK, GM_WIDTH)),
            layer_block((D_MODEL, 1)),
            layer_block((D_MODEL, D_MODEL)),
        ],
        out_specs=pl.BlockSpec((QUAD, D_MODEL), finished),
        scratch_shapes=[
            pltpu.VMEM((D_MODEL, P_TOTAL), BF16),
            pltpu.VMEM((D_MODEL, D_MODEL), BF16),
            pltpu.VMEM((PAIR, P_TOTAL), F32),
            pltpu.VMEM((PAIR, P_TOTAL), F32),
            pltpu.VMEM((PAIR, D_MODEL), BF16),
            pltpu.VMEM((PAIR, D_MODEL), BF16),
            pltpu.VMEM((PAIR, D_MODEL), BF16),
            pltpu.VMEM((PAIR, D_MODEL), BF16),
            pltpu.VMEM((SUBLANES + CHUNK, SSD_XBC), F32),
            pltpu.VMEM((SSD_GROUPS, SSD_STATE, GROUP_WIDTH), F32),
            pltpu.VMEM((CHUNK, ATT_KV_WIDTH), BF16),
            pltpu.VMEM((CHUNK, ATT_KV_WIDTH), BF16),
            pltpu.VMEM((2, ATT_Q_HEADS, CHUNK, CHUNK), F32),
            pltpu.VMEM((CHUNK, GM_GROUPS * CHUNK), BF16),
        ],
    )
    return pl.pallas_call(
        functools.partial(_mixer_kernel, chunks_per_seq=chunks_per_seq),
        out_shape=jax.ShapeDtypeStruct(x2d.shape, F32),
        grid_spec=grid_spec,
        compiler_params=pltpu.CompilerParams(
            dimension_semantics=("arbitrary",), vmem_limit_bytes=VMEM_LIMIT_MIXER),
        name="token_mixer",
    )(sinks.astype(F32), rel_bias.reshape(-1).astype(F32),
      x2d, x2d, in_gain, w_in, conv_w, row(conv_b), per_head_rows(dt_bias), per_head_rows(a_log),
      row(jnp.repeat(d_skip, SSD_HEAD_DIM)), triu, expand, bucket,
      row(ln_w), row(ln_b), sgu_w, sgu_b_e, out_gain, w_out)


def kernel(x, ffn1_norm, ffn1_w_gate, ffn1_w_up, ffn1_w_down, mix_norm, w_in, conv_w, conv_b, dt_bias, a_log, d_skip, ssd_norm, attn_sinks, rel_bias, attn_out_norm, sgu_ln_w, sgu_ln_b, sgu_w, sgu_b, sgu_out_norm, w_out, ffn2_norm, ffn2_w_gate, ffn2_w_up, ffn2_w_down, final_norm):
    batch, seq, d = x.shape
    depth = w_in.shape[0]
    fw = final_norm.reshape(1, d)
    x2 = x.reshape(batch * seq, d)
    ffn1_gain = ffn1_norm.reshape(depth, 1, d)
    ffn2_gain = ffn2_norm.reshape(depth, 1, d)
    mix_in_gain = mix_norm.reshape(depth, d, 1)
    att_cols = np.concatenate([np.arange(h * ATT_HEAD_DIM, (h + 1) * ATT_HEAD_DIM) for h in ATT_POS_HEADS])
    mix_out_gain = jnp.concatenate([ssd_norm, attn_out_norm[:, att_cols], sgu_out_norm], axis=1).reshape(depth, d, 1)
    for l in range(depth):
        x2 = _ffn(x2, l, ffn1_gain, ffn1_w_gate, ffn1_w_up, ffn1_w_down, fw, final_norm=False)
        x2 = _mixer(x2, seq, l, mix_in_gain, w_in, conv_w[l], conv_b[l], dt_bias[l],
                    a_log[l], d_skip[l], attn_sinks[l], rel_bias,
                    sgu_ln_w[l], sgu_ln_b[l], sgu_w[l], sgu_b[l], mix_out_gain, w_out)
        x2 = _ffn(x2, l, ffn2_gain, ffn2_w_gate, ffn2_w_up, ffn2_w_down, fw,
                  final_norm=(l == depth - 1))
    return x2.reshape(batch, seq, d)
```

```python
import functools
import math

import jax
import jax.numpy as jnp
import numpy as np
from jax import lax
from jax.experimental import pallas as pl
from jax.experimental.pallas import tpu as pltpu

F32 = jnp.float32
BF16 = jnp.bfloat16

D_MODEL = 1024
D_FF = 2816
EPS = 1e-6

SSD_HEADS = 8
SSD_HEAD_DIM = 64
SSD_INNER = SSD_HEADS * SSD_HEAD_DIM
SSD_GROUPS = 2
SSD_STATE = 128
SSD_CONV = 4
SSD_XBC = SSD_INNER + 2 * SSD_GROUPS * SSD_STATE
HEADS_PER_GROUP = SSD_HEADS // SSD_GROUPS
GROUP_WIDTH = HEADS_PER_GROUP * SSD_HEAD_DIM

ATT_Q_HEADS = 4
ATT_KV_HEADS = 2
ATT_HEAD_DIM = 64
ATT_WIDTH = ATT_Q_HEADS * ATT_HEAD_DIM
ATT_KV_WIDTH = ATT_KV_HEADS * ATT_HEAD_DIM
REL_BUCKETS = 32
REL_MAX_DIST = 128
ATT_POS_HEADS = (0, 2, 1, 3)

GM_GROUPS = 4
GM_GROUP_DIM = 64
GM_WIDTH = GM_GROUPS * GM_GROUP_DIM

CHUNK = 128
PAIR = 2 * CHUNK
QUAD = 2 * PAIR
PROJ_PIECE = 256
CONV_COLS = 256
MIX_SECTIONS = 18
MIX_LAG = 4

SUBLANES = 8
LANES = 128
HALF_LANES = LANES // 2
DT_PAD = LANES

P_Z = 0
P_XBC = P_Z + SSD_INNER
P_Q = P_XBC + SSD_XBC
P_K = P_Q + ATT_WIDTH
P_V = P_K + ATT_KV_WIDTH
P_U = P_V + ATT_KV_WIDTH
P_GV = P_U + GM_WIDTH
P_DT = P_GV + GM_WIDTH
P_TOTAL = P_DT + DT_PAD

R_XBC = SSD_INNER
R_DT = R_XBC + SSD_XBC
R_Q = R_DT + SSD_HEADS
R_K = R_Q + ATT_WIDTH

Y_ATT = SSD_INNER
Y_SGU = Y_ATT + ATT_WIDTH

MIX_READS = {
    **{i: (P_XBC + i * CONV_COLS, P_XBC + (i + 1) * CONV_COLS) for i in range(SSD_XBC // CONV_COLS)},
    4: (P_DT, P_TOTAL),
    8: (P_Z, P_Z + GROUP_WIDTH),
    9: (P_Z + GROUP_WIDTH, P_Z + 2 * GROUP_WIDTH),
    10: (P_Q, P_U),
    15: (P_U, P_GV),
    16: (P_GV, P_DT),
}

MASKED = -1e30
LOG2E = math.log2(math.e)

FFN_TOKENS = 1024
FFN_CHUNK = 256
FFN_PIECES = D_FF // FFN_CHUNK
VMEM_LIMIT_FFN = 56 * 1024 * 1024
VMEM_LIMIT_MIXER = 48 * 1024 * 1024


def _unit_rms(x):
    return x * lax.rsqrt(jnp.mean(x * x, axis=-1, keepdims=True) + EPS)


def _sigmoid(x):
    return 1.0 / (1.0 + jnp.exp(-x))


def _gelu_tanh(x):
    c = math.sqrt(2.0 / math.pi)
    return 0.5 * x * (1.0 + jnp.tanh(c * (x + 0.044715 * (x * x * x))))


def _top_bits(x):
    bits = lax.bitcast_convert_type(x, jnp.uint32) & jnp.uint32(0xFFFF0000)
    return lax.bitcast_convert_type(bits, F32)


def _split3(x):
    hi = _top_bits(x)
    r = x - hi
    mid = _top_bits(r)
    return hi, mid, r - mid


def _dot(a, b):
    return jnp.dot(a, b, preferred_element_type=F32)


def _dot_nt(a, b):
    return lax.dot_general(a, b, (((1,), (1,)), ((), ())), preferred_element_type=F32)


def _lane_cumsum(x, upper_ones):
    rows = x.shape[0]
    parts = _dot(jnp.concatenate(_split3(x), axis=0).astype(BF16), upper_ones)
    return parts[0:rows] + parts[rows:2 * rows] + parts[2 * rows:3 * rows]


def _ffn_kernel(x_ref, nw_ref, wg_ref, wu_ref, wd_ref, fw_ref, o_ref,
                wg_s, wu_s, wd_s, h_ref, xn_s, acc_s, *, final_norm):
    i = pl.program_id(0)

    def normalized(x):
        return (_unit_rms(x) * nw_ref[...]).astype(BF16)

    def gated(xn, wg, wu):
        g = _dot(xn, wg)
        u = _dot(xn, wu)
        return (g * _sigmoid(g) * u).astype(BF16)

    def finish(rows, y):
        out = x_ref[rows, :] + 0.5 * y
        if final_norm:
            out = _unit_rms(out) * fw_ref[...]
        o_ref[rows, :] = out

    @pl.when(i < FFN_PIECES)
    def _first_rows_while_loading():
        wg_s[i] = wg_ref[...].astype(BF16)
        wu_s[i] = wu_ref[...].astype(BF16)
        wd_s[i] = wd_ref[...].astype(BF16)

        @pl.when(i == 0)
        def _start():
            xn_s[...] = normalized(x_ref[...])
            acc_s[...] = jnp.zeros_like(acc_s)

        acc_s[...] += _dot(gated(xn_s[...], wg_s[i], wu_s[i]), wd_s[i])

        @pl.when(i == FFN_PIECES - 1)
        def _end():
            finish(slice(None), acc_s[...])

    @pl.when(i >= FFN_PIECES)
    def _token_rows():
        xn = normalized(x_ref[...])
        for c in range(FFN_PIECES):
            h_ref[:, c * FFN_CHUNK:(c + 1) * FFN_CHUNK] = gated(xn, wg_s[c], wu_s[c])
        wd = wd_s[...].reshape(D_FF, D_MODEL)
        for r in range(0, FFN_TOKENS, FFN_TOKENS // 2):
            rows = slice(r, r + FFN_TOKENS // 2)
            finish(rows, _dot(h_ref[rows, :], wd))


def _ffn(x2d, layer, norm_w, w_gate, w_up, w_down, final_w, *, final_norm):
    tokens = x2d.shape[0]
    last = FFN_PIECES - 1
    rows = lambda i: (jnp.maximum(i - last, 0), 0)
    return pl.pallas_call(
        functools.partial(_ffn_kernel, final_norm=final_norm),
        out_shape=jax.ShapeDtypeStruct(x2d.shape, F32),
        grid=(last + tokens // FFN_TOKENS,),
        in_specs=[
            pl.BlockSpec((FFN_TOKENS, D_MODEL), rows),
            pl.BlockSpec((None, 1, D_MODEL), lambda i: (layer, 0, 0)),
            pl.BlockSpec((None, D_MODEL, FFN_CHUNK), lambda i: (layer, 0, jnp.minimum(i, last))),
            pl.BlockSpec((None, D_MODEL, FFN_CHUNK), lambda i: (layer, 0, jnp.minimum(i, last))),
            pl.BlockSpec((None, FFN_CHUNK, D_MODEL), lambda i: (layer, jnp.minimum(i, last), 0)),
            pl.BlockSpec((1, D_MODEL), lambda i: (0, 0)),
        ],
        out_specs=pl.BlockSpec((FFN_TOKENS, D_MODEL), rows),
        scratch_shapes=[
            pltpu.VMEM((FFN_PIECES, D_MODEL, FFN_CHUNK), BF16),
            pltpu.VMEM((FFN_PIECES, D_MODEL, FFN_CHUNK), BF16),
            pltpu.VMEM((FFN_PIECES, FFN_CHUNK, D_MODEL), BF16),
            pltpu.VMEM((FFN_TOKENS, D_FF), BF16),
            pltpu.VMEM((FFN_TOKENS, D_MODEL), BF16),
            pltpu.VMEM((FFN_TOKENS, D_MODEL), F32),
        ],
        compiler_params=pltpu.CompilerParams(
            dimension_semantics=("arbitrary",), vmem_limit_bytes=VMEM_LIMIT_FFN),
        name="ffn_halfstep",
    )(x2d, norm_w, w_gate, w_up, w_down, final_w)


def _mix_chunk(proj_ref, ycat_ref, keep, bias_idx, p):
    row = lax.broadcasted_iota(jnp.int32, (CHUNK, CHUNK), 0)
    lane = lax.broadcasted_iota(jnp.int32, (CHUNK, CHUNK), 1)
    causal = row >= lane
    row_b = row.astype(F32).astype(BF16)
    lane_b = lane.astype(F32).astype(BF16)
    causal_b = row_b >= lane_b
    upper_half_b = lane_b >= HALF_LANES
    zero_b = jnp.zeros((CHUNK, CHUNK), BF16)
    group_b = (lax.broadcasted_iota(jnp.int32, (CHUNK, GROUP_WIDTH), 1) // SSD_HEAD_DIM).astype(F32).astype(BF16)

    def block_diag(x_b):
        zero = jnp.zeros_like(x_b)
        return jnp.concatenate([jnp.where(group_b == i, x_b, zero) for i in range(HEADS_PER_GROUP)], axis=0)

    xbc_parts = []
    for c0 in range(0, SSD_XBC, CONV_COLS):
        cs = slice(c0, c0 + CONV_COLS)
        if keep is not None:
            p.ext[0:SUBLANES, cs] = p.ext[0:SUBLANES, cs] * keep
        raw = proj_ref[:, P_XBC + c0:P_XBC + c0 + CONV_COLS]
        p.ext[SUBLANES:, cs] = raw
        ext = p.ext[:, cs]
        w = p.convw[:, cs]
        ext1 = pltpu.roll(ext, 1, 0)
        older = w[1:2, :] * ext + w[0:1, :] * ext1
        conv = (p.convb[:, cs] + w[3:4, :] * raw + w[2:3, :] * ext1[SUBLANES:, :]
                + pltpu.roll(older, 2, 0)[SUBLANES:, :])
        p.ext[0:SUBLANES, cs] = raw[CHUNK - SUBLANES:, :]
        xbc_parts.append(conv * _sigmoid(conv))
        yield P_XBC + c0, P_XBC + c0 + CONV_COLS
    xbc = jnp.concatenate(xbc_parts, axis=-1)
    xs = xbc[:, :SSD_INNER]
    bm = xbc[:, SSD_INNER:SSD_INNER + SSD_GROUPS * SSD_STATE]
    cm = xbc[:, SSD_INNER + SSD_GROUPS * SSD_STATE:]
    x_b = xs.astype(BF16)
    bm_b = bm.astype(BF16)
    cm_b = cm.astype(BF16)

    dtr = proj_ref[:, P_DT:P_DT + DT_PAD].T[0:SSD_HEADS, :] + p.dtb[...]
    dt = jnp.maximum(dtr, 0.0) + jnp.log1p(jnp.exp(-jnp.abs(dtr)))
    acs = _lane_cumsum(dt * (-LOG2E * jnp.exp(p.alog[...])), p.triu[...])
    grow = jnp.exp2(acs)
    to_end = dt * jnp.exp2(acs[:, CHUNK - 1:CHUNK] - acs)
    src = acs - jnp.log2(dt)
    stack = jnp.concatenate(
        [acs, *_split3(grow), *_split3(to_end), jnp.zeros((CHUNK - 7 * SSD_HEADS, LANES), F32)], axis=0)
    cols = stack.T
    expanded = _dot(cols.astype(BF16), p.expand[...])
    grow_e = expanded[:, :SSD_INNER]
    to_end_e = expanded[:, SSD_INNER:]
    xd_b = (xs * to_end_e).astype(BF16)
    yield P_DT, P_TOTAL

    y_diag = []
    for g in range(SSD_GROUPS):
        n0 = g * SSD_STATE
        cb = _dot_nt(cm_b[:, n0:n0 + SSD_STATE], bm_b[:, n0:n0 + SSD_STATE])
        parts = []
        for hh in range(HEADS_PER_GROUP):
            h = g * HEADS_PER_GROUP + hh
            seg = cols[:, h:h + 1] - src[h:h + 1, :]
            parts.append((cb * jnp.exp2(jnp.where(causal, seg, -jnp.inf))).astype(BF16))
        w0 = g * GROUP_WIDTH
        y_diag.append(_dot(jnp.concatenate(parts, axis=-1), block_diag(x_b[:, w0:w0 + GROUP_WIDTH])))
        yield
    y_off = []
    for g in range(SSD_GROUPS):
        n0 = g * SSD_STATE
        w0 = g * GROUP_WIDTH
        prev = p.st[g]
        if keep is not None:
            prev = prev * keep
        y_off.append(_dot(cm_b[:, n0:n0 + SSD_STATE], prev.astype(BF16)))
        bt = bm[:, n0:n0 + SSD_STATE].T.astype(BF16)
        p.st[g] = (prev * grow_e[CHUNK - 1:CHUNK, w0:w0 + GROUP_WIDTH]
                   + _dot(bt, xd_b[:, w0:w0 + GROUP_WIDTH]))
    yield
    gated, sumsq = [], 0.0
    for g in range(SSD_GROUPS):
        ws = slice(g * GROUP_WIDTH, (g + 1) * GROUP_WIDTH)
        y = y_diag[g] + y_off[g] * grow_e[:, ws] + xs[:, ws] * p.dskip[:, ws]
        z = proj_ref[:, P_Z + g * GROUP_WIDTH:P_Z + (g + 1) * GROUP_WIDTH]
        gated.append(y * (z * _sigmoid(z)))
        sumsq = sumsq + jnp.sum(gated[g] * gated[g], axis=-1, keepdims=True)
        yield P_Z + g * GROUP_WIDTH, P_Z + (g + 1) * GROUP_WIDTH
    inv_rms = lax.rsqrt(sumsq * (1.0 / SSD_INNER) + EPS)
    for g in range(SSD_GROUPS):
        ycat_ref[:, g * GROUP_WIDTH:(g + 1) * GROUP_WIDTH] = (gated[g] * inv_rms).astype(BF16)

    q_b = (proj_ref[:, P_Q:P_Q + ATT_WIDTH] * (LOG2E / math.sqrt(ATT_HEAD_DIM))).astype(BF16)
    k_b = proj_ref[:, P_K:P_K + ATT_KV_WIDTH].astype(BF16)
    v_b = proj_ref[:, P_V:P_V + ATT_KV_WIDTH].astype(BF16)
    keys = jnp.concatenate([p.kprev[...], k_b], axis=0)
    v_prev = p.vprev[...]
    halves = lambda t: (jnp.where(upper_half_b, zero_b, t), jnp.where(upper_half_b, t, zero_b))
    v_prev_half, v_half = halves(v_prev), halves(v_b)
    p.kprev[...] = k_b
    p.vprev[...] = v_b
    probs, values = [], []
    for pos, hq in enumerate(ATT_POS_HEADS):
        tile, half = divmod(pos, 2)
        q_tile = q_b[:, tile * LANES:(tile + 1) * LANES]
        q_head = halves(q_tile)[half]
        s = _dot_nt(q_head, keys)
        s = jnp.where(causal, s[:, CHUNK:], s[:, :CHUNK]) + p.bias[bias_idx, pos]
        sink = p.sink[hq] * LOG2E
        m = jnp.maximum(jnp.max(s, axis=-1, keepdims=True), sink)
        e = jnp.exp2(s - m)
        denom = jnp.sum(e, axis=-1, keepdims=True) + jnp.exp2(sink - m)
        pn = (e * (1.0 / denom)).astype(BF16)
        probs.append(jnp.where(causal_b, zero_b, pn))
        probs.append(jnp.where(causal_b, pn, zero_b))
        for vh in (v_prev_half[half], v_half[half]):
            values.append(jnp.concatenate([zero_b, vh] if tile else [vh, zero_b], axis=-1))
        yield (P_Q, P_U) if pos == 0 else None
    att = _dot(jnp.concatenate(probs, axis=-1), jnp.concatenate(values, axis=0))
    ycat_ref[:, Y_ATT:Y_SGU] = _unit_rms(att).astype(BF16)
    yield

    u = _gelu_tanh(proj_ref[:, P_U:P_U + GM_WIDTH])
    yield P_U, P_GV
    gv = _gelu_tanh(proj_ref[:, P_GV:P_GV + GM_WIDTH])
    yield P_GV, P_DT
    mu = jnp.mean(gv, axis=-1, keepdims=True)
    gc = gv - mu
    gv = gc * lax.rsqrt(jnp.mean(gc * gc, axis=-1, keepdims=True) + EPS) * p.lnw[...] + p.lnb[...]
    gv_b = gv.astype(BF16)
    mixed = _dot(p.wcat[...], block_diag(gv_b)) + p.sgub[...]
    ycat_ref[:, Y_SGU:] = _unit_rms(u * mixed).astype(BF16)
    yield


class _MixerRefs:
    def __init__(self, **refs):
        self.__dict__.update(refs)


def _mixer_kernel(sink_ref, rel_ref,
                  xa_ref, xc_ref, ing_ref, win_ref, convw_ref, convb_ref, dtb_ref, alog_ref,
                  dskip_ref, triu_ref, expand_ref, bucket_ref,
                  lnw_ref, lnb_ref, sguw_ref, sgub_ref, outg_ref, wout_ref,
                  o_ref,
                  win_s, wout_s, projx_ref, projy_ref, ycatx_ref, ycaty_ref, xnx_ref, xny_ref,
                  ext_ref, st_ref, kprev_ref, vprev_ref, bias_ref, wcat_ref,
                  *, chunks_per_seq):
    t = pl.program_id(0)
    p = _MixerRefs(sink=sink_ref, convw=convw_ref, convb=convb_ref, dtb=dtb_ref, alog=alog_ref,
                   dskip=dskip_ref, triu=triu_ref, expand=expand_ref, lnw=lnw_ref, lnb=lnb_ref,
                   sgub=sgub_ref, ext=ext_ref, st=st_ref, kprev=kprev_ref, vprev=vprev_ref,
                   bias=bias_ref, wcat=wcat_ref)

    @pl.when(t == 0)
    def _init():
        row = lax.broadcasted_iota(jnp.int32, (CHUNK, CHUNK), 0)
        col = lax.broadcasted_iota(jnp.int32, (CHUNK, CHUNK), 1)
        bucket = bucket_ref[...]
        for pos, hq in enumerate(ATT_POS_HEADS):
            acc = jnp.zeros((CHUNK, CHUNK), F32)
            for k in range(REL_BUCKETS):
                acc = jnp.where(bucket == k, rel_ref[k * ATT_Q_HEADS + hq] * LOG2E, acc)
            bias_ref[0, pos] = acc
            bias_ref[1, pos] = jnp.where(row >= col, acc, MASKED)
        for g in range(GM_GROUPS):
            wcat_ref[:, g * CHUNK:(g + 1) * CHUNK] = jnp.where(row >= col, sguw_ref[g], 0.0).astype(BF16)

        def pack_rows(i, carry):
            r = pl.ds(pl.multiple_of(i * CHUNK, CHUNK), CHUNK)
            gain = ing_ref[r, :]
            win_s[r, 0:R_DT] = (win_ref[r, 0:R_DT] * gain).astype(BF16)
            tail = win_ref[r, R_DT:] * gain
            q0 = SSD_HEADS
            q = [tail[:, q0 + h * ATT_HEAD_DIM:q0 + (h + 1) * ATT_HEAD_DIM] for h in ATT_POS_HEADS]
            dt_tile = jnp.concatenate([tail[:, :q0], jnp.zeros((CHUNK, DT_PAD - SSD_HEADS), F32)], axis=1)
            win_s[r, R_DT:] = jnp.concatenate(q + [tail[:, q0 + ATT_WIDTH:], dt_tile], axis=1).astype(BF16)
            return carry

        lax.fori_loop(0, D_MODEL // CHUNK, pack_rows, 0)
        blocks = D_MODEL // ATT_HEAD_DIM
        first_att = Y_ATT // ATT_HEAD_DIM
        for j in range(blocks):
            src = first_att + ATT_POS_HEADS[j - first_att] if first_att <= j < first_att + ATT_Q_HEADS else j
            dst_rows = slice(j * ATT_HEAD_DIM, (j + 1) * ATT_HEAD_DIM)
            src_rows = slice(src * ATT_HEAD_DIM, (src + 1) * ATT_HEAD_DIM)
            wout_s[dst_rows, :] = (wout_ref[src_rows, :] * outg_ref[dst_rows, :]).astype(BF16)
        projy_ref[...] = jnp.zeros_like(projy_ref)
        ycatx_ref[...] = jnp.zeros_like(ycatx_ref)
        st_ref[...] = jnp.zeros_like(st_ref)
        ext_ref[0:SUBLANES, :] = jnp.zeros((SUBLANES, SSD_XBC), F32)
        kprev_ref[...] = jnp.zeros_like(kprev_ref)
        vprev_ref[...] = jnp.zeros_like(vprev_ref)

    starts_seq = lax.rem(4 * t, chunks_per_seq) == 0
    keep = jnp.where(starts_seq, 0.0, 1.0).astype(F32)
    first_idx = jnp.where(starts_seq, 1, 0).astype(jnp.int32)
    even, odd = pl.ds(0, CHUNK), pl.ds(CHUNK, CHUNK)

    def projection_pieces(rows, proj_new, ycat_out, xn):
        def normalize():
            xn[...] = _unit_rms(xa_ref[rows, :]).astype(BF16)

        def project(c0, c1):
            proj_new[:, c0:c1] = _dot(xn[...], win_s[:, c0:c1])

        def output(c0, c1):
            o_ref[rows, c0:c1] = xc_ref[rows, c0:c1] + _dot(ycat_out[...], wout_s[:, c0:c1])

        in_piece = lambda lo, hi: ("in", lo, hi, functools.partial(project, lo, hi))
        out_piece = lambda i: ("out", i * PROJ_PIECE, (i + 1) * PROJ_PIECE,
                               functools.partial(output, i * PROJ_PIECE, (i + 1) * PROJ_PIECE))
        xbc = [in_piece(P_XBC + c, P_XBC + c + PROJ_PIECE) for c in range(0, SSD_XBC, PROJ_PIECE)]
        z = [in_piece(P_Z + c, P_Z + c + PROJ_PIECE) for c in range(0, SSD_INNER, PROJ_PIECE)]
        return ([("norm", 0, 0, normalize)] + z + [out_piece(0)] + xbc[:2] + [out_piece(1)] + xbc[2:]
                + [out_piece(2), in_piece(P_Q, P_K), in_piece(P_K, P_U), out_piece(3),
                   in_piece(P_U, P_GV), in_piece(P_GV, P_TOTAL)])

    mixes = [
        _mix_chunk(projy_ref.at[even], ycaty_ref.at[even], None, 0, p),
        _mix_chunk(projy_ref.at[odd], ycaty_ref.at[odd], None, 0, p),
        _mix_chunk(projx_ref.at[even], ycatx_ref.at[even], keep, first_idx, p),
        _mix_chunk(projx_ref.at[odd], ycatx_ref.at[odd], None, 0, p),
    ]
    pieces = [projection_pieces(pl.ds(0, PAIR), projx_ref, ycatx_ref, xnx_ref),
              projection_pieces(pl.ds(PAIR, PAIR), projy_ref, ycaty_ref, xny_ref)]
    overlaps = lambda reads, piece: reads is not None and reads[0] < piece[2] and piece[1] < reads[1]
    done = [0] * len(mixes)
    issued = [0, 0]

    def mix_may_run(k):
        section = done[k]
        if section == MIX_SECTIONS:
            return False
        if k > 0 and done[k - 1] < min(section + MIX_LAG, MIX_SECTIONS):
            return False
        if k >= 2 and issued[0] < len(pieces[0]):
            return False
        return True

    def piece_may_issue(phase):
        if issued[phase] == len(pieces[phase]):
            return False
        piece = pieces[phase][issued[phase]]
        if phase == 1:
            if piece[0] == "in":
                return all(done[k] > s for k in (0, 1) for s, r in MIX_READS.items() if overlaps(r, piece))
            if piece[0] == "out":
                return done[0] == done[1] == MIX_SECTIONS
        return True

    while min(done) < MIX_SECTIONS or issued != [len(pieces[0]), len(pieces[1])]:
        progressed = False
        for k, mix in enumerate(mixes):
            if mix_may_run(k):
                assert next(mix) == MIX_READS.get(done[k]), "MIX_READS is stale"
                done[k] += 1
                progressed = True
        for phase in (0, 1):
            if piece_may_issue(phase):
                pieces[phase][issued[phase]][3]()
                issued[phase] += 1
                progressed = True
                break
        assert progressed, "mixer step schedule is stuck"
    for mix in mixes:
        assert next(mix, None) is None and mix.gi_frame is None, "MIX_SECTIONS is stale"


def _folded_bucket_tile():
    i = np.arange(CHUNK)[:, None]
    j = np.arange(CHUNK)[None, :]
    n = np.where(j <= i, i - j, i - j + CHUNK)
    max_exact = REL_BUCKETS // 2
    large = max_exact + (np.log(np.maximum(n, 1) / max_exact) / np.log(REL_MAX_DIST / max_exact)
                         * (REL_BUCKETS - max_exact)).astype(np.int32)
    large = np.minimum(large, REL_BUCKETS - 1)
    return np.where(n < max_exact, n, large).astype(np.int32)


def _mixer(x2d, seq, layer, in_gain, w_in, conv_w, conv_b, dt_bias, a_log, d_skip, sinks, rel_bias,
           ln_w, ln_b, sgu_w, sgu_b, out_gain, w_out):
    tokens = x2d.shape[0]
    n_quads = tokens // QUAD
    chunks_per_seq = seq // CHUNK
    assert tokens % QUAD == 0 and seq % QUAD == 0 and GM_GROUPS == HEADS_PER_GROUP
    assert P_Q == R_DT and w_in.shape[-1] == R_K + P_DT - P_K
    per_head_rows = lambda v: jnp.broadcast_to(v[:, None], (SSD_HEADS, LANES))
    row = lambda v: v.reshape(1, -1)
    expand = np.zeros((CHUNK, 2 * SSD_INNER), np.float32)
    for r in range(SSD_HEADS, 7 * SSD_HEADS):
        h = r % SSD_HEADS
        c0 = ((r // SSD_HEADS - 1) // 3) * SSD_INNER + h * SSD_HEAD_DIM
        expand[r, c0:c0 + SSD_HEAD_DIM] = 1.0
    expand = jnp.asarray(expand, BF16)
    triu = jnp.asarray(np.triu(np.ones((CHUNK, CHUNK), np.float32)), BF16)
    bucket = jnp.asarray(_folded_bucket_tile())
    sgu_b_e = jnp.repeat(jnp.transpose(sgu_b), GM_GROUP_DIM, axis=1)

    const2 = lambda t, *_: (0, 0)
    const3 = lambda t, *_: (0, 0, 0)
    full2 = lambda shape: pl.BlockSpec(shape, const2)
    layer_block = lambda shape: pl.BlockSpec((None,) + shape, lambda t, *_: (layer, 0, 0),
                                             pipeline_mode=pl.Buffered(1))
    projected = lambda t, *_: (jnp.minimum(t, n_quads - 1), 0)
    finished = lambda t, *_: (jnp.maximum(t - 1, 0), 0)
    grid_spec = pltpu.PrefetchScalarGridSpec(
        num_scalar_prefetch=2,
        grid=(n_quads + 1,),
        in_specs=[
            pl.BlockSpec((QUAD, D_MODEL), projected),
            pl.BlockSpec((QUAD, D_MODEL), finished),
            layer_block((D_MODEL, 1)),
            layer_block((D_MODEL, w_in.shape[-1])),
            full2((SSD_CONV, SSD_XBC)),
            full2((1, SSD_XBC)),
            full2((SSD_HEADS, LANES)),
            full2((SSD_HEADS, LANES)),
            full2((1, SSD_INNER)),
            full2((CHUNK, CHUNK)),
            full2((CHUNK, 2 * SSD_INNER)),
            full2((CHUNK, CHUNK)),
            full2((1, GM_WIDTH)),
            full2((1, GM_WIDTH)),
            pl.BlockSpec((GM_GROUPS, CHUNK, CHUNK), const3),
            full2((CHUNK, GM_WIDTH)),
            layer_block((D_MODEL, 1)),
            layer_block((D_MODEL, D_MODEL)),
        ],
        out_specs=pl.BlockSpec((QUAD, D_MODEL), finished),
        scratch_shapes=[
            pltpu.VMEM((D_MODEL, P_TOTAL), BF16),
            pltpu.VMEM((D_MODEL, D_MODEL), BF16),
            pltpu.VMEM((PAIR, P_TOTAL), F32),
            pltpu.VMEM((PAIR, P_TOTAL), F32),
            pltpu.VMEM((PAIR, D_MODEL), BF16),
            pltpu.VMEM((PAIR, D_MODEL), BF16),
            pltpu.VMEM((PAIR, D_MODEL), BF16),
            pltpu.VMEM((PAIR, D_MODEL), BF16),
            pltpu.VMEM((SUBLANES + CHUNK, SSD_XBC), F32),
            pltpu.VMEM((SSD_GROUPS, SSD_STATE, GROUP_WIDTH), F32),
            pltpu.VMEM((CHUNK, ATT_KV_WIDTH), BF16),
            pltpu.VMEM((CHUNK, ATT_KV_WIDTH), BF16),
            pltpu.VMEM((2, ATT_Q_HEADS, CHUNK, CHUNK), F32),
            pltpu.VMEM((CHUNK, GM_GROUPS * CHUNK), BF16),
        ],
    )
    return pl.pallas_call(
        functools.partial(_mixer_kernel, chunks_per_seq=chunks_per_seq),
        out_shape=jax.ShapeDtypeStruct(x2d.shape, F32),
        grid_spec=grid_spec,
        compiler_params=pltpu.CompilerParams(
            dimension_semantics=("arbitrary",), vmem_limit_bytes=VMEM_LIMIT_MIXER),
        name="token_mixer",
    )(sinks.astype(F32), rel_bias.reshape(-1).astype(F32),
      x2d, x2d, in_gain, w_in, conv_w, row(conv_b), per_head_rows(dt_bias), per_head_rows(a_log),
      row(jnp.repeat(d_skip, SSD_HEAD_DIM)), triu, expand, bucket,
      row(ln_w), row(ln_b), sgu_w, sgu_b_e, out_gain, w_out)


def kernel(x, ffn1_norm, ffn1_w_gate, ffn1_w_up, ffn1_w_down, mix_norm, w_in, conv_w, conv_b, dt_bias, a_log, d_skip, ssd_norm, attn_sinks, rel_bias, attn_out_norm, sgu_ln_w, sgu_ln_b, sgu_w, sgu_b, sgu_out_norm, w_out, ffn2_norm, ffn2_w_gate, ffn2_w_up, ffn2_w_down, final_norm):
    batch, seq, d = x.shape
    depth = w_in.shape[0]
    fw = final_norm.reshape(1, d)
    x2 = x.reshape(batch * seq, d)
    ffn1_gain = ffn1_norm.reshape(depth, 1, d)
    ffn2_gain = ffn2_norm.reshape(depth, 1, d)
    mix_in_gain = mix_norm.reshape(depth, d, 1)
    att_cols = np.concatenate([np.arange(h * ATT_HEAD_DIM, (h + 1) * ATT_HEAD_DIM) for h in ATT_POS_HEADS])
    mix_out_gain = jnp.concatenate([ssd_norm, attn_out_norm[:, att_cols], sgu_out_norm], axis=1).reshape(depth, d, 1)
    for l in range(depth):
        x2 = _ffn(x2, l, ffn1_gain, ffn1_w_gate, ffn1_w_up, ffn1_w_down, fw, final_norm=False)
        x2 = _mixer(x2, seq, l, mix_in_gain, w_in, conv_w[l], conv_b[l], dt_bias[l],
                    a_log[l], d_skip[l], attn_sinks[l], rel_bias,
                    sgu_ln_w[l], sgu_ln_b[l], sgu_w[l], sgu_b[l], mix_out_gain, w_out)
        x2 = _ffn(x2, l, ffn2_gain, ffn2_w_gate, ffn2_w_up, ffn2_w_down, fw,
                  final_norm=(l == depth - 1))
    return x2.reshape(batch, seq, d)
```

```python
import functools
import math

import jax
import jax.numpy as jnp
import numpy as np
from jax import lax
from jax.experimental import pallas as pl
from jax.experimental.pallas import tpu as pltpu

F32 = jnp.float32
BF16 = jnp.bfloat16

D_MODEL = 1024
D_FF = 2816
EPS = 1e-6

SSD_HEADS = 8
SSD_HEAD_DIM = 64
SSD_INNER = SSD_HEADS * SSD_HEAD_DIM
SSD_GROUPS = 2
SSD_STATE = 128
SSD_CONV = 4
SSD_XBC = SSD_INNER + 2 * SSD_GROUPS * SSD_STATE
HEADS_PER_GROUP = SSD_HEADS // SSD_GROUPS
GROUP_WIDTH = HEADS_PER_GROUP * SSD_HEAD_DIM

ATT_Q_HEADS = 4
ATT_KV_HEADS = 2
ATT_HEAD_DIM = 64
ATT_WIDTH = ATT_Q_HEADS * ATT_HEAD_DIM
ATT_KV_WIDTH = ATT_KV_HEADS * ATT_HEAD_DIM
REL_BUCKETS = 32
REL_MAX_DIST = 128
ATT_POS_HEADS = (0, 2, 1, 3)

GM_GROUPS = 4
GM_GROUP_DIM = 64
GM_WIDTH = GM_GROUPS * GM_GROUP_DIM

CHUNK = 128
PAIR = 2 * CHUNK
QUAD = 2 * PAIR
PROJ_PIECE = 256
CONV_COLS = 256
MIX_SECTIONS = 18
MIX_LAG = 4

SUBLANES = 8
LANES = 128
HALF_LANES = LANES // 2
DT_PAD = LANES

P_Z = 0
P_XBC = P_Z + SSD_INNER
P_Q = P_XBC + SSD_XBC
P_K = P_Q + ATT_WIDTH
P_V = P_K + ATT_KV_WIDTH
P_U = P_V + ATT_KV_WIDTH
P_GV = P_U + GM_WIDTH
P_DT = P_GV + GM_WIDTH
P_TOTAL = P_DT + DT_PAD

R_XBC = SSD_INNER
R_DT = R_XBC + SSD_XBC
R_Q = R_DT + SSD_HEADS
R_K = R_Q + ATT_WIDTH

Y_ATT = SSD_INNER
Y_SGU = Y_ATT + ATT_WIDTH

MIX_READS = {
    **{i: (P_XBC + i * CONV_COLS, P_XBC + (i + 1) * CONV_COLS) for i in range(SSD_XBC // CONV_COLS)},
    4: (P_DT, P_TOTAL),
    8: (P_Z, P_Z + GROUP_WIDTH),
    9: (P_Z + GROUP_WIDTH, P_Z + 2 * GROUP_WIDTH),
    10: (P_Q, P_U),
    15: (P_U, P_GV),
    16: (P_GV, P_DT),
}

MASKED = -1e30
LOG2E = math.log2(math.e)

FFN_TOKENS = 1024
FFN_CHUNK = 256
FFN_PIECES = D_FF // FFN_CHUNK
FFN_DOWN_ROWS = 128
VMEM_LIMIT_FFN = 56 * 1024 * 1024
VMEM_LIMIT_MIXER = 48 * 1024 * 1024


def _unit_rms(x):
    return x * lax.rsqrt(jnp.mean(x * x, axis=-1, keepdims=True) + EPS)


def _sigmoid(x):
    return 1.0 / (1.0 + jnp.exp(-x))


def _gelu_tanh(x):
    c = math.sqrt(2.0 / math.pi)
    return 0.5 * x * (1.0 + jnp.tanh(c * (x + 0.044715 * (x * x * x))))


def _top_bits(x):
    bits = lax.bitcast_convert_type(x, jnp.uint32) & jnp.uint32(0xFFFF0000)
    return lax.bitcast_convert_type(bits, F32)


def _split3(x):
    hi = _top_bits(x)
    r = x - hi
    mid = _top_bits(r)
    return hi, mid, r - mid


def _dot(a, b):
    return jnp.dot(a, b, preferred_element_type=F32)


def _dot_nt(a, b):
    return lax.dot_general(a, b, (((1,), (1,)), ((), ())), preferred_element_type=F32)


def _lane_cumsum(x, upper_ones):
    rows = x.shape[0]
    parts = _dot(jnp.concatenate(_split3(x), axis=0).astype(BF16), upper_ones)
    return parts[0:rows] + parts[rows:2 * rows] + parts[2 * rows:3 * rows]


def _ffn_kernel(*refs, final_norm, cast_next):
    x_ref, nw_ref, wg_ref, wu_ref, wd_ref, fw_ref = refs[:6]
    refs = refs[6:]
    if cast_next:
        ng_ref, nu_ref, nd_ref, o_ref, og_ref, ou_ref, od_ref = refs[:7]
        refs = refs[7:]
    else:
        o_ref = refs[0]
        refs = refs[1:]
    wg_s, wu_s, wd_s, h_ref, xn_s, acc_s = refs
    i = pl.program_id(0)

    def normalized(x):
        return (_unit_rms(x) * nw_ref[...]).astype(BF16)

    def gated(xn, wg, wu):
        g = _dot(xn, wg)
        u = _dot(xn, wu)
        return (g * _sigmoid(g) * u).astype(BF16)

    def finish(x, y):
        out = x + 0.5 * y
        if final_norm:
            out = _unit_rms(out) * fw_ref[...]
        o_ref[...] = out

    if cast_next:
        piece = i - (FFN_PIECES - 1)

        @pl.when(piece >= 0)
        def _cast_rows():
            og_ref[...] = ng_ref[...].astype(BF16)
            ou_ref[...] = nu_ref[...].astype(BF16)

        @pl.when(jnp.logical_and(piece >= 0, piece < D_FF // FFN_DOWN_ROWS))
        def _cast_down_rows():
            od_ref[...] = nd_ref[...].astype(BF16)

    @pl.when(i < FFN_PIECES)
    def _first_rows_while_loading():
        wg_s[i] = wg_ref[...].astype(BF16)
        wu_s[i] = wu_ref[...].astype(BF16)
        wd_s[i] = wd_ref[...].astype(BF16)

        @pl.when(i == 0)
        def _start():
            xn_s[...] = normalized(x_ref[...])
            acc_s[...] = jnp.zeros_like(acc_s)

        acc_s[...] += _dot(gated(xn_s[...], wg_s[i], wu_s[i]), wd_s[i])

        @pl.when(i == FFN_PIECES - 1)
        def _end():
            finish(x_ref[...], acc_s[...])

    @pl.when(i >= FFN_PIECES)
    def _token_rows():
        x = x_ref[...]
        xn = normalized(x)
        for c in range(FFN_PIECES):
            h_ref[:, c * FFN_CHUNK:(c + 1) * FFN_CHUNK] = gated(xn, wg_s[c], wu_s[c])
        finish(x, _dot(h_ref[...], wd_s[...].reshape(D_FF, D_MODEL)))


def _ffn(x2d, norm_w, layer, weights, next_weights, final_w, *, final_norm):
    tokens = x2d.shape[0]
    last = FFN_PIECES - 1
    steps = tokens // FFN_TOKENS
    rows = lambda i: (jnp.maximum(i - last, 0), 0)
    piece = lambda i: jnp.minimum(i, last)
    w_layer, w_gate, w_up, w_down = weights
    if w_layer is None:
        weight_specs = [
            pl.BlockSpec((D_MODEL, FFN_CHUNK), lambda i: (0, piece(i))),
            pl.BlockSpec((D_MODEL, FFN_CHUNK), lambda i: (0, piece(i))),
            pl.BlockSpec((FFN_CHUNK, D_MODEL), lambda i: (piece(i), 0)),
        ]
    else:
        weight_specs = [
            pl.BlockSpec((None, D_MODEL, FFN_CHUNK), lambda i: (w_layer, 0, piece(i))),
            pl.BlockSpec((None, D_MODEL, FFN_CHUNK), lambda i: (w_layer, 0, piece(i))),
            pl.BlockSpec((None, FFN_CHUNK, D_MODEL), lambda i: (w_layer, piece(i), 0)),
        ]
    operands = [x2d, norm_w, w_gate, w_up, w_down, final_w]
    in_specs = [
        pl.BlockSpec((FFN_TOKENS, D_MODEL), rows),
        pl.BlockSpec((None, 1, D_MODEL), lambda i: (layer, 0, 0)),
        *weight_specs,
        pl.BlockSpec((1, D_MODEL), lambda i: (0, 0)),
    ]
    out_shape = [jax.ShapeDtypeStruct(x2d.shape, F32)]
    out_specs = [pl.BlockSpec((FFN_TOKENS, D_MODEL), rows)]
    if next_weights is not None:
        n_layer, n_gate, n_up, n_down = next_weights
        up_rows = D_MODEL // steps
        down_pieces = D_FF // FFN_DOWN_ROWS
        assert D_MODEL % steps == 0 and up_rows % 16 == 0 and down_pieces <= steps
        up_piece = lambda i: jnp.maximum(i - last, 0)
        down_piece = lambda i: jnp.clip(i - last, 0, down_pieces - 1)
        operands += [n_gate, n_up, n_down]
        in_specs += [
            pl.BlockSpec((None, up_rows, D_FF), lambda i: (n_layer, up_piece(i), 0)),
            pl.BlockSpec((None, up_rows, D_FF), lambda i: (n_layer, up_piece(i), 0)),
            pl.BlockSpec((None, FFN_DOWN_ROWS, D_MODEL), lambda i: (n_layer, down_piece(i), 0)),
        ]
        out_shape += [jax.ShapeDtypeStruct((D_MODEL, D_FF), BF16),
                      jax.ShapeDtypeStruct((D_MODEL, D_FF), BF16),
                      jax.ShapeDtypeStruct((D_FF, D_MODEL), BF16)]
        out_specs += [
            pl.BlockSpec((up_rows, D_FF), lambda i: (up_piece(i), 0)),
            pl.BlockSpec((up_rows, D_FF), lambda i: (up_piece(i), 0)),
            pl.BlockSpec((FFN_DOWN_ROWS, D_MODEL), lambda i: (down_piece(i), 0)),
        ]
    outs = pl.pallas_call(
        functools.partial(_ffn_kernel, final_norm=final_norm, cast_next=next_weights is not None),
        out_shape=out_shape,
        grid=(last + steps,),
        in_specs=in_specs,
        out_specs=out_specs,
        scratch_shapes=[
            pltpu.VMEM((FFN_PIECES, D_MODEL, FFN_CHUNK), BF16),
            pltpu.VMEM((FFN_PIECES, D_MODEL, FFN_CHUNK), BF16),
            pltpu.VMEM((FFN_PIECES, FFN_CHUNK, D_MODEL), BF16),
            pltpu.VMEM((FFN_TOKENS, D_FF), BF16),
            pltpu.VMEM((FFN_TOKENS, D_MODEL), BF16),
            pltpu.VMEM((FFN_TOKENS, D_MODEL), F32),
        ],
        compiler_params=pltpu.CompilerParams(
            dimension_semantics=("arbitrary",), vmem_limit_bytes=VMEM_LIMIT_FFN),
        name="ffn_halfstep",
    )(*operands)
    if next_weights is None:
        return outs[0], None
    return outs[0], (None, outs[1], outs[2], outs[3])


def _mix_chunk(proj_ref, ycat_ref, keep, bias_idx, p):
    row = lax.broadcasted_iota(jnp.int32, (CHUNK, CHUNK), 0)
    lane = lax.broadcasted_iota(jnp.int32, (CHUNK, CHUNK), 1)
    causal = row >= lane
    row_b = row.astype(F32).astype(BF16)
    lane_b = lane.astype(F32).astype(BF16)
    causal_b = row_b >= lane_b
    upper_half_b = lane_b >= HALF_LANES
    zero_b = jnp.zeros((CHUNK, CHUNK), BF16)
    group_b = (lax.broadcasted_iota(jnp.int32, (CHUNK, GROUP_WIDTH), 1) // SSD_HEAD_DIM).astype(F32).astype(BF16)

    def block_diag(x_b):
        zero = jnp.zeros_like(x_b)
        return jnp.concatenate([jnp.where(group_b == i, x_b, zero) for i in range(HEADS_PER_GROUP)], axis=0)

    xbc_parts = []
    for c0 in range(0, SSD_XBC, CONV_COLS):
        cs = slice(c0, c0 + CONV_COLS)
        if keep is not None:
            p.ext[0:SUBLANES, cs] = p.ext[0:SUBLANES, cs] * keep
        raw = proj_ref[:, P_XBC + c0:P_XBC + c0 + CONV_COLS]
        p.ext[SUBLANES:, cs] = raw
        ext = p.ext[:, cs]
        w = p.convw[:, cs]
        ext1 = pltpu.roll(ext, 1, 0)
        older = w[1:2, :] * ext + w[0:1, :] * ext1
        conv = (p.convb[:, cs] + w[3:4, :] * raw + w[2:3, :] * ext1[SUBLANES:, :]
                + pltpu.roll(older, 2, 0)[SUBLANES:, :])
        p.ext[0:SUBLANES, cs] = raw[CHUNK - SUBLANES:, :]
        xbc_parts.append(conv * _sigmoid(conv))
        yield P_XBC + c0, P_XBC + c0 + CONV_COLS
    xbc = jnp.concatenate(xbc_parts, axis=-1)
    xs = xbc[:, :SSD_INNER]
    bm = xbc[:, SSD_INNER:SSD_INNER + SSD_GROUPS * SSD_STATE]
    cm = xbc[:, SSD_INNER + SSD_GROUPS * SSD_STATE:]
    x_b = xs.astype(BF16)
    bm_b = bm.astype(BF16)
    cm_b = cm.astype(BF16)

    dtr = proj_ref[:, P_DT:P_DT + DT_PAD].T[0:SSD_HEADS, :] + p.dtb[...]
    dt = jnp.maximum(dtr, 0.0) + jnp.log1p(jnp.exp(-jnp.abs(dtr)))
    acs = _lane_cumsum(dt * (-LOG2E * jnp.exp(p.alog[...])), p.triu[...])
    grow = jnp.exp2(acs)
    to_end = dt * jnp.exp2(acs[:, CHUNK - 1:CHUNK] - acs)
    src = acs - jnp.log2(dt)
    stack = jnp.concatenate(
        [acs, *_split3(grow), *_split3(to_end), jnp.zeros((CHUNK - 7 * SSD_HEADS, LANES), F32)], axis=0)
    cols = stack.T
    expanded = _dot(cols.astype(BF16), p.expand[...])
    grow_e = expanded[:, :SSD_INNER]
    to_end_e = expanded[:, SSD_INNER:]
    xd_b = (xs * to_end_e).astype(BF16)
    yield P_DT, P_TOTAL

    y_diag = []
    for g in range(SSD_GROUPS):
        n0 = g * SSD_STATE
        cb = _dot_nt(cm_b[:, n0:n0 + SSD_STATE], bm_b[:, n0:n0 + SSD_STATE])
        parts = []
        for hh in range(HEADS_PER_GROUP):
            h = g * HEADS_PER_GROUP + hh
            seg = cols[:, h:h + 1] - src[h:h + 1, :]
            parts.append((cb * jnp.exp2(jnp.where(causal, seg, -jnp.inf))).astype(BF16))
        w0 = g * GROUP_WIDTH
        y_diag.append(_dot(jnp.concatenate(parts, axis=-1), block_diag(x_b[:, w0:w0 + GROUP_WIDTH])))
        yield
    y_off = []
    for g in range(SSD_GROUPS):
        n0 = g * SSD_STATE
        w0 = g * GROUP_WIDTH
        prev = p.st[g]
        if keep is not None:
            prev = prev * keep
        y_off.append(_dot(cm_b[:, n0:n0 + SSD_STATE], prev.astype(BF16)))
        bt = bm[:, n0:n0 + SSD_STATE].T.astype(BF16)
        p.st[g] = (prev * grow_e[CHUNK - 1:CHUNK, w0:w0 + GROUP_WIDTH]
                   + _dot(bt, xd_b[:, w0:w0 + GROUP_WIDTH]))
    yield
    gated, sumsq = [], 0.0
    for g in range(SSD_GROUPS):
        ws = slice(g * GROUP_WIDTH, (g + 1) * GROUP_WIDTH)
        y = y_diag[g] + y_off[g] * grow_e[:, ws] + xs[:, ws] * p.dskip[:, ws]
        z = proj_ref[:, P_Z + g * GROUP_WIDTH:P_Z + (g + 1) * GROUP_WIDTH]
        gated.append(y * (z * _sigmoid(z)))
        sumsq = sumsq + jnp.sum(gated[g] * gated[g], axis=-1, keepdims=True)
        yield P_Z + g * GROUP_WIDTH, P_Z + (g + 1) * GROUP_WIDTH
    inv_rms = lax.rsqrt(sumsq * (1.0 / SSD_INNER) + EPS)
    for g in range(SSD_GROUPS):
        ycat_ref[:, g * GROUP_WIDTH:(g + 1) * GROUP_WIDTH] = (gated[g] * inv_rms).astype(BF16)

    q_b = (proj_ref[:, P_Q:P_Q + ATT_WIDTH] * (LOG2E / math.sqrt(ATT_HEAD_DIM))).astype(BF16)
    k_b = proj_ref[:, P_K:P_K + ATT_KV_WIDTH].astype(BF16)
    v_b = proj_ref[:, P_V:P_V + ATT_KV_WIDTH].astype(BF16)
    keys = jnp.concatenate([p.kprev[...], k_b], axis=0)
    v_prev = p.vprev[...]
    halves = lambda t: (jnp.where(upper_half_b, zero_b, t), jnp.where(upper_half_b, t, zero_b))
    v_prev_half, v_half = halves(v_prev), halves(v_b)
    p.kprev[...] = k_b
    p.vprev[...] = v_b
    probs, values = [], []
    for pos, hq in enumerate(ATT_POS_HEADS):
        tile, half = divmod(pos, 2)
        q_tile = q_b[:, tile * LANES:(tile + 1) * LANES]
        q_head = halves(q_tile)[half]
        s = _dot_nt(q_head, keys)
        s = jnp.where(causal, s[:, CHUNK:], s[:, :CHUNK]) + p.bias[bias_idx, pos]
        sink = p.sink[hq] * LOG2E
        m = jnp.maximum(jnp.max(s, axis=-1, keepdims=True), sink)
        e = jnp.exp2(s - m)
        denom = jnp.sum(e, axis=-1, keepdims=True) + jnp.exp2(sink - m)
        pn = (e * (1.0 / denom)).astype(BF16)
        probs.append(jnp.where(causal_b, zero_b, pn))
        probs.append(jnp.where(causal_b, pn, zero_b))
        for vh in (v_prev_half[half], v_half[half]):
            values.append(jnp.concatenate([zero_b, vh] if tile else [vh, zero_b], axis=-1))
        yield (P_Q, P_U) if pos == 0 else None
    att = _dot(jnp.concatenate(probs, axis=-1), jnp.concatenate(values, axis=0))
    ycat_ref[:, Y_ATT:Y_SGU] = _unit_rms(att).astype(BF16)
    yield

    u = _gelu_tanh(proj_ref[:, P_U:P_U + GM_WIDTH])
    yield P_U, P_GV
    gv = _gelu_tanh(proj_ref[:, P_GV:P_GV + GM_WIDTH])
    yield P_GV, P_DT
    mu = jnp.mean(gv, axis=-1, keepdims=True)
    gc = gv - mu
    gv = gc * lax.rsqrt(jnp.mean(gc * gc, axis=-1, keepdims=True) + EPS) * p.lnw[...] + p.lnb[...]
    gv_b = gv.astype(BF16)
    mixed = _dot(p.wcat[...], block_diag(gv_b)) + p.sgub[...]
    ycat_ref[:, Y_SGU:] = _unit_rms(u * mixed).astype(BF16)
    yield


class _MixerRefs:
    def __init__(self, **refs):
        self.__dict__.update(refs)


def _mixer_kernel(sink_ref, rel_ref,
                  xa_ref, xc_ref, ing_ref, win_ref, convw_ref, convb_ref, dtb_ref, alog_ref,
                  dskip_ref, triu_ref, expand_ref, bucket_ref,
                  lnw_ref, lnb_ref, sguw_ref, sgub_ref, outg_ref, wout_ref,
                  o_ref,
                  win_s, wout_s, projx_ref, projy_ref, ycatx_ref, ycaty_ref, xnx_ref, xny_ref,
                  ext_ref, st_ref, kprev_ref, vprev_ref, bias_ref, wcat_ref,
                  *, chunks_per_seq):
    t = pl.program_id(0)
    p = _MixerRefs(sink=sink_ref, convw=convw_ref, convb=convb_ref, dtb=dtb_ref, alog=alog_ref,
                   dskip=dskip_ref, triu=triu_ref, expand=expand_ref, lnw=lnw_ref, lnb=lnb_ref,
                   sgub=sgub_ref, ext=ext_ref, st=st_ref, kprev=kprev_ref, vprev=vprev_ref,
                   bias=bias_ref, wcat=wcat_ref)

    @pl.when(t == 0)
    def _init():
        row = lax.broadcasted_iota(jnp.int32, (CHUNK, CHUNK), 0)
        col = lax.broadcasted_iota(jnp.int32, (CHUNK, CHUNK), 1)
        bucket = bucket_ref[...]
        for pos, hq in enumerate(ATT_POS_HEADS):
            acc = jnp.zeros((CHUNK, CHUNK), F32)
            for k in range(REL_BUCKETS):
                acc = jnp.where(bucket == k, rel_ref[k * ATT_Q_HEADS + hq] * LOG2E, acc)
            bias_ref[0, pos] = acc
            bias_ref[1, pos] = jnp.where(row >= col, acc, MASKED)
        for g in range(GM_GROUPS):
            wcat_ref[:, g * CHUNK:(g + 1) * CHUNK] = jnp.where(row >= col, sguw_ref[g], 0.0).astype(BF16)

        def pack_rows(i, carry):
            r = pl.ds(pl.multiple_of(i * CHUNK, CHUNK), CHUNK)
            gain = ing_ref[r, :]
            win_s[r, 0:R_DT] = (win_ref[r, 0:R_DT] * gain).astype(BF16)
            tail = win_ref[r, R_DT:] * gain
            q0 = SSD_HEADS
            q = [tail[:, q0 + h * ATT_HEAD_DIM:q0 + (h + 1) * ATT_HEAD_DIM] for h in ATT_POS_HEADS]
            dt_tile = jnp.concatenate([tail[:, :q0], jnp.zeros((CHUNK, DT_PAD - SSD_HEADS), F32)], axis=1)
            win_s[r, R_DT:] = jnp.concatenate(q + [tail[:, q0 + ATT_WIDTH:], dt_tile], axis=1).astype(BF16)
            return carry

        lax.fori_loop(0, D_MODEL // CHUNK, pack_rows, 0)
        blocks = D_MODEL // ATT_HEAD_DIM
        first_att = Y_ATT // ATT_HEAD_DIM
        for j in range(blocks):
            src = first_att + ATT_POS_HEADS[j - first_att] if first_att <= j < first_att + ATT_Q_HEADS else j
            dst_rows = slice(j * ATT_HEAD_DIM, (j + 1) * ATT_HEAD_DIM)
            src_rows = slice(src * ATT_HEAD_DIM, (src + 1) * ATT_HEAD_DIM)
            wout_s[dst_rows, :] = (wout_ref[src_rows, :] * outg_ref[dst_rows, :]).astype(BF16)
        projy_ref[...] = jnp.zeros_like(projy_ref)
        ycatx_ref[...] = jnp.zeros_like(ycatx_ref)
        st_ref[...] = jnp.zeros_like(st_ref)
        ext_ref[0:SUBLANES, :] = jnp.zeros((SUBLANES, SSD_XBC), F32)
        kprev_ref[...] = jnp.zeros_like(kprev_ref)
        vprev_ref[...] = jnp.zeros_like(vprev_ref)

    starts_seq = lax.rem(4 * t, chunks_per_seq) == 0
    keep = jnp.where(starts_seq, 0.0, 1.0).astype(F32)
    first_idx = jnp.where(starts_seq, 1, 0).astype(jnp.int32)
    even, odd = pl.ds(0, CHUNK), pl.ds(CHUNK, CHUNK)

    def projection_pieces(rows, proj_new, ycat_out, xn):
        def normalize():
            xn[...] = _unit_rms(xa_ref[rows, :]).astype(BF16)

        def project(c0, c1):
            proj_new[:, c0:c1] = _dot(xn[...], win_s[:, c0:c1])

        def output(c0, c1):
            o_ref[rows, c0:c1] = xc_ref[rows, c0:c1] + _dot(ycat_out[...], wout_s[:, c0:c1])

        in_piece = lambda lo, hi: ("in", lo, hi, functools.partial(project, lo, hi))
        out_piece = lambda i: ("out", i * PROJ_PIECE, (i + 1) * PROJ_PIECE,
                               functools.partial(output, i * PROJ_PIECE, (i + 1) * PROJ_PIECE))
        xbc = [in_piece(P_XBC + c, P_XBC + c + PROJ_PIECE) for c in range(0, SSD_XBC, PROJ_PIECE)]
        z = [in_piece(P_Z + c, P_Z + c + PROJ_PIECE) for c in range(0, SSD_INNER, PROJ_PIECE)]
        return ([("norm", 0, 0, normalize)] + z + [out_piece(0)] + xbc[:2] + [out_piece(1)] + xbc[2:]
                + [out_piece(2), in_piece(P_Q, P_K), in_piece(P_K, P_U), out_piece(3),
                   in_piece(P_U, P_GV), in_piece(P_GV, P_TOTAL)])

    mixes = [
        _mix_chunk(projy_ref.at[even], ycaty_ref.at[even], None, 0, p),
        _mix_chunk(projy_ref.at[odd], ycaty_ref.at[odd], None, 0, p),
        _mix_chunk(projx_ref.at[even], ycatx_ref.at[even], keep, first_idx, p),
        _mix_chunk(projx_ref.at[odd], ycatx_ref.at[odd], None, 0, p),
    ]
    pieces = [projection_pieces(pl.ds(0, PAIR), projx_ref, ycatx_ref, xnx_ref),
              projection_pieces(pl.ds(PAIR, PAIR), projy_ref, ycaty_ref, xny_ref)]
    overlaps = lambda reads, piece: reads is not None and reads[0] < piece[2] and piece[1] < reads[1]
    done = [0] * len(mixes)
    issued = [0, 0]

    def mix_may_run(k):
        section = done[k]
        if section == MIX_SECTIONS:
            return False
        if k > 0 and done[k - 1] < min(section + MIX_LAG, MIX_SECTIONS):
            return False
        if k >= 2 and issued[0] < len(pieces[0]):
            return False
        return True

    def piece_may_issue(phase):
        if issued[phase] == len(pieces[phase]):
            return False
        piece = pieces[phase][issued[phase]]
        if phase == 1:
            if piece[0] == "in":
                return all(done[k] > s for k in (0, 1) for s, r in MIX_READS.items() if overlaps(r, piece))
            if piece[0] == "out":
                return done[0] == done[1] == MIX_SECTIONS
        return True

    while min(done) < MIX_SECTIONS or issued != [len(pieces[0]), len(pieces[1])]:
        progressed = False
        for k, mix in enumerate(mixes):
            if mix_may_run(k):
                assert next(mix) == MIX_READS.get(done[k]), "MIX_READS is stale"
                done[k] += 1
                progressed = True
        for phase in (0, 1):
            if piece_may_issue(phase):
                pieces[phase][issued[phase]][3]()
                issued[phase] += 1
                progressed = True
                break
        assert progressed, "mixer step schedule is stuck"
    for mix in mixes:
        assert next(mix, None) is None and mix.gi_frame is None, "MIX_SECTIONS is stale"


def _folded_bucket_tile():
    i = np.arange(CHUNK)[:, None]
    j = np.arange(CHUNK)[None, :]
    n = np.where(j <= i, i - j, i - j + CHUNK)
    max_exact = REL_BUCKETS // 2
    large = max_exact + (np.log(np.maximum(n, 1) / max_exact) / np.log(REL_MAX_DIST / max_exact)
                         * (REL_BUCKETS - max_exact)).astype(np.int32)
    large = np.minimum(large, REL_BUCKETS - 1)
    return np.where(n < max_exact, n, large).astype(np.int32)


def _mixer(x2d, seq, layer, in_gain, w_in, conv_w, conv_b, dt_bias, a_log, d_skip, sinks, rel_bias,
           ln_w, ln_b, sgu_w, sgu_b, out_gain, w_out):
    tokens = x2d.shape[0]
    n_quads = tokens // QUAD
    chunks_per_seq = seq // CHUNK
    assert tokens % QUAD == 0 and seq % QUAD == 0 and GM_GROUPS == HEADS_PER_GROUP
    assert P_Q == R_DT and w_in.shape[-1] == R_K + P_DT - P_K
    per_head_rows = lambda v: jnp.broadcast_to(v[:, None], (SSD_HEADS, LANES))
    row = lambda v: v.reshape(1, -1)
    expand = np.zeros((CHUNK, 2 * SSD_INNER), np.float32)
    for r in range(SSD_HEADS, 7 * SSD_HEADS):
        h = r % SSD_HEADS
        c0 = ((r // SSD_HEADS - 1) // 3) * SSD_INNER + h * SSD_HEAD_DIM
        expand[r, c0:c0 + SSD_HEAD_DIM] = 1.0
    expand = jnp.asarray(expand, BF16)
    triu = jnp.asarray(np.triu(np.ones((CHUNK, CHUNK), np.float32)), BF16)
    bucket = jnp.asarray(_folded_bucket_tile())
    sgu_b_e = jnp.repeat(jnp.transpose(sgu_b), GM_GROUP_DIM, axis=1)

    const2 = lambda t, *_: (0, 0)
    const3 = lambda t, *_: (0, 0, 0)
    full2 = lambda shape: pl.BlockSpec(shape, const2)
    layer_block = lambda shape: pl.BlockSpec((None,) + shape, lambda t, *_: (layer, 0, 0),
                                             pipeline_mode=pl.Buffered(1))
    projected = lambda t, *_: (jnp.minimum(t, n_quads - 1), 0)
    finished = lambda t, *_: (jnp.maximum(t - 1, 0), 0)
    grid_spec = pltpu.PrefetchScalarGridSpec(
        num_scalar_prefetch=2,
        grid=(n_quads + 1,),
        in_specs=[
            pl.BlockSpec((QUAD, D_MODEL), projected),
            pl.BlockSpec((QUAD, D_MODEL), finished),
            layer_block((D_MODEL, 1)),
            layer_block((D_MODEL, w_in.shape[-1])),
            full2((SSD_CONV, SSD_XBC)),
            full2((1, SSD_XBC)),
            full2((SSD_HEADS, LANES)),
            full2((SSD_HEADS, LANES)),
            full2((1, SSD_INNER)),
            full2((CHUNK, CHUNK)),
            full2((CHUNK, 2 * SSD_INNER)),
            full2((CHUNK, CHUNK)),
            full2((1, GM_WIDTH)),
            full2((1, GM_WIDTH)),
            pl.BlockSpec((GM_GROUPS, CHUNK, CHUNK), const3),
            full2((CHUNK, GM_WIDTH)),
            layer_block((D_MODEL, 1)),
            layer_block((D_MODEL, D_MODEL)),
        ],
        out_specs=pl.BlockSpec((QUAD, D_MODEL), finished),
        scratch_shapes=[
            pltpu.VMEM((D_MODEL, P_TOTAL), BF16),
            pltpu.VMEM((D_MODEL, D_MODEL), BF16),
            pltpu.VMEM((PAIR, P_TOTAL), F32),
            pltpu.VMEM((PAIR, P_TOTAL), F32),
            pltpu.VMEM((PAIR, D_MODEL), BF16),
            pltpu.VMEM((PAIR, D_MODEL), BF16),
            pltpu.VMEM((PAIR, D_MODEL), BF16),
            pltpu.VMEM((PAIR, D_MODEL), BF16),
            pltpu.VMEM((SUBLANES + CHUNK, SSD_XBC), F32),
            pltpu.VMEM((SSD_GROUPS, SSD_STATE, GROUP_WIDTH), F32),
            pltpu.VMEM((CHUNK, ATT_KV_WIDTH), BF16),
            pltpu.VMEM((CHUNK, ATT_KV_WIDTH), BF16),
            pltpu.VMEM((2, ATT_Q_HEADS, CHUNK, CHUNK), F32),
            pltpu.VMEM((CHUNK, GM_GROUPS * CHUNK), BF16),
        ],
    )
    return pl.pallas_call(
        functools.partial(_mixer_kernel, chunks_per_seq=chunks_per_seq),
        out_shape=jax.ShapeDtypeStruct(x2d.shape, F32),
        grid_spec=grid_spec,
        compiler_params=pltpu.CompilerParams(
            dimension_semantics=("arbitrary",), vmem_limit_bytes=VMEM_LIMIT_MIXER),
        name="token_mixer",
    )(sinks.astype(F32), rel_bias.reshape(-1).astype(F32),
      x2d, x2d, in_gain, w_in, conv_w, row(conv_b), per_head_rows(dt_bias), per_head_rows(a_log),
      row(jnp.repeat(d_skip, SSD_HEAD_DIM)), triu, expand, bucket,
      row(ln_w), row(ln_b), sgu_w, sgu_b_e, out_gain, w_out)


def kernel(x, ffn1_norm, ffn1_w_gate, ffn1_w_up, ffn1_w_down, mix_norm, w_in, conv_w, conv_b, dt_bias, a_log, d_skip, ssd_norm, attn_sinks, rel_bias, attn_out_norm, sgu_ln_w, sgu_ln_b, sgu_w, sgu_b, sgu_out_norm, w_out, ffn2_norm, ffn2_w_gate, ffn2_w_up, ffn2_w_down, final_norm):
    batch, seq, d = x.shape
    depth = w_in.shape[0]
    fw = final_norm.reshape(1, d)
    x2 = x.reshape(batch * seq, d)
    ffn1_gain = ffn1_norm.reshape(depth, 1, d)
    ffn2_gain = ffn2_norm.reshape(depth, 1, d)
    mix_in_gain = mix_norm.reshape(depth, d, 1)
    att_cols = np.concatenate([np.arange(h * ATT_HEAD_DIM, (h + 1) * ATT_HEAD_DIM) for h in ATT_POS_HEADS])
    mix_out_gain = jnp.concatenate([ssd_norm, attn_out_norm[:, att_cols], sgu_out_norm], axis=1).reshape(depth, d, 1)
    ffn1 = lambda l: (l, ffn1_w_gate, ffn1_w_up, ffn1_w_down)
    ffn2 = lambda l: (l, ffn2_w_gate, ffn2_w_up, ffn2_w_down)
    weights = ffn1(0)
    for l in range(depth):
        x2, weights = _ffn(x2, ffn1_gain, l, weights, ffn2(l), fw, final_norm=False)
        x2 = _mixer(x2, seq, l, mix_in_gain, w_in, conv_w[l], conv_b[l], dt_bias[l],
                    a_log[l], d_skip[l], attn_sinks[l], rel_bias,
                    sgu_ln_w[l], sgu_ln_b[l], sgu_w[l], sgu_b[l], mix_out_gain, w_out)
        last_layer = l == depth - 1
        x2, weights = _ffn(x2, ffn2_gain, l, weights, None if last_layer else ffn1(l + 1), fw,
                           final_norm=last_layer)
    return x2.reshape(batch, seq, d)
```

```python
import functools
import math

import jax
import jax.numpy as jnp
import numpy as np
from jax import lax
from jax.experimental import pallas as pl
from jax.experimental.pallas import tpu as pltpu

F32 = jnp.float32
BF16 = jnp.bfloat16

D_MODEL = 1024
D_FF = 2816
EPS = 1e-6

SSD_HEADS = 8
SSD_HEAD_DIM = 64
SSD_INNER = SSD_HEADS * SSD_HEAD_DIM
SSD_GROUPS = 2
SSD_STATE = 128
SSD_CONV = 4
SSD_XBC = SSD_INNER + 2 * SSD_GROUPS * SSD_STATE
HEADS_PER_GROUP = SSD_HEADS // SSD_GROUPS
GROUP_WIDTH = HEADS_PER_GROUP * SSD_HEAD_DIM

ATT_Q_HEADS = 4
ATT_KV_HEADS = 2
ATT_HEAD_DIM = 64
ATT_WIDTH = ATT_Q_HEADS * ATT_HEAD_DIM
ATT_KV_WIDTH = ATT_KV_HEADS * ATT_HEAD_DIM
REL_BUCKETS = 32
REL_MAX_DIST = 128
ATT_POS_HEADS = (0, 2, 1, 3)

GM_GROUPS = 4
GM_GROUP_DIM = 64
GM_WIDTH = GM_GROUPS * GM_GROUP_DIM

CHUNK = 128
PAIR = 2 * CHUNK
QUAD = 2 * PAIR
PROJ_PIECE = 256
CONV_COLS = 256
MIX_SECTIONS = 18
MIX_LAG = 4

SUBLANES = 8
LANES = 128
HALF_LANES = LANES // 2
DT_PAD = LANES
DT_ROWS = 16

P_Z = 0
P_XBC = P_Z + SSD_INNER
P_Q = P_XBC + SSD_XBC
P_K = P_Q + ATT_WIDTH
P_V = P_K + ATT_KV_WIDTH
P_U = P_V + ATT_KV_WIDTH
P_GV = P_U + GM_WIDTH
P_DT = P_GV + GM_WIDTH
P_TOTAL = P_DT + DT_PAD

R_XBC = SSD_INNER
R_DT = R_XBC + SSD_XBC
R_Q = R_DT + SSD_HEADS
R_K = R_Q + ATT_WIDTH

Y_ATT = SSD_INNER
Y_SGU = Y_ATT + ATT_WIDTH

MIX_READS = {
    **{i: (P_XBC + i * CONV_COLS, P_XBC + (i + 1) * CONV_COLS) for i in range(SSD_XBC // CONV_COLS)},
    4: (P_DT, P_TOTAL),
    8: (P_Z, P_Z + GROUP_WIDTH),
    9: (P_Z + GROUP_WIDTH, P_Z + 2 * GROUP_WIDTH),
    10: (P_Q, P_U),
    15: (P_U, P_GV),
    16: (P_GV, P_DT),
}

MASKED = -1e30
LOG2E = math.log2(math.e)

FFN_TOKENS = 1024
FFN_CHUNK = 256
FFN_PIECES = D_FF // FFN_CHUNK
VMEM_LIMIT_FFN = 56 * 1024 * 1024
VMEM_LIMIT_MIXER = 48 * 1024 * 1024


def _unit_rms(x):
    return x * lax.rsqrt(jnp.mean(x * x, axis=-1, keepdims=True) + EPS)


def _sigmoid(x):
    return 1.0 / (1.0 + jnp.exp(-x))


def _gelu_tanh(x):
    c = math.sqrt(2.0 / math.pi)
    return 0.5 * x * (1.0 + jnp.tanh(c * (x + 0.044715 * (x * x * x))))


def _top_bits(x):
    bits = lax.bitcast_convert_type(x, jnp.uint32) & jnp.uint32(0xFFFF0000)
    return lax.bitcast_convert_type(bits, F32)


def _split3(x):
    hi = _top_bits(x)
    r = x - hi
    mid = _top_bits(r)
    return hi, mid, r - mid


def _dot(a, b):
    return jnp.dot(a, b, preferred_element_type=F32)


def _dot_nt(a, b):
    return lax.dot_general(a, b, (((1,), (1,)), ((), ())), preferred_element_type=F32)


def _lane_cumsum(x, upper_ones):
    rows = x.shape[0]
    parts = _dot(jnp.concatenate(_split3(x), axis=0).astype(BF16), upper_ones)
    return parts[0:rows] + parts[rows:2 * rows] + parts[2 * rows:3 * rows]


def _ffn_kernel(x_ref, nw_ref, wg_ref, wu_ref, wd_ref, fw_ref, o_ref,
                wg_s, wu_s, wd_s, h_ref, xn_s, acc_s, *, final_norm):
    i = pl.program_id(0)

    def normalized(x):
        return (_unit_rms(x) * nw_ref[...]).astype(BF16)

    def gated(xn, wg, wu):
        g = _dot(xn, wg)
        u = _dot(xn, wu)
        return (g * _sigmoid(g) * u).astype(BF16)

    def finish(x, y):
        out = x + 0.5 * y
        if final_norm:
            out = _unit_rms(out) * fw_ref[...]
        o_ref[...] = out

    @pl.when(i < FFN_PIECES)
    def _first_rows_while_loading():
        wg_s[i] = wg_ref[...].astype(BF16)
        wu_s[i] = wu_ref[...].astype(BF16)
        wd_s[i] = wd_ref[...].astype(BF16)

        @pl.when(i == 0)
        def _start():
            xn_s[...] = normalized(x_ref[...])
            acc_s[...] = jnp.zeros_like(acc_s)

        acc_s[...] += _dot(gated(xn_s[...], wg_s[i], wu_s[i]), wd_s[i])

        @pl.when(i == FFN_PIECES - 1)
        def _end():
            finish(x_ref[...], acc_s[...])

    @pl.when(i >= FFN_PIECES)
    def _token_rows():
        x = x_ref[...]
        xn = normalized(x)
        for c in range(FFN_PIECES):
            h_ref[:, c * FFN_CHUNK:(c + 1) * FFN_CHUNK] = gated(xn, wg_s[c], wu_s[c])
        finish(x, _dot(h_ref[...], wd_s[...].reshape(D_FF, D_MODEL)))


def _ffn(x2d, layer, norm_w, w_gate, w_up, w_down, final_w, *, final_norm):
    tokens = x2d.shape[0]
    last = FFN_PIECES - 1
    rows = lambda i: (jnp.maximum(i - last, 0), 0)
    return pl.pallas_call(
        functools.partial(_ffn_kernel, final_norm=final_norm),
        out_shape=jax.ShapeDtypeStruct(x2d.shape, F32),
        grid=(last + tokens // FFN_TOKENS,),
        in_specs=[
            pl.BlockSpec((FFN_TOKENS, D_MODEL), rows),
            pl.BlockSpec((None, 1, D_MODEL), lambda i: (layer, 0, 0)),
            pl.BlockSpec((None, D_MODEL, FFN_CHUNK), lambda i: (layer, 0, jnp.minimum(i, last))),
            pl.BlockSpec((None, D_MODEL, FFN_CHUNK), lambda i: (layer, 0, jnp.minimum(i, last))),
            pl.BlockSpec((None, FFN_CHUNK, D_MODEL), lambda i: (layer, jnp.minimum(i, last), 0)),
            pl.BlockSpec((1, D_MODEL), lambda i: (0, 0)),
        ],
        out_specs=pl.BlockSpec((FFN_TOKENS, D_MODEL), rows),
        scratch_shapes=[
            pltpu.VMEM((FFN_PIECES, D_MODEL, FFN_CHUNK), BF16),
            pltpu.VMEM((FFN_PIECES, D_MODEL, FFN_CHUNK), BF16),
            pltpu.VMEM((FFN_PIECES, FFN_CHUNK, D_MODEL), BF16),
            pltpu.VMEM((FFN_TOKENS, D_FF), BF16),
            pltpu.VMEM((FFN_TOKENS, D_MODEL), BF16),
            pltpu.VMEM((FFN_TOKENS, D_MODEL), F32),
        ],
        compiler_params=pltpu.CompilerParams(
            dimension_semantics=("arbitrary",), vmem_limit_bytes=VMEM_LIMIT_FFN),
        name="ffn_halfstep",
    )(x2d, norm_w, w_gate, w_up, w_down, final_w)


def _mix_chunk(proj_ref, dt_ref, ycat_ref, keep, bias_idx, p):
    row = lax.broadcasted_iota(jnp.int32, (CHUNK, CHUNK), 0)
    lane = lax.broadcasted_iota(jnp.int32, (CHUNK, CHUNK), 1)
    causal = row >= lane
    row_b = row.astype(F32).astype(BF16)
    lane_b = lane.astype(F32).astype(BF16)
    causal_b = row_b >= lane_b
    upper_half_b = lane_b >= HALF_LANES
    zero_b = jnp.zeros((CHUNK, CHUNK), BF16)
    group_b = (lax.broadcasted_iota(jnp.int32, (CHUNK, GROUP_WIDTH), 1) // SSD_HEAD_DIM).astype(F32).astype(BF16)

    def block_diag(x_b):
        zero = jnp.zeros_like(x_b)
        return jnp.concatenate([jnp.where(group_b == i, x_b, zero) for i in range(HEADS_PER_GROUP)], axis=0)

    xbc_parts = []
    for c0 in range(0, SSD_XBC, CONV_COLS):
        cs = slice(c0, c0 + CONV_COLS)
        if keep is not None:
            p.ext[0:SUBLANES, cs] = p.ext[0:SUBLANES, cs] * keep
        raw = proj_ref[:, P_XBC + c0:P_XBC + c0 + CONV_COLS]
        p.ext[SUBLANES:, cs] = raw
        ext = p.ext[:, cs]
        w = p.convw[:, cs]
        ext1 = pltpu.roll(ext, 1, 0)
        older = w[1:2, :] * ext + w[0:1, :] * ext1
        conv = (p.convb[:, cs] + w[3:4, :] * raw + w[2:3, :] * ext1[SUBLANES:, :]
                + pltpu.roll(older, 2, 0)[SUBLANES:, :])
        p.ext[0:SUBLANES, cs] = raw[CHUNK - SUBLANES:, :]
        xbc_parts.append(conv * _sigmoid(conv))
        yield P_XBC + c0, P_XBC + c0 + CONV_COLS
    xbc = jnp.concatenate(xbc_parts, axis=-1)
    xs = xbc[:, :SSD_INNER]
    bm = xbc[:, SSD_INNER:SSD_INNER + SSD_GROUPS * SSD_STATE]
    cm = xbc[:, SSD_INNER + SSD_GROUPS * SSD_STATE:]
    x_b = xs.astype(BF16)
    bm_b = bm.astype(BF16)
    cm_b = cm.astype(BF16)

    dtr = dt_ref[0:SSD_HEADS, :] + p.dtb[...]
    dt = jnp.maximum(dtr, 0.0) + jnp.log1p(jnp.exp(-jnp.abs(dtr)))
    acs = _lane_cumsum(dt * (-LOG2E * jnp.exp(p.alog[...])), p.triu[...])
    grow = jnp.exp2(acs)
    to_end = dt * jnp.exp2(acs[:, CHUNK - 1:CHUNK] - acs)
    src = acs - jnp.log2(dt)
    stack = jnp.concatenate(
        [acs, *_split3(grow), *_split3(to_end), jnp.zeros((CHUNK - 7 * SSD_HEADS, LANES), F32)], axis=0)
    cols = stack.T
    expanded = _dot(cols.astype(BF16), p.expand[...])
    grow_e = expanded[:, :SSD_INNER]
    to_end_e = expanded[:, SSD_INNER:]
    xd_b = (xs * to_end_e).astype(BF16)
    yield P_DT, P_TOTAL

    y_diag = []
    for g in range(SSD_GROUPS):
        n0 = g * SSD_STATE
        cb = _dot_nt(cm_b[:, n0:n0 + SSD_STATE], bm_b[:, n0:n0 + SSD_STATE])
        parts = []
        for hh in range(HEADS_PER_GROUP):
            h = g * HEADS_PER_GROUP + hh
            seg = cols[:, h:h + 1] - src[h:h + 1, :]
            parts.append((cb * jnp.exp2(jnp.where(causal, seg, -jnp.inf))).astype(BF16))
        w0 = g * GROUP_WIDTH
        y_diag.append(_dot(jnp.concatenate(parts, axis=-1), block_diag(x_b[:, w0:w0 + GROUP_WIDTH])))
        yield
    y_off = []
    for g in range(SSD_GROUPS):
        n0 = g * SSD_STATE
        w0 = g * GROUP_WIDTH
        prev = p.st[g]
        if keep is not None:
            prev = prev * keep
        y_off.append(_dot(cm_b[:, n0:n0 + SSD_STATE], prev.astype(BF16)))
        bt = bm[:, n0:n0 + SSD_STATE].T.astype(BF16)
        p.st[g] = (prev * grow_e[CHUNK - 1:CHUNK, w0:w0 + GROUP_WIDTH]
                   + _dot(bt, xd_b[:, w0:w0 + GROUP_WIDTH]))
    yield
    gated, sumsq = [], 0.0
    for g in range(SSD_GROUPS):
        ws = slice(g * GROUP_WIDTH, (g + 1) * GROUP_WIDTH)
        y = y_diag[g] + y_off[g] * grow_e[:, ws] + xs[:, ws] * p.dskip[:, ws]
        z = proj_ref[:, P_Z + g * GROUP_WIDTH:P_Z + (g + 1) * GROUP_WIDTH]
        gated.append(y * (z * _sigmoid(z)))
        sumsq = sumsq + jnp.sum(gated[g] * gated[g], axis=-1, keepdims=True)
        yield P_Z + g * GROUP_WIDTH, P_Z + (g + 1) * GROUP_WIDTH
    inv_rms = lax.rsqrt(sumsq * (1.0 / SSD_INNER) + EPS)
    for g in range(SSD_GROUPS):
        ycat_ref[:, g * GROUP_WIDTH:(g + 1) * GROUP_WIDTH] = (gated[g] * inv_rms).astype(BF16)

    q_b = (proj_ref[:, P_Q:P_Q + ATT_WIDTH] * (LOG2E / math.sqrt(ATT_HEAD_DIM))).astype(BF16)
    k_b = proj_ref[:, P_K:P_K + ATT_KV_WIDTH].astype(BF16)
    v_b = proj_ref[:, P_V:P_V + ATT_KV_WIDTH].astype(BF16)
    keys = jnp.concatenate([p.kprev[...], k_b], axis=0)
    v_prev = p.vprev[...]
    halves = lambda t: (jnp.where(upper_half_b, zero_b, t), jnp.where(upper_half_b, t, zero_b))
    v_prev_half, v_half = halves(v_prev), halves(v_b)
    p.kprev[...] = k_b
    p.vprev[...] = v_b
    probs, values = [], []
    for pos, hq in enumerate(ATT_POS_HEADS):
        tile, half = divmod(pos, 2)
        q_tile = q_b[:, tile * LANES:(tile + 1) * LANES]
        q_head = halves(q_tile)[half]
        s = _dot_nt(q_head, keys)
        s = jnp.where(causal, s[:, CHUNK:], s[:, :CHUNK]) + p.bias[bias_idx, pos]
        sink = p.sink[hq] * LOG2E
        m = jnp.maximum(jnp.max(s, axis=-1, keepdims=True), sink)
        e = jnp.exp2(s - m)
        denom = jnp.sum(e, axis=-1, keepdims=True) + jnp.exp2(sink - m)
        pn = (e * (1.0 / denom)).astype(BF16)
        probs.append(jnp.where(causal_b, zero_b, pn))
        probs.append(jnp.where(causal_b, pn, zero_b))
        for vh in (v_prev_half[half], v_half[half]):
            values.append(jnp.concatenate([zero_b, vh] if tile else [vh, zero_b], axis=-1))
        yield (P_Q, P_U) if pos == 0 else None
    att = _dot(jnp.concatenate(probs, axis=-1), jnp.concatenate(values, axis=0))
    ycat_ref[:, Y_ATT:Y_SGU] = _unit_rms(att).astype(BF16)
    yield

    u = _gelu_tanh(proj_ref[:, P_U:P_U + GM_WIDTH])
    yield P_U, P_GV
    gv = _gelu_tanh(proj_ref[:, P_GV:P_GV + GM_WIDTH])
    yield P_GV, P_DT
    mu = jnp.mean(gv, axis=-1, keepdims=True)
    gc = gv - mu
    gv = gc * lax.rsqrt(jnp.mean(gc * gc, axis=-1, keepdims=True) + EPS) * p.lnw[...] + p.lnb[...]
    gv_b = gv.astype(BF16)
    mixed = _dot(p.wcat[...], block_diag(gv_b)) + p.sgub[...]
    ycat_ref[:, Y_SGU:] = _unit_rms(u * mixed).astype(BF16)
    yield


class _MixerRefs:
    def __init__(self, **refs):
        self.__dict__.update(refs)


def _mixer_kernel(sink_ref, rel_ref,
                  xa_ref, xc_ref, ing_ref, win_ref, convw_ref, convb_ref, dtb_ref, alog_ref,
                  dskip_ref, triu_ref, expand_ref, bucket_ref,
                  lnw_ref, lnb_ref, sguw_ref, sgub_ref, outg_ref, wout_ref,
                  o_ref,
                  win_s, wdt_s, wout_s, projx_ref, projy_ref, dtx_ref, dty_ref, ycatx_ref, ycaty_ref,
                  xnx_ref, xny_ref,
                  ext_ref, st_ref, kprev_ref, vprev_ref, bias_ref, wcat_ref,
                  *, chunks_per_seq):
    t = pl.program_id(0)
    p = _MixerRefs(sink=sink_ref, convw=convw_ref, convb=convb_ref, dtb=dtb_ref, alog=alog_ref,
                   dskip=dskip_ref, triu=triu_ref, expand=expand_ref, lnw=lnw_ref, lnb=lnb_ref,
                   sgub=sgub_ref, ext=ext_ref, st=st_ref, kprev=kprev_ref, vprev=vprev_ref,
                   bias=bias_ref, wcat=wcat_ref)

    @pl.when(t == 0)
    def _init():
        row = lax.broadcasted_iota(jnp.int32, (CHUNK, CHUNK), 0)
        col = lax.broadcasted_iota(jnp.int32, (CHUNK, CHUNK), 1)
        bucket = bucket_ref[...]
        for pos, hq in enumerate(ATT_POS_HEADS):
            acc = jnp.zeros((CHUNK, CHUNK), F32)
            for k in range(REL_BUCKETS):
                acc = jnp.where(bucket == k, rel_ref[k * ATT_Q_HEADS + hq] * LOG2E, acc)
            bias_ref[0, pos] = acc
            bias_ref[1, pos] = jnp.where(row >= col, acc, MASKED)
        for g in range(GM_GROUPS):
            wcat_ref[:, g * CHUNK:(g + 1) * CHUNK] = jnp.where(row >= col, sguw_ref[g], 0.0).astype(BF16)

        def pack_rows(i, carry):
            r = pl.ds(pl.multiple_of(i * CHUNK, CHUNK), CHUNK)
            gain = ing_ref[r, :]
            win_s[r, 0:R_DT] = (win_ref[r, 0:R_DT] * gain).astype(BF16)
            tail = win_ref[r, R_DT:] * gain
            q0 = SSD_HEADS
            q = [tail[:, q0 + h * ATT_HEAD_DIM:q0 + (h + 1) * ATT_HEAD_DIM] for h in ATT_POS_HEADS]
            win_s[r, R_DT:] = jnp.concatenate(q + [tail[:, q0 + ATT_WIDTH:]], axis=1).astype(BF16)
            return carry

        lax.fori_loop(0, D_MODEL // CHUNK, pack_rows, 0)
        for j in range(D_MODEL // CHUNK):
            r = slice(j * CHUNK, (j + 1) * CHUNK)
            tile_t = (win_ref[r, R_DT:R_DT + LANES] * ing_ref[r, :]).T
            keep_rows = lax.broadcasted_iota(jnp.int32, (DT_ROWS, CHUNK), 0) < SSD_HEADS
            wdt_s[:, r] = jnp.where(keep_rows, tile_t[0:DT_ROWS, :], 0.0).astype(BF16)
        blocks = D_MODEL // ATT_HEAD_DIM
        first_att = Y_ATT // ATT_HEAD_DIM
        for j in range(blocks):
            src = first_att + ATT_POS_HEADS[j - first_att] if first_att <= j < first_att + ATT_Q_HEADS else j
            dst_rows = slice(j * ATT_HEAD_DIM, (j + 1) * ATT_HEAD_DIM)
            src_rows = slice(src * ATT_HEAD_DIM, (src + 1) * ATT_HEAD_DIM)
            wout_s[dst_rows, :] = (wout_ref[src_rows, :] * outg_ref[dst_rows, :]).astype(BF16)
        projy_ref[...] = jnp.zeros_like(projy_ref)
        dty_ref[...] = jnp.zeros_like(dty_ref)
        ycatx_ref[...] = jnp.zeros_like(ycatx_ref)
        st_ref[...] = jnp.zeros_like(st_ref)
        ext_ref[0:SUBLANES, :] = jnp.zeros((SUBLANES, SSD_XBC), F32)
        kprev_ref[...] = jnp.zeros_like(kprev_ref)
        vprev_ref[...] = jnp.zeros_like(vprev_ref)

    starts_seq = lax.rem(4 * t, chunks_per_seq) == 0
    keep = jnp.where(starts_seq, 0.0, 1.0).astype(F32)
    first_idx = jnp.where(starts_seq, 1, 0).astype(jnp.int32)
    even, odd = pl.ds(0, CHUNK), pl.ds(CHUNK, CHUNK)

    def projection_pieces(rows, proj_new, dt_new, ycat_out, xn):
        def normalize():
            xn[...] = _unit_rms(xa_ref[rows, :]).astype(BF16)

        def project(c0, c1):
            proj_new[:, c0:c1] = _dot(xn[...], win_s[:, c0:c1])

        def project_gv_dt():
            project(P_GV, P_DT)
            dt_t = _dot_nt(wdt_s[...], xn[...])
            dt_new[0] = dt_t[:, :CHUNK]
            dt_new[1] = dt_t[:, CHUNK:]

        def output(c0, c1):
            o_ref[rows, c0:c1] = xc_ref[rows, c0:c1] + _dot(ycat_out[...], wout_s[:, c0:c1])

        in_piece = lambda lo, hi: ("in", lo, hi, functools.partial(project, lo, hi))
        out_piece = lambda i: ("out", i * PROJ_PIECE, (i + 1) * PROJ_PIECE,
                               functools.partial(output, i * PROJ_PIECE, (i + 1) * PROJ_PIECE))
        xbc = [in_piece(P_XBC + c, P_XBC + c + PROJ_PIECE) for c in range(0, SSD_XBC, PROJ_PIECE)]
        z = [in_piece(P_Z + c, P_Z + c + PROJ_PIECE) for c in range(0, SSD_INNER, PROJ_PIECE)]
        return ([("norm", 0, 0, normalize)] + z + [out_piece(0)] + xbc[:2] + [out_piece(1)] + xbc[2:]
                + [out_piece(2), in_piece(P_Q, P_K), in_piece(P_K, P_U), out_piece(3),
                   in_piece(P_U, P_GV), ("in", P_GV, P_TOTAL, project_gv_dt)])

    mixes = [
        _mix_chunk(projy_ref.at[even], dty_ref.at[0], ycaty_ref.at[even], None, 0, p),
        _mix_chunk(projy_ref.at[odd], dty_ref.at[1], ycaty_ref.at[odd], None, 0, p),
        _mix_chunk(projx_ref.at[even], dtx_ref.at[0], ycatx_ref.at[even], keep, first_idx, p),
        _mix_chunk(projx_ref.at[odd], dtx_ref.at[1], ycatx_ref.at[odd], None, 0, p),
    ]
    pieces = [projection_pieces(pl.ds(0, PAIR), projx_ref, dtx_ref, ycatx_ref, xnx_ref),
              projection_pieces(pl.ds(PAIR, PAIR), projy_ref, dty_ref, ycaty_ref, xny_ref)]
    overlaps = lambda reads, piece: reads is not None and reads[0] < piece[2] and piece[1] < reads[1]
    done = [0] * len(mixes)
    issued = [0, 0]

    def mix_may_run(k):
        section = done[k]
        if section == MIX_SECTIONS:
            return False
        if k > 0 and done[k - 1] < min(section + MIX_LAG, MIX_SECTIONS):
            return False
        if k >= 2 and issued[0] < len(pieces[0]):
            return False
        return True

    def piece_may_issue(phase):
        if issued[phase] == len(pieces[phase]):
            return False
        piece = pieces[phase][issued[phase]]
        if phase == 1:
            if piece[0] == "in":
                return all(done[k] > s for k in (0, 1) for s, r in MIX_READS.items() if overlaps(r, piece))
            if piece[0] == "out":
                return done[0] == done[1] == MIX_SECTIONS
        return True

    while min(done) < MIX_SECTIONS or issued != [len(pieces[0]), len(pieces[1])]:
        progressed = False
        for k, mix in enumerate(mixes):
            if mix_may_run(k):
                assert next(mix) == MIX_READS.get(done[k]), "MIX_READS is stale"
                done[k] += 1
                progressed = True
        for phase in (0, 1):
            if piece_may_issue(phase):
                pieces[phase][issued[phase]][3]()
                issued[phase] += 1
                progressed = True
                break
        assert progressed, "mixer step schedule is stuck"
    for mix in mixes:
        assert next(mix, None) is None and mix.gi_frame is None, "MIX_SECTIONS is stale"


def _folded_bucket_tile():
    i = np.arange(CHUNK)[:, None]
    j = np.arange(CHUNK)[None, :]
    n = np.where(j <= i, i - j, i - j + CHUNK)
    max_exact = REL_BUCKETS // 2
    large = max_exact + (np.log(np.maximum(n, 1) / max_exact) / np.log(REL_MAX_DIST / max_exact)
                         * (REL_BUCKETS - max_exact)).astype(np.int32)
    large = np.minimum(large, REL_BUCKETS - 1)
    return np.where(n < max_exact, n, large).astype(np.int32)


def _mixer(x2d, seq, layer, in_gain, w_in, conv_w, conv_b, dt_bias, a_log, d_skip, sinks, rel_bias,
           ln_w, ln_b, sgu_w, sgu_b, out_gain, w_out):
    tokens = x2d.shape[0]
    n_quads = tokens // QUAD
    chunks_per_seq = seq // CHUNK
    assert tokens % QUAD == 0 and seq % QUAD == 0 and GM_GROUPS == HEADS_PER_GROUP
    assert P_Q == R_DT and w_in.shape[-1] == R_K + P_DT - P_K
    per_head_rows = lambda v: jnp.broadcast_to(v[:, None], (SSD_HEADS, LANES))
    row = lambda v: v.reshape(1, -1)
    expand = np.zeros((CHUNK, 2 * SSD_INNER), np.float32)
    for r in range(SSD_HEADS, 7 * SSD_HEADS):
        h = r % SSD_HEADS
        c0 = ((r // SSD_HEADS - 1) // 3) * SSD_INNER + h * SSD_HEAD_DIM
        expand[r, c0:c0 + SSD_HEAD_DIM] = 1.0
    expand = jnp.asarray(expand, BF16)
    triu = jnp.asarray(np.triu(np.ones((CHUNK, CHUNK), np.float32)), BF16)
    bucket = jnp.asarray(_folded_bucket_tile())
    sgu_b_e = jnp.repeat(jnp.transpose(sgu_b), GM_GROUP_DIM, axis=1)

    const2 = lambda t, *_: (0, 0)
    const3 = lambda t, *_: (0, 0, 0)
    full2 = lambda shape: pl.BlockSpec(shape, const2)
    layer_block = lambda shape: pl.BlockSpec((None,) + shape, lambda t, *_: (layer, 0, 0),
                                             pipeline_mode=pl.Buffered(1))
    projected = lambda t, *_: (jnp.minimum(t, n_quads - 1), 0)
    finished = lambda t, *_: (jnp.maximum(t - 1, 0), 0)
    grid_spec = pltpu.PrefetchScalarGridSpec(
        num_scalar_prefetch=2,
        grid=(n_quads + 1,),
        in_specs=[
            pl.BlockSpec((QUAD, D_MODEL), projected),
            pl.BlockSpec((QUAD, D_MODEL), finished),
            layer_block((D_MODEL, 1)),
            layer_block((D_MODEL, w_in.shape[-1])),
            full2((SSD_CONV, SSD_XBC)),
            full2((1, SSD_XBC)),
            full2((SSD_HEADS, LANES)),
            full2((SSD_HEADS, LANES)),
            full2((1, SSD_INNER)),
            full2((CHUNK, CHUNK)),
            full2((CHUNK, 2 * SSD_INNER)),
            full2((CHUNK, CHUNK)),
            full2((1, GM_WIDTH)),
            full2((1, GM_WIDTH)),
            pl.BlockSpec((GM_GROUPS, CHUNK, CHUNK), const3),
            full2((CHUNK, GM_WIDTH)),
            layer_block((D_MODEL, 1)),
            layer_block((D_MODEL, D_MODEL)),
        ],
        out_specs=pl.BlockSpec((QUAD, D_MODEL), finished),
        scratch_shapes=[
            pltpu.VMEM((D_MODEL, P_DT), BF16),
            pltpu.VMEM((DT_ROWS, D_MODEL), BF16),
            pltpu.VMEM((D_MODEL, D_MODEL), BF16),
            pltpu.VMEM((PAIR, P_DT), F32),
            pltpu.VMEM((PAIR, P_DT), F32),
            pltpu.VMEM((2, DT_ROWS, CHUNK), F32),
            pltpu.VMEM((2, DT_ROWS, CHUNK), F32),
            pltpu.VMEM((PAIR, D_MODEL), BF16),
            pltpu.VMEM((PAIR, D_MODEL), BF16),
            pltpu.VMEM((PAIR, D_MODEL), BF16),
            pltpu.VMEM((PAIR, D_MODEL), BF16),
            pltpu.VMEM((SUBLANES + CHUNK, SSD_XBC), F32),
            pltpu.VMEM((SSD_GROUPS, SSD_STATE, GROUP_WIDTH), F32),
            pltpu.VMEM((CHUNK, ATT_KV_WIDTH), BF16),
            pltpu.VMEM((CHUNK, ATT_KV_WIDTH), BF16),
            pltpu.VMEM((2, ATT_Q_HEADS, CHUNK, CHUNK), F32),
            pltpu.VMEM((CHUNK, GM_GROUPS * CHUNK), BF16),
        ],
    )
    return pl.pallas_call(
        functools.partial(_mixer_kernel, chunks_per_seq=chunks_per_seq),
        out_shape=jax.ShapeDtypeStruct(x2d.shape, F32),
        grid_spec=grid_spec,
        compiler_params=pltpu.CompilerParams(
            dimension_semantics=("arbitrary",), vmem_limit_bytes=VMEM_LIMIT_MIXER),
        name="token_mixer",
    )(sinks.astype(F32), rel_bias.reshape(-1).astype(F32),
      x2d, x2d, in_gain, w_in, conv_w, row(conv_b), per_head_rows(dt_bias), per_head_rows(a_log),
      row(jnp.repeat(d_skip, SSD_HEAD_DIM)), triu, expand, bucket,
      row(ln_w), row(ln_b), sgu_w, sgu_b_e, out_gain, w_out)


def kernel(x, ffn1_norm, ffn1_w_gate, ffn1_w_up, ffn1_w_down, mix_norm, w_in, conv_w, conv_b, dt_bias, a_log, d_skip, ssd_norm, attn_sinks, rel_bias, attn_out_norm, sgu_ln_w, sgu_ln_b, sgu_w, sgu_b, sgu_out_norm, w_out, ffn2_norm, ffn2_w_gate, ffn2_w_up, ffn2_w_down, final_norm):
    batch, seq, d = x.shape
    depth = w_in.shape[0]
    fw = final_norm.reshape(1, d)
    x2 = x.reshape(batch * seq, d)
    ffn1_gain = ffn1_norm.reshape(depth, 1, d)
    ffn2_gain = ffn2_norm.reshape(depth, 1, d)
    mix_in_gain = mix_norm.reshape(depth, d, 1)
    att_cols = np.concatenate([np.arange(h * ATT_HEAD_DIM, (h + 1) * ATT_HEAD_DIM) for h in ATT_POS_HEADS])
    mix_out_gain = jnp.concatenate([ssd_norm, attn_out_norm[:, att_cols], sgu_out_norm], axis=1).reshape(depth, d, 1)
    for l in range(depth):
        x2 = _ffn(x2, l, ffn1_gain, ffn1_w_gate, ffn1_w_up, ffn1_w_down, fw, final_norm=False)
        x2 = _mixer(x2, seq, l, mix_in_gain, w_in, conv_w[l], conv_b[l], dt_bias[l],
                    a_log[l], d_skip[l], attn_sinks[l], rel_bias,
                    sgu_ln_w[l], sgu_ln_b[l], sgu_w[l], sgu_b[l], mix_out_gain, w_out)
        x2 = _ffn(x2, l, ffn2_gain, ffn2_w_gate, ffn2_w_up, ffn2_w_down, fw,
                  final_norm=(l == depth - 1))
    return x2.reshape(batch, seq, d)
```

```python
import functools
import math

import jax
import jax.numpy as jnp
import numpy as np
from jax import lax
from jax.experimental import pallas as pl
from jax.experimental.pallas import tpu as pltpu

F32 = jnp.float32
BF16 = jnp.bfloat16

D_MODEL = 1024
D_FF = 2816
EPS = 1e-6

SSD_HEADS = 8
SSD_HEAD_DIM = 64
SSD_INNER = SSD_HEADS * SSD_HEAD_DIM
SSD_GROUPS = 2
SSD_STATE = 128
SSD_CONV = 4
SSD_XBC = SSD_INNER + 2 * SSD_GROUPS * SSD_STATE
HEADS_PER_GROUP = SSD_HEADS // SSD_GROUPS
GROUP_WIDTH = HEADS_PER_GROUP * SSD_HEAD_DIM

ATT_Q_HEADS = 4
ATT_KV_HEADS = 2
ATT_HEAD_DIM = 64
ATT_WIDTH = ATT_Q_HEADS * ATT_HEAD_DIM
ATT_KV_WIDTH = ATT_KV_HEADS * ATT_HEAD_DIM
REL_BUCKETS = 32
REL_MAX_DIST = 128
ATT_POS_HEADS = (0, 2, 1, 3)

GM_GROUPS = 4
GM_GROUP_DIM = 64
GM_WIDTH = GM_GROUPS * GM_GROUP_DIM

CHUNK = 128
PAIR = 2 * CHUNK
QUAD = 2 * PAIR
PROJ_PIECE = 256
CONV_COLS = 256
MIX_SECTIONS = 18
MIX_LAG = 4

SUBLANES = 8
LANES = 128
HALF_LANES = LANES // 2
DT_PAD = LANES
DT_ROWS = 16

P_Z = 0
P_XBC = P_Z + SSD_INNER
P_Q = P_XBC + SSD_XBC
P_K = P_Q + ATT_WIDTH
P_V = P_K + ATT_KV_WIDTH
P_U = P_V + ATT_KV_WIDTH
P_GV = P_U + GM_WIDTH
P_DT = P_GV + GM_WIDTH
P_TOTAL = P_DT + DT_PAD

R_XBC = SSD_INNER
R_DT = R_XBC + SSD_XBC
R_Q = R_DT + SSD_HEADS
R_K = R_Q + ATT_WIDTH

Y_ATT = SSD_INNER
Y_SGU = Y_ATT + ATT_WIDTH

MIX_READS = {
    **{i: (P_XBC + i * CONV_COLS, P_XBC + (i + 1) * CONV_COLS) for i in range(SSD_XBC // CONV_COLS)},
    4: (P_DT, P_TOTAL),
    8: (P_Z, P_Z + GROUP_WIDTH),
    9: (P_Z + GROUP_WIDTH, P_Z + 2 * GROUP_WIDTH),
    10: (P_Q, P_U),
    15: (P_U, P_GV),
    16: (P_GV, P_DT),
}

MASKED = -1e30
LOG2E = math.log2(math.e)

FFN_TOKENS = 1024
FFN_CHUNK = 256
FFN_PIECES = D_FF // FFN_CHUNK
VMEM_LIMIT_FFN = 56 * 1024 * 1024
VMEM_LIMIT_MIXER = 48 * 1024 * 1024


def _unit_rms(x):
    return x * lax.rsqrt(jnp.mean(x * x, axis=-1, keepdims=True) + EPS)


def _sigmoid(x):
    return 1.0 / (1.0 + jnp.exp(-x))


def _gelu_tanh(x):
    c = math.sqrt(2.0 / math.pi)
    return 0.5 * x * (1.0 + jnp.tanh(c * (x + 0.044715 * (x * x * x))))


def _top_bits(x):
    bits = lax.bitcast_convert_type(x, jnp.uint32) & jnp.uint32(0xFFFF0000)
    return lax.bitcast_convert_type(bits, F32)


def _split3(x):
    hi = _top_bits(x)
    r = x - hi
    mid = _top_bits(r)
    return hi, mid, r - mid


def _dot(a, b):
    return jnp.dot(a, b, preferred_element_type=F32)


def _dot_nt(a, b):
    return lax.dot_general(a, b, (((1,), (1,)), ((), ())), preferred_element_type=F32)


def _lane_cumsum(x, upper_ones):
    rows = x.shape[0]
    parts = _dot(jnp.concatenate(_split3(x), axis=0).astype(BF16), upper_ones)
    return parts[0:rows] + parts[rows:2 * rows] + parts[2 * rows:3 * rows]


def _ffn_kernel(x_ref, nw_ref, wg_ref, wu_ref, wd_ref, fw_ref, o_ref,
                wg_s, wu_s, wd_s, h_ref, xn_s, acc_s, *, final_norm):
    i = pl.program_id(0)

    def normalized(x):
        return (_unit_rms(x) * nw_ref[...]).astype(BF16)

    def gated(xn, wg, wu):
        g = _dot(xn, wg)
        u = _dot(xn, wu)
        return (g * _sigmoid(g) * u).astype(BF16)

    def finish(x, y):
        out = x + 0.5 * y
        if final_norm:
            out = _unit_rms(out) * fw_ref[...]
        o_ref[...] = out

    @pl.when(i < FFN_PIECES)
    def _first_rows_while_loading():
        wg_s[i] = wg_ref[...].astype(BF16)
        wu_s[i] = wu_ref[...].astype(BF16)
        wd_s[i] = wd_ref[...].astype(BF16)

        @pl.when(i == 0)
        def _start():
            xn_s[...] = normalized(x_ref[...])
            acc_s[...] = jnp.zeros_like(acc_s)

        acc_s[...] += _dot(gated(xn_s[...], wg_s[i], wu_s[i]), wd_s[i])

        @pl.when(i == FFN_PIECES - 1)
        def _end():
            finish(x_ref[...], acc_s[...])

    @pl.when(i >= FFN_PIECES)
    def _token_rows():
        x = x_ref[...]
        xn = normalized(x)
        for c in range(FFN_PIECES):
            h_ref[:, c * FFN_CHUNK:(c + 1) * FFN_CHUNK] = gated(xn, wg_s[c], wu_s[c])
        finish(x, _dot(h_ref[...], wd_s[...].reshape(D_FF, D_MODEL)))


def _ffn(x2d, layer, norm_w, w_gate, w_up, w_down, final_w, *, final_norm):
    tokens = x2d.shape[0]
    last = FFN_PIECES - 1
    rows = lambda i: (jnp.maximum(i - last, 0), 0)
    return pl.pallas_call(
        functools.partial(_ffn_kernel, final_norm=final_norm),
        out_shape=jax.ShapeDtypeStruct(x2d.shape, F32),
        grid=(last + tokens // FFN_TOKENS,),
        in_specs=[
            pl.BlockSpec((FFN_TOKENS, D_MODEL), rows),
            pl.BlockSpec((None, 1, D_MODEL), lambda i: (layer, 0, 0)),
            pl.BlockSpec((None, D_MODEL, FFN_CHUNK), lambda i: (layer, 0, jnp.minimum(i, last))),
            pl.BlockSpec((None, D_MODEL, FFN_CHUNK), lambda i: (layer, 0, jnp.minimum(i, last))),
            pl.BlockSpec((None, FFN_CHUNK, D_MODEL), lambda i: (layer, jnp.minimum(i, last), 0)),
            pl.BlockSpec((1, D_MODEL), lambda i: (0, 0)),
        ],
        out_specs=pl.BlockSpec((FFN_TOKENS, D_MODEL), rows),
        scratch_shapes=[
            pltpu.VMEM((FFN_PIECES, D_MODEL, FFN_CHUNK), BF16),
            pltpu.VMEM((FFN_PIECES, D_MODEL, FFN_CHUNK), BF16),
            pltpu.VMEM((FFN_PIECES, FFN_CHUNK, D_MODEL), BF16),
            pltpu.VMEM((FFN_TOKENS, D_FF), BF16),
            pltpu.VMEM((FFN_TOKENS, D_MODEL), BF16),
            pltpu.VMEM((FFN_TOKENS, D_MODEL), F32),
        ],
        compiler_params=pltpu.CompilerParams(
            dimension_semantics=("arbitrary",), vmem_limit_bytes=VMEM_LIMIT_FFN),
        name="ffn_halfstep",
    )(x2d, norm_w, w_gate, w_up, w_down, final_w)


def _mix_chunk(proj_ref, dt_ref, ycat_ref, keep, bias_idx, p):
    row = lax.broadcasted_iota(jnp.int32, (CHUNK, CHUNK), 0)
    lane = lax.broadcasted_iota(jnp.int32, (CHUNK, CHUNK), 1)
    causal = row >= lane
    row_b = row.astype(F32).astype(BF16)
    lane_b = lane.astype(F32).astype(BF16)
    causal_b = row_b >= lane_b
    upper_half_b = lane_b >= HALF_LANES
    zero_b = jnp.zeros((CHUNK, CHUNK), BF16)
    group_b = (lax.broadcasted_iota(jnp.int32, (CHUNK, GROUP_WIDTH), 1) // SSD_HEAD_DIM).astype(F32).astype(BF16)

    def block_diag(x_b):
        zero = jnp.zeros_like(x_b)
        return jnp.concatenate([jnp.where(group_b == i, x_b, zero) for i in range(HEADS_PER_GROUP)], axis=0)

    xbc_parts = []
    for c0 in range(0, SSD_XBC, CONV_COLS):
        cs = slice(c0, c0 + CONV_COLS)
        if keep is not None:
            p.ext[0:SUBLANES, cs] = p.ext[0:SUBLANES, cs] * keep
        raw = proj_ref[:, P_XBC + c0:P_XBC + c0 + CONV_COLS]
        p.ext[SUBLANES:, cs] = raw
        ext = p.ext[:, cs]
        w = p.convw[:, cs]
        ext1 = pltpu.roll(ext, 1, 0)
        older = w[1:2, :] * ext + w[0:1, :] * ext1
        conv = (p.convb[:, cs] + w[3:4, :] * raw + w[2:3, :] * ext1[SUBLANES:, :]
                + pltpu.roll(older, 2, 0)[SUBLANES:, :])
        p.ext[0:SUBLANES, cs] = raw[CHUNK - SUBLANES:, :]
        xbc_parts.append(conv * _sigmoid(conv))
        yield P_XBC + c0, P_XBC + c0 + CONV_COLS
    xbc = jnp.concatenate(xbc_parts, axis=-1)
    xs = xbc[:, :SSD_INNER]
    bm = xbc[:, SSD_INNER:SSD_INNER + SSD_GROUPS * SSD_STATE]
    cm = xbc[:, SSD_INNER + SSD_GROUPS * SSD_STATE:]
    x_b = xs.astype(BF16)
    bm_b = bm.astype(BF16)
    cm_b = cm.astype(BF16)

    dtr = dt_ref[0:SSD_HEADS, :] + p.dtb[...]
    dt = jnp.maximum(dtr, 0.0) + jnp.log1p(jnp.exp(-jnp.abs(dtr)))
    acs = _lane_cumsum(dt * (-LOG2E * jnp.exp(p.alog[...])), p.triu[...])
    grow = jnp.exp2(acs)
    to_end = dt * jnp.exp2(acs[:, CHUNK - 1:CHUNK] - acs)
    src = acs - jnp.log2(dt)
    stack = jnp.concatenate(
        [acs, *_split3(grow), *_split3(to_end), jnp.zeros((CHUNK - 7 * SSD_HEADS, LANES), F32)], axis=0)
    cols = stack.T
    expanded = _dot(cols.astype(BF16), p.expand[...])
    grow_e = expanded[:, :SSD_INNER]
    to_end_e = expanded[:, SSD_INNER:]
    xd_b = (xs * to_end_e).astype(BF16)
    yield P_DT, P_TOTAL

    y_diag = []
    for g in range(SSD_GROUPS):
        n0 = g * SSD_STATE
        cb = _dot_nt(cm_b[:, n0:n0 + SSD_STATE], bm_b[:, n0:n0 + SSD_STATE])
        parts = []
        for hh in range(HEADS_PER_GROUP):
            h = g * HEADS_PER_GROUP + hh
            seg = cols[:, h:h + 1] - src[h:h + 1, :]
            parts.append((cb * jnp.exp2(jnp.where(causal, seg, -jnp.inf))).astype(BF16))
        w0 = g * GROUP_WIDTH
        y_diag.append(_dot(jnp.concatenate(parts, axis=-1), block_diag(x_b[:, w0:w0 + GROUP_WIDTH])))
        yield
    y_off = []
    for g in range(SSD_GROUPS):
        n0 = g * SSD_STATE
        w0 = g * GROUP_WIDTH
        prev = p.st[g]
        if keep is not None:
            prev = prev * keep
        y_off.append(_dot(cm_b[:, n0:n0 + SSD_STATE], prev.astype(BF16)))
        bt = bm[:, n0:n0 + SSD_STATE].T.astype(BF16)
        p.st[g] = (prev * grow_e[CHUNK - 1:CHUNK, w0:w0 + GROUP_WIDTH]
                   + _dot(bt, xd_b[:, w0:w0 + GROUP_WIDTH]))
    yield
    gated, sumsq = [], 0.0
    for g in range(SSD_GROUPS):
        ws = slice(g * GROUP_WIDTH, (g + 1) * GROUP_WIDTH)
        y = y_diag[g] + y_off[g] * grow_e[:, ws] + xs[:, ws] * p.dskip[:, ws]
        z = proj_ref[:, P_Z + g * GROUP_WIDTH:P_Z + (g + 1) * GROUP_WIDTH]
        gated.append(y * (z * _sigmoid(z)))
        sumsq = sumsq + jnp.sum(gated[g] * gated[g], axis=-1, keepdims=True)
        yield P_Z + g * GROUP_WIDTH, P_Z + (g + 1) * GROUP_WIDTH
    inv_rms = lax.rsqrt(sumsq * (1.0 / SSD_INNER) + EPS)
    for g in range(SSD_GROUPS):
        ycat_ref[:, g * GROUP_WIDTH:(g + 1) * GROUP_WIDTH] = (gated[g] * inv_rms).astype(BF16)

    q_b = (proj_ref[:, P_Q:P_Q + ATT_WIDTH] * (LOG2E / math.sqrt(ATT_HEAD_DIM))).astype(BF16)
    k_b = proj_ref[:, P_K:P_K + ATT_KV_WIDTH].astype(BF16)
    v_b = proj_ref[:, P_V:P_V + ATT_KV_WIDTH].astype(BF16)
    keys = jnp.concatenate([p.kprev[...], k_b], axis=0)
    v_prev = p.vprev[...]
    halves = lambda t: (jnp.where(upper_half_b, zero_b, t), jnp.where(upper_half_b, t, zero_b))
    v_prev_half, v_half = halves(v_prev), halves(v_b)
    p.kprev[...] = k_b
    p.vprev[...] = v_b
    probs, values = [], []
    for pos, hq in enumerate(ATT_POS_HEADS):
        tile, half = divmod(pos, 2)
        q_tile = q_b[:, tile * LANES:(tile + 1) * LANES]
        q_head = halves(q_tile)[half]
        s = _dot_nt(q_head, keys)
        s = jnp.where(causal, s[:, CHUNK:], s[:, :CHUNK]) + p.bias[bias_idx, pos]
        sink = p.sink[hq] * LOG2E
        m = jnp.maximum(jnp.max(s, axis=-1, keepdims=True), sink)
        e = jnp.exp2(s - m)
        denom = jnp.sum(e, axis=-1, keepdims=True) + jnp.exp2(sink - m)
        pn = (e * (1.0 / denom)).astype(BF16)
        probs.append(jnp.where(causal_b, zero_b, pn))
        probs.append(jnp.where(causal_b, pn, zero_b))
        for vh in (v_prev_half[half], v_half[half]):
            values.append(jnp.concatenate([zero_b, vh] if tile else [vh, zero_b], axis=-1))
        yield (P_Q, P_U) if pos == 0 else None
    att = _dot(jnp.concatenate(probs, axis=-1), jnp.concatenate(values, axis=0))
    ycat_ref[:, Y_ATT:Y_SGU] = _unit_rms(att).astype(BF16)
    yield

    u = _gelu_tanh(proj_ref[:, P_U:P_U + GM_WIDTH])
    yield P_U, P_GV
    gv = _gelu_tanh(proj_ref[:, P_GV:P_GV + GM_WIDTH])
    yield P_GV, P_DT
    mu = jnp.mean(gv, axis=-1, keepdims=True)
    gc = gv - mu
    gv = gc * lax.rsqrt(jnp.mean(gc * gc, axis=-1, keepdims=True) + EPS) * p.lnw[...] + p.lnb[...]
    gv_b = gv.astype(BF16)
    mixed = _dot(p.wcat[...], block_diag(gv_b)) + p.sgub[...]
    ycat_ref[:, Y_SGU:] = _unit_rms(u * mixed).astype(BF16)
    yield


class _MixerRefs:
    def __init__(self, **refs):
        self.__dict__.update(refs)


def _mixer_kernel(sink_ref, rel_ref,
                  xa_ref, xc_ref, ing_ref, win_ref, convw_ref, convb_ref, dtb_ref, alog_ref,
                  dskip_ref, triu_ref, expand_ref, bucket_ref,
                  lnw_ref, lnb_ref, sguw_ref, sgub_ref, outg_ref, wout_ref,
                  o_ref,
                  win_s, wdt_s, wout_s, projx_ref, projy_ref, dtx_ref, dty_ref, ycatx_ref, ycaty_ref,
                  xnx_ref, xny_ref,
                  ext_ref, st_ref, kprev_ref, vprev_ref, bias_ref, wcat_ref,
                  *, chunks_per_seq):
    t = pl.program_id(0)
    p = _MixerRefs(sink=sink_ref, convw=convw_ref, convb=convb_ref, dtb=dtb_ref, alog=alog_ref,
                   dskip=dskip_ref, triu=triu_ref, expand=expand_ref, lnw=lnw_ref, lnb=lnb_ref,
                   sgub=sgub_ref, ext=ext_ref, st=st_ref, kprev=kprev_ref, vprev=vprev_ref,
                   bias=bias_ref, wcat=wcat_ref)

    @pl.when(t == 0)
    def _init():
        row = lax.broadcasted_iota(jnp.int32, (CHUNK, CHUNK), 0)
        col = lax.broadcasted_iota(jnp.int32, (CHUNK, CHUNK), 1)
        bucket = bucket_ref[...]
        for pos, hq in enumerate(ATT_POS_HEADS):
            acc = jnp.zeros((CHUNK, CHUNK), F32)
            for k in range(REL_BUCKETS):
                acc = jnp.where(bucket == k, rel_ref[k * ATT_Q_HEADS + hq] * LOG2E, acc)
            bias_ref[0, pos] = acc
            bias_ref[1, pos] = jnp.where(row >= col, acc, MASKED)
        for g in range(GM_GROUPS):
            wcat_ref[:, g * CHUNK:(g + 1) * CHUNK] = jnp.where(row >= col, sguw_ref[g], 0.0).astype(BF16)

        def source_rows(b):
            lo = b * LANES
            if lo < P_Q:
                return [(lo, LANES)]
            if lo < P_K:
                pos = (lo - P_Q) // ATT_HEAD_DIM
                return [(R_Q + ATT_POS_HEADS[pos + i] * ATT_HEAD_DIM, ATT_HEAD_DIM)
                        for i in range(LANES // ATT_HEAD_DIM)]
            return [(lo - P_K + R_K, LANES)]

        def pack_rows(i, carry):
            r = pl.ds(pl.multiple_of(i * CHUNK, CHUNK), CHUNK)
            gain = ing_ref[:, r]
            for b in range(P_DT // LANES):
                tile = jnp.concatenate([win_ref[r0:r0 + n, r] for r0, n in source_rows(b)], axis=0)
                win_s[r, b * LANES:(b + 1) * LANES] = (tile * gain).T.astype(BF16)
            return carry

        lax.fori_loop(0, D_MODEL // CHUNK, pack_rows, 0)
        wdt_s[...] = jnp.concatenate(
            [win_ref[R_DT:R_DT + SSD_HEADS, :] * ing_ref[...],
             jnp.zeros((DT_ROWS - SSD_HEADS, D_MODEL), F32)], axis=0).astype(BF16)
        blocks = D_MODEL // ATT_HEAD_DIM
        first_att = Y_ATT // ATT_HEAD_DIM
        for j in range(blocks):
            src = first_att + ATT_POS_HEADS[j - first_att] if first_att <= j < first_att + ATT_Q_HEADS else j
            dst_rows = slice(j * ATT_HEAD_DIM, (j + 1) * ATT_HEAD_DIM)
            src_rows = slice(src * ATT_HEAD_DIM, (src + 1) * ATT_HEAD_DIM)
            wout_s[dst_rows, :] = (wout_ref[src_rows, :] * outg_ref[dst_rows, :]).astype(BF16)
        projy_ref[...] = jnp.zeros_like(projy_ref)
        dty_ref[...] = jnp.zeros_like(dty_ref)
        ycatx_ref[...] = jnp.zeros_like(ycatx_ref)
        st_ref[...] = jnp.zeros_like(st_ref)
        ext_ref[0:SUBLANES, :] = jnp.zeros((SUBLANES, SSD_XBC), F32)
        kprev_ref[...] = jnp.zeros_like(kprev_ref)
        vprev_ref[...] = jnp.zeros_like(vprev_ref)

    starts_seq = lax.rem(4 * t, chunks_per_seq) == 0
    keep = jnp.where(starts_seq, 0.0, 1.0).astype(F32)
    first_idx = jnp.where(starts_seq, 1, 0).astype(jnp.int32)
    even, odd = pl.ds(0, CHUNK), pl.ds(CHUNK, CHUNK)

    def projection_pieces(rows, proj_new, dt_new, ycat_out, xn):
        def normalize():
            xn[...] = _unit_rms(xa_ref[rows, :]).astype(BF16)

        def project(c0, c1):
            proj_new[:, c0:c1] = _dot(xn[...], win_s[:, c0:c1])

        def project_gv_dt():
            project(P_GV, P_DT)
            dt_t = _dot_nt(wdt_s[...], xn[...])
            dt_new[0] = dt_t[:, :CHUNK]
            dt_new[1] = dt_t[:, CHUNK:]

        def output(c0, c1):
            o_ref[rows, c0:c1] = xc_ref[rows, c0:c1] + _dot(ycat_out[...], wout_s[:, c0:c1])

        in_piece = lambda lo, hi: ("in", lo, hi, functools.partial(project, lo, hi))
        out_piece = lambda i: ("out", i * PROJ_PIECE, (i + 1) * PROJ_PIECE,
                               functools.partial(output, i * PROJ_PIECE, (i + 1) * PROJ_PIECE))
        xbc = [in_piece(P_XBC + c, P_XBC + c + PROJ_PIECE) for c in range(0, SSD_XBC, PROJ_PIECE)]
        z = [in_piece(P_Z + c, P_Z + c + PROJ_PIECE) for c in range(0, SSD_INNER, PROJ_PIECE)]
        return ([("norm", 0, 0, normalize)] + z + [out_piece(0)] + xbc[:2] + [out_piece(1)] + xbc[2:]
                + [out_piece(2), in_piece(P_Q, P_K), in_piece(P_K, P_U), out_piece(3),
                   in_piece(P_U, P_GV), ("in", P_GV, P_TOTAL, project_gv_dt)])

    mixes = [
        _mix_chunk(projy_ref.at[even], dty_ref.at[0], ycaty_ref.at[even], None, 0, p),
        _mix_chunk(projy_ref.at[odd], dty_ref.at[1], ycaty_ref.at[odd], None, 0, p),
        _mix_chunk(projx_ref.at[even], dtx_ref.at[0], ycatx_ref.at[even], keep, first_idx, p),
        _mix_chunk(projx_ref.at[odd], dtx_ref.at[1], ycatx_ref.at[odd], None, 0, p),
    ]
    pieces = [projection_pieces(pl.ds(0, PAIR), projx_ref, dtx_ref, ycatx_ref, xnx_ref),
              projection_pieces(pl.ds(PAIR, PAIR), projy_ref, dty_ref, ycaty_ref, xny_ref)]
    overlaps = lambda reads, piece: reads is not None and reads[0] < piece[2] and piece[1] < reads[1]
    done = [0] * len(mixes)
    issued = [0, 0]

    def mix_may_run(k):
        section = done[k]
        if section == MIX_SECTIONS:
            return False
        if k > 0 and done[k - 1] < min(section + MIX_LAG, MIX_SECTIONS):
            return False
        if k >= 2 and issued[0] < len(pieces[0]):
            return False
        return True

    def piece_may_issue(phase):
        if issued[phase] == len(pieces[phase]):
            return False
        piece = pieces[phase][issued[phase]]
        if phase == 1:
            if piece[0] == "in":
                return all(done[k] > s for k in (0, 1) for s, r in MIX_READS.items() if overlaps(r, piece))
            if piece[0] == "out":
                return done[0] == done[1] == MIX_SECTIONS
        return True

    while min(done) < MIX_SECTIONS or issued != [len(pieces[0]), len(pieces[1])]:
        progressed = False
        for k, mix in enumerate(mixes):
            if mix_may_run(k):
                assert next(mix) == MIX_READS.get(done[k]), "MIX_READS is stale"
                done[k] += 1
                progressed = True
        for phase in (0, 1):
            if piece_may_issue(phase):
                pieces[phase][issued[phase]][3]()
                issued[phase] += 1
                progressed = True
                break
        assert progressed, "mixer step schedule is stuck"
    for mix in mixes:
        assert next(mix, None) is None and mix.gi_frame is None, "MIX_SECTIONS is stale"


def _folded_bucket_tile():
    i = np.arange(CHUNK)[:, None]
    j = np.arange(CHUNK)[None, :]
    n = np.where(j <= i, i - j, i - j + CHUNK)
    max_exact = REL_BUCKETS // 2
    large = max_exact + (np.log(np.maximum(n, 1) / max_exact) / np.log(REL_MAX_DIST / max_exact)
                         * (REL_BUCKETS - max_exact)).astype(np.int32)
    large = np.minimum(large, REL_BUCKETS - 1)
    return np.where(n < max_exact, n, large).astype(np.int32)


def _mixer(x2d, seq, layer, in_gain, w_in_t, conv_w, conv_b, dt_bias, a_log, d_skip, sinks, rel_bias,
           ln_w, ln_b, sgu_w, sgu_b, out_gain, w_out):
    tokens = x2d.shape[0]
    n_quads = tokens // QUAD
    chunks_per_seq = seq // CHUNK
    assert tokens % QUAD == 0 and seq % QUAD == 0 and GM_GROUPS == HEADS_PER_GROUP
    assert P_Q == R_DT and w_in_t.shape[1] == R_K + P_DT - P_K
    per_head_rows = lambda v: jnp.broadcast_to(v[:, None], (SSD_HEADS, LANES))
    row = lambda v: v.reshape(1, -1)
    expand = np.zeros((CHUNK, 2 * SSD_INNER), np.float32)
    for r in range(SSD_HEADS, 7 * SSD_HEADS):
        h = r % SSD_HEADS
        c0 = ((r // SSD_HEADS - 1) // 3) * SSD_INNER + h * SSD_HEAD_DIM
        expand[r, c0:c0 + SSD_HEAD_DIM] = 1.0
    expand = jnp.asarray(expand, BF16)
    triu = jnp.asarray(np.triu(np.ones((CHUNK, CHUNK), np.float32)), BF16)
    bucket = jnp.asarray(_folded_bucket_tile())
    sgu_b_e = jnp.repeat(jnp.transpose(sgu_b), GM_GROUP_DIM, axis=1)

    const2 = lambda t, *_: (0, 0)
    const3 = lambda t, *_: (0, 0, 0)
    full2 = lambda shape: pl.BlockSpec(shape, const2)
    layer_block = lambda shape: pl.BlockSpec((None,) + shape, lambda t, *_: (layer, 0, 0),
                                             pipeline_mode=pl.Buffered(1))
    projected = lambda t, *_: (jnp.minimum(t, n_quads - 1), 0)
    finished = lambda t, *_: (jnp.maximum(t - 1, 0), 0)
    grid_spec = pltpu.PrefetchScalarGridSpec(
        num_scalar_prefetch=2,
        grid=(n_quads + 1,),
        in_specs=[
            pl.BlockSpec((QUAD, D_MODEL), projected),
            pl.BlockSpec((QUAD, D_MODEL), finished),
            layer_block((1, D_MODEL)),
            layer_block((w_in_t.shape[1], D_MODEL)),
            full2((SSD_CONV, SSD_XBC)),
            full2((1, SSD_XBC)),
            full2((SSD_HEADS, LANES)),
            full2((SSD_HEADS, LANES)),
            full2((1, SSD_INNER)),
            full2((CHUNK, CHUNK)),
            full2((CHUNK, 2 * SSD_INNER)),
            full2((CHUNK, CHUNK)),
            full2((1, GM_WIDTH)),
            full2((1, GM_WIDTH)),
            pl.BlockSpec((GM_GROUPS, CHUNK, CHUNK), const3),
            full2((CHUNK, GM_WIDTH)),
            layer_block((D_MODEL, 1)),
            layer_block((D_MODEL, D_MODEL)),
        ],
        out_specs=pl.BlockSpec((QUAD, D_MODEL), finished),
        scratch_shapes=[
            pltpu.VMEM((D_MODEL, P_DT), BF16),
            pltpu.VMEM((DT_ROWS, D_MODEL), BF16),
            pltpu.VMEM((D_MODEL, D_MODEL), BF16),
            pltpu.VMEM((PAIR, P_DT), F32),
            pltpu.VMEM((PAIR, P_DT), F32),
            pltpu.VMEM((2, DT_ROWS, CHUNK), F32),
            pltpu.VMEM((2, DT_ROWS, CHUNK), F32),
            pltpu.VMEM((PAIR, D_MODEL), BF16),
            pltpu.VMEM((PAIR, D_MODEL), BF16),
            pltpu.VMEM((PAIR, D_MODEL), BF16),
            pltpu.VMEM((PAIR, D_MODEL), BF16),
            pltpu.VMEM((SUBLANES + CHUNK, SSD_XBC), F32),
            pltpu.VMEM((SSD_GROUPS, SSD_STATE, GROUP_WIDTH), F32),
            pltpu.VMEM((CHUNK, ATT_KV_WIDTH), BF16),
            pltpu.VMEM((CHUNK, ATT_KV_WIDTH), BF16),
            pltpu.VMEM((2, ATT_Q_HEADS, CHUNK, CHUNK), F32),
            pltpu.VMEM((CHUNK, GM_GROUPS * CHUNK), BF16),
        ],
    )
    return pl.pallas_call(
        functools.partial(_mixer_kernel, chunks_per_seq=chunks_per_seq),
        out_shape=jax.ShapeDtypeStruct(x2d.shape, F32),
        grid_spec=grid_spec,
        compiler_params=pltpu.CompilerParams(
            dimension_semantics=("arbitrary",), vmem_limit_bytes=VMEM_LIMIT_MIXER),
        name="token_mixer",
    )(sinks.astype(F32), rel_bias.reshape(-1).astype(F32),
      x2d, x2d, in_gain, w_in_t, conv_w, row(conv_b), per_head_rows(dt_bias), per_head_rows(a_log),
      row(jnp.repeat(d_skip, SSD_HEAD_DIM)), triu, expand, bucket,
      row(ln_w), row(ln_b), sgu_w, sgu_b_e, out_gain, w_out)


def kernel(x, ffn1_norm, ffn1_w_gate, ffn1_w_up, ffn1_w_down, mix_norm, w_in, conv_w, conv_b, dt_bias, a_log, d_skip, ssd_norm, attn_sinks, rel_bias, attn_out_norm, sgu_ln_w, sgu_ln_b, sgu_w, sgu_b, sgu_out_norm, w_out, ffn2_norm, ffn2_w_gate, ffn2_w_up, ffn2_w_down, final_norm):
    batch, seq, d = x.shape
    depth = w_in.shape[0]
    fw = final_norm.reshape(1, d)
    x2 = x.reshape(batch * seq, d)
    ffn1_gain = ffn1_norm.reshape(depth, 1, d)
    ffn2_gain = ffn2_norm.reshape(depth, 1, d)
    mix_in_gain = mix_norm.reshape(depth, 1, d)
    w_in_t = jnp.swapaxes(w_in, 1, 2)
    att_cols = np.concatenate([np.arange(h * ATT_HEAD_DIM, (h + 1) * ATT_HEAD_DIM) for h in ATT_POS_HEADS])
    mix_out_gain = jnp.concatenate([ssd_norm, attn_out_norm[:, att_cols], sgu_out_norm], axis=1).reshape(depth, d, 1)
    for l in range(depth):
        x2 = _ffn(x2, l, ffn1_gain, ffn1_w_gate, ffn1_w_up, ffn1_w_down, fw, final_norm=False)
        x2 = _mixer(x2, seq, l, mix_in_gain, w_in_t, conv_w[l], conv_b[l], dt_bias[l],
                    a_log[l], d_skip[l], attn_sinks[l], rel_bias,
                    sgu_ln_w[l], sgu_ln_b[l], sgu_w[l], sgu_b[l], mix_out_gain, w_out)
        x2 = _ffn(x2, l, ffn2_gain, ffn2_w_gate, ffn2_w_up, ffn2_w_down, fw,
                  final_norm=(l == depth - 1))
    return x2.reshape(batch, seq, d)
```

```python
import functools
import math

import jax
import jax.numpy as jnp
import numpy as np
from jax import lax
from jax.experimental import pallas as pl
from jax.experimental.pallas import tpu as pltpu

F32 = jnp.float32
BF16 = jnp.bfloat16

D_MODEL = 1024
D_FF = 2816
EPS = 1e-6

SSD_HEADS = 8
SSD_HEAD_DIM = 64
SSD_INNER = SSD_HEADS * SSD_HEAD_DIM
SSD_GROUPS = 2
SSD_STATE = 128
SSD_CONV = 4
SSD_XBC = SSD_INNER + 2 * SSD_GROUPS * SSD_STATE
HEADS_PER_GROUP = SSD_HEADS // SSD_GROUPS
GROUP_WIDTH = HEADS_PER_GROUP * SSD_HEAD_DIM

ATT_Q_HEADS = 4
ATT_KV_HEADS = 2
ATT_HEAD_DIM = 64
ATT_WIDTH = ATT_Q_HEADS * ATT_HEAD_DIM
ATT_KV_WIDTH = ATT_KV_HEADS * ATT_HEAD_DIM
REL_BUCKETS = 32
REL_MAX_DIST = 128
ATT_POS_HEADS = (0, 2, 1, 3)

GM_GROUPS = 4
GM_GROUP_DIM = 64
GM_WIDTH = GM_GROUPS * GM_GROUP_DIM

CHUNK = 128
PAIR = 2 * CHUNK
QUAD = 2 * PAIR
PROJ_PIECE = 256
CONV_COLS = 256
MIX_SECTIONS = 18
MIX_LAG = 4

SUBLANES = 8
LANES = 128
HALF_LANES = LANES // 2
DT_PAD = LANES
DT_ROWS = 16

P_Z = 0
P_XBC = P_Z + SSD_INNER
P_Q = P_XBC + SSD_XBC
P_K = P_Q + ATT_WIDTH
P_V = P_K + ATT_KV_WIDTH
P_U = P_V + ATT_KV_WIDTH
P_GV = P_U + GM_WIDTH
P_DT = P_GV + GM_WIDTH
P_TOTAL = P_DT + DT_PAD

R_XBC = SSD_INNER
R_DT = R_XBC + SSD_XBC
R_Q = R_DT + SSD_HEADS
R_K = R_Q + ATT_WIDTH

Y_ATT = SSD_INNER
Y_SGU = Y_ATT + ATT_WIDTH

MIX_READS = {
    **{i: (P_XBC + i * CONV_COLS, P_XBC + (i + 1) * CONV_COLS) for i in range(SSD_XBC // CONV_COLS)},
    4: (P_DT, P_TOTAL),
    8: (P_Z, P_Z + GROUP_WIDTH),
    9: (P_Z + GROUP_WIDTH, P_Z + 2 * GROUP_WIDTH),
    10: (P_Q, P_U),
    15: (P_U, P_GV),
    16: (P_GV, P_DT),
}

MASKED = -1e30
LOG2E = math.log2(math.e)

FFN_TOKENS = 1024
FFN_CHUNK = 256
FFN_PIECES = D_FF // FFN_CHUNK
VMEM_LIMIT_FFN = 56 * 1024 * 1024
VMEM_LIMIT_MIXER = 48 * 1024 * 1024


def _unit_rms(x):
    return x * lax.rsqrt(jnp.mean(x * x, axis=-1, keepdims=True) + EPS)


def _sigmoid(x):
    return 1.0 / (1.0 + jnp.exp(-x))


def _gelu_tanh(x):
    c = math.sqrt(2.0 / math.pi)
    return 0.5 * x * (1.0 + jnp.tanh(c * (x + 0.044715 * (x * x * x))))


def _top_bits(x):
    bits = lax.bitcast_convert_type(x, jnp.uint32) & jnp.uint32(0xFFFF0000)
    return lax.bitcast_convert_type(bits, F32)


def _split3(x):
    hi = _top_bits(x)
    r = x - hi
    mid = _top_bits(r)
    return hi, mid, r - mid


def _dot(a, b):
    return jnp.dot(a, b, preferred_element_type=F32)


def _dot_nt(a, b):
    return lax.dot_general(a, b, (((1,), (1,)), ((), ())), preferred_element_type=F32)


def _lane_cumsum(x, upper_ones):
    rows = x.shape[0]
    parts = _dot(jnp.concatenate(_split3(x), axis=0).astype(BF16), upper_ones)
    return parts[0:rows] + parts[rows:2 * rows] + parts[2 * rows:3 * rows]


def _ffn_kernel(x_ref, nw_ref, wg_ref, wu_ref, wd_ref, fw_ref, o_ref,
                wg_s, wu_s, wd_s, h_ref, xn_s, acc_s, *, final_norm):
    i = pl.program_id(0)

    def normalized(x):
        return (_unit_rms(x) * nw_ref[...]).astype(BF16)

    def gated(xn, wg, wu):
        g = _dot(xn, wg)
        u = _dot(xn, wu)
        return (g * _sigmoid(g) * u).astype(BF16)

    def finish(x, y):
        out = x + 0.5 * y
        if final_norm:
            out = _unit_rms(out) * fw_ref[...]
        o_ref[...] = out

    @pl.when(i < FFN_PIECES)
    def _first_rows_while_loading():
        wg_s[i] = wg_ref[...].astype(BF16)
        wu_s[i] = wu_ref[...].astype(BF16)
        wd_s[i] = wd_ref[...].astype(BF16)

        @pl.when(i == 0)
        def _start():
            xn_s[...] = normalized(x_ref[...])
            acc_s[...] = jnp.zeros_like(acc_s)

        acc_s[...] += _dot(gated(xn_s[...], wg_s[i], wu_s[i]), wd_s[i])

        @pl.when(i == FFN_PIECES - 1)
        def _end():
            finish(x_ref[...], acc_s[...])

    @pl.when(i >= FFN_PIECES)
    def _token_rows():
        x = x_ref[...]
        xn = normalized(x)
        for c in range(FFN_PIECES):
            h_ref[:, c * FFN_CHUNK:(c + 1) * FFN_CHUNK] = gated(xn, wg_s[c], wu_s[c])
        finish(x, _dot(h_ref[...], wd_s[...].reshape(D_FF, D_MODEL)))


def _ffn(x2d, layer, norm_w, w_gate, w_up, w_down, final_w, *, final_norm):
    tokens = x2d.shape[0]
    last = FFN_PIECES - 1
    rows = lambda i: (jnp.maximum(i - last, 0), 0)
    return pl.pallas_call(
        functools.partial(_ffn_kernel, final_norm=final_norm),
        out_shape=jax.ShapeDtypeStruct(x2d.shape, F32),
        grid=(last + tokens // FFN_TOKENS,),
        in_specs=[
            pl.BlockSpec((FFN_TOKENS, D_MODEL), rows),
            pl.BlockSpec((None, 1, D_MODEL), lambda i: (layer, 0, 0)),
            pl.BlockSpec((None, D_MODEL, FFN_CHUNK), lambda i: (layer, 0, jnp.minimum(i, last))),
            pl.BlockSpec((None, D_MODEL, FFN_CHUNK), lambda i: (layer, 0, jnp.minimum(i, last))),
            pl.BlockSpec((None, FFN_CHUNK, D_MODEL), lambda i: (layer, jnp.minimum(i, last), 0)),
            pl.BlockSpec((1, D_MODEL), lambda i: (0, 0)),
        ],
        out_specs=pl.BlockSpec((FFN_TOKENS, D_MODEL), rows),
        scratch_shapes=[
            pltpu.VMEM((FFN_PIECES, D_MODEL, FFN_CHUNK), BF16),
            pltpu.VMEM((FFN_PIECES, D_MODEL, FFN_CHUNK), BF16),
            pltpu.VMEM((FFN_PIECES, FFN_CHUNK, D_MODEL), BF16),
            pltpu.VMEM((FFN_TOKENS, D_FF), BF16),
            pltpu.VMEM((FFN_TOKENS, D_MODEL), BF16),
            pltpu.VMEM((FFN_TOKENS, D_MODEL), F32),
        ],
        compiler_params=pltpu.CompilerParams(
            dimension_semantics=("arbitrary",), vmem_limit_bytes=VMEM_LIMIT_FFN),
        name="ffn_halfstep",
    )(x2d, norm_w, w_gate, w_up, w_down, final_w)


def _mix_chunk(proj_ref, dt_ref, ycat_ref, keep, bias_idx, p):
    row = lax.broadcasted_iota(jnp.int32, (CHUNK, CHUNK), 0)
    lane = lax.broadcasted_iota(jnp.int32, (CHUNK, CHUNK), 1)
    causal = row >= lane
    row_b = row.astype(F32).astype(BF16)
    lane_b = lane.astype(F32).astype(BF16)
    causal_b = row_b >= lane_b
    upper_half_b = lane_b >= HALF_LANES
    zero_b = jnp.zeros((CHUNK, CHUNK), BF16)
    group_b = (lax.broadcasted_iota(jnp.int32, (CHUNK, GROUP_WIDTH), 1) // SSD_HEAD_DIM).astype(F32).astype(BF16)

    def block_diag(x_b):
        zero = jnp.zeros_like(x_b)
        return jnp.concatenate([jnp.where(group_b == i, x_b, zero) for i in range(HEADS_PER_GROUP)], axis=0)

    xbc_parts = []
    for c0 in range(0, SSD_XBC, CONV_COLS):
        cs = slice(c0, c0 + CONV_COLS)
        if keep is not None:
            p.ext[0:SUBLANES, cs] = p.ext[0:SUBLANES, cs] * keep
        raw = proj_ref[:, P_XBC + c0:P_XBC + c0 + CONV_COLS]
        p.ext[SUBLANES:, cs] = raw
        ext = p.ext[:, cs]
        w = p.convw[:, cs]
        ext1 = pltpu.roll(ext, 1, 0)
        older = w[1:2, :] * ext + w[0:1, :] * ext1
        conv = (p.convb[:, cs] + w[3:4, :] * raw + w[2:3, :] * ext1[SUBLANES:, :]
                + pltpu.roll(older, 2, 0)[SUBLANES:, :])
        p.ext[0:SUBLANES, cs] = raw[CHUNK - SUBLANES:, :]
        xbc_parts.append(conv * _sigmoid(conv))
        yield P_XBC + c0, P_XBC + c0 + CONV_COLS
    xbc = jnp.concatenate(xbc_parts, axis=-1)
    xs = xbc[:, :SSD_INNER]
    bm = xbc[:, SSD_INNER:SSD_INNER + SSD_GROUPS * SSD_STATE]
    cm = xbc[:, SSD_INNER + SSD_GROUPS * SSD_STATE:]
    x_b = xs.astype(BF16)
    bm_b = bm.astype(BF16)
    cm_b = cm.astype(BF16)

    dtr = dt_ref[0:SSD_HEADS, :] + p.dtb[...]
    dt = jnp.maximum(dtr, 0.0) + jnp.log1p(jnp.exp(-jnp.abs(dtr)))
    acs = _lane_cumsum(dt * (-LOG2E * jnp.exp(p.alog[...])), p.triu[...])
    grow = jnp.exp2(acs)
    to_end = dt * jnp.exp2(acs[:, CHUNK - 1:CHUNK] - acs)
    src = acs - jnp.log2(dt)
    stack = jnp.concatenate(
        [acs, *_split3(grow), *_split3(to_end), jnp.zeros((CHUNK - 7 * SSD_HEADS, LANES), F32)], axis=0)
    cols = stack.T
    expanded = _dot(cols.astype(BF16), p.expand[...])
    grow_e = expanded[:, :SSD_INNER]
    to_end_e = expanded[:, SSD_INNER:]
    xd_b = (xs * to_end_e).astype(BF16)
    yield P_DT, P_TOTAL

    y_diag = []
    for g in range(SSD_GROUPS):
        n0 = g * SSD_STATE
        cb = _dot_nt(cm_b[:, n0:n0 + SSD_STATE], bm_b[:, n0:n0 + SSD_STATE])
        parts = []
        for hh in range(HEADS_PER_GROUP):
            h = g * HEADS_PER_GROUP + hh
            seg = cols[:, h:h + 1] - src[h:h + 1, :]
            parts.append((cb * jnp.exp2(jnp.where(causal, seg, -jnp.inf))).astype(BF16))
        w0 = g * GROUP_WIDTH
        y_diag.append(_dot(jnp.concatenate(parts, axis=-1), block_diag(x_b[:, w0:w0 + GROUP_WIDTH])))
        yield
    y_off = []
    for g in range(SSD_GROUPS):
        n0 = g * SSD_STATE
        w0 = g * GROUP_WIDTH
        prev = p.st[g]
        if keep is not None:
            prev = prev * keep
        y_off.append(_dot(cm_b[:, n0:n0 + SSD_STATE], prev.astype(BF16)))
        bt = bm[:, n0:n0 + SSD_STATE].T.astype(BF16)
        p.st[g] = (prev * grow_e[CHUNK - 1:CHUNK, w0:w0 + GROUP_WIDTH]
                   + _dot(bt, xd_b[:, w0:w0 + GROUP_WIDTH]))
    yield
    gated, sumsq = [], 0.0
    for g in range(SSD_GROUPS):
        ws = slice(g * GROUP_WIDTH, (g + 1) * GROUP_WIDTH)
        y = y_diag[g] + y_off[g] * grow_e[:, ws] + xs[:, ws] * p.dskip[:, ws]
        z = proj_ref[:, P_Z + g * GROUP_WIDTH:P_Z + (g + 1) * GROUP_WIDTH]
        gated.append(y * (z * _sigmoid(z)))
        sumsq = sumsq + jnp.sum(gated[g] * gated[g], axis=-1, keepdims=True)
        yield P_Z + g * GROUP_WIDTH, P_Z + (g + 1) * GROUP_WIDTH
    inv_rms = lax.rsqrt(sumsq * (1.0 / SSD_INNER) + EPS)
    for g in range(SSD_GROUPS):
        ycat_ref[:, g * GROUP_WIDTH:(g + 1) * GROUP_WIDTH] = (gated[g] * inv_rms).astype(BF16)

    q_b = (proj_ref[:, P_Q:P_Q + ATT_WIDTH] * (LOG2E / math.sqrt(ATT_HEAD_DIM))).astype(BF16)
    k_b = proj_ref[:, P_K:P_K + ATT_KV_WIDTH].astype(BF16)
    v_b = proj_ref[:, P_V:P_V + ATT_KV_WIDTH].astype(BF16)
    keys = jnp.concatenate([p.kprev[...], k_b], axis=0)
    v_prev = p.vprev[...]
    halves = lambda t: (jnp.where(upper_half_b, zero_b, t), jnp.where(upper_half_b, t, zero_b))
    v_prev_half, v_half = halves(v_prev), halves(v_b)
    p.kprev[...] = k_b
    p.vprev[...] = v_b
    probs, values = [], []
    for pos, hq in enumerate(ATT_POS_HEADS):
        tile, half = divmod(pos, 2)
        q_tile = q_b[:, tile * LANES:(tile + 1) * LANES]
        q_head = halves(q_tile)[half]
        s = _dot_nt(q_head, keys)
        s = jnp.where(causal, s[:, CHUNK:], s[:, :CHUNK]) + p.bias[bias_idx, pos]
        sink = p.sink[hq] * LOG2E
        m = jnp.maximum(jnp.max(s, axis=-1, keepdims=True), sink)
        e = jnp.exp2(s - m)
        denom = jnp.sum(e, axis=-1, keepdims=True) + jnp.exp2(sink - m)
        pn = (e * (1.0 / denom)).astype(BF16)
        probs.append(jnp.where(causal_b, zero_b, pn))
        probs.append(jnp.where(causal_b, pn, zero_b))
        for vh in (v_prev_half[half], v_half[half]):
            values.append(jnp.concatenate([zero_b, vh] if tile else [vh, zero_b], axis=-1))
        yield (P_Q, P_U) if pos == 0 else None
    att = _dot(jnp.concatenate(probs, axis=-1), jnp.concatenate(values, axis=0))
    ycat_ref[:, Y_ATT:Y_SGU] = _unit_rms(att).astype(BF16)
    yield

    u = _gelu_tanh(proj_ref[:, P_U:P_U + GM_WIDTH])
    yield P_U, P_GV
    gv = _gelu_tanh(proj_ref[:, P_GV:P_GV + GM_WIDTH])
    yield P_GV, P_DT
    mu = jnp.mean(gv, axis=-1, keepdims=True)
    gc = gv - mu
    gv = gc * lax.rsqrt(jnp.mean(gc * gc, axis=-1, keepdims=True) + EPS) * p.lnw[...] + p.lnb[...]
    gv_b = gv.astype(BF16)
    mixed = _dot(p.wcat[...], block_diag(gv_b)) + p.sgub[...]
    ycat_ref[:, Y_SGU:] = _unit_rms(u * mixed).astype(BF16)
    yield


class _MixerRefs:
    def __init__(self, **refs):
        self.__dict__.update(refs)


def _mixer_kernel(sink_ref, rel_ref,
                  xa_ref, ing_ref, win_ref, convw_ref, convb_ref, dtb_ref, alog_ref,
                  dskip_ref, triu_ref, expand_ref, bucket_ref,
                  lnw_ref, lnb_ref, sguw_ref, sgub_ref, outg_ref, wout_ref,
                  o_ref,
                  win_s, wdt_s, wout_s, projx_ref, projy_ref, dtx_ref, dty_ref, ycatx_ref, ycaty_ref,
                  xnx_ref, xny_ref, xres_ref,
                  ext_ref, st_ref, kprev_ref, vprev_ref, bias_ref, wcat_ref,
                  *, chunks_per_seq):
    t = pl.program_id(0)
    p = _MixerRefs(sink=sink_ref, convw=convw_ref, convb=convb_ref, dtb=dtb_ref, alog=alog_ref,
                   dskip=dskip_ref, triu=triu_ref, expand=expand_ref, lnw=lnw_ref, lnb=lnb_ref,
                   sgub=sgub_ref, ext=ext_ref, st=st_ref, kprev=kprev_ref, vprev=vprev_ref,
                   bias=bias_ref, wcat=wcat_ref)

    @pl.when(t == 0)
    def _init():
        row = lax.broadcasted_iota(jnp.int32, (CHUNK, CHUNK), 0)
        col = lax.broadcasted_iota(jnp.int32, (CHUNK, CHUNK), 1)
        bucket = bucket_ref[...]
        for pos, hq in enumerate(ATT_POS_HEADS):
            acc = jnp.zeros((CHUNK, CHUNK), F32)
            for k in range(REL_BUCKETS):
                acc = jnp.where(bucket == k, rel_ref[k * ATT_Q_HEADS + hq] * LOG2E, acc)
            bias_ref[0, pos] = acc
            bias_ref[1, pos] = jnp.where(row >= col, acc, MASKED)
        for g in range(GM_GROUPS):
            wcat_ref[:, g * CHUNK:(g + 1) * CHUNK] = jnp.where(row >= col, sguw_ref[g], 0.0).astype(BF16)

        def source_rows(b):
            lo = b * LANES
            if lo < P_Q:
                return [(lo, LANES)]
            if lo < P_K:
                pos = (lo - P_Q) // ATT_HEAD_DIM
                return [(R_Q + ATT_POS_HEADS[pos + i] * ATT_HEAD_DIM, ATT_HEAD_DIM)
                        for i in range(LANES // ATT_HEAD_DIM)]
            return [(lo - P_K + R_K, LANES)]

        def pack_rows(i, carry):
            r = pl.ds(pl.multiple_of(i * CHUNK, CHUNK), CHUNK)
            gain = ing_ref[:, r]
            for b in range(P_DT // LANES):
                tile = jnp.concatenate([win_ref[r0:r0 + n, r] for r0, n in source_rows(b)], axis=0)
                win_s[r, b * LANES:(b + 1) * LANES] = (tile * gain).T.astype(BF16)
            return carry

        lax.fori_loop(0, D_MODEL // CHUNK, pack_rows, 0)
        wdt_s[...] = jnp.concatenate(
            [win_ref[R_DT:R_DT + SSD_HEADS, :] * ing_ref[...],
             jnp.zeros((DT_ROWS - SSD_HEADS, D_MODEL), F32)], axis=0).astype(BF16)
        blocks = D_MODEL // ATT_HEAD_DIM
        first_att = Y_ATT // ATT_HEAD_DIM
        for j in range(blocks):
            src = first_att + ATT_POS_HEADS[j - first_att] if first_att <= j < first_att + ATT_Q_HEADS else j
            dst_rows = slice(j * ATT_HEAD_DIM, (j + 1) * ATT_HEAD_DIM)
            src_rows = slice(src * ATT_HEAD_DIM, (src + 1) * ATT_HEAD_DIM)
            wout_s[dst_rows, :] = (wout_ref[src_rows, :] * outg_ref[dst_rows, :]).astype(BF16)
        projy_ref[...] = jnp.zeros_like(projy_ref)
        dty_ref[...] = jnp.zeros_like(dty_ref)
        ycatx_ref[...] = jnp.zeros_like(ycatx_ref)
        st_ref[...] = jnp.zeros_like(st_ref)
        ext_ref[0:SUBLANES, :] = jnp.zeros((SUBLANES, SSD_XBC), F32)
        kprev_ref[...] = jnp.zeros_like(kprev_ref)
        vprev_ref[...] = jnp.zeros_like(vprev_ref)

    starts_seq = lax.rem(4 * t, chunks_per_seq) == 0
    keep = jnp.where(starts_seq, 0.0, 1.0).astype(F32)
    first_idx = jnp.where(starts_seq, 1, 0).astype(jnp.int32)
    even, odd = pl.ds(0, CHUNK), pl.ds(CHUNK, CHUNK)

    def projection_pieces(rows, proj_new, dt_new, ycat_out, xn):
        def normalize():
            xn[...] = _unit_rms(xa_ref[rows, :]).astype(BF16)

        def project(c0, c1):
            proj_new[:, c0:c1] = _dot(xn[...], win_s[:, c0:c1])

        def project_gv_dt():
            project(P_GV, P_DT)
            dt_t = _dot_nt(wdt_s[...], xn[...])
            dt_new[0] = dt_t[:, :CHUNK]
            dt_new[1] = dt_t[:, CHUNK:]

        def output(c0, c1):
            o_ref[rows, c0:c1] = xres_ref[rows, c0:c1] + _dot(ycat_out[...], wout_s[:, c0:c1])
            xres_ref[rows, c0:c1] = xa_ref[rows, c0:c1]

        in_piece = lambda lo, hi: ("in", lo, hi, functools.partial(project, lo, hi))
        out_piece = lambda i: ("out", i * PROJ_PIECE, (i + 1) * PROJ_PIECE,
                               functools.partial(output, i * PROJ_PIECE, (i + 1) * PROJ_PIECE))
        xbc = [in_piece(P_XBC + c, P_XBC + c + PROJ_PIECE) for c in range(0, SSD_XBC, PROJ_PIECE)]
        z = [in_piece(P_Z + c, P_Z + c + PROJ_PIECE) for c in range(0, SSD_INNER, PROJ_PIECE)]
        return ([("norm", 0, 0, normalize)] + z + [out_piece(0)] + xbc[:2] + [out_piece(1)] + xbc[2:]
                + [out_piece(2), in_piece(P_Q, P_K), in_piece(P_K, P_U), out_piece(3),
                   in_piece(P_U, P_GV), ("in", P_GV, P_TOTAL, project_gv_dt)])

    mixes = [
        _mix_chunk(projy_ref.at[even], dty_ref.at[0], ycaty_ref.at[even], None, 0, p),
        _mix_chunk(projy_ref.at[odd], dty_ref.at[1], ycaty_ref.at[odd], None, 0, p),
        _mix_chunk(projx_ref.at[even], dtx_ref.at[0], ycatx_ref.at[even], keep, first_idx, p),
        _mix_chunk(projx_ref.at[odd], dtx_ref.at[1], ycatx_ref.at[odd], None, 0, p),
    ]
    pieces = [projection_pieces(pl.ds(0, PAIR), projx_ref, dtx_ref, ycatx_ref, xnx_ref),
              projection_pieces(pl.ds(PAIR, PAIR), projy_ref, dty_ref, ycaty_ref, xny_ref)]
    overlaps = lambda reads, piece: reads is not None and reads[0] < piece[2] and piece[1] < reads[1]
    done = [0] * len(mixes)
    issued = [0, 0]

    def mix_may_run(k):
        section = done[k]
        if section == MIX_SECTIONS:
            return False
        if k > 0 and done[k - 1] < min(section + MIX_LAG, MIX_SECTIONS):
            return False
        if k >= 2 and issued[0] < len(pieces[0]):
            return False
        return True

    def piece_may_issue(phase):
        if issued[phase] == len(pieces[phase]):
            return False
        piece = pieces[phase][issued[phase]]
        if phase == 1:
            if piece[0] == "in":
                return all(done[k] > s for k in (0, 1) for s, r in MIX_READS.items() if overlaps(r, piece))
            if piece[0] == "out":
                return done[0] == done[1] == MIX_SECTIONS
        return True

    while min(done) < MIX_SECTIONS or issued != [len(pieces[0]), len(pieces[1])]:
        progressed = False
        for k, mix in enumerate(mixes):
            if mix_may_run(k):
                assert next(mix) == MIX_READS.get(done[k]), "MIX_READS is stale"
                done[k] += 1
                progressed = True
        for phase in (0, 1):
            if piece_may_issue(phase):
                pieces[phase][issued[phase]][3]()
                issued[phase] += 1
                progressed = True
                break
        assert progressed, "mixer step schedule is stuck"
    for mix in mixes:
        assert next(mix, None) is None and mix.gi_frame is None, "MIX_SECTIONS is stale"


def _folded_bucket_tile():
    i = np.arange(CHUNK)[:, None]
    j = np.arange(CHUNK)[None, :]
    n = np.where(j <= i, i - j, i - j + CHUNK)
    max_exact = REL_BUCKETS // 2
    large = max_exact + (np.log(np.maximum(n, 1) / max_exact) / np.log(REL_MAX_DIST / max_exact)
                         * (REL_BUCKETS - max_exact)).astype(np.int32)
    large = np.minimum(large, REL_BUCKETS - 1)
    return np.where(n < max_exact, n, large).astype(np.int32)


def _mixer(x2d, seq, layer, in_gain, w_in_t, conv_w, conv_b, dt_bias, a_log, d_skip, sinks, rel_bias,
           ln_w, ln_b, sgu_w, sgu_b, out_gain, w_out):
    tokens = x2d.shape[0]
    n_quads = tokens // QUAD
    chunks_per_seq = seq // CHUNK
    assert tokens % QUAD == 0 and seq % QUAD == 0 and GM_GROUPS == HEADS_PER_GROUP
    assert P_Q == R_DT and w_in_t.shape[1] == R_K + P_DT - P_K
    per_head_rows = lambda v: jnp.broadcast_to(v[:, None], (SSD_HEADS, LANES))
    row = lambda v: v.reshape(1, -1)
    expand = np.zeros((CHUNK, 2 * SSD_INNER), np.float32)
    for r in range(SSD_HEADS, 7 * SSD_HEADS):
        h = r % SSD_HEADS
        c0 = ((r // SSD_HEADS - 1) // 3) * SSD_INNER + h * SSD_HEAD_DIM
        expand[r, c0:c0 + SSD_HEAD_DIM] = 1.0
    expand = jnp.asarray(expand, BF16)
    triu = jnp.asarray(np.triu(np.ones((CHUNK, CHUNK), np.float32)), BF16)
    bucket = jnp.asarray(_folded_bucket_tile())
    sgu_b_e = jnp.repeat(jnp.transpose(sgu_b), GM_GROUP_DIM, axis=1)

    const2 = lambda t, *_: (0, 0)
    const3 = lambda t, *_: (0, 0, 0)
    full2 = lambda shape: pl.BlockSpec(shape, const2)
    layer_block = lambda shape: pl.BlockSpec((None,) + shape, lambda t, *_: (layer, 0, 0),
                                             pipeline_mode=pl.Buffered(1))
    projected = lambda t, *_: (jnp.minimum(t, n_quads - 1), 0)
    finished = lambda t, *_: (jnp.maximum(t - 1, 0), 0)
    grid_spec = pltpu.PrefetchScalarGridSpec(
        num_scalar_prefetch=2,
        grid=(n_quads + 1,),
        in_specs=[
            pl.BlockSpec((QUAD, D_MODEL), projected),
            layer_block((1, D_MODEL)),
            layer_block((w_in_t.shape[1], D_MODEL)),
            full2((SSD_CONV, SSD_XBC)),
            full2((1, SSD_XBC)),
            full2((SSD_HEADS, LANES)),
            full2((SSD_HEADS, LANES)),
            full2((1, SSD_INNER)),
            full2((CHUNK, CHUNK)),
            full2((CHUNK, 2 * SSD_INNER)),
            full2((CHUNK, CHUNK)),
            full2((1, GM_WIDTH)),
            full2((1, GM_WIDTH)),
            pl.BlockSpec((GM_GROUPS, CHUNK, CHUNK), const3),
            full2((CHUNK, GM_WIDTH)),
            layer_block((D_MODEL, 1)),
            layer_block((D_MODEL, D_MODEL)),
        ],
        out_specs=pl.BlockSpec((QUAD, D_MODEL), finished),
        scratch_shapes=[
            pltpu.VMEM((D_MODEL, P_DT), BF16),
            pltpu.VMEM((DT_ROWS, D_MODEL), BF16),
            pltpu.VMEM((D_MODEL, D_MODEL), BF16),
            pltpu.VMEM((PAIR, P_DT), F32),
            pltpu.VMEM((PAIR, P_DT), F32),
            pltpu.VMEM((2, DT_ROWS, CHUNK), F32),
            pltpu.VMEM((2, DT_ROWS, CHUNK), F32),
            pltpu.VMEM((PAIR, D_MODEL), BF16),
            pltpu.VMEM((PAIR, D_MODEL), BF16),
            pltpu.VMEM((PAIR, D_MODEL), BF16),
            pltpu.VMEM((PAIR, D_MODEL), BF16),
            pltpu.VMEM((QUAD, D_MODEL), F32),
            pltpu.VMEM((SUBLANES + CHUNK, SSD_XBC), F32),
            pltpu.VMEM((SSD_GROUPS, SSD_STATE, GROUP_WIDTH), F32),
            pltpu.VMEM((CHUNK, ATT_KV_WIDTH), BF16),
            pltpu.VMEM((CHUNK, ATT_KV_WIDTH), BF16),
            pltpu.VMEM((2, ATT_Q_HEADS, CHUNK, CHUNK), F32),
            pltpu.VMEM((CHUNK, GM_GROUPS * CHUNK), BF16),
        ],
    )
    return pl.pallas_call(
        functools.partial(_mixer_kernel, chunks_per_seq=chunks_per_seq),
        out_shape=jax.ShapeDtypeStruct(x2d.shape, F32),
        grid_spec=grid_spec,
        compiler_params=pltpu.CompilerParams(
            dimension_semantics=("arbitrary",), vmem_limit_bytes=VMEM_LIMIT_MIXER),
        name="token_mixer",
    )(sinks.astype(F32), rel_bias.reshape(-1).astype(F32),
      x2d, in_gain, w_in_t, conv_w, row(conv_b), per_head_rows(dt_bias), per_head_rows(a_log),
      row(jnp.repeat(d_skip, SSD_HEAD_DIM)), triu, expand, bucket,
      row(ln_w), row(ln_b), sgu_w, sgu_b_e, out_gain, w_out)


def kernel(x, ffn1_norm, ffn1_w_gate, ffn1_w_up, ffn1_w_down, mix_norm, w_in, conv_w, conv_b, dt_bias, a_log, d_skip, ssd_norm, attn_sinks, rel_bias, attn_out_norm, sgu_ln_w, sgu_ln_b, sgu_w, sgu_b, sgu_out_norm, w_out, ffn2_norm, ffn2_w_gate, ffn2_w_up, ffn2_w_down, final_norm):
    batch, seq, d = x.shape
    depth = w_in.shape[0]
    fw = final_norm.reshape(1, d)
    x2 = x.reshape(batch * seq, d)
    ffn1_gain = ffn1_norm.reshape(depth, 1, d)
    ffn2_gain = ffn2_norm.reshape(depth, 1, d)
    mix_in_gain = mix_norm.reshape(depth, 1, d)
    w_in_t = jnp.swapaxes(w_in, 1, 2)
    att_cols = np.concatenate([np.arange(h * ATT_HEAD_DIM, (h + 1) * ATT_HEAD_DIM) for h in ATT_POS_HEADS])
    mix_out_gain = jnp.concatenate([ssd_norm, attn_out_norm[:, att_cols], sgu_out_norm], axis=1).reshape(depth, d, 1)
    for l in range(depth):
        x2 = _ffn(x2, l, ffn1_gain, ffn1_w_gate, ffn1_w_up, ffn1_w_down, fw, final_norm=False)
        x2 = _mixer(x2, seq, l, mix_in_gain, w_in_t, conv_w[l], conv_b[l], dt_bias[l],
                    a_log[l], d_skip[l], attn_sinks[l], rel_bias,
                    sgu_ln_w[l], sgu_ln_b[l], sgu_w[l], sgu_b[l], mix_out_gain, w_out)
        x2 = _ffn(x2, l, ffn2_gain, ffn2_w_gate, ffn2_w_up, ffn2_w_down, fw,
                  final_norm=(l == depth - 1))
    return x2.reshape(batch, seq, d)
```

```python
import functools
import math

import jax
import jax.numpy as jnp
import numpy as np
from jax import lax
from jax.experimental import pallas as pl
from jax.experimental.pallas import tpu as pltpu

F32 = jnp.float32
BF16 = jnp.bfloat16

D_MODEL = 1024
D_FF = 2816
EPS = 1e-6

SSD_HEADS = 8
SSD_HEAD_DIM = 64
SSD_INNER = SSD_HEADS * SSD_HEAD_DIM
SSD_GROUPS = 2
SSD_STATE = 128
SSD_CONV = 4
SSD_XBC = SSD_INNER + 2 * SSD_GROUPS * SSD_STATE
HEADS_PER_GROUP = SSD_HEADS // SSD_GROUPS
GROUP_WIDTH = HEADS_PER_GROUP * SSD_HEAD_DIM

ATT_Q_HEADS = 4
ATT_KV_HEADS = 2
ATT_HEAD_DIM = 64
ATT_WIDTH = ATT_Q_HEADS * ATT_HEAD_DIM
ATT_KV_WIDTH = ATT_KV_HEADS * ATT_HEAD_DIM
REL_BUCKETS = 32
REL_MAX_DIST = 128
ATT_POS_HEADS = (0, 2, 1, 3)

GM_GROUPS = 4
GM_GROUP_DIM = 64
GM_WIDTH = GM_GROUPS * GM_GROUP_DIM

CHUNK = 128
PAIR = 2 * CHUNK
QUAD = 2 * PAIR
PROJ_PIECE = 256
CONV_COLS = 256
MIX_SECTIONS = 18
MIX_LAG = 4

SUBLANES = 8
LANES = 128
HALF_LANES = LANES // 2
DT_PAD = LANES
DT_ROWS = 16

P_Z = 0
P_XBC = P_Z + SSD_INNER
P_Q = P_XBC + SSD_XBC
P_K = P_Q + ATT_WIDTH
P_V = P_K + ATT_KV_WIDTH
P_U = P_V + ATT_KV_WIDTH
P_GV = P_U + GM_WIDTH
P_DT = P_GV + GM_WIDTH
P_TOTAL = P_DT + DT_PAD

R_XBC = SSD_INNER
R_DT = R_XBC + SSD_XBC
R_Q = R_DT + SSD_HEADS
R_K = R_Q + ATT_WIDTH

Y_ATT = SSD_INNER
Y_SGU = Y_ATT + ATT_WIDTH

MIX_READS = {
    **{i: (P_XBC + i * CONV_COLS, P_XBC + (i + 1) * CONV_COLS) for i in range(SSD_XBC // CONV_COLS)},
    4: (P_DT, P_TOTAL),
    8: (P_Z, P_Z + GROUP_WIDTH),
    9: (P_Z + GROUP_WIDTH, P_Z + 2 * GROUP_WIDTH),
    10: (P_Q, P_U),
    15: (P_U, P_GV),
    16: (P_GV, P_DT),
}

MASKED = -1e30
LOG2E = math.log2(math.e)

FFN_TOKENS = 1024
FFN_CHUNK = 256
FFN_PIECES = D_FF // FFN_CHUNK
VMEM_LIMIT_FFN = 56 * 1024 * 1024
VMEM_LIMIT_MIXER = 48 * 1024 * 1024


def _unit_rms(x):
    return x * lax.rsqrt(jnp.mean(x * x, axis=-1, keepdims=True) + EPS)


def _sigmoid(x):
    return 1.0 / (1.0 + jnp.exp(-x))


def _gelu_tanh(x):
    c = math.sqrt(2.0 / math.pi)
    return 0.5 * x * (1.0 + jnp.tanh(c * (x + 0.044715 * (x * x * x))))


def _top_bits(x):
    bits = lax.bitcast_convert_type(x, jnp.uint32) & jnp.uint32(0xFFFF0000)
    return lax.bitcast_convert_type(bits, F32)


def _split3(x):
    hi = _top_bits(x)
    r = x - hi
    mid = _top_bits(r)
    return hi, mid, r - mid


def _dot(a, b):
    return jnp.dot(a, b, preferred_element_type=F32)


def _dot_nt(a, b):
    return lax.dot_general(a, b, (((1,), (1,)), ((), ())), preferred_element_type=F32)


def _lane_cumsum(x, upper_ones):
    rows = x.shape[0]
    parts = _dot(jnp.concatenate(_split3(x), axis=0).astype(BF16), upper_ones)
    return parts[0:rows] + parts[rows:2 * rows] + parts[2 * rows:3 * rows]


def _ffn_kernel(x_ref, nw_ref, wg_ref, wu_ref, wd_ref, fw_ref, o_ref,
                wg_s, wu_s, wd_s, h_ref, xn_s, acc_s, *, final_norm):
    i = pl.program_id(0)

    def normalized(x):
        return (_unit_rms(x) * nw_ref[...]).astype(BF16)

    def gated(xn, wg, wu):
        g = _dot(xn, wg)
        u = _dot(xn, wu)
        return (g * _sigmoid(g) * u).astype(BF16)

    def finish(rows, y):
        out = x_ref[rows, :] + 0.5 * y
        if final_norm:
            out = _unit_rms(out) * fw_ref[...]
        o_ref[rows, :] = out

    @pl.when(i < FFN_PIECES)
    def _first_rows_while_loading():
        wg_s[i] = wg_ref[...].astype(BF16)
        wu_s[i] = wu_ref[...].astype(BF16)
        wd_s[i] = wd_ref[...].astype(BF16)

        @pl.when(i == 0)
        def _start():
            xn_s[...] = normalized(x_ref[...])
            acc_s[...] = jnp.zeros_like(acc_s)

        acc_s[...] += _dot(gated(xn_s[...], wg_s[i], wu_s[i]), wd_s[i])

        @pl.when(i == FFN_PIECES - 1)
        def _end():
            finish(slice(None), acc_s[...])

    @pl.when(i >= FFN_PIECES)
    def _token_rows():
        xn = normalized(x_ref[...])
        for c in range(FFN_PIECES):
            h_ref[:, c * FFN_CHUNK:(c + 1) * FFN_CHUNK] = gated(xn, wg_s[c], wu_s[c])
        wd = wd_s[...].reshape(D_FF, D_MODEL)
        block = FFN_TOKENS // 4 if final_norm else FFN_TOKENS
        for r in range(0, FFN_TOKENS, block):
            rows = slice(r, r + block)
            finish(rows, _dot(h_ref[rows, :], wd))


def _ffn(x2d, layer, norm_w, w_gate, w_up, w_down, final_w, *, final_norm):
    tokens = x2d.shape[0]
    last = FFN_PIECES - 1
    rows = lambda i: (jnp.maximum(i - last, 0), 0)
    return pl.pallas_call(
        functools.partial(_ffn_kernel, final_norm=final_norm),
        out_shape=jax.ShapeDtypeStruct(x2d.shape, F32),
        grid=(last + tokens // FFN_TOKENS,),
        in_specs=[
            pl.BlockSpec((FFN_TOKENS, D_MODEL), rows),
            pl.BlockSpec((None, 1, D_MODEL), lambda i: (layer, 0, 0)),
            pl.BlockSpec((None, D_MODEL, FFN_CHUNK), lambda i: (layer, 0, jnp.minimum(i, last))),
            pl.BlockSpec((None, D_MODEL, FFN_CHUNK), lambda i: (layer, 0, jnp.minimum(i, last))),
            pl.BlockSpec((None, FFN_CHUNK, D_MODEL), lambda i: (layer, jnp.minimum(i, last), 0)),
            pl.BlockSpec((1, D_MODEL), lambda i: (0, 0)),
        ],
        out_specs=pl.BlockSpec((FFN_TOKENS, D_MODEL), rows),
        scratch_shapes=[
            pltpu.VMEM((FFN_PIECES, D_MODEL, FFN_CHUNK), BF16),
            pltpu.VMEM((FFN_PIECES, D_MODEL, FFN_CHUNK), BF16),
            pltpu.VMEM((FFN_PIECES, FFN_CHUNK, D_MODEL), BF16),
            pltpu.VMEM((FFN_TOKENS, D_FF), BF16),
            pltpu.VMEM((FFN_TOKENS, D_MODEL), BF16),
            pltpu.VMEM((FFN_TOKENS, D_MODEL), F32),
        ],
        compiler_params=pltpu.CompilerParams(
            dimension_semantics=("arbitrary",), vmem_limit_bytes=VMEM_LIMIT_FFN),
        name="ffn_halfstep",
    )(x2d, norm_w, w_gate, w_up, w_down, final_w)


def _mix_chunk(proj_ref, dt_ref, ycat_ref, keep, bias_idx, p):
    row = lax.broadcasted_iota(jnp.int32, (CHUNK, CHUNK), 0)
    lane = lax.broadcasted_iota(jnp.int32, (CHUNK, CHUNK), 1)
    causal = row >= lane
    row_b = row.astype(F32).astype(BF16)
    lane_b = lane.astype(F32).astype(BF16)
    causal_b = row_b >= lane_b
    upper_half_b = lane_b >= HALF_LANES
    zero_b = jnp.zeros((CHUNK, CHUNK), BF16)
    group_b = (lax.broadcasted_iota(jnp.int32, (CHUNK, GROUP_WIDTH), 1) // SSD_HEAD_DIM).astype(F32).astype(BF16)

    def block_diag(x_b):
        zero = jnp.zeros_like(x_b)
        return jnp.concatenate([jnp.where(group_b == i, x_b, zero) for i in range(HEADS_PER_GROUP)], axis=0)

    xbc_parts = []
    for c0 in range(0, SSD_XBC, CONV_COLS):
        cs = slice(c0, c0 + CONV_COLS)
        if keep is not None:
            p.ext[0:SUBLANES, cs] = p.ext[0:SUBLANES, cs] * keep
        raw = proj_ref[:, P_XBC + c0:P_XBC + c0 + CONV_COLS]
        p.ext[SUBLANES:, cs] = raw
        ext = p.ext[:, cs]
        w = p.convw[:, cs]
        ext1 = pltpu.roll(ext, 1, 0)
        older = w[1:2, :] * ext + w[0:1, :] * ext1
        conv = (p.convb[:, cs] + w[3:4, :] * raw + w[2:3, :] * ext1[SUBLANES:, :]
                + pltpu.roll(older, 2, 0)[SUBLANES:, :])
        p.ext[0:SUBLANES, cs] = raw[CHUNK - SUBLANES:, :]
        xbc_parts.append(conv * _sigmoid(conv))
        yield P_XBC + c0, P_XBC + c0 + CONV_COLS
    xbc = jnp.concatenate(xbc_parts, axis=-1)
    xs = xbc[:, :SSD_INNER]
    bm = xbc[:, SSD_INNER:SSD_INNER + SSD_GROUPS * SSD_STATE]
    cm = xbc[:, SSD_INNER + SSD_GROUPS * SSD_STATE:]
    x_b = xs.astype(BF16)
    bm_b = bm.astype(BF16)
    cm_b = cm.astype(BF16)

    dtr = dt_ref[0:SSD_HEADS, :] + p.dtb[...]
    dt = jnp.maximum(dtr, 0.0) + jnp.log1p(jnp.exp(-jnp.abs(dtr)))
    acs = _lane_cumsum(dt * (-LOG2E * jnp.exp(p.alog[...])), p.triu[...])
    grow = jnp.exp2(acs)
    to_end = dt * jnp.exp2(acs[:, CHUNK - 1:CHUNK] - acs)
    src = acs - jnp.log2(dt)
    stack = jnp.concatenate(
        [acs, *_split3(grow), *_split3(to_end), jnp.zeros((CHUNK - 7 * SSD_HEADS, LANES), F32)], axis=0)
    cols = stack.T
    expanded = _dot(cols.astype(BF16), p.expand[...])
    grow_e = expanded[:, :SSD_INNER]
    to_end_e = expanded[:, SSD_INNER:]
    xd_b = (xs * to_end_e).astype(BF16)
    yield P_DT, P_TOTAL

    y_diag = []
    for g in range(SSD_GROUPS):
        n0 = g * SSD_STATE
        cb = _dot_nt(cm_b[:, n0:n0 + SSD_STATE], bm_b[:, n0:n0 + SSD_STATE])
        parts = []
        for hh in range(HEADS_PER_GROUP):
            h = g * HEADS_PER_GROUP + hh
            seg = cols[:, h:h + 1] - src[h:h + 1, :]
            parts.append((cb * jnp.exp2(jnp.where(causal, seg, -jnp.inf))).astype(BF16))
        w0 = g * GROUP_WIDTH
        y_diag.append(_dot(jnp.concatenate(parts, axis=-1), block_diag(x_b[:, w0:w0 + GROUP_WIDTH])))
        yield
    y_off = []
    for g in range(SSD_GROUPS):
        n0 = g * SSD_STATE
        w0 = g * GROUP_WIDTH
        prev = p.st[g]
        if keep is not None:
            prev = prev * keep
        y_off.append(_dot(cm_b[:, n0:n0 + SSD_STATE], prev.astype(BF16)))
        bt = bm[:, n0:n0 + SSD_STATE].T.astype(BF16)
        p.st[g] = (prev * grow_e[CHUNK - 1:CHUNK, w0:w0 + GROUP_WIDTH]
                   + _dot(bt, xd_b[:, w0:w0 + GROUP_WIDTH]))
    yield
    gated, sumsq = [], 0.0
    for g in range(SSD_GROUPS):
        ws = slice(g * GROUP_WIDTH, (g + 1) * GROUP_WIDTH)
        y = y_diag[g] + y_off[g] * grow_e[:, ws] + xs[:, ws] * p.dskip[:, ws]
        z = proj_ref[:, P_Z + g * GROUP_WIDTH:P_Z + (g + 1) * GROUP_WIDTH]
        gated.append(y * (z * _sigmoid(z)))
        sumsq = sumsq + jnp.sum(gated[g] * gated[g], axis=-1, keepdims=True)
        yield P_Z + g * GROUP_WIDTH, P_Z + (g + 1) * GROUP_WIDTH
    inv_rms = lax.rsqrt(sumsq * (1.0 / SSD_INNER) + EPS)
    for g in range(SSD_GROUPS):
        ycat_ref[:, g * GROUP_WIDTH:(g + 1) * GROUP_WIDTH] = (gated[g] * inv_rms).astype(BF16)

    q_b = (proj_ref[:, P_Q:P_Q + ATT_WIDTH] * (LOG2E / math.sqrt(ATT_HEAD_DIM))).astype(BF16)
    k_b = proj_ref[:, P_K:P_K + ATT_KV_WIDTH].astype(BF16)
    v_b = proj_ref[:, P_V:P_V + ATT_KV_WIDTH].astype(BF16)
    keys = jnp.concatenate([p.kprev[...], k_b], axis=0)
    v_prev = p.vprev[...]
    halves = lambda t: (jnp.where(upper_half_b, zero_b, t), jnp.where(upper_half_b, t, zero_b))
    v_prev_half, v_half = halves(v_prev), halves(v_b)
    p.kprev[...] = k_b
    p.vprev[...] = v_b
    probs, values = [], []
    for pos, hq in enumerate(ATT_POS_HEADS):
        tile, half = divmod(pos, 2)
        q_tile = q_b[:, tile * LANES:(tile + 1) * LANES]
        q_head = halves(q_tile)[half]
        s = _dot_nt(q_head, keys)
        s = jnp.where(causal, s[:, CHUNK:], s[:, :CHUNK]) + p.bias[bias_idx, pos]
        sink = p.sink[hq] * LOG2E
        m = jnp.maximum(jnp.max(s, axis=-1, keepdims=True), sink)
        e = jnp.exp2(s - m)
        denom = jnp.sum(e, axis=-1, keepdims=True) + jnp.exp2(sink - m)
        pn = (e * (1.0 / denom)).astype(BF16)
        probs.append(jnp.where(causal_b, zero_b, pn))
        probs.append(jnp.where(causal_b, pn, zero_b))
        for vh in (v_prev_half[half], v_half[half]):
            values.append(jnp.concatenate([zero_b, vh] if tile else [vh, zero_b], axis=-1))
        yield (P_Q, P_U) if pos == 0 else None
    att = _dot(jnp.concatenate(probs, axis=-1), jnp.concatenate(values, axis=0))
    ycat_ref[:, Y_ATT:Y_SGU] = _unit_rms(att).astype(BF16)
    yield

    u = _gelu_tanh(proj_ref[:, P_U:P_U + GM_WIDTH])
    yield P_U, P_GV
    gv = _gelu_tanh(proj_ref[:, P_GV:P_GV + GM_WIDTH])
    yield P_GV, P_DT
    mu = jnp.mean(gv, axis=-1, keepdims=True)
    gc = gv - mu
    gv = gc * lax.rsqrt(jnp.mean(gc * gc, axis=-1, keepdims=True) + EPS) * p.lnw[...] + p.lnb[...]
    gv_b = gv.astype(BF16)
    mixed = _dot(p.wcat[...], block_diag(gv_b)) + p.sgub[...]
    ycat_ref[:, Y_SGU:] = _unit_rms(u * mixed).astype(BF16)
    yield


class _MixerRefs:
    def __init__(self, **refs):
        self.__dict__.update(refs)


def _mixer_kernel(sink_ref, rel_ref,
                  xa_ref, xc_ref, ing_ref, win_ref, convw_ref, convb_ref, dtb_ref, alog_ref,
                  dskip_ref, triu_ref, expand_ref, bucket_ref,
                  lnw_ref, lnb_ref, sguw_ref, sgub_ref, outg_ref, wout_ref,
                  o_ref,
                  win_s, wdt_s, wout_s, projx_ref, projy_ref, dtx_ref, dty_ref, ycatx_ref, ycaty_ref,
                  xnx_ref, xny_ref,
                  ext_ref, st_ref, kprev_ref, vprev_ref, bias_ref, wcat_ref,
                  *, chunks_per_seq):
    t = pl.program_id(0)
    p = _MixerRefs(sink=sink_ref, convw=convw_ref, convb=convb_ref, dtb=dtb_ref, alog=alog_ref,
                   dskip=dskip_ref, triu=triu_ref, expand=expand_ref, lnw=lnw_ref, lnb=lnb_ref,
                   sgub=sgub_ref, ext=ext_ref, st=st_ref, kprev=kprev_ref, vprev=vprev_ref,
                   bias=bias_ref, wcat=wcat_ref)

    @pl.when(t == 0)
    def _init():
        row = lax.broadcasted_iota(jnp.int32, (CHUNK, CHUNK), 0)
        col = lax.broadcasted_iota(jnp.int32, (CHUNK, CHUNK), 1)
        bucket = bucket_ref[...]
        for pos, hq in enumerate(ATT_POS_HEADS):
            acc = jnp.zeros((CHUNK, CHUNK), F32)
            for k in range(REL_BUCKETS):
                acc = jnp.where(bucket == k, rel_ref[k * ATT_Q_HEADS + hq] * LOG2E, acc)
            bias_ref[0, pos] = acc
            bias_ref[1, pos] = jnp.where(row >= col, acc, MASKED)
        for g in range(GM_GROUPS):
            wcat_ref[:, g * CHUNK:(g + 1) * CHUNK] = jnp.where(row >= col, sguw_ref[g], 0.0).astype(BF16)

        def source_rows(b):
            lo = b * LANES
            if lo < P_Q:
                return [(lo, LANES)]
            if lo < P_K:
                pos = (lo - P_Q) // ATT_HEAD_DIM
                return [(R_Q + ATT_POS_HEADS[pos + i] * ATT_HEAD_DIM, ATT_HEAD_DIM)
                        for i in range(LANES // ATT_HEAD_DIM)]
            return [(lo - P_K + R_K, LANES)]

        def pack_rows(i, carry):
            r = pl.ds(pl.multiple_of(i * CHUNK, CHUNK), CHUNK)
            gain = ing_ref[:, r]
            for b in range(P_DT // LANES):
                tile = jnp.concatenate([win_ref[r0:r0 + n, r] for r0, n in source_rows(b)], axis=0)
                win_s[r, b * LANES:(b + 1) * LANES] = (tile * gain).T.astype(BF16)
            return carry

        lax.fori_loop(0, D_MODEL // CHUNK, pack_rows, 0)
        wdt_s[...] = jnp.concatenate(
            [win_ref[R_DT:R_DT + SSD_HEADS, :] * ing_ref[...],
             jnp.zeros((DT_ROWS - SSD_HEADS, D_MODEL), F32)], axis=0).astype(BF16)
        blocks = D_MODEL // ATT_HEAD_DIM
        first_att = Y_ATT // ATT_HEAD_DIM
        for j in range(blocks):
            src = first_att + ATT_POS_HEADS[j - first_att] if first_att <= j < first_att + ATT_Q_HEADS else j
            dst_rows = slice(j * ATT_HEAD_DIM, (j + 1) * ATT_HEAD_DIM)
            src_rows = slice(src * ATT_HEAD_DIM, (src + 1) * ATT_HEAD_DIM)
            wout_s[dst_rows, :] = (wout_ref[src_rows, :] * outg_ref[dst_rows, :]).astype(BF16)
        projy_ref[...] = jnp.zeros_like(projy_ref)
        dty_ref[...] = jnp.zeros_like(dty_ref)
        ycatx_ref[...] = jnp.zeros_like(ycatx_ref)
        st_ref[...] = jnp.zeros_like(st_ref)
        ext_ref[0:SUBLANES, :] = jnp.zeros((SUBLANES, SSD_XBC), F32)
        kprev_ref[...] = jnp.zeros_like(kprev_ref)
        vprev_ref[...] = jnp.zeros_like(vprev_ref)

    starts_seq = lax.rem(4 * t, chunks_per_seq) == 0
    keep = jnp.where(starts_seq, 0.0, 1.0).astype(F32)
    first_idx = jnp.where(starts_seq, 1, 0).astype(jnp.int32)
    even, odd = pl.ds(0, CHUNK), pl.ds(CHUNK, CHUNK)

    def projection_pieces(rows, proj_new, dt_new, ycat_out, xn):
        def normalize():
            xn[...] = _unit_rms(xa_ref[rows, :]).astype(BF16)

        def project(c0, c1):
            proj_new[:, c0:c1] = _dot(xn[...], win_s[:, c0:c1])

        def project_gv_dt():
            project(P_GV, P_DT)
            dt_t = _dot_nt(wdt_s[...], xn[...])
            dt_new[0] = dt_t[:, :CHUNK]
            dt_new[1] = dt_t[:, CHUNK:]

        def output(c0, c1):
            o_ref[rows, c0:c1] = xc_ref[rows, c0:c1] + _dot(ycat_out[...], wout_s[:, c0:c1])

        in_piece = lambda lo, hi: ("in", lo, hi, functools.partial(project, lo, hi))
        out_piece = lambda i: ("out", i * PROJ_PIECE, (i + 1) * PROJ_PIECE,
                               functools.partial(output, i * PROJ_PIECE, (i + 1) * PROJ_PIECE))
        xbc = [in_piece(P_XBC + c, P_XBC + c + PROJ_PIECE) for c in range(0, SSD_XBC, PROJ_PIECE)]
        z = [in_piece(P_Z + c, P_Z + c + PROJ_PIECE) for c in range(0, SSD_INNER, PROJ_PIECE)]
        return ([("norm", 0, 0, normalize)] + z + [out_piece(0)] + xbc[:2] + [out_piece(1)] + xbc[2:]
                + [out_piece(2), in_piece(P_Q, P_K), in_piece(P_K, P_U), out_piece(3),
                   in_piece(P_U, P_GV), ("in", P_GV, P_TOTAL, project_gv_dt)])

    def step_program(live_mixes, piece_kinds):
        mixes = [
            _mix_chunk(projy_ref.at[even], dty_ref.at[0], ycaty_ref.at[even], None, 0, p),
            _mix_chunk(projy_ref.at[odd], dty_ref.at[1], ycaty_ref.at[odd], None, 0, p),
            _mix_chunk(projx_ref.at[even], dtx_ref.at[0], ycatx_ref.at[even], keep, first_idx, p),
            _mix_chunk(projx_ref.at[odd], dtx_ref.at[1], ycatx_ref.at[odd], None, 0, p),
        ]
        pieces = [projection_pieces(pl.ds(0, PAIR), projx_ref, dtx_ref, ycatx_ref, xnx_ref),
                  projection_pieces(pl.ds(PAIR, PAIR), projy_ref, dty_ref, ycaty_ref, xny_ref)]
        pieces = [[piece for piece in phase if piece[0] in piece_kinds] for phase in pieces]
        overlaps = lambda reads, piece: reads is not None and reads[0] < piece[2] and piece[1] < reads[1]
        done = [0 if k in live_mixes else MIX_SECTIONS for k in range(len(mixes))]
        issued = [0, 0]

        def mix_may_run(k):
            section = done[k]
            if section == MIX_SECTIONS:
                return False
            if k > 0 and done[k - 1] < min(section + MIX_LAG, MIX_SECTIONS):
                return False
            if k >= 2 and issued[0] < len(pieces[0]):
                return False
            return True

        def piece_may_issue(phase):
            if issued[phase] == len(pieces[phase]):
                return False
            piece = pieces[phase][issued[phase]]
            if phase == 1:
                if piece[0] == "in":
                    return all(done[k] > s for k in (0, 1) for s, r in MIX_READS.items() if overlaps(r, piece))
                if piece[0] == "out":
                    return done[0] == done[1] == MIX_SECTIONS
            return True

        while min(done) < MIX_SECTIONS or issued != [len(pieces[0]), len(pieces[1])]:
            progressed = False
            for k, mix in enumerate(mixes):
                if mix_may_run(k):
                    assert next(mix) == MIX_READS.get(done[k]), "MIX_READS is stale"
                    done[k] += 1
                    progressed = True
            for phase in (0, 1):
                if piece_may_issue(phase):
                    pieces[phase][issued[phase]][3]()
                    issued[phase] += 1
                    progressed = True
                    break
            assert progressed, "mixer step schedule is stuck"
        for k in live_mixes:
            assert next(mixes[k], None) is None and mixes[k].gi_frame is None, "MIX_SECTIONS is stale"

    last_step = pl.num_programs(0) - 1

    @pl.when(t == 0)
    def _fill():
        step_program((2, 3), ("norm", "in"))

    @pl.when(jnp.logical_and(t > 0, t < last_step))
    def _steady():
        step_program((0, 1, 2, 3), ("norm", "in", "out"))

    @pl.when(t == last_step)
    def _drain():
        step_program((0, 1), ("out",))


def _folded_bucket_tile():
    i = np.arange(CHUNK)[:, None]
    j = np.arange(CHUNK)[None, :]
    n = np.where(j <= i, i - j, i - j + CHUNK)
    max_exact = REL_BUCKETS // 2
    large = max_exact + (np.log(np.maximum(n, 1) / max_exact) / np.log(REL_MAX_DIST / max_exact)
                         * (REL_BUCKETS - max_exact)).astype(np.int32)
    large = np.minimum(large, REL_BUCKETS - 1)
    return np.where(n < max_exact, n, large).astype(np.int32)


def _mixer(x2d, seq, layer, in_gain, w_in_t, conv_w, conv_b, dt_bias, a_log, d_skip, sinks, rel_bias,
           ln_w, ln_b, sgu_w, sgu_b, out_gain, w_out):
    tokens = x2d.shape[0]
    n_quads = tokens // QUAD
    chunks_per_seq = seq // CHUNK
    assert tokens % QUAD == 0 and seq % QUAD == 0 and GM_GROUPS == HEADS_PER_GROUP
    assert P_Q == R_DT and w_in_t.shape[1] == R_K + P_DT - P_K
    per_head_rows = lambda v: jnp.broadcast_to(v[:, None], (SSD_HEADS, LANES))
    row = lambda v: v.reshape(1, -1)
    expand = np.zeros((CHUNK, 2 * SSD_INNER), np.float32)
    for r in range(SSD_HEADS, 7 * SSD_HEADS):
        h = r % SSD_HEADS
        c0 = ((r // SSD_HEADS - 1) // 3) * SSD_INNER + h * SSD_HEAD_DIM
        expand[r, c0:c0 + SSD_HEAD_DIM] = 1.0
    expand = jnp.asarray(expand, BF16)
    triu = jnp.asarray(np.triu(np.ones((CHUNK, CHUNK), np.float32)), BF16)
    bucket = jnp.asarray(_folded_bucket_tile())
    sgu_b_e = jnp.repeat(jnp.transpose(sgu_b), GM_GROUP_DIM, axis=1)

    const2 = lambda t, *_: (0, 0)
    const3 = lambda t, *_: (0, 0, 0)
    full2 = lambda shape: pl.BlockSpec(shape, const2)
    layer_block = lambda shape: pl.BlockSpec((None,) + shape, lambda t, *_: (layer, 0, 0),
                                             pipeline_mode=pl.Buffered(1))
    projected = lambda t, *_: (jnp.minimum(t, n_quads - 1), 0)
    finished = lambda t, *_: (jnp.maximum(t - 1, 0), 0)
    grid_spec = pltpu.PrefetchScalarGridSpec(
        num_scalar_prefetch=2,
        grid=(n_quads + 1,),
        in_specs=[
            pl.BlockSpec((QUAD, D_MODEL), projected),
            pl.BlockSpec((QUAD, D_MODEL), finished),
            layer_block((1, D_MODEL)),
            layer_block((w_in_t.shape[1], D_MODEL)),
            full2((SSD_CONV, SSD_XBC)),
            full2((1, SSD_XBC)),
            full2((SSD_HEADS, LANES)),
            full2((SSD_HEADS, LANES)),
            full2((1, SSD_INNER)),
            full2((CHUNK, CHUNK)),
            full2((CHUNK, 2 * SSD_INNER)),
            full2((CHUNK, CHUNK)),
            full2((1, GM_WIDTH)),
            full2((1, GM_WIDTH)),
            pl.BlockSpec((GM_GROUPS, CHUNK, CHUNK), const3),
            full2((CHUNK, GM_WIDTH)),
            layer_block((D_MODEL, 1)),
            layer_block((D_MODEL, D_MODEL)),
        ],
        out_specs=pl.BlockSpec((QUAD, D_MODEL), finished),
        scratch_shapes=[
            pltpu.VMEM((D_MODEL, P_DT), BF16),
            pltpu.VMEM((DT_ROWS, D_MODEL), BF16),
            pltpu.VMEM((D_MODEL, D_MODEL), BF16),
            pltpu.VMEM((PAIR, P_DT), F32),
            pltpu.VMEM((PAIR, P_DT), F32),
            pltpu.VMEM((2, DT_ROWS, CHUNK), F32),
            pltpu.VMEM((2, DT_ROWS, CHUNK), F32),
            pltpu.VMEM((PAIR, D_MODEL), BF16),
            pltpu.VMEM((PAIR, D_MODEL), BF16),
            pltpu.VMEM((PAIR, D_MODEL), BF16),
            pltpu.VMEM((PAIR, D_MODEL), BF16),
            pltpu.VMEM((SUBLANES + CHUNK, SSD_XBC), F32),
            pltpu.VMEM((SSD_GROUPS, SSD_STATE, GROUP_WIDTH), F32),
            pltpu.VMEM((CHUNK, ATT_KV_WIDTH), BF16),
            pltpu.VMEM((CHUNK, ATT_KV_WIDTH), BF16),
            pltpu.VMEM((2, ATT_Q_HEADS, CHUNK, CHUNK), F32),
            pltpu.VMEM((CHUNK, GM_GROUPS * CHUNK), BF16),
        ],
    )
    return pl.pallas_call(
        functools.partial(_mixer_kernel, chunks_per_seq=chunks_per_seq),
        out_shape=jax.ShapeDtypeStruct(x2d.shape, F32),
        grid_spec=grid_spec,
        compiler_params=pltpu.CompilerParams(
            dimension_semantics=("arbitrary",), vmem_limit_bytes=VMEM_LIMIT_MIXER),
        name="token_mixer",
    )(sinks.astype(F32), rel_bias.reshape(-1).astype(F32),
      x2d, x2d, in_gain, w_in_t, conv_w, row(conv_b), per_head_rows(dt_bias), per_head_rows(a_log),
      row(jnp.repeat(d_skip, SSD_HEAD_DIM)), triu, expand, bucket,
      row(ln_w), row(ln_b), sgu_w, sgu_b_e, out_gain, w_out)


def kernel(x, ffn1_norm, ffn1_w_gate, ffn1_w_up, ffn1_w_down, mix_norm, w_in, conv_w, conv_b, dt_bias, a_log, d_skip, ssd_norm, attn_sinks, rel_bias, attn_out_norm, sgu_ln_w, sgu_ln_b, sgu_w, sgu_b, sgu_out_norm, w_out, ffn2_norm, ffn2_w_gate, ffn2_w_up, ffn2_w_down, final_norm):
    batch, seq, d = x.shape
    depth = w_in.shape[0]
    fw = final_norm.reshape(1, d)
    x2 = x.reshape(batch * seq, d)
    ffn1_gain = ffn1_norm.reshape(depth, 1, d)
    ffn2_gain = ffn2_norm.reshape(depth, 1, d)
    mix_in_gain = mix_norm.reshape(depth, 1, d)
    w_in_t = jnp.swapaxes(w_in, 1, 2)
    att_cols = np.concatenate([np.arange(h * ATT_HEAD_DIM, (h + 1) * ATT_HEAD_DIM) for h in ATT_POS_HEADS])
    mix_out_gain = jnp.concatenate([ssd_norm, attn_out_norm[:, att_cols], sgu_out_norm], axis=1).reshape(depth, d, 1)
    for l in range(depth):
        x2 = _ffn(x2, l, ffn1_gain, ffn1_w_gate, ffn1_w_up, ffn1_w_down, fw, final_norm=False)
        x2 = _mixer(x2, seq, l, mix_in_gain, w_in_t, conv_w[l], conv_b[l], dt_bias[l],
                    a_log[l], d_skip[l], attn_sinks[l], rel_bias,
                    sgu_ln_w[l], sgu_ln_b[l], sgu_w[l], sgu_b[l], mix_out_gain, w_out)
        x2 = _ffn(x2, l, ffn2_gain, ffn2_w_gate, ffn2_w_up, ffn2_w_down, fw,
                  final_norm=(l == depth - 1))
    return x2.reshape(batch, seq, d)
```

```python
import functools
import math

import jax
import jax.numpy as jnp
import numpy as np
from jax import lax
from jax.experimental import pallas as pl
from jax.experimental.pallas import tpu as pltpu

F32 = jnp.float32
BF16 = jnp.bfloat16

D_MODEL = 1024
D_FF = 2816
EPS = 1e-6

SSD_HEADS = 8
SSD_HEAD_DIM = 64
SSD_INNER = SSD_HEADS * SSD_HEAD_DIM
SSD_GROUPS = 2
SSD_STATE = 128
SSD_CONV = 4
SSD_XBC = SSD_INNER + 2 * SSD_GROUPS * SSD_STATE
HEADS_PER_GROUP = SSD_HEADS // SSD_GROUPS
GROUP_WIDTH = HEADS_PER_GROUP * SSD_HEAD_DIM

ATT_Q_HEADS = 4
ATT_KV_HEADS = 2
ATT_HEAD_DIM = 64
ATT_WIDTH = ATT_Q_HEADS * ATT_HEAD_DIM
ATT_KV_WIDTH = ATT_KV_HEADS * ATT_HEAD_DIM
REL_BUCKETS = 32
REL_MAX_DIST = 128
ATT_POS_HEADS = (0, 2, 1, 3)

GM_GROUPS = 4
GM_GROUP_DIM = 64
GM_WIDTH = GM_GROUPS * GM_GROUP_DIM

CHUNK = 128
PAIR = 2 * CHUNK
QUAD = 2 * PAIR
PROJ_PIECE = 256
CONV_COLS = 256
MIX_SECTIONS = 18
MIX_LAG = 4

SUBLANES = 8
LANES = 128
HALF_LANES = LANES // 2
DT_PAD = LANES
DT_ROWS = 16

P_Z = 0
P_XBC = P_Z + SSD_INNER
P_Q = P_XBC + SSD_XBC
P_K = P_Q + ATT_WIDTH
P_V = P_K + ATT_KV_WIDTH
P_U = P_V + ATT_KV_WIDTH
P_GV = P_U + GM_WIDTH
P_DT = P_GV + GM_WIDTH
P_TOTAL = P_DT + DT_PAD

R_XBC = SSD_INNER
R_DT = R_XBC + SSD_XBC
R_Q = R_DT + SSD_HEADS
R_K = R_Q + ATT_WIDTH

Y_ATT = SSD_INNER
Y_SGU = Y_ATT + ATT_WIDTH

MIX_READS = {
    **{i: (P_XBC + i * CONV_COLS, P_XBC + (i + 1) * CONV_COLS) for i in range(SSD_XBC // CONV_COLS)},
    4: (P_DT, P_TOTAL),
    8: (P_Z, P_Z + GROUP_WIDTH),
    9: (P_Z + GROUP_WIDTH, P_Z + 2 * GROUP_WIDTH),
    10: (P_Q, P_U),
    15: (P_U, P_GV),
    16: (P_GV, P_DT),
}

MASKED = -1e30
LOG2E = math.log2(math.e)

FFN_TOKENS = 1024
FFN_CHUNK = 256
FFN_PIECES = D_FF // FFN_CHUNK
VMEM_LIMIT_FFN = 56 * 1024 * 1024
VMEM_LIMIT_MIXER = 48 * 1024 * 1024


def _unit_rms(x):
    return x * lax.rsqrt(jnp.mean(x * x, axis=-1, keepdims=True) + EPS)


def _sigmoid(x):
    return 1.0 / (1.0 + jnp.exp(-x))


def _gelu_tanh(x):
    c = math.sqrt(2.0 / math.pi)
    return 0.5 * x * (1.0 + jnp.tanh(c * (x + 0.044715 * (x * x * x))))


def _top_bits(x):
    bits = lax.bitcast_convert_type(x, jnp.uint32) & jnp.uint32(0xFFFF0000)
    return lax.bitcast_convert_type(bits, F32)


def _split3(x):
    hi = _top_bits(x)
    r = x - hi
    mid = _top_bits(r)
    return hi, mid, r - mid


def _dot(a, b):
    return jnp.dot(a, b, preferred_element_type=F32)


def _dot_nt(a, b):
    return lax.dot_general(a, b, (((1,), (1,)), ((), ())), preferred_element_type=F32)


def _lane_cumsum(x, upper_ones):
    rows = x.shape[0]
    parts = _dot(jnp.concatenate(_split3(x), axis=0).astype(BF16), upper_ones)
    return parts[0:rows] + parts[rows:2 * rows] + parts[2 * rows:3 * rows]


def _ffn_kernel(x_ref, nw_ref, wg_ref, wu_ref, wd_ref, fw_ref, o_ref,
                wg_s, wu_s, wd_s, h_ref, xn_s, acc_s, *, layer, final_norm):
    i = pl.program_id(0)

    def normalized(x):
        return (_unit_rms(x) * nw_ref[layer:layer + 1, :]).astype(BF16)

    def gated(xn, wg, wu):
        g = _dot(xn, wg)
        u = _dot(xn, wu)
        return (g * _sigmoid(g) * u).astype(BF16)

    def finish(rows, y):
        out = x_ref[rows, :] + 0.5 * y
        if final_norm:
            out = _unit_rms(out) * fw_ref[...]
        o_ref[rows, :] = out

    @pl.when(i < FFN_PIECES)
    def _first_rows_while_loading():
        wg_s[i] = wg_ref[...].astype(BF16)
        wu_s[i] = wu_ref[...].astype(BF16)
        wd_s[i] = wd_ref[...].astype(BF16)

        @pl.when(i == 0)
        def _start():
            xn_s[...] = normalized(x_ref[...])
            acc_s[...] = jnp.zeros_like(acc_s)

        acc_s[...] += _dot(gated(xn_s[...], wg_s[i], wu_s[i]), wd_s[i])

        @pl.when(i == FFN_PIECES - 1)
        def _end():
            finish(slice(None), acc_s[...])

    @pl.when(i >= FFN_PIECES)
    def _token_rows():
        xn = normalized(x_ref[...])
        for c in range(FFN_PIECES):
            h_ref[:, c * FFN_CHUNK:(c + 1) * FFN_CHUNK] = gated(xn, wg_s[c], wu_s[c])
        wd = wd_s[...].reshape(D_FF, D_MODEL)
        block = FFN_TOKENS // 4 if final_norm else FFN_TOKENS
        for r in range(0, FFN_TOKENS, block):
            rows = slice(r, r + block)
            finish(rows, _dot(h_ref[rows, :], wd))


def _ffn(x2d, layer, norm_w, w_gate, w_up, w_down, final_w, *, final_norm):
    tokens = x2d.shape[0]
    last = FFN_PIECES - 1
    rows = lambda i: (jnp.maximum(i - last, 0), 0)
    return pl.pallas_call(
        functools.partial(_ffn_kernel, layer=layer, final_norm=final_norm),
        out_shape=jax.ShapeDtypeStruct(x2d.shape, F32),
        grid=(last + tokens // FFN_TOKENS,),
        in_specs=[
            pl.BlockSpec((FFN_TOKENS, D_MODEL), rows),
            pl.BlockSpec(norm_w.shape, lambda i: (0, 0)),
            pl.BlockSpec((None, D_MODEL, FFN_CHUNK), lambda i: (layer, 0, jnp.minimum(i, last))),
            pl.BlockSpec((None, D_MODEL, FFN_CHUNK), lambda i: (layer, 0, jnp.minimum(i, last))),
            pl.BlockSpec((None, FFN_CHUNK, D_MODEL), lambda i: (layer, jnp.minimum(i, last), 0)),
            pl.BlockSpec((1, D_MODEL), lambda i: (0, 0)),
        ],
        out_specs=pl.BlockSpec((FFN_TOKENS, D_MODEL), rows),
        scratch_shapes=[
            pltpu.VMEM((FFN_PIECES, D_MODEL, FFN_CHUNK), BF16),
            pltpu.VMEM((FFN_PIECES, D_MODEL, FFN_CHUNK), BF16),
            pltpu.VMEM((FFN_PIECES, FFN_CHUNK, D_MODEL), BF16),
            pltpu.VMEM((FFN_TOKENS, D_FF), BF16),
            pltpu.VMEM((FFN_TOKENS, D_MODEL), BF16),
            pltpu.VMEM((FFN_TOKENS, D_MODEL), F32),
        ],
        compiler_params=pltpu.CompilerParams(
            dimension_semantics=("arbitrary",), vmem_limit_bytes=VMEM_LIMIT_FFN),
        name="ffn_halfstep",
    )(x2d, norm_w, w_gate, w_up, w_down, final_w)


def _mix_chunk(proj_ref, dt_ref, ycat_ref, keep, bias_idx, p):
    row = lax.broadcasted_iota(jnp.int32, (CHUNK, CHUNK), 0)
    lane = lax.broadcasted_iota(jnp.int32, (CHUNK, CHUNK), 1)
    causal = row >= lane
    row_b = row.astype(F32).astype(BF16)
    lane_b = lane.astype(F32).astype(BF16)
    causal_b = row_b >= lane_b
    upper_half_b = lane_b >= HALF_LANES
    zero_b = jnp.zeros((CHUNK, CHUNK), BF16)
    group_b = (lax.broadcasted_iota(jnp.int32, (CHUNK, GROUP_WIDTH), 1) // SSD_HEAD_DIM).astype(F32).astype(BF16)

    def block_diag(x_b):
        zero = jnp.zeros_like(x_b)
        return jnp.concatenate([jnp.where(group_b == i, x_b, zero) for i in range(HEADS_PER_GROUP)], axis=0)

    xbc_parts = []
    for c0 in range(0, SSD_XBC, CONV_COLS):
        cs = slice(c0, c0 + CONV_COLS)
        if keep is not None:
            p.ext[0:SUBLANES, cs] = p.ext[0:SUBLANES, cs] * keep
        raw = proj_ref[:, P_XBC + c0:P_XBC + c0 + CONV_COLS]
        p.ext[SUBLANES:, cs] = raw
        ext = p.ext[:, cs]
        w = p.convw[:, cs]
        ext1 = pltpu.roll(ext, 1, 0)
        older = w[1:2, :] * ext + w[0:1, :] * ext1
        conv = (p.convb[:, cs] + w[3:4, :] * raw + w[2:3, :] * ext1[SUBLANES:, :]
                + pltpu.roll(older, 2, 0)[SUBLANES:, :])
        p.ext[0:SUBLANES, cs] = raw[CHUNK - SUBLANES:, :]
        xbc_parts.append(conv * _sigmoid(conv))
        yield P_XBC + c0, P_XBC + c0 + CONV_COLS
    xbc = jnp.concatenate(xbc_parts, axis=-1)
    xs = xbc[:, :SSD_INNER]
    bm = xbc[:, SSD_INNER:SSD_INNER + SSD_GROUPS * SSD_STATE]
    cm = xbc[:, SSD_INNER + SSD_GROUPS * SSD_STATE:]
    x_b = xs.astype(BF16)
    bm_b = bm.astype(BF16)
    cm_b = cm.astype(BF16)

    dtr = dt_ref[0:SSD_HEADS, :] + p.dtb[...]
    dt = jnp.maximum(dtr, 0.0) + jnp.log1p(jnp.exp(-jnp.abs(dtr)))
    acs = _lane_cumsum(dt * (-LOG2E * jnp.exp(p.alog[...])), p.triu[...])
    grow = jnp.exp2(acs)
    to_end = dt * jnp.exp2(acs[:, CHUNK - 1:CHUNK] - acs)
    src = acs - jnp.log2(dt)
    stack = jnp.concatenate(
        [acs, *_split3(grow), *_split3(to_end), jnp.zeros((CHUNK - 7 * SSD_HEADS, LANES), F32)], axis=0)
    cols = stack.T
    expanded = _dot(cols.astype(BF16), p.expand[...])
    grow_e = expanded[:, :SSD_INNER]
    to_end_e = expanded[:, SSD_INNER:]
    xd_b = (xs * to_end_e).astype(BF16)
    yield P_DT, P_TOTAL

    y_diag = []
    for g in range(SSD_GROUPS):
        n0 = g * SSD_STATE
        cb = _dot_nt(cm_b[:, n0:n0 + SSD_STATE], bm_b[:, n0:n0 + SSD_STATE])
        parts = []
        for hh in range(HEADS_PER_GROUP):
            h = g * HEADS_PER_GROUP + hh
            seg = cols[:, h:h + 1] - src[h:h + 1, :]
            parts.append((cb * jnp.exp2(jnp.where(causal, seg, -jnp.inf))).astype(BF16))
        w0 = g * GROUP_WIDTH
        y_diag.append(_dot(jnp.concatenate(parts, axis=-1), block_diag(x_b[:, w0:w0 + GROUP_WIDTH])))
        yield
    y_off = []
    for g in range(SSD_GROUPS):
        n0 = g * SSD_STATE
        w0 = g * GROUP_WIDTH
        prev = p.st[g]
        if keep is not None:
            prev = prev * keep
        y_off.append(_dot(cm_b[:, n0:n0 + SSD_STATE], prev.astype(BF16)))
        bt = bm[:, n0:n0 + SSD_STATE].T.astype(BF16)
        p.st[g] = (prev * grow_e[CHUNK - 1:CHUNK, w0:w0 + GROUP_WIDTH]
                   + _dot(bt, xd_b[:, w0:w0 + GROUP_WIDTH]))
    yield
    gated, sumsq = [], 0.0
    for g in range(SSD_GROUPS):
        ws = slice(g * GROUP_WIDTH, (g + 1) * GROUP_WIDTH)
        y = y_diag[g] + y_off[g] * grow_e[:, ws] + xs[:, ws] * p.dskip[:, ws]
        z = proj_ref[:, P_Z + g * GROUP_WIDTH:P_Z + (g + 1) * GROUP_WIDTH]
        gated.append(y * (z * _sigmoid(z)))
        sumsq = sumsq + jnp.sum(gated[g] * gated[g], axis=-1, keepdims=True)
        yield P_Z + g * GROUP_WIDTH, P_Z + (g + 1) * GROUP_WIDTH
    inv_rms = lax.rsqrt(sumsq * (1.0 / SSD_INNER) + EPS)
    for g in range(SSD_GROUPS):
        ycat_ref[:, g * GROUP_WIDTH:(g + 1) * GROUP_WIDTH] = (gated[g] * inv_rms).astype(BF16)

    q_b = (proj_ref[:, P_Q:P_Q + ATT_WIDTH] * (LOG2E / math.sqrt(ATT_HEAD_DIM))).astype(BF16)
    k_b = proj_ref[:, P_K:P_K + ATT_KV_WIDTH].astype(BF16)
    v_b = proj_ref[:, P_V:P_V + ATT_KV_WIDTH].astype(BF16)
    keys = jnp.concatenate([p.kprev[...], k_b], axis=0)
    v_prev = p.vprev[...]
    halves = lambda t: (jnp.where(upper_half_b, zero_b, t), jnp.where(upper_half_b, t, zero_b))
    v_prev_half, v_half = halves(v_prev), halves(v_b)
    p.kprev[...] = k_b
    p.vprev[...] = v_b
    probs, values = [], []
    for pos, hq in enumerate(ATT_POS_HEADS):
        tile, half = divmod(pos, 2)
        q_tile = q_b[:, tile * LANES:(tile + 1) * LANES]
        q_head = halves(q_tile)[half]
        s = _dot_nt(q_head, keys)
        s = jnp.where(causal, s[:, CHUNK:], s[:, :CHUNK]) + p.bias[bias_idx, pos]
        sink = p.sink(hq) * LOG2E
        m = jnp.maximum(jnp.max(s, axis=-1, keepdims=True), sink)
        e = jnp.exp2(s - m)
        denom = jnp.sum(e, axis=-1, keepdims=True) + jnp.exp2(sink - m)
        pn = (e * (1.0 / denom)).astype(BF16)
        probs.append(jnp.where(causal_b, zero_b, pn))
        probs.append(jnp.where(causal_b, pn, zero_b))
        for vh in (v_prev_half[half], v_half[half]):
            values.append(jnp.concatenate([zero_b, vh] if tile else [vh, zero_b], axis=-1))
        yield (P_Q, P_U) if pos == 0 else None
    att = _dot(jnp.concatenate(probs, axis=-1), jnp.concatenate(values, axis=0))
    ycat_ref[:, Y_ATT:Y_SGU] = _unit_rms(att).astype(BF16)
    yield

    u = _gelu_tanh(proj_ref[:, P_U:P_U + GM_WIDTH])
    yield P_U, P_GV
    gv = _gelu_tanh(proj_ref[:, P_GV:P_GV + GM_WIDTH])
    yield P_GV, P_DT
    mu = jnp.mean(gv, axis=-1, keepdims=True)
    gc = gv - mu
    gv = gc * lax.rsqrt(jnp.mean(gc * gc, axis=-1, keepdims=True) + EPS) * p.lnw[...] + p.lnb[...]
    gv_b = gv.astype(BF16)
    mixed = _dot(p.wcat[...], block_diag(gv_b)) + p.sgub[...]
    ycat_ref[:, Y_SGU:] = _unit_rms(u * mixed).astype(BF16)
    yield


class _MixerRefs:
    def __init__(self, **refs):
        self.__dict__.update(refs)


def _mixer_kernel(sink_ref, rel_ref,
                  xa_ref, xc_ref, ing_ref, win_ref, convw_ref, convb_ref, dtb_ref, alog_ref,
                  dskip_ref, triu_ref, expand_ref, bucket_ref,
                  lnw_ref, lnb_ref, sguw_ref, sgub_ref, outg_ref, wout_ref,
                  o_ref,
                  win_s, wdt_s, wout_s, projx_ref, projy_ref, dtx_ref, dty_ref, ycatx_ref, ycaty_ref,
                  xnx_ref, xny_ref,
                  ext_ref, st_ref, kprev_ref, vprev_ref, bias_ref, wcat_ref,
                  *, layer, chunks_per_seq):
    t = pl.program_id(0)
    p = _MixerRefs(sink=lambda hq: sink_ref[layer, hq], convw=convw_ref, convb=convb_ref, dtb=dtb_ref, alog=alog_ref,
                   dskip=dskip_ref, triu=triu_ref, expand=expand_ref, lnw=lnw_ref, lnb=lnb_ref,
                   sgub=sgub_ref, ext=ext_ref, st=st_ref, kprev=kprev_ref, vprev=vprev_ref,
                   bias=bias_ref, wcat=wcat_ref)

    @pl.when(t == 0)
    def _init():
        row = lax.broadcasted_iota(jnp.int32, (CHUNK, CHUNK), 0)
        col = lax.broadcasted_iota(jnp.int32, (CHUNK, CHUNK), 1)
        bucket = bucket_ref[...]
        for pos, hq in enumerate(ATT_POS_HEADS):
            acc = jnp.zeros((CHUNK, CHUNK), F32)
            for k in range(REL_BUCKETS):
                acc = jnp.where(bucket == k, rel_ref[k, hq] * LOG2E, acc)
            bias_ref[0, pos] = acc
            bias_ref[1, pos] = jnp.where(row >= col, acc, MASKED)
        for g in range(GM_GROUPS):
            wcat_ref[:, g * CHUNK:(g + 1) * CHUNK] = jnp.where(row >= col, sguw_ref[g], 0.0).astype(BF16)

        def source_rows(b):
            lo = b * LANES
            if lo < P_Q:
                return [(lo, LANES)]
            if lo < P_K:
                pos = (lo - P_Q) // ATT_HEAD_DIM
                return [(R_Q + ATT_POS_HEADS[pos + i] * ATT_HEAD_DIM, ATT_HEAD_DIM)
                        for i in range(LANES // ATT_HEAD_DIM)]
            return [(lo - P_K + R_K, LANES)]

        def pack_rows(i, carry):
            r = pl.ds(pl.multiple_of(i * CHUNK, CHUNK), CHUNK)
            gain = ing_ref[:, r]
            for b in range(P_DT // LANES):
                tile = jnp.concatenate([win_ref[r0:r0 + n, r] for r0, n in source_rows(b)], axis=0)
                win_s[r, b * LANES:(b + 1) * LANES] = (tile * gain).T.astype(BF16)
            return carry

        lax.fori_loop(0, D_MODEL // CHUNK, pack_rows, 0)
        wdt_s[...] = jnp.concatenate(
            [win_ref[R_DT:R_DT + SSD_HEADS, :] * ing_ref[...],
             jnp.zeros((DT_ROWS - SSD_HEADS, D_MODEL), F32)], axis=0).astype(BF16)
        blocks = D_MODEL // ATT_HEAD_DIM
        first_att = Y_ATT // ATT_HEAD_DIM
        for j in range(blocks):
            src = first_att + ATT_POS_HEADS[j - first_att] if first_att <= j < first_att + ATT_Q_HEADS else j
            dst_rows = slice(j * ATT_HEAD_DIM, (j + 1) * ATT_HEAD_DIM)
            src_rows = slice(src * ATT_HEAD_DIM, (src + 1) * ATT_HEAD_DIM)
            wout_s[dst_rows, :] = (wout_ref[src_rows, :] * outg_ref[dst_rows, :]).astype(BF16)
        projy_ref[...] = jnp.zeros_like(projy_ref)
        dty_ref[...] = jnp.zeros_like(dty_ref)
        ycatx_ref[...] = jnp.zeros_like(ycatx_ref)
        st_ref[...] = jnp.zeros_like(st_ref)
        ext_ref[0:SUBLANES, :] = jnp.zeros((SUBLANES, SSD_XBC), F32)
        kprev_ref[...] = jnp.zeros_like(kprev_ref)
        vprev_ref[...] = jnp.zeros_like(vprev_ref)

    starts_seq = lax.rem(4 * t, chunks_per_seq) == 0
    keep = jnp.where(starts_seq, 0.0, 1.0).astype(F32)
    first_idx = jnp.where(starts_seq, 1, 0).astype(jnp.int32)
    even, odd = pl.ds(0, CHUNK), pl.ds(CHUNK, CHUNK)

    def projection_pieces(rows, proj_new, dt_new, ycat_out, xn):
        def normalize():
            xn[...] = _unit_rms(xa_ref[rows, :]).astype(BF16)

        def project(c0, c1):
            proj_new[:, c0:c1] = _dot(xn[...], win_s[:, c0:c1])

        def project_gv_dt():
            project(P_GV, P_DT)
            dt_t = _dot_nt(wdt_s[...], xn[...])
            dt_new[0] = dt_t[:, :CHUNK]
            dt_new[1] = dt_t[:, CHUNK:]

        def output(c0, c1):
            o_ref[rows, c0:c1] = xc_ref[rows, c0:c1] + _dot(ycat_out[...], wout_s[:, c0:c1])

        in_piece = lambda lo, hi: ("in", lo, hi, functools.partial(project, lo, hi))
        out_piece = lambda i: ("out", i * PROJ_PIECE, (i + 1) * PROJ_PIECE,
                               functools.partial(output, i * PROJ_PIECE, (i + 1) * PROJ_PIECE))
        xbc = [in_piece(P_XBC + c, P_XBC + c + PROJ_PIECE) for c in range(0, SSD_XBC, PROJ_PIECE)]
        z = [in_piece(P_Z + c, P_Z + c + PROJ_PIECE) for c in range(0, SSD_INNER, PROJ_PIECE)]
        return ([("norm", 0, 0, normalize)] + z + [out_piece(0)] + xbc[:2] + [out_piece(1)] + xbc[2:]
                + [out_piece(2), in_piece(P_Q, P_K), in_piece(P_K, P_U), out_piece(3),
                   in_piece(P_U, P_GV), ("in", P_GV, P_TOTAL, project_gv_dt)])

    def step_program(live_mixes, piece_kinds):
        mixes = [
            _mix_chunk(projy_ref.at[even], dty_ref.at[0], ycaty_ref.at[even], None, 0, p),
            _mix_chunk(projy_ref.at[odd], dty_ref.at[1], ycaty_ref.at[odd], None, 0, p),
            _mix_chunk(projx_ref.at[even], dtx_ref.at[0], ycatx_ref.at[even], keep, first_idx, p),
            _mix_chunk(projx_ref.at[odd], dtx_ref.at[1], ycatx_ref.at[odd], None, 0, p),
        ]
        pieces = [projection_pieces(pl.ds(0, PAIR), projx_ref, dtx_ref, ycatx_ref, xnx_ref),
                  projection_pieces(pl.ds(PAIR, PAIR), projy_ref, dty_ref, ycaty_ref, xny_ref)]
        pieces = [[piece for piece in phase if piece[0] in piece_kinds] for phase in pieces]
        overlaps = lambda reads, piece: reads is not None and reads[0] < piece[2] and piece[1] < reads[1]
        done = [0 if k in live_mixes else MIX_SECTIONS for k in range(len(mixes))]
        issued = [0, 0]

        def mix_may_run(k):
            section = done[k]
            if section == MIX_SECTIONS:
                return False
            if k > 0 and done[k - 1] < min(section + MIX_LAG, MIX_SECTIONS):
                return False
            if k >= 2 and issued[0] < len(pieces[0]):
                return False
            return True

        def piece_may_issue(phase):
            if issued[phase] == len(pieces[phase]):
                return False
            piece = pieces[phase][issued[phase]]
            if phase == 1:
                if piece[0] == "in":
                    return all(done[k] > s for k in (0, 1) for s, r in MIX_READS.items() if overlaps(r, piece))
                if piece[0] == "out":
                    return done[0] == done[1] == MIX_SECTIONS
            return True

        while min(done) < MIX_SECTIONS or issued != [len(pieces[0]), len(pieces[1])]:
            progressed = False
            for k, mix in enumerate(mixes):
                if mix_may_run(k):
                    assert next(mix) == MIX_READS.get(done[k]), "MIX_READS is stale"
                    done[k] += 1
                    progressed = True
            for phase in (0, 1):
                if piece_may_issue(phase):
                    pieces[phase][issued[phase]][3]()
                    issued[phase] += 1
                    progressed = True
                    break
            assert progressed, "mixer step schedule is stuck"
        for k in live_mixes:
            assert next(mixes[k], None) is None and mixes[k].gi_frame is None, "MIX_SECTIONS is stale"

    last_step = pl.num_programs(0) - 1

    @pl.when(t == 0)
    def _fill():
        step_program((2, 3), ("norm", "in"))

    @pl.when(jnp.logical_and(t > 0, t < last_step))
    def _steady():
        step_program((0, 1, 2, 3), ("norm", "in", "out"))

    @pl.when(t == last_step)
    def _drain():
        step_program((0, 1), ("out",))


def _folded_bucket_tile():
    i = np.arange(CHUNK)[:, None]
    j = np.arange(CHUNK)[None, :]
    n = np.where(j <= i, i - j, i - j + CHUNK)
    max_exact = REL_BUCKETS // 2
    large = max_exact + (np.log(np.maximum(n, 1) / max_exact) / np.log(REL_MAX_DIST / max_exact)
                         * (REL_BUCKETS - max_exact)).astype(np.int32)
    large = np.minimum(large, REL_BUCKETS - 1)
    return np.where(n < max_exact, n, large).astype(np.int32)


def _mixer(x2d, seq, layer, in_gain, w_in_t, conv_w, conv_b, dt_bias, a_log, d_skip, sinks, rel_bias,
           ln_w, ln_b, sgu_w, sgu_b, out_gain, w_out):
    tokens = x2d.shape[0]
    n_quads = tokens // QUAD
    chunks_per_seq = seq // CHUNK
    assert tokens % QUAD == 0 and seq % QUAD == 0 and GM_GROUPS == HEADS_PER_GROUP
    assert P_Q == R_DT and w_in_t.shape[1] == R_K + P_DT - P_K
    per_head_rows = lambda v: jnp.broadcast_to(v[:, None], (SSD_HEADS, LANES))
    row = lambda v: v.reshape(1, -1)
    expand = np.zeros((CHUNK, 2 * SSD_INNER), np.float32)
    for r in range(SSD_HEADS, 7 * SSD_HEADS):
        h = r % SSD_HEADS
        c0 = ((r // SSD_HEADS - 1) // 3) * SSD_INNER + h * SSD_HEAD_DIM
        expand[r, c0:c0 + SSD_HEAD_DIM] = 1.0
    expand = jnp.asarray(expand, BF16)
    triu = jnp.asarray(np.triu(np.ones((CHUNK, CHUNK), np.float32)), BF16)
    bucket = jnp.asarray(_folded_bucket_tile())
    sgu_b_e = jnp.repeat(jnp.transpose(sgu_b), GM_GROUP_DIM, axis=1)

    const2 = lambda t, *_: (0, 0)
    const3 = lambda t, *_: (0, 0, 0)
    full2 = lambda shape: pl.BlockSpec(shape, const2)
    layer_block = lambda shape: pl.BlockSpec((None,) + shape, lambda t, *_: (layer, 0, 0),
                                             pipeline_mode=pl.Buffered(1))
    projected = lambda t, *_: (jnp.minimum(t, n_quads - 1), 0)
    finished = lambda t, *_: (jnp.maximum(t - 1, 0), 0)
    grid_spec = pltpu.PrefetchScalarGridSpec(
        num_scalar_prefetch=2,
        grid=(n_quads + 1,),
        in_specs=[
            pl.BlockSpec((QUAD, D_MODEL), projected),
            pl.BlockSpec((QUAD, D_MODEL), finished),
            layer_block((1, D_MODEL)),
            layer_block((w_in_t.shape[1], D_MODEL)),
            full2((SSD_CONV, SSD_XBC)),
            full2((1, SSD_XBC)),
            full2((SSD_HEADS, LANES)),
            full2((SSD_HEADS, LANES)),
            full2((1, SSD_INNER)),
            full2((CHUNK, CHUNK)),
            full2((CHUNK, 2 * SSD_INNER)),
            full2((CHUNK, CHUNK)),
            full2((1, GM_WIDTH)),
            full2((1, GM_WIDTH)),
            pl.BlockSpec((GM_GROUPS, CHUNK, CHUNK), const3),
            full2((CHUNK, GM_WIDTH)),
            layer_block((D_MODEL, 1)),
            layer_block((D_MODEL, D_MODEL)),
        ],
        out_specs=pl.BlockSpec((QUAD, D_MODEL), finished),
        scratch_shapes=[
            pltpu.VMEM((D_MODEL, P_DT), BF16),
            pltpu.VMEM((DT_ROWS, D_MODEL), BF16),
            pltpu.VMEM((D_MODEL, D_MODEL), BF16),
            pltpu.VMEM((PAIR, P_DT), F32),
            pltpu.VMEM((PAIR, P_DT), F32),
            pltpu.VMEM((2, DT_ROWS, CHUNK), F32),
            pltpu.VMEM((2, DT_ROWS, CHUNK), F32),
            pltpu.VMEM((PAIR, D_MODEL), BF16),
            pltpu.VMEM((PAIR, D_MODEL), BF16),
            pltpu.VMEM((PAIR, D_MODEL), BF16),
            pltpu.VMEM((PAIR, D_MODEL), BF16),
            pltpu.VMEM((SUBLANES + CHUNK, SSD_XBC), F32),
            pltpu.VMEM((SSD_GROUPS, SSD_STATE, GROUP_WIDTH), F32),
            pltpu.VMEM((CHUNK, ATT_KV_WIDTH), BF16),
            pltpu.VMEM((CHUNK, ATT_KV_WIDTH), BF16),
            pltpu.VMEM((2, ATT_Q_HEADS, CHUNK, CHUNK), F32),
            pltpu.VMEM((CHUNK, GM_GROUPS * CHUNK), BF16),
        ],
    )
    return pl.pallas_call(
        functools.partial(_mixer_kernel, layer=layer, chunks_per_seq=chunks_per_seq),
        out_shape=jax.ShapeDtypeStruct(x2d.shape, F32),
        grid_spec=grid_spec,
        compiler_params=pltpu.CompilerParams(
            dimension_semantics=("arbitrary",), vmem_limit_bytes=VMEM_LIMIT_MIXER),
        name="token_mixer",
    )(sinks.astype(F32), rel_bias.astype(F32),
      x2d, x2d, in_gain, w_in_t, conv_w, row(conv_b), per_head_rows(dt_bias), per_head_rows(a_log),
      row(jnp.repeat(d_skip, SSD_HEAD_DIM)), triu, expand, bucket,
      row(ln_w), row(ln_b), sgu_w, sgu_b_e, out_gain, w_out)


def kernel(x, ffn1_norm, ffn1_w_gate, ffn1_w_up, ffn1_w_down, mix_norm, w_in, conv_w, conv_b, dt_bias, a_log, d_skip, ssd_norm, attn_sinks, rel_bias, attn_out_norm, sgu_ln_w, sgu_ln_b, sgu_w, sgu_b, sgu_out_norm, w_out, ffn2_norm, ffn2_w_gate, ffn2_w_up, ffn2_w_down, final_norm):
    batch, seq, d = x.shape
    depth = w_in.shape[0]
    fw = final_norm.reshape(1, d)
    x2 = x.reshape(batch * seq, d)
    mix_in_gain = mix_norm.reshape(depth, 1, d)
    w_in_t = jnp.swapaxes(w_in, 1, 2)
    att_cols = np.concatenate([np.arange(h * ATT_HEAD_DIM, (h + 1) * ATT_HEAD_DIM) for h in ATT_POS_HEADS])
    mix_out_gain = jnp.concatenate([ssd_norm, attn_out_norm[:, att_cols], sgu_out_norm], axis=1).reshape(depth, d, 1)
    for l in range(depth):
        x2 = _ffn(x2, l, ffn1_norm, ffn1_w_gate, ffn1_w_up, ffn1_w_down, fw, final_norm=False)
        x2 = _mixer(x2, seq, l, mix_in_gain, w_in_t, conv_w[l], conv_b[l], dt_bias[l],
                    a_log[l], d_skip[l], attn_sinks, rel_bias,
                    sgu_ln_w[l], sgu_ln_b[l], sgu_w[l], sgu_b[l], mix_out_gain, w_out)
        x2 = _ffn(x2, l, ffn2_norm, ffn2_w_gate, ffn2_w_up, ffn2_w_down, fw,
                  final_norm=(l == depth - 1))
    return x2.reshape(batch, seq, d)
```

```python
import functools
import math

import jax
import jax.numpy as jnp
import numpy as np
from jax import lax
from jax.experimental import pallas as pl
from jax.experimental.pallas import tpu as pltpu

F32 = jnp.float32
BF16 = jnp.bfloat16

D_MODEL = 1024
D_FF = 2816
EPS = 1e-6

SSD_HEADS = 8
SSD_HEAD_DIM = 64
SSD_INNER = SSD_HEADS * SSD_HEAD_DIM
SSD_GROUPS = 2
SSD_STATE = 128
SSD_CONV = 4
SSD_XBC = SSD_INNER + 2 * SSD_GROUPS * SSD_STATE
HEADS_PER_GROUP = SSD_HEADS // SSD_GROUPS
GROUP_WIDTH = HEADS_PER_GROUP * SSD_HEAD_DIM

ATT_Q_HEADS = 4
ATT_KV_HEADS = 2
ATT_HEAD_DIM = 64
ATT_WIDTH = ATT_Q_HEADS * ATT_HEAD_DIM
ATT_KV_WIDTH = ATT_KV_HEADS * ATT_HEAD_DIM
REL_BUCKETS = 32
REL_MAX_DIST = 128
ATT_POS_HEADS = (0, 2, 1, 3)

GM_GROUPS = 4
GM_GROUP_DIM = 64
GM_WIDTH = GM_GROUPS * GM_GROUP_DIM

CHUNK = 128
PAIR = 2 * CHUNK
QUAD = 2 * PAIR
PROJ_PIECE = 256
CONV_COLS = 256
MIX_SECTIONS = 18
MIX_LAG = 4

SUBLANES = 8
LANES = 128
HALF_LANES = LANES // 2
DT_PAD = LANES
DT_ROWS = 16

P_Z = 0
P_XBC = P_Z + SSD_INNER
P_Q = P_XBC + SSD_XBC
P_K = P_Q + ATT_WIDTH
P_V = P_K + ATT_KV_WIDTH
P_U = P_V + ATT_KV_WIDTH
P_GV = P_U + GM_WIDTH
P_DT = P_GV + GM_WIDTH
P_TOTAL = P_DT + DT_PAD

R_XBC = SSD_INNER
R_DT = R_XBC + SSD_XBC
R_Q = R_DT + SSD_HEADS
R_K = R_Q + ATT_WIDTH

Y_ATT = SSD_INNER
Y_SGU = Y_ATT + ATT_WIDTH

MIX_READS = {
    **{i: (P_XBC + i * CONV_COLS, P_XBC + (i + 1) * CONV_COLS) for i in range(SSD_XBC // CONV_COLS)},
    4: (P_DT, P_TOTAL),
    8: (P_Z, P_Z + GROUP_WIDTH),
    9: (P_Z + GROUP_WIDTH, P_Z + 2 * GROUP_WIDTH),
    10: (P_Q, P_U),
    15: (P_U, P_GV),
    16: (P_GV, P_DT),
}

MASKED = -1e30
LOG2E = math.log2(math.e)

FFN_TOKENS = 1024
FFN_CHUNK = 256
FFN_PIECES = D_FF // FFN_CHUNK
VMEM_LIMIT_FFN = 56 * 1024 * 1024
VMEM_LIMIT_MIXER = 48 * 1024 * 1024


def _unit_rms(x):
    return x * lax.rsqrt(jnp.mean(x * x, axis=-1, keepdims=True) + EPS)


def _sigmoid(x):
    return 1.0 / (1.0 + jnp.exp(-x))


def _gelu_tanh(x):
    c = math.sqrt(2.0 / math.pi)
    return 0.5 * x * (1.0 + jnp.tanh(c * (x + 0.044715 * (x * x * x))))


def _top_bits(x):
    bits = lax.bitcast_convert_type(x, jnp.uint32) & jnp.uint32(0xFFFF0000)
    return lax.bitcast_convert_type(bits, F32)


def _split3(x):
    hi = _top_bits(x)
    r = x - hi
    mid = _top_bits(r)
    return hi, mid, r - mid


def _dot(a, b):
    return jnp.dot(a, b, preferred_element_type=F32)


def _dot_nt(a, b):
    return lax.dot_general(a, b, (((1,), (1,)), ((), ())), preferred_element_type=F32)


def _lane_cumsum(x, upper_ones):
    rows = x.shape[0]
    parts = _dot(jnp.concatenate(_split3(x), axis=0).astype(BF16), upper_ones)
    return parts[0:rows] + parts[rows:2 * rows] + parts[2 * rows:3 * rows]


def _ffn_kernel(x_ref, nw_ref, wg_ref, wu_ref, wd_ref, fw_ref, o_ref,
                wg_s, wu_s, wd_s, h_ref, xn_s, acc_s, *, layer, final_norm):
    i = pl.program_id(0)

    def normalized(x):
        return (_unit_rms(x) * nw_ref[layer:layer + 1, :]).astype(BF16)

    def gated(xn, wg, wu):
        g = _dot(xn, wg)
        u = _dot(xn, wu)
        return (g * _sigmoid(g) * u).astype(BF16)

    def finish(rows, y):
        out = x_ref[rows, :] + 0.5 * y
        if final_norm:
            out = _unit_rms(out) * fw_ref[...]
        o_ref[rows, :] = out

    @pl.when(i < FFN_PIECES)
    def _first_rows_while_loading():
        wg_s[i] = wg_ref[...].astype(BF16)
        wu_s[i] = wu_ref[...].astype(BF16)
        wd_s[i] = wd_ref[...].astype(BF16)

        @pl.when(i == 0)
        def _start():
            xn_s[...] = normalized(x_ref[...])
            acc_s[...] = jnp.zeros_like(acc_s)

        acc_s[...] += _dot(gated(xn_s[...], wg_s[i], wu_s[i]), wd_s[i])

        @pl.when(i == FFN_PIECES - 1)
        def _end():
            finish(slice(None), acc_s[...])

    @pl.when(i >= FFN_PIECES)
    def _token_rows():
        xn = normalized(x_ref[...])
        for c in range(FFN_PIECES):
            h_ref[:, c * FFN_CHUNK:(c + 1) * FFN_CHUNK] = gated(xn, wg_s[c], wu_s[c])
        wd = wd_s[...].reshape(D_FF, D_MODEL)
        block = FFN_TOKENS // 4 if final_norm else FFN_TOKENS
        for r in range(0, FFN_TOKENS, block):
            rows = slice(r, r + block)
            finish(rows, _dot(h_ref[rows, :], wd))


def _ffn(x2d, layer, norm_w, w_gate, w_up, w_down, final_w, *, final_norm):
    tokens = x2d.shape[0]
    last = FFN_PIECES - 1
    rows = lambda i: (jnp.maximum(i - last, 0), 0)
    return pl.pallas_call(
        functools.partial(_ffn_kernel, layer=layer, final_norm=final_norm),
        out_shape=jax.ShapeDtypeStruct(x2d.shape, F32),
        grid=(last + tokens // FFN_TOKENS,),
        in_specs=[
            pl.BlockSpec((FFN_TOKENS, D_MODEL), rows),
            pl.BlockSpec(norm_w.shape, lambda i: (0, 0)),
            pl.BlockSpec((None, D_MODEL, FFN_CHUNK), lambda i: (layer, 0, jnp.minimum(i, last))),
            pl.BlockSpec((None, D_MODEL, FFN_CHUNK), lambda i: (layer, 0, jnp.minimum(i, last))),
            pl.BlockSpec((None, FFN_CHUNK, D_MODEL), lambda i: (layer, jnp.minimum(i, last), 0)),
            pl.BlockSpec((1, D_MODEL), lambda i: (0, 0)),
        ],
        out_specs=pl.BlockSpec((FFN_TOKENS, D_MODEL), rows),
        scratch_shapes=[
            pltpu.VMEM((FFN_PIECES, D_MODEL, FFN_CHUNK), BF16),
            pltpu.VMEM((FFN_PIECES, D_MODEL, FFN_CHUNK), BF16),
            pltpu.VMEM((FFN_PIECES, FFN_CHUNK, D_MODEL), BF16),
            pltpu.VMEM((FFN_TOKENS, D_FF), BF16),
            pltpu.VMEM((FFN_TOKENS, D_MODEL), BF16),
            pltpu.VMEM((FFN_TOKENS, D_MODEL), F32),
        ],
        compiler_params=pltpu.CompilerParams(
            dimension_semantics=("arbitrary",), vmem_limit_bytes=VMEM_LIMIT_FFN),
        name="ffn_halfstep",
    )(x2d, norm_w, w_gate, w_up, w_down, final_w)


def _mix_chunk(proj_ref, dt_ref, ycat_ref, keep, bias_idx, p):
    row = lax.broadcasted_iota(jnp.int32, (CHUNK, CHUNK), 0)
    lane = lax.broadcasted_iota(jnp.int32, (CHUNK, CHUNK), 1)
    causal = row >= lane
    row_b = row.astype(F32).astype(BF16)
    lane_b = lane.astype(F32).astype(BF16)
    causal_b = row_b >= lane_b
    upper_half_b = lane_b >= HALF_LANES
    zero_b = jnp.zeros((CHUNK, CHUNK), BF16)
    group_b = (lax.broadcasted_iota(jnp.int32, (CHUNK, GROUP_WIDTH), 1) // SSD_HEAD_DIM).astype(F32).astype(BF16)

    def block_diag(x_b):
        zero = jnp.zeros_like(x_b)
        return jnp.concatenate([jnp.where(group_b == i, x_b, zero) for i in range(HEADS_PER_GROUP)], axis=0)

    xbc_parts = []
    for c0 in range(0, SSD_XBC, CONV_COLS):
        cs = slice(c0, c0 + CONV_COLS)
        if keep is not None:
            p.ext[0:SUBLANES, cs] = p.ext[0:SUBLANES, cs] * keep
        raw = proj_ref[:, P_XBC + c0:P_XBC + c0 + CONV_COLS]
        p.ext[SUBLANES:, cs] = raw
        ext = p.ext[:, cs]
        w = p.convw[:, cs]
        ext1 = pltpu.roll(ext, 1, 0)
        older = w[1:2, :] * ext + w[0:1, :] * ext1
        conv = (p.convb[:, cs] + w[3:4, :] * raw + w[2:3, :] * ext1[SUBLANES:, :]
                + pltpu.roll(older, 2, 0)[SUBLANES:, :])
        p.ext[0:SUBLANES, cs] = raw[CHUNK - SUBLANES:, :]
        xbc_parts.append(conv * _sigmoid(conv))
        yield P_XBC + c0, P_XBC + c0 + CONV_COLS
    xbc = jnp.concatenate(xbc_parts, axis=-1)
    xs = xbc[:, :SSD_INNER]
    bm = xbc[:, SSD_INNER:SSD_INNER + SSD_GROUPS * SSD_STATE]
    cm = xbc[:, SSD_INNER + SSD_GROUPS * SSD_STATE:]
    x_b = xs.astype(BF16)
    bm_b = bm.astype(BF16)
    cm_b = cm.astype(BF16)

    dtr = dt_ref[0:SSD_HEADS, :] + p.dtb[...]
    dt = jnp.maximum(dtr, 0.0) + jnp.log1p(jnp.exp(-jnp.abs(dtr)))
    acs = _lane_cumsum(dt * (-LOG2E * jnp.exp(p.alog[...])), p.triu[...])
    grow = jnp.exp2(acs)
    to_end = dt * jnp.exp2(acs[:, CHUNK - 1:CHUNK] - acs)
    src = acs - jnp.log2(dt)
    stack = jnp.concatenate(
        [acs, *_split3(grow), *_split3(to_end), jnp.zeros((CHUNK - 7 * SSD_HEADS, LANES), F32)], axis=0)
    cols = stack.T
    expanded = _dot(cols.astype(BF16), p.expand[...])
    grow_e = expanded[:, :SSD_INNER]
    to_end_e = expanded[:, SSD_INNER:]
    xd_b = (xs * to_end_e).astype(BF16)
    yield P_DT, P_TOTAL

    y_diag = []
    for g in range(SSD_GROUPS):
        n0 = g * SSD_STATE
        cb = _dot_nt(cm_b[:, n0:n0 + SSD_STATE], bm_b[:, n0:n0 + SSD_STATE])
        parts = []
        for hh in range(HEADS_PER_GROUP):
            h = g * HEADS_PER_GROUP + hh
            seg = cols[:, h:h + 1] - src[h:h + 1, :]
            parts.append((cb * jnp.exp2(jnp.where(causal, seg, -jnp.inf))).astype(BF16))
        w0 = g * GROUP_WIDTH
        y_diag.append(_dot(jnp.concatenate(parts, axis=-1), block_diag(x_b[:, w0:w0 + GROUP_WIDTH])))
        yield
    y_off = []
    for g in range(SSD_GROUPS):
        n0 = g * SSD_STATE
        w0 = g * GROUP_WIDTH
        prev = p.st[g]
        if keep is not None:
            prev = prev * keep
        y_off.append(_dot(cm_b[:, n0:n0 + SSD_STATE], prev.astype(BF16)))
        bt = bm[:, n0:n0 + SSD_STATE].T.astype(BF16)
        p.st[g] = (prev * grow_e[CHUNK - 1:CHUNK, w0:w0 + GROUP_WIDTH]
                   + _dot(bt, xd_b[:, w0:w0 + GROUP_WIDTH]))
    yield
    gated, sumsq = [], 0.0
    for g in range(SSD_GROUPS):
        ws = slice(g * GROUP_WIDTH, (g + 1) * GROUP_WIDTH)
        y = y_diag[g] + y_off[g] * grow_e[:, ws] + xs[:, ws] * p.dskip[:, ws]
        z = proj_ref[:, P_Z + g * GROUP_WIDTH:P_Z + (g + 1) * GROUP_WIDTH]
        gated.append(y * (z * _sigmoid(z)))
        sumsq = sumsq + jnp.sum(gated[g] * gated[g], axis=-1, keepdims=True)
        yield P_Z + g * GROUP_WIDTH, P_Z + (g + 1) * GROUP_WIDTH
    inv_rms = lax.rsqrt(sumsq * (1.0 / SSD_INNER) + EPS)
    for g in range(SSD_GROUPS):
        ycat_ref[:, g * GROUP_WIDTH:(g + 1) * GROUP_WIDTH] = (gated[g] * inv_rms).astype(BF16)

    q_b = (proj_ref[:, P_Q:P_Q + ATT_WIDTH] * (LOG2E / math.sqrt(ATT_HEAD_DIM))).astype(BF16)
    k_b = proj_ref[:, P_K:P_K + ATT_KV_WIDTH].astype(BF16)
    v_b = proj_ref[:, P_V:P_V + ATT_KV_WIDTH].astype(BF16)
    keys = jnp.concatenate([p.kprev[...], k_b], axis=0)
    v_prev = p.vprev[...]
    halves = lambda t: (jnp.where(upper_half_b, zero_b, t), jnp.where(upper_half_b, t, zero_b))
    v_prev_half, v_half = halves(v_prev), halves(v_b)
    p.kprev[...] = k_b
    p.vprev[...] = v_b
    probs, values = [], []
    for pos, hq in enumerate(ATT_POS_HEADS):
        tile, half = divmod(pos, 2)
        q_tile = q_b[:, tile * LANES:(tile + 1) * LANES]
        q_head = halves(q_tile)[half]
        s = _dot_nt(q_head, keys)
        s = jnp.where(causal, s[:, CHUNK:], s[:, :CHUNK]) + p.bias[bias_idx, pos]
        sink = p.sink(hq) * LOG2E
        m = jnp.maximum(jnp.max(s, axis=-1, keepdims=True), sink)
        e = jnp.exp2(s - m)
        denom = jnp.sum(e, axis=-1, keepdims=True) + jnp.exp2(sink - m)
        pn = (e * (1.0 / denom)).astype(BF16)
        probs.append(jnp.where(causal_b, zero_b, pn))
        probs.append(jnp.where(causal_b, pn, zero_b))
        for vh in (v_prev_half[half], v_half[half]):
            values.append(jnp.concatenate([zero_b, vh] if tile else [vh, zero_b], axis=-1))
        yield (P_Q, P_U) if pos == 0 else None
    att = _dot(jnp.concatenate(probs, axis=-1), jnp.concatenate(values, axis=0))
    ycat_ref[:, Y_ATT:Y_SGU] = _unit_rms(att).astype(BF16)
    yield

    u = _gelu_tanh(proj_ref[:, P_U:P_U + GM_WIDTH])
    yield P_U, P_GV
    gv = _gelu_tanh(proj_ref[:, P_GV:P_GV + GM_WIDTH])
    yield P_GV, P_DT
    mu = jnp.mean(gv, axis=-1, keepdims=True)
    gc = gv - mu
    gv = gc * lax.rsqrt(jnp.mean(gc * gc, axis=-1, keepdims=True) + EPS) * p.lnw[...] + p.lnb[...]
    gv_b = gv.astype(BF16)
    mixed = _dot(p.wcat[...], block_diag(gv_b)) + p.sgub[...]
    ycat_ref[:, Y_SGU:] = _unit_rms(u * mixed).astype(BF16)
    yield


class _MixerRefs:
    def __init__(self, **refs):
        self.__dict__.update(refs)


def _mixer_kernel(sink_ref, rel_ref,
                  xa_ref, xc_ref, ing_ref, win_ref, convw_ref, convb_ref, dtb_ref, alog_ref,
                  dskip_ref, triu_ref, expand_ref, bucket_ref,
                  lnw_ref, lnb_ref, sguw_ref, sgub_ref, outg_ref, wout_ref,
                  o_ref,
                  win_s, wdt_s, wout_s, projx_ref, projy_ref, dtx_ref, dty_ref, ycatx_ref, ycaty_ref,
                  xnx_ref, xny_ref,
                  ext_ref, st_ref, kprev_ref, vprev_ref, bias_ref, wcat_ref,
                  *, layer, chunks_per_seq):
    t = pl.program_id(0)
    this_layer = pl.ds(layer, 1)
    p = _MixerRefs(sink=lambda hq: sink_ref[layer, hq], convw=convw_ref, convb=convb_ref.at[this_layer],
                   dtb=dtb_ref, alog=alog_ref, dskip=dskip_ref, triu=triu_ref, expand=expand_ref,
                   lnw=lnw_ref.at[this_layer], lnb=lnb_ref.at[this_layer],
                   sgub=sgub_ref, ext=ext_ref, st=st_ref, kprev=kprev_ref, vprev=vprev_ref,
                   bias=bias_ref, wcat=wcat_ref)

    @pl.when(t == 0)
    def _init():
        row = lax.broadcasted_iota(jnp.int32, (CHUNK, CHUNK), 0)
        col = lax.broadcasted_iota(jnp.int32, (CHUNK, CHUNK), 1)
        bucket = bucket_ref[...]
        for pos, hq in enumerate(ATT_POS_HEADS):
            acc = jnp.zeros((CHUNK, CHUNK), F32)
            for k in range(REL_BUCKETS):
                acc = jnp.where(bucket == k, rel_ref[k, hq] * LOG2E, acc)
            bias_ref[0, pos] = acc
            bias_ref[1, pos] = jnp.where(row >= col, acc, MASKED)
        for g in range(GM_GROUPS):
            wcat_ref[:, g * CHUNK:(g + 1) * CHUNK] = jnp.where(row >= col, sguw_ref[g], 0.0).astype(BF16)

        def source_rows(b):
            lo = b * LANES
            if lo < P_Q:
                return [(lo, LANES)]
            if lo < P_K:
                pos = (lo - P_Q) // ATT_HEAD_DIM
                return [(R_Q + ATT_POS_HEADS[pos + i] * ATT_HEAD_DIM, ATT_HEAD_DIM)
                        for i in range(LANES // ATT_HEAD_DIM)]
            return [(lo - P_K + R_K, LANES)]

        def pack_rows(i, carry):
            r = pl.ds(pl.multiple_of(i * CHUNK, CHUNK), CHUNK)
            gain = ing_ref[:, r]
            for b in range(P_DT // LANES):
                tile = jnp.concatenate([win_ref[r0:r0 + n, r] for r0, n in source_rows(b)], axis=0)
                win_s[r, b * LANES:(b + 1) * LANES] = (tile * gain).T.astype(BF16)
            return carry

        lax.fori_loop(0, D_MODEL // CHUNK, pack_rows, 0)
        wdt_s[...] = jnp.concatenate(
            [win_ref[R_DT:R_DT + SSD_HEADS, :] * ing_ref[...],
             jnp.zeros((DT_ROWS - SSD_HEADS, D_MODEL), F32)], axis=0).astype(BF16)
        blocks = D_MODEL // ATT_HEAD_DIM
        first_att = Y_ATT // ATT_HEAD_DIM
        for j in range(blocks):
            src = first_att + ATT_POS_HEADS[j - first_att] if first_att <= j < first_att + ATT_Q_HEADS else j
            dst_rows = slice(j * ATT_HEAD_DIM, (j + 1) * ATT_HEAD_DIM)
            src_rows = slice(src * ATT_HEAD_DIM, (src + 1) * ATT_HEAD_DIM)
            wout_s[dst_rows, :] = (wout_ref[src_rows, :] * outg_ref[dst_rows, :]).astype(BF16)
        projy_ref[...] = jnp.zeros_like(projy_ref)
        dty_ref[...] = jnp.zeros_like(dty_ref)
        ycatx_ref[...] = jnp.zeros_like(ycatx_ref)
        st_ref[...] = jnp.zeros_like(st_ref)
        ext_ref[0:SUBLANES, :] = jnp.zeros((SUBLANES, SSD_XBC), F32)
        kprev_ref[...] = jnp.zeros_like(kprev_ref)
        vprev_ref[...] = jnp.zeros_like(vprev_ref)

    starts_seq = lax.rem(4 * t, chunks_per_seq) == 0
    keep = jnp.where(starts_seq, 0.0, 1.0).astype(F32)
    first_idx = jnp.where(starts_seq, 1, 0).astype(jnp.int32)
    even, odd = pl.ds(0, CHUNK), pl.ds(CHUNK, CHUNK)

    def projection_pieces(rows, proj_new, dt_new, ycat_out, xn):
        def normalize():
            xn[...] = _unit_rms(xa_ref[rows, :]).astype(BF16)

        def project(c0, c1):
            proj_new[:, c0:c1] = _dot(xn[...], win_s[:, c0:c1])

        def project_gv_dt():
            project(P_GV, P_DT)
            dt_t = _dot_nt(wdt_s[...], xn[...])
            dt_new[0] = dt_t[:, :CHUNK]
            dt_new[1] = dt_t[:, CHUNK:]

        def output(c0, c1):
            o_ref[rows, c0:c1] = xc_ref[rows, c0:c1] + _dot(ycat_out[...], wout_s[:, c0:c1])

        in_piece = lambda lo, hi: ("in", lo, hi, functools.partial(project, lo, hi))
        out_piece = lambda i: ("out", i * PROJ_PIECE, (i + 1) * PROJ_PIECE,
                               functools.partial(output, i * PROJ_PIECE, (i + 1) * PROJ_PIECE))
        xbc = [in_piece(P_XBC + c, P_XBC + c + PROJ_PIECE) for c in range(0, SSD_XBC, PROJ_PIECE)]
        z = [in_piece(P_Z + c, P_Z + c + PROJ_PIECE) for c in range(0, SSD_INNER, PROJ_PIECE)]
        return ([("norm", 0, 0, normalize)] + z + [out_piece(0)] + xbc[:2] + [out_piece(1)] + xbc[2:]
                + [out_piece(2), in_piece(P_Q, P_K), in_piece(P_K, P_U), out_piece(3),
                   in_piece(P_U, P_GV), ("in", P_GV, P_TOTAL, project_gv_dt)])

    def step_program(live_mixes, piece_kinds):
        mixes = [
            _mix_chunk(projy_ref.at[even], dty_ref.at[0], ycaty_ref.at[even], None, 0, p),
            _mix_chunk(projy_ref.at[odd], dty_ref.at[1], ycaty_ref.at[odd], None, 0, p),
            _mix_chunk(projx_ref.at[even], dtx_ref.at[0], ycatx_ref.at[even], keep, first_idx, p),
            _mix_chunk(projx_ref.at[odd], dtx_ref.at[1], ycatx_ref.at[odd], None, 0, p),
        ]
        pieces = [projection_pieces(pl.ds(0, PAIR), projx_ref, dtx_ref, ycatx_ref, xnx_ref),
                  projection_pieces(pl.ds(PAIR, PAIR), projy_ref, dty_ref, ycaty_ref, xny_ref)]
        pieces = [[piece for piece in phase if piece[0] in piece_kinds] for phase in pieces]
        overlaps = lambda reads, piece: reads is not None and reads[0] < piece[2] and piece[1] < reads[1]
        done = [0 if k in live_mixes else MIX_SECTIONS for k in range(len(mixes))]
        issued = [0, 0]

        def mix_may_run(k):
            section = done[k]
            if section == MIX_SECTIONS:
                return False
            if k > 0 and done[k - 1] < min(section + MIX_LAG, MIX_SECTIONS):
                return False
            if k >= 2 and issued[0] < len(pieces[0]):
                return False
            return True

        def piece_may_issue(phase):
            if issued[phase] == len(pieces[phase]):
                return False
            piece = pieces[phase][issued[phase]]
            if phase == 1:
                if piece[0] == "in":
                    return all(done[k] > s for k in (0, 1) for s, r in MIX_READS.items() if overlaps(r, piece))
                if piece[0] == "out":
                    return done[0] == done[1] == MIX_SECTIONS
            return True

        while min(done) < MIX_SECTIONS or issued != [len(pieces[0]), len(pieces[1])]:
            progressed = False
            for k, mix in enumerate(mixes):
                if mix_may_run(k):
                    assert next(mix) == MIX_READS.get(done[k]), "MIX_READS is stale"
                    done[k] += 1
                    progressed = True
            for phase in (0, 1):
                if piece_may_issue(phase):
                    pieces[phase][issued[phase]][3]()
                    issued[phase] += 1
                    progressed = True
                    break
            assert progressed, "mixer step schedule is stuck"
        for k in live_mixes:
            assert next(mixes[k], None) is None and mixes[k].gi_frame is None, "MIX_SECTIONS is stale"

    last_step = pl.num_programs(0) - 1

    @pl.when(t == 0)
    def _fill():
        step_program((2, 3), ("norm", "in"))

    @pl.when(jnp.logical_and(t > 0, t < last_step))
    def _steady():
        step_program((0, 1, 2, 3), ("norm", "in", "out"))

    @pl.when(t == last_step)
    def _drain():
        step_program((0, 1), ("out",))


def _folded_bucket_tile():
    i = np.arange(CHUNK)[:, None]
    j = np.arange(CHUNK)[None, :]
    n = np.where(j <= i, i - j, i - j + CHUNK)
    max_exact = REL_BUCKETS // 2
    large = max_exact + (np.log(np.maximum(n, 1) / max_exact) / np.log(REL_MAX_DIST / max_exact)
                         * (REL_BUCKETS - max_exact)).astype(np.int32)
    large = np.minimum(large, REL_BUCKETS - 1)
    return np.where(n < max_exact, n, large).astype(np.int32)


def _mixer(x2d, seq, layer, in_gain, w_in_t, conv_w, conv_b, dt_bias, a_log, d_skip, sinks, rel_bias,
           ln_w, ln_b, sgu_w, sgu_b, out_gain, w_out):
    tokens = x2d.shape[0]
    n_quads = tokens // QUAD
    chunks_per_seq = seq // CHUNK
    assert tokens % QUAD == 0 and seq % QUAD == 0 and GM_GROUPS == HEADS_PER_GROUP
    assert P_Q == R_DT and w_in_t.shape[1] == R_K + P_DT - P_K
    per_head_rows = lambda v: jnp.broadcast_to(v[:, None], (SSD_HEADS, LANES))
    row = lambda v: v.reshape(1, -1)
    expand = np.zeros((CHUNK, 2 * SSD_INNER), np.float32)
    for r in range(SSD_HEADS, 7 * SSD_HEADS):
        h = r % SSD_HEADS
        c0 = ((r // SSD_HEADS - 1) // 3) * SSD_INNER + h * SSD_HEAD_DIM
        expand[r, c0:c0 + SSD_HEAD_DIM] = 1.0
    expand = jnp.asarray(expand, BF16)
    triu = jnp.asarray(np.triu(np.ones((CHUNK, CHUNK), np.float32)), BF16)
    bucket = jnp.asarray(_folded_bucket_tile())
    sgu_b_e = jnp.repeat(jnp.transpose(sgu_b), GM_GROUP_DIM, axis=1)

    const2 = lambda t, *_: (0, 0)
    const3 = lambda t, *_: (0, 0, 0)
    full2 = lambda shape: pl.BlockSpec(shape, const2)
    layer_block = lambda shape: pl.BlockSpec((None,) + shape, lambda t, *_: (layer, 0, 0),
                                             pipeline_mode=pl.Buffered(1))
    projected = lambda t, *_: (jnp.minimum(t, n_quads - 1), 0)
    finished = lambda t, *_: (jnp.maximum(t - 1, 0), 0)
    grid_spec = pltpu.PrefetchScalarGridSpec(
        num_scalar_prefetch=2,
        grid=(n_quads + 1,),
        in_specs=[
            pl.BlockSpec((QUAD, D_MODEL), projected),
            pl.BlockSpec((QUAD, D_MODEL), finished),
            layer_block((1, D_MODEL)),
            layer_block((w_in_t.shape[1], D_MODEL)),
            layer_block((SSD_CONV, SSD_XBC)),
            full2(conv_b.shape),
            full2((SSD_HEADS, LANES)),
            full2((SSD_HEADS, LANES)),
            full2((1, SSD_INNER)),
            full2((CHUNK, CHUNK)),
            full2((CHUNK, 2 * SSD_INNER)),
            full2((CHUNK, CHUNK)),
            full2(ln_w.shape),
            full2(ln_b.shape),
            pl.BlockSpec((None, GM_GROUPS, CHUNK, CHUNK), lambda t, *_: (layer, 0, 0, 0)),
            full2((CHUNK, GM_WIDTH)),
            layer_block((D_MODEL, 1)),
            layer_block((D_MODEL, D_MODEL)),
        ],
        out_specs=pl.BlockSpec((QUAD, D_MODEL), finished),
        scratch_shapes=[
            pltpu.VMEM((D_MODEL, P_DT), BF16),
            pltpu.VMEM((DT_ROWS, D_MODEL), BF16),
            pltpu.VMEM((D_MODEL, D_MODEL), BF16),
            pltpu.VMEM((PAIR, P_DT), F32),
            pltpu.VMEM((PAIR, P_DT), F32),
            pltpu.VMEM((2, DT_ROWS, CHUNK), F32),
            pltpu.VMEM((2, DT_ROWS, CHUNK), F32),
            pltpu.VMEM((PAIR, D_MODEL), BF16),
            pltpu.VMEM((PAIR, D_MODEL), BF16),
            pltpu.VMEM((PAIR, D_MODEL), BF16),
            pltpu.VMEM((PAIR, D_MODEL), BF16),
            pltpu.VMEM((SUBLANES + CHUNK, SSD_XBC), F32),
            pltpu.VMEM((SSD_GROUPS, SSD_STATE, GROUP_WIDTH), F32),
            pltpu.VMEM((CHUNK, ATT_KV_WIDTH), BF16),
            pltpu.VMEM((CHUNK, ATT_KV_WIDTH), BF16),
            pltpu.VMEM((2, ATT_Q_HEADS, CHUNK, CHUNK), F32),
            pltpu.VMEM((CHUNK, GM_GROUPS * CHUNK), BF16),
        ],
    )
    return pl.pallas_call(
        functools.partial(_mixer_kernel, layer=layer, chunks_per_seq=chunks_per_seq),
        out_shape=jax.ShapeDtypeStruct(x2d.shape, F32),
        grid_spec=grid_spec,
        compiler_params=pltpu.CompilerParams(
            dimension_semantics=("arbitrary",), vmem_limit_bytes=VMEM_LIMIT_MIXER),
        name="token_mixer",
    )(sinks.astype(F32), rel_bias.astype(F32),
      x2d, x2d, in_gain, w_in_t, conv_w, conv_b, per_head_rows(dt_bias), per_head_rows(a_log),
      row(jnp.repeat(d_skip, SSD_HEAD_DIM)), triu, expand, bucket,
      ln_w, ln_b, sgu_w, sgu_b_e, out_gain, w_out)


def kernel(x, ffn1_norm, ffn1_w_gate, ffn1_w_up, ffn1_w_down, mix_norm, w_in, conv_w, conv_b, dt_bias, a_log, d_skip, ssd_norm, attn_sinks, rel_bias, attn_out_norm, sgu_ln_w, sgu_ln_b, sgu_w, sgu_b, sgu_out_norm, w_out, ffn2_norm, ffn2_w_gate, ffn2_w_up, ffn2_w_down, final_norm):
    batch, seq, d = x.shape
    depth = w_in.shape[0]
    fw = final_norm.reshape(1, d)
    x2 = x.reshape(batch * seq, d)
    mix_in_gain = mix_norm.reshape(depth, 1, d)
    w_in_t = jnp.swapaxes(w_in, 1, 2)
    att_cols = np.concatenate([np.arange(h * ATT_HEAD_DIM, (h + 1) * ATT_HEAD_DIM) for h in ATT_POS_HEADS])
    mix_out_gain = jnp.concatenate([ssd_norm, attn_out_norm[:, att_cols], sgu_out_norm], axis=1).reshape(depth, d, 1)
    for l in range(depth):
        x2 = _ffn(x2, l, ffn1_norm, ffn1_w_gate, ffn1_w_up, ffn1_w_down, fw, final_norm=False)
        x2 = _mixer(x2, seq, l, mix_in_gain, w_in_t, conv_w, conv_b, dt_bias[l],
                    a_log[l], d_skip[l], attn_sinks, rel_bias,
                    sgu_ln_w, sgu_ln_b, sgu_w, sgu_b[l], mix_out_gain, w_out)
        x2 = _ffn(x2, l, ffn2_norm, ffn2_w_gate, ffn2_w_up, ffn2_w_down, fw,
                  final_norm=(l == depth - 1))
    return x2.reshape(batch, seq, d)
```

```python
import functools
import math

import jax
import jax.numpy as jnp
import numpy as np
from jax import lax
from jax.experimental import pallas as pl
from jax.experimental.pallas import tpu as pltpu

F32 = jnp.float32
BF16 = jnp.bfloat16

D_MODEL = 1024
D_FF = 2816
EPS = 1e-6

SSD_HEADS = 8
SSD_HEAD_DIM = 64
SSD_INNER = SSD_HEADS * SSD_HEAD_DIM
SSD_GROUPS = 2
SSD_STATE = 128
SSD_CONV = 4
SSD_XBC = SSD_INNER + 2 * SSD_GROUPS * SSD_STATE
HEADS_PER_GROUP = SSD_HEADS // SSD_GROUPS
GROUP_WIDTH = HEADS_PER_GROUP * SSD_HEAD_DIM

ATT_Q_HEADS = 4
ATT_KV_HEADS = 2
ATT_HEAD_DIM = 64
ATT_WIDTH = ATT_Q_HEADS * ATT_HEAD_DIM
ATT_KV_WIDTH = ATT_KV_HEADS * ATT_HEAD_DIM
REL_BUCKETS = 32
REL_MAX_DIST = 128
ATT_POS_HEADS = (0, 2, 1, 3)

GM_GROUPS = 4
GM_GROUP_DIM = 64
GM_WIDTH = GM_GROUPS * GM_GROUP_DIM

CHUNK = 128
PAIR = 2 * CHUNK
QUAD = 2 * PAIR
PROJ_PIECE = 256
CONV_COLS = 256
MIX_SECTIONS = 18
MIX_LAG = 4

SUBLANES = 8
LANES = 128
HALF_LANES = LANES // 2
DT_PAD = LANES
DT_ROWS = 16

P_Z = 0
P_XBC = P_Z + SSD_INNER
P_Q = P_XBC + SSD_XBC
P_K = P_Q + ATT_WIDTH
P_V = P_K + ATT_KV_WIDTH
P_U = P_V + ATT_KV_WIDTH
P_GV = P_U + GM_WIDTH
P_DT = P_GV + GM_WIDTH
P_TOTAL = P_DT + DT_PAD

R_XBC = SSD_INNER
R_DT = R_XBC + SSD_XBC
R_Q = R_DT + SSD_HEADS
R_K = R_Q + ATT_WIDTH

Y_ATT = SSD_INNER
Y_SGU = Y_ATT + ATT_WIDTH

MIX_READS = {
    **{i: (P_XBC + i * CONV_COLS, P_XBC + (i + 1) * CONV_COLS) for i in range(SSD_XBC // CONV_COLS)},
    4: (P_DT, P_TOTAL),
    8: (P_Z, P_Z + GROUP_WIDTH),
    9: (P_Z + GROUP_WIDTH, P_Z + 2 * GROUP_WIDTH),
    10: (P_Q, P_U),
    15: (P_U, P_GV),
    16: (P_GV, P_DT),
}

MASKED = -1e30
LOG2E = math.log2(math.e)

FFN_TOKENS = 1024
FFN_CHUNK = 256
FFN_PIECES = D_FF // FFN_CHUNK
VMEM_LIMIT_FFN = 56 * 1024 * 1024
VMEM_LIMIT_MIXER = 48 * 1024 * 1024


def _unit_rms(x):
    return x * lax.rsqrt(jnp.mean(x * x, axis=-1, keepdims=True) + EPS)


def _sigmoid(x):
    return 1.0 / (1.0 + jnp.exp(-x))


def _gelu_tanh(x):
    c = math.sqrt(2.0 / math.pi)
    return 0.5 * x * (1.0 + jnp.tanh(c * (x + 0.044715 * (x * x * x))))


def _top_bits(x):
    bits = lax.bitcast_convert_type(x, jnp.uint32) & jnp.uint32(0xFFFF0000)
    return lax.bitcast_convert_type(bits, F32)


def _split3(x):
    hi = _top_bits(x)
    r = x - hi
    mid = _top_bits(r)
    return hi, mid, r - mid


def _dot(a, b):
    return jnp.dot(a, b, preferred_element_type=F32)


def _dot_nt(a, b):
    return lax.dot_general(a, b, (((1,), (1,)), ((), ())), preferred_element_type=F32)


def _lane_cumsum(x, upper_ones):
    rows = x.shape[0]
    parts = _dot(jnp.concatenate(_split3(x), axis=0).astype(BF16), upper_ones)
    return parts[0:rows] + parts[rows:2 * rows] + parts[2 * rows:3 * rows]


def _ffn_kernel(x_ref, nw_ref, wg_ref, wu_ref, wd_ref, fw_ref, o_ref,
                wg_s, wu_s, wd_s, h_ref, xn_s, acc_s, *, layer, final_norm):
    i = pl.program_id(0)

    def normalized(x):
        return (_unit_rms(x) * nw_ref[layer:layer + 1, :]).astype(BF16)

    def gated(xn, wg, wu):
        g = _dot(xn, wg)
        u = _dot(xn, wu)
        return (g * _sigmoid(g) * u).astype(BF16)

    def finish(rows, y):
        out = x_ref[rows, :] + 0.5 * y
        if final_norm:
            out = _unit_rms(out) * fw_ref[...]
        o_ref[rows, :] = out

    @pl.when(i < FFN_PIECES)
    def _first_rows_while_loading():
        wg_s[i] = wg_ref[...].astype(BF16)
        wu_s[i] = wu_ref[...].astype(BF16)
        wd_s[i] = wd_ref[...].astype(BF16)

        @pl.when(i == 0)
        def _start():
            block = FFN_TOKENS // 4
            for r in range(0, FFN_TOKENS, block):
                rows = slice(r, r + block)
                xn = normalized(x_ref[rows, :])
                xn_s[rows, :] = xn
                acc_s[rows, :] = _dot(gated(xn, wg_s[0], wu_s[0]), wd_s[0])

        @pl.when(i > 0)
        def _accumulate():
            acc_s[...] += _dot(gated(xn_s[...], wg_s[i], wu_s[i]), wd_s[i])

        @pl.when(i == FFN_PIECES - 1)
        def _end():
            finish(slice(None), acc_s[...])

    @pl.when(i >= FFN_PIECES)
    def _token_rows():
        xn = normalized(x_ref[...])
        for c in range(FFN_PIECES):
            h_ref[:, c * FFN_CHUNK:(c + 1) * FFN_CHUNK] = gated(xn, wg_s[c], wu_s[c])
        wd = wd_s[...].reshape(D_FF, D_MODEL)
        block = FFN_TOKENS // 4 if final_norm else FFN_TOKENS
        for r in range(0, FFN_TOKENS, block):
            rows = slice(r, r + block)
            finish(rows, _dot(h_ref[rows, :], wd))


def _ffn(x2d, layer, norm_w, w_gate, w_up, w_down, final_w, *, final_norm):
    tokens = x2d.shape[0]
    last = FFN_PIECES - 1
    rows = lambda i: (jnp.maximum(i - last, 0), 0)
    return pl.pallas_call(
        functools.partial(_ffn_kernel, layer=layer, final_norm=final_norm),
        out_shape=jax.ShapeDtypeStruct(x2d.shape, F32),
        grid=(last + tokens // FFN_TOKENS,),
        in_specs=[
            pl.BlockSpec((FFN_TOKENS, D_MODEL), rows),
            pl.BlockSpec(norm_w.shape, lambda i: (0, 0)),
            pl.BlockSpec((None, D_MODEL, FFN_CHUNK), lambda i: (layer, 0, jnp.minimum(i, last))),
            pl.BlockSpec((None, D_MODEL, FFN_CHUNK), lambda i: (layer, 0, jnp.minimum(i, last))),
            pl.BlockSpec((None, FFN_CHUNK, D_MODEL), lambda i: (layer, jnp.minimum(i, last), 0)),
            pl.BlockSpec((1, D_MODEL), lambda i: (0, 0)),
        ],
        out_specs=pl.BlockSpec((FFN_TOKENS, D_MODEL), rows),
        scratch_shapes=[
            pltpu.VMEM((FFN_PIECES, D_MODEL, FFN_CHUNK), BF16),
            pltpu.VMEM((FFN_PIECES, D_MODEL, FFN_CHUNK), BF16),
            pltpu.VMEM((FFN_PIECES, FFN_CHUNK, D_MODEL), BF16),
            pltpu.VMEM((FFN_TOKENS, D_FF), BF16),
            pltpu.VMEM((FFN_TOKENS, D_MODEL), BF16),
            pltpu.VMEM((FFN_TOKENS, D_MODEL), F32),
        ],
        compiler_params=pltpu.CompilerParams(
            dimension_semantics=("arbitrary",), vmem_limit_bytes=VMEM_LIMIT_FFN),
        name="ffn_halfstep",
    )(x2d, norm_w, w_gate, w_up, w_down, final_w)


def _mix_chunk(proj_ref, dt_ref, ycat_ref, keep, bias_idx, p):
    row = lax.broadcasted_iota(jnp.int32, (CHUNK, CHUNK), 0)
    lane = lax.broadcasted_iota(jnp.int32, (CHUNK, CHUNK), 1)
    causal = row >= lane
    row_b = row.astype(F32).astype(BF16)
    lane_b = lane.astype(F32).astype(BF16)
    causal_b = row_b >= lane_b
    upper_half_b = lane_b >= HALF_LANES
    zero_b = jnp.zeros((CHUNK, CHUNK), BF16)
    group_b = (lax.broadcasted_iota(jnp.int32, (CHUNK, GROUP_WIDTH), 1) // SSD_HEAD_DIM).astype(F32).astype(BF16)

    def block_diag(x_b):
        zero = jnp.zeros_like(x_b)
        return jnp.concatenate([jnp.where(group_b == i, x_b, zero) for i in range(HEADS_PER_GROUP)], axis=0)

    xbc_parts = []
    for c0 in range(0, SSD_XBC, CONV_COLS):
        cs = slice(c0, c0 + CONV_COLS)
        if keep is not None:
            p.ext[0:SUBLANES, cs] = p.ext[0:SUBLANES, cs] * keep
        raw = proj_ref[:, P_XBC + c0:P_XBC + c0 + CONV_COLS]
        p.ext[SUBLANES:, cs] = raw
        ext = p.ext[:, cs]
        w = p.convw[:, cs]
        ext1 = pltpu.roll(ext, 1, 0)
        older = w[1:2, :] * ext + w[0:1, :] * ext1
        conv = (p.convb[:, cs] + w[3:4, :] * raw + w[2:3, :] * ext1[SUBLANES:, :]
                + pltpu.roll(older, 2, 0)[SUBLANES:, :])
        p.ext[0:SUBLANES, cs] = raw[CHUNK - SUBLANES:, :]
        xbc_parts.append(conv * _sigmoid(conv))
        yield P_XBC + c0, P_XBC + c0 + CONV_COLS
    xbc = jnp.concatenate(xbc_parts, axis=-1)
    xs = xbc[:, :SSD_INNER]
    bm = xbc[:, SSD_INNER:SSD_INNER + SSD_GROUPS * SSD_STATE]
    cm = xbc[:, SSD_INNER + SSD_GROUPS * SSD_STATE:]
    x_b = xs.astype(BF16)
    bm_b = bm.astype(BF16)
    cm_b = cm.astype(BF16)

    dtr = dt_ref[0:SSD_HEADS, :] + p.dtb[...]
    dt = jnp.maximum(dtr, 0.0) + jnp.log1p(jnp.exp(-jnp.abs(dtr)))
    acs = _lane_cumsum(dt * (-LOG2E * jnp.exp(p.alog[...])), p.triu[...])
    grow = jnp.exp2(acs)
    to_end = dt * jnp.exp2(acs[:, CHUNK - 1:CHUNK] - acs)
    src = acs - jnp.log2(dt)
    stack = jnp.concatenate(
        [acs, *_split3(grow), *_split3(to_end), jnp.zeros((CHUNK - 7 * SSD_HEADS, LANES), F32)], axis=0)
    cols = stack.T
    expanded = _dot(cols.astype(BF16), p.expand[...])
    grow_e = expanded[:, :SSD_INNER]
    to_end_e = expanded[:, SSD_INNER:]
    xd_b = (xs * to_end_e).astype(BF16)
    yield P_DT, P_TOTAL

    y_diag = []
    for g in range(SSD_GROUPS):
        n0 = g * SSD_STATE
        cb = _dot_nt(cm_b[:, n0:n0 + SSD_STATE], bm_b[:, n0:n0 + SSD_STATE])
        parts = []
        for hh in range(HEADS_PER_GROUP):
            h = g * HEADS_PER_GROUP + hh
            seg = cols[:, h:h + 1] - src[h:h + 1, :]
            parts.append((cb * jnp.exp2(jnp.where(causal, seg, -jnp.inf))).astype(BF16))
        w0 = g * GROUP_WIDTH
        y_diag.append(_dot(jnp.concatenate(parts, axis=-1), block_diag(x_b[:, w0:w0 + GROUP_WIDTH])))
        yield
    y_off = []
    for g in range(SSD_GROUPS):
        n0 = g * SSD_STATE
        w0 = g * GROUP_WIDTH
        prev = p.st[g]
        if keep is not None:
            prev = prev * keep
        y_off.append(_dot(cm_b[:, n0:n0 + SSD_STATE], prev.astype(BF16)))
        bt = bm[:, n0:n0 + SSD_STATE].T.astype(BF16)
        p.st[g] = (prev * grow_e[CHUNK - 1:CHUNK, w0:w0 + GROUP_WIDTH]
                   + _dot(bt, xd_b[:, w0:w0 + GROUP_WIDTH]))
    yield
    gated, sumsq = [], 0.0
    for g in range(SSD_GROUPS):
        ws = slice(g * GROUP_WIDTH, (g + 1) * GROUP_WIDTH)
        y = y_diag[g] + y_off[g] * grow_e[:, ws] + xs[:, ws] * p.dskip[:, ws]
        z = proj_ref[:, P_Z + g * GROUP_WIDTH:P_Z + (g + 1) * GROUP_WIDTH]
        gated.append(y * (z * _sigmoid(z)))
        sumsq = sumsq + jnp.sum(gated[g] * gated[g], axis=-1, keepdims=True)
        yield P_Z + g * GROUP_WIDTH, P_Z + (g + 1) * GROUP_WIDTH
    inv_rms = lax.rsqrt(sumsq * (1.0 / SSD_INNER) + EPS)
    for g in range(SSD_GROUPS):
        ycat_ref[:, g * GROUP_WIDTH:(g + 1) * GROUP_WIDTH] = (gated[g] * inv_rms).astype(BF16)

    q_b = (proj_ref[:, P_Q:P_Q + ATT_WIDTH] * (LOG2E / math.sqrt(ATT_HEAD_DIM))).astype(BF16)
    k_b = proj_ref[:, P_K:P_K + ATT_KV_WIDTH].astype(BF16)
    v_b = proj_ref[:, P_V:P_V + ATT_KV_WIDTH].astype(BF16)
    keys = jnp.concatenate([p.kprev[...], k_b], axis=0)
    v_prev = p.vprev[...]
    halves = lambda t: (jnp.where(upper_half_b, zero_b, t), jnp.where(upper_half_b, t, zero_b))
    v_prev_half, v_half = halves(v_prev), halves(v_b)
    p.kprev[...] = k_b
    p.vprev[...] = v_b
    probs, values = [], []
    for pos, hq in enumerate(ATT_POS_HEADS):
        tile, half = divmod(pos, 2)
        q_tile = q_b[:, tile * LANES:(tile + 1) * LANES]
        q_head = halves(q_tile)[half]
        s = _dot_nt(q_head, keys)
        s = jnp.where(causal, s[:, CHUNK:], s[:, :CHUNK]) + p.bias[bias_idx, pos]
        sink = p.sink(hq) * LOG2E
        m = jnp.maximum(jnp.max(s, axis=-1, keepdims=True), sink)
        e = jnp.exp2(s - m)
        denom = jnp.sum(e, axis=-1, keepdims=True) + jnp.exp2(sink - m)
        pn = (e * (1.0 / denom)).astype(BF16)
        probs.append(jnp.where(causal_b, zero_b, pn))
        probs.append(jnp.where(causal_b, pn, zero_b))
        for vh in (v_prev_half[half], v_half[half]):
            values.append(jnp.concatenate([zero_b, vh] if tile else [vh, zero_b], axis=-1))
        yield (P_Q, P_U) if pos == 0 else None
    att = _dot(jnp.concatenate(probs, axis=-1), jnp.concatenate(values, axis=0))
    ycat_ref[:, Y_ATT:Y_SGU] = _unit_rms(att).astype(BF16)
    yield

    u = _gelu_tanh(proj_ref[:, P_U:P_U + GM_WIDTH])
    yield P_U, P_GV
    gv = _gelu_tanh(proj_ref[:, P_GV:P_GV + GM_WIDTH])
    yield P_GV, P_DT
    mu = jnp.mean(gv, axis=-1, keepdims=True)
    gc = gv - mu
    gv = gc * lax.rsqrt(jnp.mean(gc * gc, axis=-1, keepdims=True) + EPS) * p.lnw[...] + p.lnb[...]
    gv_b = gv.astype(BF16)
    mixed = _dot(p.wcat[...], block_diag(gv_b)) + p.sgub[...]
    ycat_ref[:, Y_SGU:] = _unit_rms(u * mixed).astype(BF16)
    yield


class _MixerRefs:
    def __init__(self, **refs):
        self.__dict__.update(refs)


def _mixer_kernel(sink_ref, rel_ref,
                  xa_ref, xc_ref, ing_ref, win_ref, convw_ref, convb_ref, dtb_ref, alog_ref,
                  dskip_ref, triu_ref, expand_ref, bucket_ref,
                  lnw_ref, lnb_ref, sguw_ref, sgub_ref, outg_ref, wout_ref,
                  o_ref,
                  win_s, wdt_s, wout_s, projx_ref, projy_ref, dtx_ref, dty_ref, ycatx_ref, ycaty_ref,
                  xnx_ref, xny_ref,
                  ext_ref, st_ref, kprev_ref, vprev_ref, bias_ref, wcat_ref,
                  *, layer, chunks_per_seq):
    t = pl.program_id(0)
    this_layer = pl.ds(layer, 1)
    p = _MixerRefs(sink=lambda hq: sink_ref[layer, hq], convw=convw_ref, convb=convb_ref.at[this_layer],
                   dtb=dtb_ref, alog=alog_ref, dskip=dskip_ref, triu=triu_ref, expand=expand_ref,
                   lnw=lnw_ref.at[this_layer], lnb=lnb_ref.at[this_layer],
                   sgub=sgub_ref, ext=ext_ref, st=st_ref, kprev=kprev_ref, vprev=vprev_ref,
                   bias=bias_ref, wcat=wcat_ref)

    @pl.when(t == 0)
    def _init():
        row = lax.broadcasted_iota(jnp.int32, (CHUNK, CHUNK), 0)
        col = lax.broadcasted_iota(jnp.int32, (CHUNK, CHUNK), 1)
        bucket = bucket_ref[...]
        for pos, hq in enumerate(ATT_POS_HEADS):
            acc = jnp.zeros((CHUNK, CHUNK), F32)
            for k in range(REL_BUCKETS):
                acc = jnp.where(bucket == k, rel_ref[k, hq] * LOG2E, acc)
            bias_ref[0, pos] = acc
            bias_ref[1, pos] = jnp.where(row >= col, acc, MASKED)
        for g in range(GM_GROUPS):
            wcat_ref[:, g * CHUNK:(g + 1) * CHUNK] = jnp.where(row >= col, sguw_ref[g], 0.0).astype(BF16)

        def source_rows(b):
            lo = b * LANES
            if lo < P_Q:
                return [(lo, LANES)]
            if lo < P_K:
                pos = (lo - P_Q) // ATT_HEAD_DIM
                return [(R_Q + ATT_POS_HEADS[pos + i] * ATT_HEAD_DIM, ATT_HEAD_DIM)
                        for i in range(LANES // ATT_HEAD_DIM)]
            return [(lo - P_K + R_K, LANES)]

        def pack_rows(i, carry):
            r = pl.ds(pl.multiple_of(i * CHUNK, CHUNK), CHUNK)
            gain = ing_ref[:, r]
            for b in range(P_DT // LANES):
                tile = jnp.concatenate([win_ref[r0:r0 + n, r] for r0, n in source_rows(b)], axis=0)
                win_s[r, b * LANES:(b + 1) * LANES] = (tile * gain).T.astype(BF16)
            return carry

        lax.fori_loop(0, D_MODEL // CHUNK, pack_rows, 0)
        wdt_s[...] = jnp.concatenate(
            [win_ref[R_DT:R_DT + SSD_HEADS, :] * ing_ref[...],
             jnp.zeros((DT_ROWS - SSD_HEADS, D_MODEL), F32)], axis=0).astype(BF16)
        blocks = D_MODEL // ATT_HEAD_DIM
        first_att = Y_ATT // ATT_HEAD_DIM
        for j in range(blocks):
            src = first_att + ATT_POS_HEADS[j - first_att] if first_att <= j < first_att + ATT_Q_HEADS else j
            dst_rows = slice(j * ATT_HEAD_DIM, (j + 1) * ATT_HEAD_DIM)
            src_rows = slice(src * ATT_HEAD_DIM, (src + 1) * ATT_HEAD_DIM)
            wout_s[dst_rows, :] = (wout_ref[src_rows, :] * outg_ref[dst_rows, :]).astype(BF16)
        projy_ref[...] = jnp.zeros_like(projy_ref)
        dty_ref[...] = jnp.zeros_like(dty_ref)
        ycatx_ref[...] = jnp.zeros_like(ycatx_ref)
        st_ref[...] = jnp.zeros_like(st_ref)
        ext_ref[0:SUBLANES, :] = jnp.zeros((SUBLANES, SSD_XBC), F32)
        kprev_ref[...] = jnp.zeros_like(kprev_ref)
        vprev_ref[...] = jnp.zeros_like(vprev_ref)

    starts_seq = lax.rem(4 * t, chunks_per_seq) == 0
    keep = jnp.where(starts_seq, 0.0, 1.0).astype(F32)
    first_idx = jnp.where(starts_seq, 1, 0).astype(jnp.int32)
    even, odd = pl.ds(0, CHUNK), pl.ds(CHUNK, CHUNK)

    def projection_pieces(rows, proj_new, dt_new, ycat_out, xn):
        def normalize():
            xn[...] = _unit_rms(xa_ref[rows, :]).astype(BF16)

        def project(c0, c1):
            proj_new[:, c0:c1] = _dot(xn[...], win_s[:, c0:c1])

        def project_gv_dt():
            project(P_GV, P_DT)
            dt_t = _dot_nt(wdt_s[...], xn[...])
            dt_new[0] = dt_t[:, :CHUNK]
            dt_new[1] = dt_t[:, CHUNK:]

        def output(c0, c1):
            o_ref[rows, c0:c1] = xc_ref[rows, c0:c1] + _dot(ycat_out[...], wout_s[:, c0:c1])

        in_piece = lambda lo, hi: ("in", lo, hi, functools.partial(project, lo, hi))
        out_piece = lambda i: ("out", i * PROJ_PIECE, (i + 1) * PROJ_PIECE,
                               functools.partial(output, i * PROJ_PIECE, (i + 1) * PROJ_PIECE))
        xbc = [in_piece(P_XBC + c, P_XBC + c + PROJ_PIECE) for c in range(0, SSD_XBC, PROJ_PIECE)]
        z = [in_piece(P_Z + c, P_Z + c + PROJ_PIECE) for c in range(0, SSD_INNER, PROJ_PIECE)]
        return ([("norm", 0, 0, normalize)] + z + [out_piece(0)] + xbc[:2] + [out_piece(1)] + xbc[2:]
                + [out_piece(2), in_piece(P_Q, P_K), in_piece(P_K, P_U), out_piece(3),
                   in_piece(P_U, P_GV), ("in", P_GV, P_TOTAL, project_gv_dt)])

    def step_program(live_mixes, piece_kinds):
        mixes = [
            _mix_chunk(projy_ref.at[even], dty_ref.at[0], ycaty_ref.at[even], None, 0, p),
            _mix_chunk(projy_ref.at[odd], dty_ref.at[1], ycaty_ref.at[odd], None, 0, p),
            _mix_chunk(projx_ref.at[even], dtx_ref.at[0], ycatx_ref.at[even], keep, first_idx, p),
            _mix_chunk(projx_ref.at[odd], dtx_ref.at[1], ycatx_ref.at[odd], None, 0, p),
        ]
        pieces = [projection_pieces(pl.ds(0, PAIR), projx_ref, dtx_ref, ycatx_ref, xnx_ref),
                  projection_pieces(pl.ds(PAIR, PAIR), projy_ref, dty_ref, ycaty_ref, xny_ref)]
        pieces = [[piece for piece in phase if piece[0] in piece_kinds] for phase in pieces]
        overlaps = lambda reads, piece: reads is not None and reads[0] < piece[2] and piece[1] < reads[1]
        done = [0 if k in live_mixes else MIX_SECTIONS for k in range(len(mixes))]
        issued = [0, 0]

        def mix_may_run(k):
            section = done[k]
            if section == MIX_SECTIONS:
                return False
            if k > 0 and done[k - 1] < min(section + MIX_LAG, MIX_SECTIONS):
                return False
            if k >= 2 and issued[0] < len(pieces[0]):
                return False
            return True

        def piece_may_issue(phase):
            if issued[phase] == len(pieces[phase]):
                return False
            piece = pieces[phase][issued[phase]]
            if phase == 1:
                if piece[0] == "in":
                    return all(done[k] > s for k in (0, 1) for s, r in MIX_READS.items() if overlaps(r, piece))
                if piece[0] == "out":
                    return done[0] == done[1] == MIX_SECTIONS
            return True

        while min(done) < MIX_SECTIONS or issued != [len(pieces[0]), len(pieces[1])]:
            progressed = False
            for k, mix in enumerate(mixes):
                if mix_may_run(k):
                    assert next(mix) == MIX_READS.get(done[k]), "MIX_READS is stale"
                    done[k] += 1
                    progressed = True
            for phase in (0, 1):
                if piece_may_issue(phase):
                    pieces[phase][issued[phase]][3]()
                    issued[phase] += 1
                    progressed = True
                    break
            assert progressed, "mixer step schedule is stuck"
        for k in live_mixes:
            assert next(mixes[k], None) is None and mixes[k].gi_frame is None, "MIX_SECTIONS is stale"

    last_step = pl.num_programs(0) - 1

    @pl.when(t == 0)
    def _fill():
        step_program((2, 3), ("norm", "in"))

    @pl.when(jnp.logical_and(t > 0, t < last_step))
    def _steady():
        step_program((0, 1, 2, 3), ("norm", "in", "out"))

    @pl.when(t == last_step)
    def _drain():
        step_program((0, 1), ("out",))


def _folded_bucket_tile():
    i = np.arange(CHUNK)[:, None]
    j = np.arange(CHUNK)[None, :]
    n = np.where(j <= i, i - j, i - j + CHUNK)
    max_exact = REL_BUCKETS // 2
    large = max_exact + (np.log(np.maximum(n, 1) / max_exact) / np.log(REL_MAX_DIST / max_exact)
                         * (REL_BUCKETS - max_exact)).astype(np.int32)
    large = np.minimum(large, REL_BUCKETS - 1)
    return np.where(n < max_exact, n, large).astype(np.int32)


def _mixer(x2d, seq, layer, in_gain, w_in_t, conv_w, conv_b, dt_bias, a_log, d_skip, sinks, rel_bias,
           ln_w, ln_b, sgu_w, sgu_b, out_gain, w_out):
    tokens = x2d.shape[0]
    n_quads = tokens // QUAD
    chunks_per_seq = seq // CHUNK
    assert tokens % QUAD == 0 and seq % QUAD == 0 and GM_GROUPS == HEADS_PER_GROUP
    assert P_Q == R_DT and w_in_t.shape[1] == R_K + P_DT - P_K
    per_head_rows = lambda v: jnp.broadcast_to(v[:, None], (SSD_HEADS, LANES))
    row = lambda v: v.reshape(1, -1)
    expand = np.zeros((CHUNK, 2 * SSD_INNER), np.float32)
    for r in range(SSD_HEADS, 7 * SSD_HEADS):
        h = r % SSD_HEADS
        c0 = ((r // SSD_HEADS - 1) // 3) * SSD_INNER + h * SSD_HEAD_DIM
        expand[r, c0:c0 + SSD_HEAD_DIM] = 1.0
    expand = jnp.asarray(expand, BF16)
    triu = jnp.asarray(np.triu(np.ones((CHUNK, CHUNK), np.float32)), BF16)
    bucket = jnp.asarray(_folded_bucket_tile())
    sgu_b_e = jnp.repeat(jnp.transpose(sgu_b), GM_GROUP_DIM, axis=1)

    const2 = lambda t, *_: (0, 0)
    const3 = lambda t, *_: (0, 0, 0)
    full2 = lambda shape: pl.BlockSpec(shape, const2)
    layer_block = lambda shape: pl.BlockSpec((None,) + shape, lambda t, *_: (layer, 0, 0),
                                             pipeline_mode=pl.Buffered(1))
    projected = lambda t, *_: (jnp.minimum(t, n_quads - 1), 0)
    finished = lambda t, *_: (jnp.maximum(t - 1, 0), 0)
    grid_spec = pltpu.PrefetchScalarGridSpec(
        num_scalar_prefetch=2,
        grid=(n_quads + 1,),
        in_specs=[
            pl.BlockSpec((QUAD, D_MODEL), projected),
            pl.BlockSpec((QUAD, D_MODEL), finished),
            layer_block((1, D_MODEL)),
            layer_block((w_in_t.shape[1], D_MODEL)),
            layer_block((SSD_CONV, SSD_XBC)),
            full2(conv_b.shape),
            full2((SSD_HEADS, LANES)),
            full2((SSD_HEADS, LANES)),
            full2((1, SSD_INNER)),
            full2((CHUNK, CHUNK)),
            full2((CHUNK, 2 * SSD_INNER)),
            full2((CHUNK, CHUNK)),
            full2(ln_w.shape),
            full2(ln_b.shape),
            pl.BlockSpec((None, GM_GROUPS, CHUNK, CHUNK), lambda t, *_: (layer, 0, 0, 0)),
            full2((CHUNK, GM_WIDTH)),
            layer_block((D_MODEL, 1)),
            layer_block((D_MODEL, D_MODEL)),
        ],
        out_specs=pl.BlockSpec((QUAD, D_MODEL), finished),
        scratch_shapes=[
            pltpu.VMEM((D_MODEL, P_DT), BF16),
            pltpu.VMEM((DT_ROWS, D_MODEL), BF16),
            pltpu.VMEM((D_MODEL, D_MODEL), BF16),
            pltpu.VMEM((PAIR, P_DT), F32),
            pltpu.VMEM((PAIR, P_DT), F32),
            pltpu.VMEM((2, DT_ROWS, CHUNK), F32),
            pltpu.VMEM((2, DT_ROWS, CHUNK), F32),
            pltpu.VMEM((PAIR, D_MODEL), BF16),
            pltpu.VMEM((PAIR, D_MODEL), BF16),
            pltpu.VMEM((PAIR, D_MODEL), BF16),
            pltpu.VMEM((PAIR, D_MODEL), BF16),
            pltpu.VMEM((SUBLANES + CHUNK, SSD_XBC), F32),
            pltpu.VMEM((SSD_GROUPS, SSD_STATE, GROUP_WIDTH), F32),
            pltpu.VMEM((CHUNK, ATT_KV_WIDTH), BF16),
            pltpu.VMEM((CHUNK, ATT_KV_WIDTH), BF16),
            pltpu.VMEM((2, ATT_Q_HEADS, CHUNK, CHUNK), F32),
            pltpu.VMEM((CHUNK, GM_GROUPS * CHUNK), BF16),
        ],
    )
    return pl.pallas_call(
        functools.partial(_mixer_kernel, layer=layer, chunks_per_seq=chunks_per_seq),
        out_shape=jax.ShapeDtypeStruct(x2d.shape, F32),
        grid_spec=grid_spec,
        compiler_params=pltpu.CompilerParams(
            dimension_semantics=("arbitrary",), vmem_limit_bytes=VMEM_LIMIT_MIXER),
        name="token_mixer",
    )(sinks.astype(F32), rel_bias.astype(F32),
      x2d, x2d, in_gain, w_in_t, conv_w, conv_b, per_head_rows(dt_bias), per_head_rows(a_log),
      row(jnp.repeat(d_skip, SSD_HEAD_DIM)), triu, expand, bucket,
      ln_w, ln_b, sgu_w, sgu_b_e, out_gain, w_out)


def kernel(x, ffn1_norm, ffn1_w_gate, ffn1_w_up, ffn1_w_down, mix_norm, w_in, conv_w, conv_b, dt_bias, a_log, d_skip, ssd_norm, attn_sinks, rel_bias, attn_out_norm, sgu_ln_w, sgu_ln_b, sgu_w, sgu_b, sgu_out_norm, w_out, ffn2_norm, ffn2_w_gate, ffn2_w_up, ffn2_w_down, final_norm):
    batch, seq, d = x.shape
    depth = w_in.shape[0]
    fw = final_norm.reshape(1, d)
    x2 = x.reshape(batch * seq, d)
    mix_in_gain = mix_norm.reshape(depth, 1, d)
    w_in_t = jnp.swapaxes(w_in, 1, 2)
    att_cols = np.concatenate([np.arange(h * ATT_HEAD_DIM, (h + 1) * ATT_HEAD_DIM) for h in ATT_POS_HEADS])
    mix_out_gain = jnp.concatenate([ssd_norm, attn_out_norm[:, att_cols], sgu_out_norm], axis=1).reshape(depth, d, 1)
    for l in range(depth):
        x2 = _ffn(x2, l, ffn1_norm, ffn1_w_gate, ffn1_w_up, ffn1_w_down, fw, final_norm=False)
        x2 = _mixer(x2, seq, l, mix_in_gain, w_in_t, conv_w, conv_b, dt_bias[l],
                    a_log[l], d_skip[l], attn_sinks, rel_bias,
                    sgu_ln_w, sgu_ln_b, sgu_w, sgu_b[l], mix_out_gain, w_out)
        x2 = _ffn(x2, l, ffn2_norm, ffn2_w_gate, ffn2_w_up, ffn2_w_down, fw,
                  final_norm=(l == depth - 1))
    return x2.reshape(batch, seq, d)
```

```python
import functools
import math

import jax
import jax.numpy as jnp
import numpy as np
from jax import lax
from jax.experimental import pallas as pl
from jax.experimental.pallas import tpu as pltpu

F32 = jnp.float32
BF16 = jnp.bfloat16

D_MODEL = 1024
D_FF = 2816
EPS = 1e-6

SSD_HEADS = 8
SSD_HEAD_DIM = 64
SSD_INNER = SSD_HEADS * SSD_HEAD_DIM
SSD_GROUPS = 2
SSD_STATE = 128
SSD_CONV = 4
SSD_XBC = SSD_INNER + 2 * SSD_GROUPS * SSD_STATE
HEADS_PER_GROUP = SSD_HEADS // SSD_GROUPS
GROUP_WIDTH = HEADS_PER_GROUP * SSD_HEAD_DIM

ATT_Q_HEADS = 4
ATT_KV_HEADS = 2
ATT_HEAD_DIM = 64
ATT_WIDTH = ATT_Q_HEADS * ATT_HEAD_DIM
ATT_KV_WIDTH = ATT_KV_HEADS * ATT_HEAD_DIM
REL_BUCKETS = 32
REL_MAX_DIST = 128
ATT_POS_HEADS = (0, 2, 1, 3)

GM_GROUPS = 4
GM_GROUP_DIM = 64
GM_WIDTH = GM_GROUPS * GM_GROUP_DIM

CHUNK = 128
PAIR = 2 * CHUNK
QUAD = 2 * PAIR
PROJ_PIECE = 256
CONV_COLS = 256
MIX_SECTIONS = 18
MIX_LAG = 4

SUBLANES = 8
LANES = 128
HALF_LANES = LANES // 2
DT_PAD = LANES
DT_ROWS = 16

P_Z = 0
P_XBC = P_Z + SSD_INNER
P_Q = P_XBC + SSD_XBC
P_K = P_Q + ATT_WIDTH
P_V = P_K + ATT_KV_WIDTH
P_U = P_V + ATT_KV_WIDTH
P_GV = P_U + GM_WIDTH
P_DT = P_GV + GM_WIDTH
P_TOTAL = P_DT + DT_PAD

R_XBC = SSD_INNER
R_DT = R_XBC + SSD_XBC
R_Q = R_DT + SSD_HEADS
R_K = R_Q + ATT_WIDTH

Y_ATT = SSD_INNER
Y_SGU = Y_ATT + ATT_WIDTH

MIX_READS = {
    **{i: (P_XBC + i * CONV_COLS, P_XBC + (i + 1) * CONV_COLS) for i in range(SSD_XBC // CONV_COLS)},
    4: (P_DT, P_TOTAL),
    8: (P_Z, P_Z + GROUP_WIDTH),
    9: (P_Z + GROUP_WIDTH, P_Z + 2 * GROUP_WIDTH),
    10: (P_Q, P_U),
    15: (P_U, P_GV),
    16: (P_GV, P_DT),
}

MASKED = -1e30
LOG2E = math.log2(math.e)

FFN_TOKENS = 1024
FFN_CHUNK = 256
FFN_PIECES = D_FF // FFN_CHUNK
VMEM_LIMIT_FFN = 56 * 1024 * 1024
VMEM_LIMIT_MIXER = 48 * 1024 * 1024


def _unit_rms(x):
    return x * lax.rsqrt(jnp.mean(x * x, axis=-1, keepdims=True) + EPS)


def _sigmoid(x):
    return 1.0 / (1.0 + jnp.exp(-x))


def _gelu_tanh(x):
    c = math.sqrt(2.0 / math.pi)
    return 0.5 * x * (1.0 + jnp.tanh(c * (x + 0.044715 * (x * x * x))))


def _top_bits(x):
    bits = lax.bitcast_convert_type(x, jnp.uint32) & jnp.uint32(0xFFFF0000)
    return lax.bitcast_convert_type(bits, F32)


def _split3(x):
    hi = _top_bits(x)
    r = x - hi
    mid = _top_bits(r)
    return hi, mid, r - mid


def _dot(a, b):
    return jnp.dot(a, b, preferred_element_type=F32)


def _dot_nt(a, b):
    return lax.dot_general(a, b, (((1,), (1,)), ((), ())), preferred_element_type=F32)


def _lane_cumsum(x, upper_ones):
    rows = x.shape[0]
    parts = _dot(jnp.concatenate(_split3(x), axis=0).astype(BF16), upper_ones)
    return parts[0:rows] + parts[rows:2 * rows] + parts[2 * rows:3 * rows]


def _ffn_kernel(x_ref, nw_ref, wg_ref, wu_ref, wd_ref, fw_ref, o_ref,
                wg_s, wu_s, wd_s, h_ref, xn_s, acc_s, *, layer, final_norm):
    i = pl.program_id(0)

    def normalized(x):
        return (_unit_rms(x) * nw_ref[layer:layer + 1, :]).astype(BF16)

    def gated(xn, wg, wu):
        g = _dot(xn, wg)
        u = _dot(xn, wu)
        return (g * _sigmoid(g) * u).astype(BF16)

    def finish(rows, y):
        out = x_ref[rows, :] + 0.5 * y
        if final_norm:
            out = _unit_rms(out) * fw_ref[...]
        o_ref[rows, :] = out

    @pl.when(i < FFN_PIECES)
    def _first_rows_while_loading():
        wg_s[i] = wg_ref[...].astype(BF16)
        wu_s[i] = wu_ref[...].astype(BF16)
        wd_s[i] = wd_ref[...].astype(BF16)

        @pl.when(i == 0)
        def _start():
            block = FFN_TOKENS // 4
            for r in range(0, FFN_TOKENS, block):
                rows = slice(r, r + block)
                xn = normalized(x_ref[rows, :])
                xn_s[rows, :] = xn
                acc_s[rows, :] = _dot(gated(xn, wg_s[0], wu_s[0]), wd_s[0])

        @pl.when(jnp.logical_and(i > 0, i < FFN_PIECES - 1))
        def _accumulate():
            acc_s[...] += _dot(gated(xn_s[...], wg_s[i], wu_s[i]), wd_s[i])

        @pl.when(i == FFN_PIECES - 1)
        def _end():
            last = FFN_PIECES - 1
            block = FFN_TOKENS // 4
            for r in range(0, FFN_TOKENS, block):
                rows = slice(r, r + block)
                y = acc_s[rows, :] + _dot(gated(xn_s[rows, :], wg_s[last], wu_s[last]), wd_s[last])
                finish(rows, y)

    @pl.when(i >= FFN_PIECES)
    def _token_rows():
        xn = normalized(x_ref[...])
        for c in range(FFN_PIECES):
            h_ref[:, c * FFN_CHUNK:(c + 1) * FFN_CHUNK] = gated(xn, wg_s[c], wu_s[c])
        wd = wd_s[...].reshape(D_FF, D_MODEL)
        block = FFN_TOKENS // 4 if final_norm else FFN_TOKENS
        for r in range(0, FFN_TOKENS, block):
            rows = slice(r, r + block)
            finish(rows, _dot(h_ref[rows, :], wd))


def _ffn(x2d, layer, norm_w, w_gate, w_up, w_down, final_w, *, final_norm):
    tokens = x2d.shape[0]
    last = FFN_PIECES - 1
    rows = lambda i: (jnp.maximum(i - last, 0), 0)
    return pl.pallas_call(
        functools.partial(_ffn_kernel, layer=layer, final_norm=final_norm),
        out_shape=jax.ShapeDtypeStruct(x2d.shape, F32),
        grid=(last + tokens // FFN_TOKENS,),
        in_specs=[
            pl.BlockSpec((FFN_TOKENS, D_MODEL), rows),
            pl.BlockSpec(norm_w.shape, lambda i: (0, 0)),
            pl.BlockSpec((None, D_MODEL, FFN_CHUNK), lambda i: (layer, 0, jnp.minimum(i, last))),
            pl.BlockSpec((None, D_MODEL, FFN_CHUNK), lambda i: (layer, 0, jnp.minimum(i, last))),
            pl.BlockSpec((None, FFN_CHUNK, D_MODEL), lambda i: (layer, jnp.minimum(i, last), 0)),
            pl.BlockSpec((1, D_MODEL), lambda i: (0, 0)),
        ],
        out_specs=pl.BlockSpec((FFN_TOKENS, D_MODEL), rows),
        scratch_shapes=[
            pltpu.VMEM((FFN_PIECES, D_MODEL, FFN_CHUNK), BF16),
            pltpu.VMEM((FFN_PIECES, D_MODEL, FFN_CHUNK), BF16),
            pltpu.VMEM((FFN_PIECES, FFN_CHUNK, D_MODEL), BF16),
            pltpu.VMEM((FFN_TOKENS, D_FF), BF16),
            pltpu.VMEM((FFN_TOKENS, D_MODEL), BF16),
            pltpu.VMEM((FFN_TOKENS, D_MODEL), F32),
        ],
        compiler_params=pltpu.CompilerParams(
            dimension_semantics=("arbitrary",), vmem_limit_bytes=VMEM_LIMIT_FFN),
        name="ffn_halfstep",
    )(x2d, norm_w, w_gate, w_up, w_down, final_w)


def _mix_chunk(proj_ref, dt_ref, ycat_ref, keep, bias_idx, p):
    row = lax.broadcasted_iota(jnp.int32, (CHUNK, CHUNK), 0)
    lane = lax.broadcasted_iota(jnp.int32, (CHUNK, CHUNK), 1)
    causal = row >= lane
    row_b = row.astype(F32).astype(BF16)
    lane_b = lane.astype(F32).astype(BF16)
    causal_b = row_b >= lane_b
    upper_half_b = lane_b >= HALF_LANES
    zero_b = jnp.zeros((CHUNK, CHUNK), BF16)
    group_b = (lax.broadcasted_iota(jnp.int32, (CHUNK, GROUP_WIDTH), 1) // SSD_HEAD_DIM).astype(F32).astype(BF16)

    def block_diag(x_b):
        zero = jnp.zeros_like(x_b)
        return jnp.concatenate([jnp.where(group_b == i, x_b, zero) for i in range(HEADS_PER_GROUP)], axis=0)

    xbc_parts = []
    for c0 in range(0, SSD_XBC, CONV_COLS):
        cs = slice(c0, c0 + CONV_COLS)
        if keep is not None:
            p.ext[0:SUBLANES, cs] = p.ext[0:SUBLANES, cs] * keep
        raw = proj_ref[:, P_XBC + c0:P_XBC + c0 + CONV_COLS]
        p.ext[SUBLANES:, cs] = raw
        ext = p.ext[:, cs]
        w = p.convw[:, cs]
        ext1 = pltpu.roll(ext, 1, 0)
        older = w[1:2, :] * ext + w[0:1, :] * ext1
        conv = (p.convb[:, cs] + w[3:4, :] * raw + w[2:3, :] * ext1[SUBLANES:, :]
                + pltpu.roll(older, 2, 0)[SUBLANES:, :])
        p.ext[0:SUBLANES, cs] = raw[CHUNK - SUBLANES:, :]
        xbc_parts.append(conv * _sigmoid(conv))
        yield P_XBC + c0, P_XBC + c0 + CONV_COLS
    xbc = jnp.concatenate(xbc_parts, axis=-1)
    xs = xbc[:, :SSD_INNER]
    bm = xbc[:, SSD_INNER:SSD_INNER + SSD_GROUPS * SSD_STATE]
    cm = xbc[:, SSD_INNER + SSD_GROUPS * SSD_STATE:]
    x_b = xs.astype(BF16)
    bm_b = bm.astype(BF16)
    cm_b = cm.astype(BF16)

    dtr = dt_ref[0:SSD_HEADS, :] + p.dtb[...]
    dt = jnp.maximum(dtr, 0.0) + jnp.log1p(jnp.exp(-jnp.abs(dtr)))
    acs = _lane_cumsum(dt * (-LOG2E * jnp.exp(p.alog[...])), p.triu[...])
    grow = jnp.exp2(acs)
    to_end = dt * jnp.exp2(acs[:, CHUNK - 1:CHUNK] - acs)
    src = acs - jnp.log2(dt)
    stack = jnp.concatenate(
        [acs, *_split3(grow), *_split3(to_end), jnp.zeros((CHUNK - 7 * SSD_HEADS, LANES), F32)], axis=0)
    cols = stack.T
    expanded = _dot(cols.astype(BF16), p.expand[...])
    grow_e = expanded[:, :SSD_INNER]
    to_end_e = expanded[:, SSD_INNER:]
    xd_b = (xs * to_end_e).astype(BF16)
    yield P_DT, P_TOTAL

    y_diag = []
    for g in range(SSD_GROUPS):
        n0 = g * SSD_STATE
        cb = _dot_nt(cm_b[:, n0:n0 + SSD_STATE], bm_b[:, n0:n0 + SSD_STATE])
        parts = []
        for hh in range(HEADS_PER_GROUP):
            h = g * HEADS_PER_GROUP + hh
            seg = cols[:, h:h + 1] - src[h:h + 1, :]
            parts.append((cb * jnp.exp2(jnp.where(causal, seg, -jnp.inf))).astype(BF16))
        w0 = g * GROUP_WIDTH
        y_diag.append(_dot(jnp.concatenate(parts, axis=-1), block_diag(x_b[:, w0:w0 + GROUP_WIDTH])))
        yield
    y_off = []
    for g in range(SSD_GROUPS):
        n0 = g * SSD_STATE
        w0 = g * GROUP_WIDTH
        prev = p.st[g]
        if keep is not None:
            prev = prev * keep
        y_off.append(_dot(cm_b[:, n0:n0 + SSD_STATE], prev.astype(BF16)))
        bt = bm[:, n0:n0 + SSD_STATE].T.astype(BF16)
        p.st[g] = (prev * grow_e[CHUNK - 1:CHUNK, w0:w0 + GROUP_WIDTH]
                   + _dot(bt, xd_b[:, w0:w0 + GROUP_WIDTH]))
    yield
    gated, sumsq = [], 0.0
    for g in range(SSD_GROUPS):
        ws = slice(g * GROUP_WIDTH, (g + 1) * GROUP_WIDTH)
        y = y_diag[g] + y_off[g] * grow_e[:, ws] + xs[:, ws] * p.dskip[:, ws]
        z = proj_ref[:, P_Z + g * GROUP_WIDTH:P_Z + (g + 1) * GROUP_WIDTH]
        gated.append(y * (z * _sigmoid(z)))
        sumsq = sumsq + jnp.sum(gated[g] * gated[g], axis=-1, keepdims=True)
        yield P_Z + g * GROUP_WIDTH, P_Z + (g + 1) * GROUP_WIDTH
    inv_rms = lax.rsqrt(sumsq * (1.0 / SSD_INNER) + EPS)
    for g in range(SSD_GROUPS):
        ycat_ref[:, g * GROUP_WIDTH:(g + 1) * GROUP_WIDTH] = (gated[g] * inv_rms).astype(BF16)

    q_b = (proj_ref[:, P_Q:P_Q + ATT_WIDTH] * (LOG2E / math.sqrt(ATT_HEAD_DIM))).astype(BF16)
    k_b = proj_ref[:, P_K:P_K + ATT_KV_WIDTH].astype(BF16)
    v_b = proj_ref[:, P_V:P_V + ATT_KV_WIDTH].astype(BF16)
    keys = jnp.concatenate([p.kprev[...], k_b], axis=0)
    v_prev = p.vprev[...]
    halves = lambda t: (jnp.where(upper_half_b, zero_b, t), jnp.where(upper_half_b, t, zero_b))
    v_prev_half, v_half = halves(v_prev), halves(v_b)
    p.kprev[...] = k_b
    p.vprev[...] = v_b
    probs, values = [], []
    for pos, hq in enumerate(ATT_POS_HEADS):
        tile, half = divmod(pos, 2)
        q_tile = q_b[:, tile * LANES:(tile + 1) * LANES]
        q_head = halves(q_tile)[half]
        s = _dot_nt(q_head, keys)
        s = jnp.where(causal, s[:, CHUNK:], s[:, :CHUNK]) + p.bias[bias_idx, pos]
        sink = p.sink(hq) * LOG2E
        m = jnp.maximum(jnp.max(s, axis=-1, keepdims=True), sink)
        e = jnp.exp2(s - m)
        denom = jnp.sum(e, axis=-1, keepdims=True) + jnp.exp2(sink - m)
        pn = (e * (1.0 / denom)).astype(BF16)
        probs.append(jnp.where(causal_b, zero_b, pn))
        probs.append(jnp.where(causal_b, pn, zero_b))
        for vh in (v_prev_half[half], v_half[half]):
            values.append(jnp.concatenate([zero_b, vh] if tile else [vh, zero_b], axis=-1))
        yield (P_Q, P_U) if pos == 0 else None
    att = _dot(jnp.concatenate(probs, axis=-1), jnp.concatenate(values, axis=0))
    ycat_ref[:, Y_ATT:Y_SGU] = _unit_rms(att).astype(BF16)
    yield

    u = _gelu_tanh(proj_ref[:, P_U:P_U + GM_WIDTH])
    yield P_U, P_GV
    gv = _gelu_tanh(proj_ref[:, P_GV:P_GV + GM_WIDTH])
    yield P_GV, P_DT
    mu = jnp.mean(gv, axis=-1, keepdims=True)
    gc = gv - mu
    gv = gc * lax.rsqrt(jnp.mean(gc * gc, axis=-1, keepdims=True) + EPS) * p.lnw[...] + p.lnb[...]
    gv_b = gv.astype(BF16)
    mixed = _dot(p.wcat[...], block_diag(gv_b)) + p.sgub[...]
    ycat_ref[:, Y_SGU:] = _unit_rms(u * mixed).astype(BF16)
    yield


class _MixerRefs:
    def __init__(self, **refs):
        self.__dict__.update(refs)


def _mixer_kernel(sink_ref, rel_ref,
                  xa_ref, xc_ref, ing_ref, win_ref, convw_ref, convb_ref, dtb_ref, alog_ref,
                  dskip_ref, triu_ref, expand_ref, bucket_ref,
                  lnw_ref, lnb_ref, sguw_ref, sgub_ref, outg_ref, wout_ref,
                  o_ref,
                  win_s, wdt_s, wout_s, projx_ref, projy_ref, dtx_ref, dty_ref, ycatx_ref, ycaty_ref,
                  xnx_ref, xny_ref,
                  ext_ref, st_ref, kprev_ref, vprev_ref, bias_ref, wcat_ref,
                  *, layer, chunks_per_seq):
    t = pl.program_id(0)
    this_layer = pl.ds(layer, 1)
    p = _MixerRefs(sink=lambda hq: sink_ref[layer, hq], convw=convw_ref, convb=convb_ref.at[this_layer],
                   dtb=dtb_ref, alog=alog_ref, dskip=dskip_ref, triu=triu_ref, expand=expand_ref,
                   lnw=lnw_ref.at[this_layer], lnb=lnb_ref.at[this_layer],
                   sgub=sgub_ref, ext=ext_ref, st=st_ref, kprev=kprev_ref, vprev=vprev_ref,
                   bias=bias_ref, wcat=wcat_ref)

    @pl.when(t == 0)
    def _init():
        row = lax.broadcasted_iota(jnp.int32, (CHUNK, CHUNK), 0)
        col = lax.broadcasted_iota(jnp.int32, (CHUNK, CHUNK), 1)
        bucket = bucket_ref[...]
        for pos, hq in enumerate(ATT_POS_HEADS):
            acc = jnp.zeros((CHUNK, CHUNK), F32)
            for k in range(REL_BUCKETS):
                acc = jnp.where(bucket == k, rel_ref[k, hq] * LOG2E, acc)
            bias_ref[0, pos] = acc
            bias_ref[1, pos] = jnp.where(row >= col, acc, MASKED)
        for g in range(GM_GROUPS):
            wcat_ref[:, g * CHUNK:(g + 1) * CHUNK] = jnp.where(row >= col, sguw_ref[g], 0.0).astype(BF16)

        def source_rows(b):
            lo = b * LANES
            if lo < P_Q:
                return [(lo, LANES)]
            if lo < P_K:
                pos = (lo - P_Q) // ATT_HEAD_DIM
                return [(R_Q + ATT_POS_HEADS[pos + i] * ATT_HEAD_DIM, ATT_HEAD_DIM)
                        for i in range(LANES // ATT_HEAD_DIM)]
            return [(lo - P_K + R_K, LANES)]

        def pack_rows(i, carry):
            r = pl.ds(pl.multiple_of(i * CHUNK, CHUNK), CHUNK)
            gain = ing_ref[:, r]
            for b in range(P_DT // LANES):
                tile = jnp.concatenate([win_ref[r0:r0 + n, r] for r0, n in source_rows(b)], axis=0)
                win_s[r, b * LANES:(b + 1) * LANES] = (tile * gain).T.astype(BF16)
            return carry

        lax.fori_loop(0, D_MODEL // CHUNK, pack_rows, 0)
        wdt_s[...] = jnp.concatenate(
            [win_ref[R_DT:R_DT + SSD_HEADS, :] * ing_ref[...],
             jnp.zeros((DT_ROWS - SSD_HEADS, D_MODEL), F32)], axis=0).astype(BF16)
        blocks = D_MODEL // ATT_HEAD_DIM
        first_att = Y_ATT // ATT_HEAD_DIM
        for j in range(blocks):
            src = first_att + ATT_POS_HEADS[j - first_att] if first_att <= j < first_att + ATT_Q_HEADS else j
            dst_rows = slice(j * ATT_HEAD_DIM, (j + 1) * ATT_HEAD_DIM)
            src_rows = slice(src * ATT_HEAD_DIM, (src + 1) * ATT_HEAD_DIM)
            wout_s[dst_rows, :] = (wout_ref[src_rows, :] * outg_ref[dst_rows, :]).astype(BF16)
        projy_ref[...] = jnp.zeros_like(projy_ref)
        dty_ref[...] = jnp.zeros_like(dty_ref)
        ycatx_ref[...] = jnp.zeros_like(ycatx_ref)
        st_ref[...] = jnp.zeros_like(st_ref)
        ext_ref[0:SUBLANES, :] = jnp.zeros((SUBLANES, SSD_XBC), F32)
        kprev_ref[...] = jnp.zeros_like(kprev_ref)
        vprev_ref[...] = jnp.zeros_like(vprev_ref)

    starts_seq = lax.rem(4 * t, chunks_per_seq) == 0
    keep = jnp.where(starts_seq, 0.0, 1.0).astype(F32)
    first_idx = jnp.where(starts_seq, 1, 0).astype(jnp.int32)
    even, odd = pl.ds(0, CHUNK), pl.ds(CHUNK, CHUNK)

    def projection_pieces(rows, proj_new, dt_new, ycat_out, xn):
        def normalize():
            xn[...] = _unit_rms(xa_ref[rows, :]).astype(BF16)

        def project(c0, c1):
            proj_new[:, c0:c1] = _dot(xn[...], win_s[:, c0:c1])

        def project_gv_dt():
            project(P_GV, P_DT)
            dt_t = _dot_nt(wdt_s[...], xn[...])
            dt_new[0] = dt_t[:, :CHUNK]
            dt_new[1] = dt_t[:, CHUNK:]

        def output(c0, c1):
            o_ref[rows, c0:c1] = xc_ref[rows, c0:c1] + _dot(ycat_out[...], wout_s[:, c0:c1])

        in_piece = lambda lo, hi: ("in", lo, hi, functools.partial(project, lo, hi))
        out_piece = lambda i: ("out", i * PROJ_PIECE, (i + 1) * PROJ_PIECE,
                               functools.partial(output, i * PROJ_PIECE, (i + 1) * PROJ_PIECE))
        xbc = [in_piece(P_XBC + c, P_XBC + c + PROJ_PIECE) for c in range(0, SSD_XBC, PROJ_PIECE)]
        z = [in_piece(P_Z + c, P_Z + c + PROJ_PIECE) for c in range(0, SSD_INNER, PROJ_PIECE)]
        return ([("norm", 0, 0, normalize)] + z + [out_piece(0)] + xbc[:2] + [out_piece(1)] + xbc[2:]
                + [out_piece(2), in_piece(P_Q, P_K), in_piece(P_K, P_U), out_piece(3),
                   in_piece(P_U, P_GV), ("in", P_GV, P_TOTAL, project_gv_dt)])

    def step_program(live_mixes, piece_kinds):
        mixes = [
            _mix_chunk(projy_ref.at[even], dty_ref.at[0], ycaty_ref.at[even], None, 0, p),
            _mix_chunk(projy_ref.at[odd], dty_ref.at[1], ycaty_ref.at[odd], None, 0, p),
            _mix_chunk(projx_ref.at[even], dtx_ref.at[0], ycatx_ref.at[even], keep, first_idx, p),
            _mix_chunk(projx_ref.at[odd], dtx_ref.at[1], ycatx_ref.at[odd], None, 0, p),
        ]
        pieces = [projection_pieces(pl.ds(0, PAIR), projx_ref, dtx_ref, ycatx_ref, xnx_ref),
                  projection_pieces(pl.ds(PAIR, PAIR), projy_ref, dty_ref, ycaty_ref, xny_ref)]
        pieces = [[piece for piece in phase if piece[0] in piece_kinds] for phase in pieces]
        overlaps = lambda reads, piece: reads is not None and reads[0] < piece[2] and piece[1] < reads[1]
        done = [0 if k in live_mixes else MIX_SECTIONS for k in range(len(mixes))]
        issued = [0, 0]

        def mix_may_run(k):
            section = done[k]
            if section == MIX_SECTIONS:
                return False
            if k > 0 and done[k - 1] < min(section + MIX_LAG, MIX_SECTIONS):
                return False
            if k >= 2 and issued[0] < len(pieces[0]):
                return False
            return True

        def piece_may_issue(phase):
            if issued[phase] == len(pieces[phase]):
                return False
            piece = pieces[phase][issued[phase]]
            if phase == 1:
                if piece[0] == "in":
                    return all(done[k] > s for k in (0, 1) for s, r in MIX_READS.items() if overlaps(r, piece))
                if piece[0] == "out":
                    return done[0] == done[1] == MIX_SECTIONS
            return True

        while min(done) < MIX_SECTIONS or issued != [len(pieces[0]), len(pieces[1])]:
            progressed = False
            for k, mix in enumerate(mixes):
                if mix_may_run(k):
                    assert next(mix) == MIX_READS.get(done[k]), "MIX_READS is stale"
                    done[k] += 1
                    progressed = True
            for phase in (0, 1):
                if piece_may_issue(phase):
                    pieces[phase][issued[phase]][3]()
                    issued[phase] += 1
                    progressed = True
                    break
            assert progressed, "mixer step schedule is stuck"
        for k in live_mixes:
            assert next(mixes[k], None) is None and mixes[k].gi_frame is None, "MIX_SECTIONS is stale"

    last_step = pl.num_programs(0) - 1

    @pl.when(t == 0)
    def _fill():
        step_program((2, 3), ("norm", "in"))

    @pl.when(jnp.logical_and(t > 0, t < last_step))
    def _steady():
        step_program((0, 1, 2, 3), ("norm", "in", "out"))

    @pl.when(t == last_step)
    def _drain():
        step_program((0, 1), ("out",))


def _folded_bucket_tile():
    i = np.arange(CHUNK)[:, None]
    j = np.arange(CHUNK)[None, :]
    n = np.where(j <= i, i - j, i - j + CHUNK)
    max_exact = REL_BUCKETS // 2
    large = max_exact + (np.log(np.maximum(n, 1) / max_exact) / np.log(REL_MAX_DIST / max_exact)
                         * (REL_BUCKETS - max_exact)).astype(np.int32)
    large = np.minimum(large, REL_BUCKETS - 1)
    return np.where(n < max_exact, n, large).astype(np.int32)


def _mixer(x2d, seq, layer, in_gain, w_in_t, conv_w, conv_b, dt_bias, a_log, d_skip, sinks, rel_bias,
           ln_w, ln_b, sgu_w, sgu_b, out_gain, w_out):
    tokens = x2d.shape[0]
    n_quads = tokens // QUAD
    chunks_per_seq = seq // CHUNK
    assert tokens % QUAD == 0 and seq % QUAD == 0 and GM_GROUPS == HEADS_PER_GROUP
    assert P_Q == R_DT and w_in_t.shape[1] == R_K + P_DT - P_K
    per_head_rows = lambda v: jnp.broadcast_to(v[:, None], (SSD_HEADS, LANES))
    row = lambda v: v.reshape(1, -1)
    expand = np.zeros((CHUNK, 2 * SSD_INNER), np.float32)
    for r in range(SSD_HEADS, 7 * SSD_HEADS):
        h = r % SSD_HEADS
        c0 = ((r // SSD_HEADS - 1) // 3) * SSD_INNER + h * SSD_HEAD_DIM
        expand[r, c0:c0 + SSD_HEAD_DIM] = 1.0
    expand = jnp.asarray(expand, BF16)
    triu = jnp.asarray(np.triu(np.ones((CHUNK, CHUNK), np.float32)), BF16)
    bucket = jnp.asarray(_folded_bucket_tile())
    sgu_b_e = jnp.repeat(jnp.transpose(sgu_b), GM_GROUP_DIM, axis=1)

    const2 = lambda t, *_: (0, 0)
    const3 = lambda t, *_: (0, 0, 0)
    full2 = lambda shape: pl.BlockSpec(shape, const2)
    layer_block = lambda shape: pl.BlockSpec((None,) + shape, lambda t, *_: (layer, 0, 0),
                                             pipeline_mode=pl.Buffered(1))
    projected = lambda t, *_: (jnp.minimum(t, n_quads - 1), 0)
    finished = lambda t, *_: (jnp.maximum(t - 1, 0), 0)
    grid_spec = pltpu.PrefetchScalarGridSpec(
        num_scalar_prefetch=2,
        grid=(n_quads + 1,),
        in_specs=[
            pl.BlockSpec((QUAD, D_MODEL), projected),
            pl.BlockSpec((QUAD, D_MODEL), finished),
            layer_block((1, D_MODEL)),
            layer_block((w_in_t.shape[1], D_MODEL)),
            layer_block((SSD_CONV, SSD_XBC)),
            full2(conv_b.shape),
            full2((SSD_HEADS, LANES)),
            full2((SSD_HEADS, LANES)),
            full2((1, SSD_INNER)),
            full2((CHUNK, CHUNK)),
            full2((CHUNK, 2 * SSD_INNER)),
            full2((CHUNK, CHUNK)),
            full2(ln_w.shape),
            full2(ln_b.shape),
            pl.BlockSpec((None, GM_GROUPS, CHUNK, CHUNK), lambda t, *_: (layer, 0, 0, 0)),
            full2((CHUNK, GM_WIDTH)),
            layer_block((D_MODEL, 1)),
            layer_block((D_MODEL, D_MODEL)),
        ],
        out_specs=pl.BlockSpec((QUAD, D_MODEL), finished),
        scratch_shapes=[
            pltpu.VMEM((D_MODEL, P_DT), BF16),
            pltpu.VMEM((DT_ROWS, D_MODEL), BF16),
            pltpu.VMEM((D_MODEL, D_MODEL), BF16),
            pltpu.VMEM((PAIR, P_DT), F32),
            pltpu.VMEM((PAIR, P_DT), F32),
            pltpu.VMEM((2, DT_ROWS, CHUNK), F32),
            pltpu.VMEM((2, DT_ROWS, CHUNK), F32),
            pltpu.VMEM((PAIR, D_MODEL), BF16),
            pltpu.VMEM((PAIR, D_MODEL), BF16),
            pltpu.VMEM((PAIR, D_MODEL), BF16),
            pltpu.VMEM((PAIR, D_MODEL), BF16),
            pltpu.VMEM((SUBLANES + CHUNK, SSD_XBC), F32),
            pltpu.VMEM((SSD_GROUPS, SSD_STATE, GROUP_WIDTH), F32),
            pltpu.VMEM((CHUNK, ATT_KV_WIDTH), BF16),
            pltpu.VMEM((CHUNK, ATT_KV_WIDTH), BF16),
            pltpu.VMEM((2, ATT_Q_HEADS, CHUNK, CHUNK), F32),
            pltpu.VMEM((CHUNK, GM_GROUPS * CHUNK), BF16),
        ],
    )
    return pl.pallas_call(
        functools.partial(_mixer_kernel, layer=layer, chunks_per_seq=chunks_per_seq),
        out_shape=jax.ShapeDtypeStruct(x2d.shape, F32),
        grid_spec=grid_spec,
        compiler_params=pltpu.CompilerParams(
            dimension_semantics=("arbitrary",), vmem_limit_bytes=VMEM_LIMIT_MIXER),
        name="token_mixer",
    )(sinks.astype(F32), rel_bias.astype(F32),
      x2d, x2d, in_gain, w_in_t, conv_w, conv_b, per_head_rows(dt_bias), per_head_rows(a_log),
      row(jnp.repeat(d_skip, SSD_HEAD_DIM)), triu, expand, bucket,
      ln_w, ln_b, sgu_w, sgu_b_e, out_gain, w_out)


def kernel(x, ffn1_norm, ffn1_w_gate, ffn1_w_up, ffn1_w_down, mix_norm, w_in, conv_w, conv_b, dt_bias, a_log, d_skip, ssd_norm, attn_sinks, rel_bias, attn_out_norm, sgu_ln_w, sgu_ln_b, sgu_w, sgu_b, sgu_out_norm, w_out, ffn2_norm, ffn2_w_gate, ffn2_w_up, ffn2_w_down, final_norm):
    batch, seq, d = x.shape
    depth = w_in.shape[0]
    fw = final_norm.reshape(1, d)
    x2 = x.reshape(batch * seq, d)
    mix_in_gain = mix_norm.reshape(depth, 1, d)
    w_in_t = jnp.swapaxes(w_in, 1, 2)
    att_cols = np.concatenate([np.arange(h * ATT_HEAD_DIM, (h + 1) * ATT_HEAD_DIM) for h in ATT_POS_HEADS])
    mix_out_gain = jnp.concatenate([ssd_norm, attn_out_norm[:, att_cols], sgu_out_norm], axis=1).reshape(depth, d, 1)
    for l in range(depth):
        x2 = _ffn(x2, l, ffn1_norm, ffn1_w_gate, ffn1_w_up, ffn1_w_down, fw, final_norm=False)
        x2 = _mixer(x2, seq, l, mix_in_gain, w_in_t, conv_w, conv_b, dt_bias[l],
                    a_log[l], d_skip[l], attn_sinks, rel_bias,
                    sgu_ln_w, sgu_ln_b, sgu_w, sgu_b[l], mix_out_gain, w_out)
        x2 = _ffn(x2, l, ffn2_norm, ffn2_w_gate, ffn2_w_up, ffn2_w_down, fw,
                  final_norm=(l == depth - 1))
    return x2.reshape(batch, seq, d)
```

```python
import functools
import math

import jax
import jax.numpy as jnp
import numpy as np
from jax import lax
from jax.experimental import pallas as pl
from jax.experimental.pallas import tpu as pltpu

F32 = jnp.float32
BF16 = jnp.bfloat16

D_MODEL = 1024
D_FF = 2816
EPS = 1e-6

SSD_HEADS = 8
SSD_HEAD_DIM = 64
SSD_INNER = SSD_HEADS * SSD_HEAD_DIM
SSD_GROUPS = 2
SSD_STATE = 128
SSD_CONV = 4
SSD_XBC = SSD_INNER + 2 * SSD_GROUPS * SSD_STATE
HEADS_PER_GROUP = SSD_HEADS // SSD_GROUPS
GROUP_WIDTH = HEADS_PER_GROUP * SSD_HEAD_DIM

ATT_Q_HEADS = 4
ATT_KV_HEADS = 2
ATT_HEAD_DIM = 64
ATT_WIDTH = ATT_Q_HEADS * ATT_HEAD_DIM
ATT_KV_WIDTH = ATT_KV_HEADS * ATT_HEAD_DIM
REL_BUCKETS = 32
REL_MAX_DIST = 128
ATT_POS_HEADS = (0, 2, 1, 3)

GM_GROUPS = 4
GM_GROUP_DIM = 64
GM_WIDTH = GM_GROUPS * GM_GROUP_DIM

CHUNK = 128
PAIR = 2 * CHUNK
QUAD = 2 * PAIR
PROJ_PIECE = 256
CONV_COLS = 256
MIX_SECTIONS = 18
MIX_LAG = 4

SUBLANES = 8
LANES = 128
HALF_LANES = LANES // 2
DT_PAD = LANES
DT_ROWS = 16

P_Z = 0
P_XBC = P_Z + SSD_INNER
P_Q = P_XBC + SSD_XBC
P_K = P_Q + ATT_WIDTH
P_V = P_K + ATT_KV_WIDTH
P_U = P_V + ATT_KV_WIDTH
P_GV = P_U + GM_WIDTH
P_DT = P_GV + GM_WIDTH
P_TOTAL = P_DT + DT_PAD

R_XBC = SSD_INNER
R_DT = R_XBC + SSD_XBC
R_Q = R_DT + SSD_HEADS
R_K = R_Q + ATT_WIDTH

Y_ATT = SSD_INNER
Y_SGU = Y_ATT + ATT_WIDTH

MIX_READS = {
    **{i: (P_XBC + i * CONV_COLS, P_XBC + (i + 1) * CONV_COLS) for i in range(SSD_XBC // CONV_COLS)},
    4: (P_DT, P_TOTAL),
    8: (P_Z, P_Z + GROUP_WIDTH),
    9: (P_Z + GROUP_WIDTH, P_Z + 2 * GROUP_WIDTH),
    10: (P_Q, P_U),
    15: (P_U, P_GV),
    16: (P_GV, P_DT),
}

MASKED = -1e30
LOG2E = math.log2(math.e)

FFN_TOKENS = 1024
FFN_CHUNK = 256
FFN_PIECES = D_FF // FFN_CHUNK
VMEM_LIMIT_FFN = 56 * 1024 * 1024
VMEM_LIMIT_MIXER = 48 * 1024 * 1024


def _unit_rms(x):
    return x * lax.rsqrt(jnp.mean(x * x, axis=-1, keepdims=True) + EPS)


def _sigmoid(x):
    return 1.0 / (1.0 + jnp.exp(-x))


def _gelu_tanh(x):
    c = math.sqrt(2.0 / math.pi)
    return 0.5 * x * (1.0 + jnp.tanh(c * (x + 0.044715 * (x * x * x))))


def _top_bits(x):
    bits = lax.bitcast_convert_type(x, jnp.uint32) & jnp.uint32(0xFFFF0000)
    return lax.bitcast_convert_type(bits, F32)


def _split3(x):
    hi = _top_bits(x)
    r = x - hi
    mid = _top_bits(r)
    return hi, mid, r - mid


def _dot(a, b):
    return jnp.dot(a, b, preferred_element_type=F32)


def _dot_nt(a, b):
    return lax.dot_general(a, b, (((1,), (1,)), ((), ())), preferred_element_type=F32)


def _lane_cumsum(x, upper_ones):
    rows = x.shape[0]
    parts = _dot(jnp.concatenate(_split3(x), axis=0).astype(BF16), upper_ones)
    return parts[0:rows] + parts[rows:2 * rows] + parts[2 * rows:3 * rows]


def _ffn_kernel(x_ref, nw_ref, wg_ref, wu_ref, wd_ref, fw_ref, o_ref,
                wg_s, wu_s, wd_s, h_ref, xn_s, acc_s, *, layer, final_norm):
    i = pl.program_id(0)

    def normalized(x):
        return (_unit_rms(x) * nw_ref[layer:layer + 1, :]).astype(BF16)

    def gated(xn, wg, wu):
        g = _dot(xn, wg)
        u = _dot(xn, wu)
        return (g * _sigmoid(g) * u).astype(BF16)

    def finish(rows, y):
        out = x_ref[rows, :] + 0.5 * y
        if final_norm:
            out = _unit_rms(out) * fw_ref[...]
        o_ref[rows, :] = out

    @pl.when(i < FFN_PIECES)
    def _first_rows_while_loading():
        wg_s[i] = wg_ref[...].astype(BF16)
        wu_s[i] = wu_ref[...].astype(BF16)
        wd_s[i] = wd_ref[...].astype(BF16)

        @pl.when(i == 0)
        def _start():
            block = FFN_TOKENS // 4
            for r in range(0, FFN_TOKENS, block):
                rows = slice(r, r + block)
                xn = normalized(x_ref[rows, :])
                xn_s[rows, :] = xn
                acc_s[rows, :] = _dot(gated(xn, wg_s[0], wu_s[0]), wd_s[0])

        @pl.when(i > 0)
        def _accumulate():
            acc_s[...] += _dot(gated(xn_s[...], wg_s[i], wu_s[i]), wd_s[i])

        @pl.when(i == FFN_PIECES - 1)
        def _end():
            finish(slice(None), acc_s[...])

    @pl.when(i >= FFN_PIECES)
    def _token_rows():
        wd = wd_s[...].reshape(D_FF, D_MODEL)
        half = FFN_TOKENS // 2
        for r0 in range(0, FFN_TOKENS, half):
            hrows = slice(r0, r0 + half)
            xn = normalized(x_ref[hrows, :])
            for c in range(FFN_PIECES):
                h_ref[hrows, c * FFN_CHUNK:(c + 1) * FFN_CHUNK] = gated(xn, wg_s[c], wu_s[c])
            block = half // 2 if final_norm else half
            for r in range(r0, r0 + half, block):
                rows = slice(r, r + block)
                finish(rows, _dot(h_ref[rows, :], wd))


def _ffn(x2d, layer, norm_w, w_gate, w_up, w_down, final_w, *, final_norm):
    tokens = x2d.shape[0]
    last = FFN_PIECES - 1
    rows = lambda i: (jnp.maximum(i - last, 0), 0)
    return pl.pallas_call(
        functools.partial(_ffn_kernel, layer=layer, final_norm=final_norm),
        out_shape=jax.ShapeDtypeStruct(x2d.shape, F32),
        grid=(last + tokens // FFN_TOKENS,),
        in_specs=[
            pl.BlockSpec((FFN_TOKENS, D_MODEL), rows),
            pl.BlockSpec(norm_w.shape, lambda i: (0, 0)),
            pl.BlockSpec((None, D_MODEL, FFN_CHUNK), lambda i: (layer, 0, jnp.minimum(i, last))),
            pl.BlockSpec((None, D_MODEL, FFN_CHUNK), lambda i: (layer, 0, jnp.minimum(i, last))),
            pl.BlockSpec((None, FFN_CHUNK, D_MODEL), lambda i: (layer, jnp.minimum(i, last), 0)),
            pl.BlockSpec((1, D_MODEL), lambda i: (0, 0)),
        ],
        out_specs=pl.BlockSpec((FFN_TOKENS, D_MODEL), rows),
        scratch_shapes=[
            pltpu.VMEM((FFN_PIECES, D_MODEL, FFN_CHUNK), BF16),
            pltpu.VMEM((FFN_PIECES, D_MODEL, FFN_CHUNK), BF16),
            pltpu.VMEM((FFN_PIECES, FFN_CHUNK, D_MODEL), BF16),
            pltpu.VMEM((FFN_TOKENS, D_FF), BF16),
            pltpu.VMEM((FFN_TOKENS, D_MODEL), BF16),
            pltpu.VMEM((FFN_TOKENS, D_MODEL), F32),
        ],
        compiler_params=pltpu.CompilerParams(
            dimension_semantics=("arbitrary",), vmem_limit_bytes=VMEM_LIMIT_FFN),
        name="ffn_halfstep",
    )(x2d, norm_w, w_gate, w_up, w_down, final_w)


def _mix_chunk(proj_ref, dt_ref, ycat_ref, keep, bias_idx, p):
    row = lax.broadcasted_iota(jnp.int32, (CHUNK, CHUNK), 0)
    lane = lax.broadcasted_iota(jnp.int32, (CHUNK, CHUNK), 1)
    causal = row >= lane
    row_b = row.astype(F32).astype(BF16)
    lane_b = lane.astype(F32).astype(BF16)
    causal_b = row_b >= lane_b
    upper_half_b = lane_b >= HALF_LANES
    zero_b = jnp.zeros((CHUNK, CHUNK), BF16)
    group_b = (lax.broadcasted_iota(jnp.int32, (CHUNK, GROUP_WIDTH), 1) // SSD_HEAD_DIM).astype(F32).astype(BF16)

    def block_diag(x_b):
        zero = jnp.zeros_like(x_b)
        return jnp.concatenate([jnp.where(group_b == i, x_b, zero) for i in range(HEADS_PER_GROUP)], axis=0)

    xbc_parts = []
    for c0 in range(0, SSD_XBC, CONV_COLS):
        cs = slice(c0, c0 + CONV_COLS)
        if keep is not None:
            p.ext[0:SUBLANES, cs] = p.ext[0:SUBLANES, cs] * keep
        raw = proj_ref[:, P_XBC + c0:P_XBC + c0 + CONV_COLS]
        p.ext[SUBLANES:, cs] = raw
        ext = p.ext[:, cs]
        w = p.convw[:, cs]
        ext1 = pltpu.roll(ext, 1, 0)
        older = w[1:2, :] * ext + w[0:1, :] * ext1
        conv = (p.convb[:, cs] + w[3:4, :] * raw + w[2:3, :] * ext1[SUBLANES:, :]
                + pltpu.roll(older, 2, 0)[SUBLANES:, :])
        p.ext[0:SUBLANES, cs] = raw[CHUNK - SUBLANES:, :]
        xbc_parts.append(conv * _sigmoid(conv))
        yield P_XBC + c0, P_XBC + c0 + CONV_COLS
    xbc = jnp.concatenate(xbc_parts, axis=-1)
    xs = xbc[:, :SSD_INNER]
    bm = xbc[:, SSD_INNER:SSD_INNER + SSD_GROUPS * SSD_STATE]
    cm = xbc[:, SSD_INNER + SSD_GROUPS * SSD_STATE:]
    x_b = xs.astype(BF16)
    bm_b = bm.astype(BF16)
    cm_b = cm.astype(BF16)

    dtr = dt_ref[0:SSD_HEADS, :] + p.dtb[...]
    dt = jnp.maximum(dtr, 0.0) + jnp.log1p(jnp.exp(-jnp.abs(dtr)))
    acs = _lane_cumsum(dt * (-LOG2E * jnp.exp(p.alog[...])), p.triu[...])
    grow = jnp.exp2(acs)
    to_end = dt * jnp.exp2(acs[:, CHUNK - 1:CHUNK] - acs)
    src = acs - jnp.log2(dt)
    stack = jnp.concatenate(
        [acs, *_split3(grow), *_split3(to_end), jnp.zeros((CHUNK - 7 * SSD_HEADS, LANES), F32)], axis=0)
    cols = stack.T
    expanded = _dot(cols.astype(BF16), p.expand[...])
    grow_e = expanded[:, :SSD_INNER]
    to_end_e = expanded[:, SSD_INNER:]
    xd_b = (xs * to_end_e).astype(BF16)
    yield P_DT, P_TOTAL

    y_diag = []
    for g in range(SSD_GROUPS):
        n0 = g * SSD_STATE
        cb = _dot_nt(cm_b[:, n0:n0 + SSD_STATE], bm_b[:, n0:n0 + SSD_STATE])
        parts = []
        for hh in range(HEADS_PER_GROUP):
            h = g * HEADS_PER_GROUP + hh
            seg = cols[:, h:h + 1] - src[h:h + 1, :]
            parts.append((cb * jnp.exp2(jnp.where(causal, seg, -jnp.inf))).astype(BF16))
        w0 = g * GROUP_WIDTH
        y_diag.append(_dot(jnp.concatenate(parts, axis=-1), block_diag(x_b[:, w0:w0 + GROUP_WIDTH])))
        yield
    y_off = []
    for g in range(SSD_GROUPS):
        n0 = g * SSD_STATE
        w0 = g * GROUP_WIDTH
        prev = p.st[g]
        if keep is not None:
            prev = prev * keep
        y_off.append(_dot(cm_b[:, n0:n0 + SSD_STATE], prev.astype(BF16)))
        bt = bm[:, n0:n0 + SSD_STATE].T.astype(BF16)
        p.st[g] = (prev * grow_e[CHUNK - 1:CHUNK, w0:w0 + GROUP_WIDTH]
                   + _dot(bt, xd_b[:, w0:w0 + GROUP_WIDTH]))
    yield
    gated, sumsq = [], 0.0
    for g in range(SSD_GROUPS):
        ws = slice(g * GROUP_WIDTH, (g + 1) * GROUP_WIDTH)
        y = y_diag[g] + y_off[g] * grow_e[:, ws] + xs[:, ws] * p.dskip[:, ws]
        z = proj_ref[:, P_Z + g * GROUP_WIDTH:P_Z + (g + 1) * GROUP_WIDTH]
        gated.append(y * (z * _sigmoid(z)))
        sumsq = sumsq + jnp.sum(gated[g] * gated[g], axis=-1, keepdims=True)
        yield P_Z + g * GROUP_WIDTH, P_Z + (g + 1) * GROUP_WIDTH
    inv_rms = lax.rsqrt(sumsq * (1.0 / SSD_INNER) + EPS)
    for g in range(SSD_GROUPS):
        ycat_ref[:, g * GROUP_WIDTH:(g + 1) * GROUP_WIDTH] = (gated[g] * inv_rms).astype(BF16)

    q_b = (proj_ref[:, P_Q:P_Q + ATT_WIDTH] * (LOG2E / math.sqrt(ATT_HEAD_DIM))).astype(BF16)
    k_b = proj_ref[:, P_K:P_K + ATT_KV_WIDTH].astype(BF16)
    v_b = proj_ref[:, P_V:P_V + ATT_KV_WIDTH].astype(BF16)
    keys = jnp.concatenate([p.kprev[...], k_b], axis=0)
    v_prev = p.vprev[...]
    halves = lambda t: (jnp.where(upper_half_b, zero_b, t), jnp.where(upper_half_b, t, zero_b))
    v_prev_half, v_half = halves(v_prev), halves(v_b)
    p.kprev[...] = k_b
    p.vprev[...] = v_b
    probs, values = [], []
    for pos, hq in enumerate(ATT_POS_HEADS):
        tile, half = divmod(pos, 2)
        q_tile = q_b[:, tile * LANES:(tile + 1) * LANES]
        q_head = halves(q_tile)[half]
        s = _dot_nt(q_head, keys)
        s = jnp.where(causal, s[:, CHUNK:], s[:, :CHUNK]) + p.bias[bias_idx, pos]
        sink = p.sink(hq) * LOG2E
        m = jnp.maximum(jnp.max(s, axis=-1, keepdims=True), sink)
        e = jnp.exp2(s - m)
        denom = jnp.sum(e, axis=-1, keepdims=True) + jnp.exp2(sink - m)
        pn = (e * (1.0 / denom)).astype(BF16)
        probs.append(jnp.where(causal_b, zero_b, pn))
        probs.append(jnp.where(causal_b, pn, zero_b))
        for vh in (v_prev_half[half], v_half[half]):
            values.append(jnp.concatenate([zero_b, vh] if tile else [vh, zero_b], axis=-1))
        yield (P_Q, P_U) if pos == 0 else None
    att = _dot(jnp.concatenate(probs, axis=-1), jnp.concatenate(values, axis=0))
    ycat_ref[:, Y_ATT:Y_SGU] = _unit_rms(att).astype(BF16)
    yield

    u = _gelu_tanh(proj_ref[:, P_U:P_U + GM_WIDTH])
    yield P_U, P_GV
    gv = _gelu_tanh(proj_ref[:, P_GV:P_GV + GM_WIDTH])
    yield P_GV, P_DT
    mu = jnp.mean(gv, axis=-1, keepdims=True)
    gc = gv - mu
    gv = gc * lax.rsqrt(jnp.mean(gc * gc, axis=-1, keepdims=True) + EPS) * p.lnw[...] + p.lnb[...]
    gv_b = gv.astype(BF16)
    mixed = _dot(p.wcat[...], block_diag(gv_b)) + p.sgub[...]
    ycat_ref[:, Y_SGU:] = _unit_rms(u * mixed).astype(BF16)
    yield


class _MixerRefs:
    def __init__(self, **refs):
        self.__dict__.update(refs)


def _mixer_kernel(sink_ref, rel_ref,
                  xa_ref, xc_ref, ing_ref, win_ref, convw_ref, convb_ref, dtb_ref, alog_ref,
                  dskip_ref, triu_ref, expand_ref, bucket_ref,
                  lnw_ref, lnb_ref, sguw_ref, sgub_ref, outg_ref, wout_ref,
                  o_ref,
                  win_s, wdt_s, wout_s, projx_ref, projy_ref, dtx_ref, dty_ref, ycatx_ref, ycaty_ref,
                  xnx_ref, xny_ref,
                  ext_ref, st_ref, kprev_ref, vprev_ref, bias_ref, wcat_ref,
                  *, layer, chunks_per_seq):
    t = pl.program_id(0)
    this_layer = pl.ds(layer, 1)
    p = _MixerRefs(sink=lambda hq: sink_ref[layer, hq], convw=convw_ref, convb=convb_ref.at[this_layer],
                   dtb=dtb_ref, alog=alog_ref, dskip=dskip_ref, triu=triu_ref, expand=expand_ref,
                   lnw=lnw_ref.at[this_layer], lnb=lnb_ref.at[this_layer],
                   sgub=sgub_ref, ext=ext_ref, st=st_ref, kprev=kprev_ref, vprev=vprev_ref,
                   bias=bias_ref, wcat=wcat_ref)

    @pl.when(t == 0)
    def _init():
        row = lax.broadcasted_iota(jnp.int32, (CHUNK, CHUNK), 0)
        col = lax.broadcasted_iota(jnp.int32, (CHUNK, CHUNK), 1)
        bucket = bucket_ref[...]
        for pos, hq in enumerate(ATT_POS_HEADS):
            acc = jnp.zeros((CHUNK, CHUNK), F32)
            for k in range(REL_BUCKETS):
                acc = jnp.where(bucket == k, rel_ref[k, hq] * LOG2E, acc)
            bias_ref[0, pos] = acc
            bias_ref[1, pos] = jnp.where(row >= col, acc, MASKED)
        for g in range(GM_GROUPS):
            wcat_ref[:, g * CHUNK:(g + 1) * CHUNK] = jnp.where(row >= col, sguw_ref[g], 0.0).astype(BF16)

        def source_rows(b):
            lo = b * LANES
            if lo < P_Q:
                return [(lo, LANES)]
            if lo < P_K:
                pos = (lo - P_Q) // ATT_HEAD_DIM
                return [(R_Q + ATT_POS_HEADS[pos + i] * ATT_HEAD_DIM, ATT_HEAD_DIM)
                        for i in range(LANES // ATT_HEAD_DIM)]
            return [(lo - P_K + R_K, LANES)]

        def pack_rows(i, carry):
            r = pl.ds(pl.multiple_of(i * CHUNK, CHUNK), CHUNK)
            gain = ing_ref[:, r]
            for b in range(P_DT // LANES):
                tile = jnp.concatenate([win_ref[r0:r0 + n, r] for r0, n in source_rows(b)], axis=0)
                win_s[r, b * LANES:(b + 1) * LANES] = (tile * gain).T.astype(BF16)
            return carry

        lax.fori_loop(0, D_MODEL // CHUNK, pack_rows, 0)
        wdt_s[...] = jnp.concatenate(
            [win_ref[R_DT:R_DT + SSD_HEADS, :] * ing_ref[...],
             jnp.zeros((DT_ROWS - SSD_HEADS, D_MODEL), F32)], axis=0).astype(BF16)
        blocks = D_MODEL // ATT_HEAD_DIM
        first_att = Y_ATT // ATT_HEAD_DIM
        for j in range(blocks):
            src = first_att + ATT_POS_HEADS[j - first_att] if first_att <= j < first_att + ATT_Q_HEADS else j
            dst_rows = slice(j * ATT_HEAD_DIM, (j + 1) * ATT_HEAD_DIM)
            src_rows = slice(src * ATT_HEAD_DIM, (src + 1) * ATT_HEAD_DIM)
            wout_s[dst_rows, :] = (wout_ref[src_rows, :] * outg_ref[dst_rows, :]).astype(BF16)
        projy_ref[...] = jnp.zeros_like(projy_ref)
        dty_ref[...] = jnp.zeros_like(dty_ref)
        ycatx_ref[...] = jnp.zeros_like(ycatx_ref)
        st_ref[...] = jnp.zeros_like(st_ref)
        ext_ref[0:SUBLANES, :] = jnp.zeros((SUBLANES, SSD_XBC), F32)
        kprev_ref[...] = jnp.zeros_like(kprev_ref)
        vprev_ref[...] = jnp.zeros_like(vprev_ref)

    starts_seq = lax.rem(4 * t, chunks_per_seq) == 0
    keep = jnp.where(starts_seq, 0.0, 1.0).astype(F32)
    first_idx = jnp.where(starts_seq, 1, 0).astype(jnp.int32)
    even, odd = pl.ds(0, CHUNK), pl.ds(CHUNK, CHUNK)

    def projection_pieces(rows, proj_new, dt_new, ycat_out, xn):
        def normalize():
            xn[...] = _unit_rms(xa_ref[rows, :]).astype(BF16)

        def project(c0, c1):
            proj_new[:, c0:c1] = _dot(xn[...], win_s[:, c0:c1])

        def project_gv_dt():
            project(P_GV, P_DT)
            dt_t = _dot_nt(wdt_s[...], xn[...])
            dt_new[0] = dt_t[:, :CHUNK]
            dt_new[1] = dt_t[:, CHUNK:]

        def output(c0, c1):
            o_ref[rows, c0:c1] = xc_ref[rows, c0:c1] + _dot(ycat_out[...], wout_s[:, c0:c1])

        in_piece = lambda lo, hi: ("in", lo, hi, functools.partial(project, lo, hi))
        out_piece = lambda i: ("out", i * PROJ_PIECE, (i + 1) * PROJ_PIECE,
                               functools.partial(output, i * PROJ_PIECE, (i + 1) * PROJ_PIECE))
        xbc = [in_piece(P_XBC + c, P_XBC + c + PROJ_PIECE) for c in range(0, SSD_XBC, PROJ_PIECE)]
        z = [in_piece(P_Z + c, P_Z + c + PROJ_PIECE) for c in range(0, SSD_INNER, PROJ_PIECE)]
        return ([("norm", 0, 0, normalize)] + z + [out_piece(0)] + xbc[:2] + [out_piece(1)] + xbc[2:]
                + [out_piece(2), in_piece(P_Q, P_K), in_piece(P_K, P_U), out_piece(3),
                   in_piece(P_U, P_GV), ("in", P_GV, P_TOTAL, project_gv_dt)])

    def step_program(live_mixes, piece_kinds):
        mixes = [
            _mix_chunk(projy_ref.at[even], dty_ref.at[0], ycaty_ref.at[even], None, 0, p),
            _mix_chunk(projy_ref.at[odd], dty_ref.at[1], ycaty_ref.at[odd], None, 0, p),
            _mix_chunk(projx_ref.at[even], dtx_ref.at[0], ycatx_ref.at[even], keep, first_idx, p),
            _mix_chunk(projx_ref.at[odd], dtx_ref.at[1], ycatx_ref.at[odd], None, 0, p),
        ]
        pieces = [projection_pieces(pl.ds(0, PAIR), projx_ref, dtx_ref, ycatx_ref, xnx_ref),
                  projection_pieces(pl.ds(PAIR, PAIR), projy_ref, dty_ref, ycaty_ref, xny_ref)]
        pieces = [[piece for piece in phase if piece[0] in piece_kinds] for phase in pieces]
        overlaps = lambda reads, piece: reads is not None and reads[0] < piece[2] and piece[1] < reads[1]
        done = [0 if k in live_mixes else MIX_SECTIONS for k in range(len(mixes))]
        issued = [0, 0]

        def mix_may_run(k):
            section = done[k]
            if section == MIX_SECTIONS:
                return False
            if k > 0 and done[k - 1] < min(section + MIX_LAG, MIX_SECTIONS):
                return False
            if k >= 2 and issued[0] < len(pieces[0]):
                return False
            return True

        def piece_may_issue(phase):
            if issued[phase] == len(pieces[phase]):
                return False
            piece = pieces[phase][issued[phase]]
            if phase == 1:
                if piece[0] == "in":
                    return all(done[k] > s for k in (0, 1) for s, r in MIX_READS.items() if overlaps(r, piece))
                if piece[0] == "out":
                    return done[0] == done[1] == MIX_SECTIONS
            return True

        while min(done) < MIX_SECTIONS or issued != [len(pieces[0]), len(pieces[1])]:
            progressed = False
            for k, mix in enumerate(mixes):
                if mix_may_run(k):
                    assert next(mix) == MIX_READS.get(done[k]), "MIX_READS is stale"
                    done[k] += 1
                    progressed = True
            for phase in (0, 1):
                if piece_may_issue(phase):
                    pieces[phase][issued[phase]][3]()
                    issued[phase] += 1
                    progressed = True
                    break
            assert progressed, "mixer step schedule is stuck"
        for k in live_mixes:
            assert next(mixes[k], None) is None and mixes[k].gi_frame is None, "MIX_SECTIONS is stale"

    last_step = pl.num_programs(0) - 1

    @pl.when(t == 0)
    def _fill():
        step_program((2, 3), ("norm", "in"))

    @pl.when(jnp.logical_and(t > 0, t < last_step))
    def _steady():
        step_program((0, 1, 2, 3), ("norm", "in", "out"))

    @pl.when(t == last_step)
    def _drain():
        step_program((0, 1), ("out",))


def _folded_bucket_tile():
    i = np.arange(CHUNK)[:, None]
    j = np.arange(CHUNK)[None, :]
    n = np.where(j <= i, i - j, i - j + CHUNK)
    max_exact = REL_BUCKETS // 2
    large = max_exact + (np.log(np.maximum(n, 1) / max_exact) / np.log(REL_MAX_DIST / max_exact)
                         * (REL_BUCKETS - max_exact)).astype(np.int32)
    large = np.minimum(large, REL_BUCKETS - 1)
    return np.where(n < max_exact, n, large).astype(np.int32)


def _mixer(x2d, seq, layer, in_gain, w_in_t, conv_w, conv_b, dt_bias, a_log, d_skip, sinks, rel_bias,
           ln_w, ln_b, sgu_w, sgu_b, out_gain, w_out):
    tokens = x2d.shape[0]
    n_quads = tokens // QUAD
    chunks_per_seq = seq // CHUNK
    assert tokens % QUAD == 0 and seq % QUAD == 0 and GM_GROUPS == HEADS_PER_GROUP
    assert P_Q == R_DT and w_in_t.shape[1] == R_K + P_DT - P_K
    per_head_rows = lambda v: jnp.broadcast_to(v[:, None], (SSD_HEADS, LANES))
    row = lambda v: v.reshape(1, -1)
    expand = np.zeros((CHUNK, 2 * SSD_INNER), np.float32)
    for r in range(SSD_HEADS, 7 * SSD_HEADS):
        h = r % SSD_HEADS
        c0 = ((r // SSD_HEADS - 1) // 3) * SSD_INNER + h * SSD_HEAD_DIM
        expand[r, c0:c0 + SSD_HEAD_DIM] = 1.0
    expand = jnp.asarray(expand, BF16)
    triu = jnp.asarray(np.triu(np.ones((CHUNK, CHUNK), np.float32)), BF16)
    bucket = jnp.asarray(_folded_bucket_tile())
    sgu_b_e = jnp.repeat(jnp.transpose(sgu_b), GM_GROUP_DIM, axis=1)

    const2 = lambda t, *_: (0, 0)
    const3 = lambda t, *_: (0, 0, 0)
    full2 = lambda shape: pl.BlockSpec(shape, const2)
    layer_block = lambda shape: pl.BlockSpec((None,) + shape, lambda t, *_: (layer, 0, 0),
                                             pipeline_mode=pl.Buffered(1))
    projected = lambda t, *_: (jnp.minimum(t, n_quads - 1), 0)
    finished = lambda t, *_: (jnp.maximum(t - 1, 0), 0)
    grid_spec = pltpu.PrefetchScalarGridSpec(
        num_scalar_prefetch=2,
        grid=(n_quads + 1,),
        in_specs=[
            pl.BlockSpec((QUAD, D_MODEL), projected),
            pl.BlockSpec((QUAD, D_MODEL), finished),
            layer_block((1, D_MODEL)),
            layer_block((w_in_t.shape[1], D_MODEL)),
            layer_block((SSD_CONV, SSD_XBC)),
            full2(conv_b.shape),
            full2((SSD_HEADS, LANES)),
            full2((SSD_HEADS, LANES)),
            full2((1, SSD_INNER)),
            full2((CHUNK, CHUNK)),
            full2((CHUNK, 2 * SSD_INNER)),
            full2((CHUNK, CHUNK)),
            full2(ln_w.shape),
            full2(ln_b.shape),
            pl.BlockSpec((None, GM_GROUPS, CHUNK, CHUNK), lambda t, *_: (layer, 0, 0, 0)),
            full2((CHUNK, GM_WIDTH)),
            layer_block((D_MODEL, 1)),
            layer_block((D_MODEL, D_MODEL)),
        ],
        out_specs=pl.BlockSpec((QUAD, D_MODEL), finished),
        scratch_shapes=[
            pltpu.VMEM((D_MODEL, P_DT), BF16),
            pltpu.VMEM((DT_ROWS, D_MODEL), BF16),
            pltpu.VMEM((D_MODEL, D_MODEL), BF16),
            pltpu.VMEM((PAIR, P_DT), F32),
            pltpu.VMEM((PAIR, P_DT), F32),
            pltpu.VMEM((2, DT_ROWS, CHUNK), F32),
            pltpu.VMEM((2, DT_ROWS, CHUNK), F32),
            pltpu.VMEM((PAIR, D_MODEL), BF16),
            pltpu.VMEM((PAIR, D_MODEL), BF16),
            pltpu.VMEM((PAIR, D_MODEL), BF16),
            pltpu.VMEM((PAIR, D_MODEL), BF16),
            pltpu.VMEM((SUBLANES + CHUNK, SSD_XBC), F32),
            pltpu.VMEM((SSD_GROUPS, SSD_STATE, GROUP_WIDTH), F32),
            pltpu.VMEM((CHUNK, ATT_KV_WIDTH), BF16),
            pltpu.VMEM((CHUNK, ATT_KV_WIDTH), BF16),
            pltpu.VMEM((2, ATT_Q_HEADS, CHUNK, CHUNK), F32),
            pltpu.VMEM((CHUNK, GM_GROUPS * CHUNK), BF16),
        ],
    )
    return pl.pallas_call(
        functools.partial(_mixer_kernel, layer=layer, chunks_per_seq=chunks_per_seq),
        out_shape=jax.ShapeDtypeStruct(x2d.shape, F32),
        grid_spec=grid_spec,
        compiler_params=pltpu.CompilerParams(
            dimension_semantics=("arbitrary",), vmem_limit_bytes=VMEM_LIMIT_MIXER),
        name="token_mixer",
    )(sinks.astype(F32), rel_bias.astype(F32),
      x2d, x2d, in_gain, w_in_t, conv_w, conv_b, per_head_rows(dt_bias), per_head_rows(a_log),
      row(jnp.repeat(d_skip, SSD_HEAD_DIM)), triu, expand, bucket,
      ln_w, ln_b, sgu_w, sgu_b_e, out_gain, w_out)


def kernel(x, ffn1_norm, ffn1_w_gate, ffn1_w_up, ffn1_w_down, mix_norm, w_in, conv_w, conv_b, dt_bias, a_log, d_skip, ssd_norm, attn_sinks, rel_bias, attn_out_norm, sgu_ln_w, sgu_ln_b, sgu_w, sgu_b, sgu_out_norm, w_out, ffn2_norm, ffn2_w_gate, ffn2_w_up, ffn2_w_down, final_norm):
    batch, seq, d = x.shape
    depth = w_in.shape[0]
    fw = final_norm.reshape(1, d)
    x2 = x.reshape(batch * seq, d)
    mix_in_gain = mix_norm.reshape(depth, 1, d)
    w_in_t = jnp.swapaxes(w_in, 1, 2)
    att_cols = np.concatenate([np.arange(h * ATT_HEAD_DIM, (h + 1) * ATT_HEAD_DIM) for h in ATT_POS_HEADS])
    mix_out_gain = jnp.concatenate([ssd_norm, attn_out_norm[:, att_cols], sgu_out_norm], axis=1).reshape(depth, d, 1)
    for l in range(depth):
        x2 = _ffn(x2, l, ffn1_norm, ffn1_w_gate, ffn1_w_up, ffn1_w_down, fw, final_norm=False)
        x2 = _mixer(x2, seq, l, mix_in_gain, w_in_t, conv_w, conv_b, dt_bias[l],
                    a_log[l], d_skip[l], attn_sinks, rel_bias,
                    sgu_ln_w, sgu_ln_b, sgu_w, sgu_b[l], mix_out_gain, w_out)
        x2 = _ffn(x2, l, ffn2_norm, ffn2_w_gate, ffn2_w_up, ffn2_w_down, fw,
                  final_norm=(l == depth - 1))
    return x2.reshape(batch, seq, d)
```
